```python
import math
import jax, jax.numpy as jnp
from jax import lax
import numpy as np

D_MODEL = 1024
BATCH = 8
SEQ = 2048
DEPTH = 1
DEC_BATCH = 128
DEC_SEQ = 4
PAST_LEN = 16384
PAGE_SIZE = 128

N_META = 16
H_RET = 4
H_GDN = 4
HEAD_DIM = D_MODEL // (H_RET + H_GDN)
RET_WIDTH = H_RET * HEAD_DIM
GDN_WIDTH = H_GDN * HEAD_DIM
MIX_WIDTH = RET_WIDTH + GDN_WIDTH
IN_COLS = 4 * RET_WIDTH + 4 * GDN_WIDTH + 2 * H_GDN
CHUNK = 128
CONV_W = 4
D_FF = -(-8 * D_MODEL // (3 * 256)) * 256
ROPE_BASE = 10000.0
LN_EPS = 1e-5
RMS_EPS = 1e-6
ALPHA = (2 * DEPTH) ** 0.25
BETA = (8 * DEPTH) ** -0.25

kernel_name = 'hymba_retention_gated_deltanet_deepnorm_step'


def layer_norm(x, g, b):
    xf = x.astype(jnp.float32)
    mu = jnp.mean(xf, -1, keepdims=True)
    var = jnp.mean(jnp.square(xf - mu), -1, keepdims=True)
    return ((xf - mu) * lax.rsqrt(var + LN_EPS) * g.astype(jnp.float32) + b.astype(jnp.float32)).astype(x.dtype)


def rms_norm(x):
    return x * lax.rsqrt(jnp.mean(jnp.square(x), -1, keepdims=True) + RMS_EPS)


def l2_norm(x):
    return x * lax.rsqrt(jnp.sum(jnp.square(x), -1, keepdims=True) + RMS_EPS)


def rotary(x, pos):
    half = HEAD_DIM // 2
    inv = ROPE_BASE ** (-jnp.arange(half, dtype=jnp.float32) / half)
    ang = pos.astype(jnp.float32)[:, None] * inv[None, :]
    cos = jnp.cos(ang)[None, :, None, :]
    sin = jnp.sin(ang)[None, :, None, :]
    x1, x2 = x[..., :half], x[..., half:]
    return jnp.concatenate([x1 * cos - x2 * sin, x1 * sin + x2 * cos], -1)


def retention_chunks(q, k, v, log_gamma, s0):
    C = q.shape[3]
    idx = jnp.arange(C, dtype=jnp.float32)
    rel = idx[:, None] - idx[None, :]
    causal = rel >= 0
    lg = log_gamma[:, None, None]
    decay = jnp.where(causal, jnp.exp(lg * jnp.where(causal, rel, 0.0)), 0.0)
    scores = jnp.einsum('bnhid,bnhjd->bnhij', q, k) * decay
    intra = jnp.einsum('bnhij,bnhjd->bnhid', scores, v)
    q_d = q * jnp.exp(log_gamma[:, None] * (idx + 1.0))[..., None]
    k_d = k * jnp.exp(log_gamma[:, None] * (C - 1.0 - idx))[..., None]
    chunk_decay = jnp.exp(log_gamma * C)

    def step(s, xs):
        qc, kc, vc = xs
        inter = jnp.einsum('bhcd,bhde->bhce', qc, s)
        s = chunk_decay[:, None, None] * s + jnp.einsum('bhcd,bhce->bhde', kc, vc)
        return s, inter

    xs = (jnp.moveaxis(q_d, 1, 0), jnp.moveaxis(k_d, 1, 0), jnp.moveaxis(v, 1, 0))
    s, inter = lax.scan(step, s0, xs)
    return intra + jnp.moveaxis(inter, 0, 1), s


def gated_delta_chunks(q, k, v, g, beta, s0):
    C = q.shape[3]
    D = q.shape[4]
    G = jnp.cumsum(g, axis=-1)
    idx = jnp.arange(C)
    incl = idx[:, None] >= idx[None, :]
    strict = idx[:, None] > idx[None, :]
    decay = jnp.exp(jnp.where(incl, G[..., :, None] - G[..., None, :], -jnp.inf))
    kk = jnp.einsum('bnhid,bnhjd->bnhij', k, k)
    lower = jnp.where(strict, kk * decay, 0.0) * beta[..., :, None]
    a_mat = jnp.eye(C, dtype=q.dtype) + lower
    gam = jnp.exp(G)
    rhs = jnp.concatenate([v * beta[..., None], k * (beta * gam)[..., None]], -1)
    sol = lax.linalg.triangular_solve(a_mat, rhs, left_side=True, lower=True, unit_diagonal=True)
    w_v, w_k = sol[..., :D], sol[..., D:]
    attn = jnp.einsum('bnhid,bnhjd->bnhij', q, k) * decay
    q_g = q * gam[..., None]
    k_t = k * jnp.exp(G[..., -1:] - G)[..., None]
    chunk_decay = jnp.exp(G[..., -1])

    def step(s, xs):
        wv, wk, at, qg, kt, cd = xs
        u = wv - jnp.einsum('bhcd,bhde->bhce', wk, s)
        o = jnp.einsum('bhcd,bhde->bhce', qg, s) + jnp.einsum('bhij,bhje->bhie', at, u)
        s = cd[..., None, None] * s + jnp.einsum('bhcd,bhce->bhde', kt, u)
        return s, o

    xs = tuple(jnp.moveaxis(t, 1, 0) for t in (w_v, w_k, attn, q_g, k_t, chunk_decay))
    s, o = lax.scan(step, s0, xs)
    return jnp.moveaxis(o, 0, 1), s


def token_mixers(h, pos, ret_s0, gdn_s0, conv_buf, segments, w_in, conv_w, a_log, dt_bias, gdn_norm_w, w_out):
    bsz, L, _ = h.shape
    proj = h @ w_in
    R, W = RET_WIDTH, GDN_WIDTH
    rq = proj[..., 0:R]
    rk = proj[..., R:2 * R]
    rv = proj[..., 2 * R:3 * R]
    rg = proj[..., 3 * R:4 * R]
    o = 4 * R
    gqkv = proj[..., o:o + 3 * W]
    gz = proj[..., o + 3 * W:o + 4 * W]
    ga = proj[..., o + 4 * W:o + 4 * W + H_GDN]
    gb = proj[..., o + 4 * W + H_GDN:o + 4 * W + 2 * H_GDN]

    f32 = jnp.float32
    rq = rotary(rq.astype(f32).reshape(bsz, L, H_RET, HEAD_DIM), pos)
    rk = rotary(rk.astype(f32).reshape(bsz, L, H_RET, HEAD_DIM), pos) * (HEAD_DIM ** -0.5)
    rv = rv.astype(f32).reshape(bsz, L, H_RET, HEAD_DIM)
    log_gamma = jnp.log(1.0 - 2.0 ** (-5.0 - jnp.arange(H_RET, dtype=f32)))

    xc = jnp.concatenate([conv_buf.astype(gqkv.dtype), gqkv], 1)
    conv = sum(xc[:, w:w + L] * conv_w[w] for w in range(CONV_W))
    new_conv = xc[:, L:]
    conv = jax.nn.silu(conv.astype(f32))
    gq = l2_norm(conv[..., :W].reshape(bsz, L, H_GDN, HEAD_DIM)) * (HEAD_DIM ** -0.5)
    gk = l2_norm(conv[..., W:2 * W].reshape(bsz, L, H_GDN, HEAD_DIM))
    gv = conv[..., 2 * W:].reshape(bsz, L, H_GDN, HEAD_DIM)
    beta = jax.nn.sigmoid(gb.astype(f32))
    g = -jnp.exp(a_log.astype(f32)) * jax.nn.softplus(ga.astype(f32) + dt_bias.astype(f32))

    ret_s = ret_s0.astype(f32)
    gdn_s = gdn_s0.astype(f32)
    ret_outs, gdn_outs = [], []
    for start, length, chunk in segments:
        n = length // chunk

        def blk(t):
            t = t[:, start:start + length].reshape((bsz, n, chunk) + t.shape[2:])
            return jnp.swapaxes(t, 2, 3)

        def unblk(t):
            return jnp.swapaxes(t, 2, 3).reshape(bsz, length, t.shape[2], HEAD_DIM)

        o_r, ret_s = retention_chunks(blk(rq), blk(rk), blk(rv), log_gamma, ret_s)
        o_g, gdn_s = gated_delta_chunks(blk(gq), blk(gk), blk(gv), blk(g), blk(beta), gdn_s)
        ret_outs.append(unblk(o_r))
        gdn_outs.append(unblk(o_g))
    o_r = jnp.concatenate(ret_outs, 1)
    o_g = jnp.concatenate(gdn_outs, 1)

    o_r = rms_norm(o_r).reshape(bsz, L, R) * jax.nn.silu(rg.astype(f32))
    o_g = (rms_norm(o_g) * gdn_norm_w.astype(f32)).reshape(bsz, L, W) * jax.nn.silu(gz.astype(f32))
    mixed = jnp.concatenate([o_r, o_g], -1).astype(h.dtype) @ w_out
    return mixed, ret_s, gdn_s, new_conv


def swiglu(h, w_gate_up, w_down):
    gu = h @ w_gate_up
    return (jax.nn.silu(gu[..., :D_FF]) * gu[..., D_FF:]) @ w_down


def setup_inputs(seed: int = 0) -> dict:
    key = jax.random.key(seed)
    ks = jax.random.split(key, 24)
    f32 = jnp.float32
    nrm = lambda k, s, sc: jax.random.normal(k, s, f32) * sc
    dt = jnp.exp(jax.random.uniform(ks[10], (DEPTH, H_GDN), f32, math.log(1e-3), math.log(1e-1)))
    return {
        'x_prompt': nrm(ks[0], (BATCH, SEQ, D_MODEL), 1.0),
        'x_sample': nrm(ks[1], (DEC_BATCH, DEC_SEQ, D_MODEL), 1.0),
        'state_ret': nrm(ks[2], (DEPTH, DEC_BATCH, H_RET, HEAD_DIM, HEAD_DIM), 0.5),
        'state_gdn': nrm(ks[3], (DEPTH, DEC_BATCH, H_GDN, HEAD_DIM, HEAD_DIM), 0.1),
        'state_conv': nrm(ks[4], (DEPTH, DEC_BATCH, CONV_W - 1, 3 * GDN_WIDTH), 1.0),
        'meta_tokens': nrm(ks[5], (N_META, D_MODEL), 1.0),
        'emb_ln_g': 1.0 + nrm(ks[6], (D_MODEL,), 0.02),
        'emb_ln_b': nrm(ks[7], (D_MODEL,), 0.02),
        'w_in': nrm(ks[8], (DEPTH, D_MODEL, IN_COLS), D_MODEL ** -0.5),
        'conv_w': nrm(ks[9], (DEPTH, CONV_W, 3 * GDN_WIDTH), CONV_W ** -0.5),
        'a_log': jnp.log(jax.random.uniform(ks[11], (DEPTH, H_GDN), f32, 1.0, 16.0)),
        'dt_bias': dt + jnp.log(-jnp.expm1(-dt)),
        'gdn_norm_w': 1.0 + nrm(ks[12], (DEPTH, HEAD_DIM), 0.02),
        'w_out': nrm(ks[13], (DEPTH, MIX_WIDTH, D_MODEL), BETA * MIX_WIDTH ** -0.5),
        'ln1_g': 1.0 + nrm(ks[14], (DEPTH, D_MODEL), 0.02),
        'ln1_b': nrm(ks[15], (DEPTH, D_MODEL), 0.02),
        'w_gate_up': nrm(ks[16], (DEPTH, D_MODEL, 2 * D_FF), D_MODEL ** -0.5),
        'w_down': nrm(ks[17], (DEPTH, D_FF, D_MODEL), BETA * D_FF ** -0.5),
        'ln2_g': 1.0 + nrm(ks[18], (DEPTH, D_MODEL), 0.02),
        'ln2_b': nrm(ks[19], (DEPTH, D_MODEL), 0.02),
    }


def reference(x_prompt, x_sample, state_ret, state_gdn, state_conv, meta_tokens, emb_ln_g, emb_ln_b,
              w_in, conv_w, a_log, dt_bias, gdn_norm_w, w_out, ln1_g, ln1_b, w_gate_up, w_down,
              ln2_g, ln2_b):
    bsz = x_prompt.shape[0]
    meta = jnp.broadcast_to(meta_tokens.astype(x_prompt.dtype)[None], (bsz, N_META, D_MODEL))
    hp = layer_norm(jnp.concatenate([meta, x_prompt], 1), emb_ln_g, emb_ln_b)
    pos_p = jnp.arange(N_META + SEQ)
    seg_p = ((0, N_META, N_META), (N_META, SEQ, CHUNK))
    hs = layer_norm(x_sample, emb_ln_g, emb_ln_b)
    pos_s = PAST_LEN + jnp.arange(DEC_SEQ)
    seg_s = ((0, DEC_SEQ, DEC_SEQ),)

    f32 = jnp.float32
    ret_p, gdn_p, conv_p, ret_s, gdn_s, conv_s = [], [], [], [], [], []
    for layer in range(DEPTH):
        mix_args = (w_in[layer], conv_w[layer], a_log[layer], dt_bias[layer], gdn_norm_w[layer], w_out[layer])
        zeros_ret = jnp.zeros((bsz, H_RET, HEAD_DIM, HEAD_DIM), f32)
        zeros_gdn = jnp.zeros((bsz, H_GDN, HEAD_DIM, HEAD_DIM), f32)
        zeros_conv = jnp.zeros((bsz, CONV_W - 1, 3 * GDN_WIDTH), hp.dtype)
        mp, r1, g1, c1 = token_mixers(hp, pos_p, zeros_ret, zeros_gdn, zeros_conv, seg_p, *mix_args)
        ms, r2, g2, c2 = token_mixers(hs, pos_s, state_ret[layer], state_gdn[layer], state_conv[layer], seg_s, *mix_args)
        hp = layer_norm(ALPHA * hp + mp, ln1_g[layer], ln1_b[layer])
        hs = layer_norm(ALPHA * hs + ms, ln1_g[layer], ln1_b[layer])
        hp = layer_norm(ALPHA * hp + swiglu(hp, w_gate_up[layer], w_down[layer]), ln2_g[layer], ln2_b[layer])
        hs = layer_norm(ALPHA * hs + swiglu(hs, w_gate_up[layer], w_down[layer]), ln2_g[layer], ln2_b[layer])
        ret_p.append(r1); gdn_p.append(g1); conv_p.append(c1)
        ret_s.append(r2); gdn_s.append(g2); conv_s.append(c2)

    y_prompt = hp[:, N_META:]
    y_sample = hs
    return (y_prompt, y_sample, jnp.stack(ret_p), jnp.stack(gdn_p), jnp.stack(conv_p),
            jnp.stack(ret_s), jnp.stack(gdn_s), jnp.stack(conv_s))
```

```python
import functools

import jax
import jax.numpy as jnp
from jax import lax
from jax.experimental import pallas as pl
from jax.experimental.pallas import tpu as pltpu

F32 = jnp.float32
BF16 = jnp.bfloat16

D_MODEL = 1024
N_META = 16
N_HEADS = 4
HEAD_DIM = 128
GROUP_W = N_HEADS * HEAD_DIM
CONV_W = 4
CONV_COLS = 3 * GROUP_W
D_FF = 2816
PAST_LEN = 16384
ROPE_BASE = 10000.0
LN_EPS = 1e-5
RMS_EPS = 1e-6
ALPHA = 2.0 ** 0.25
TILE = 128
SAMPLE_GROUP = 8
SAMPLE_TOKENS = 4
HALO = 8
VMEM_LIMIT = 56 * 1024 * 1024


def _cparams(sem):
    return pltpu.CompilerParams(dimension_semantics=sem, vmem_limit_bytes=VMEM_LIMIT)


def _layer_norm(x, g, b):
    mu = jnp.mean(x, -1, keepdims=True)
    xc = x - mu
    var = jnp.mean(xc * xc, -1, keepdims=True)
    return xc * lax.rsqrt(var + LN_EPS) * g + b


def _sigmoid(x):
    return 1.0 / (1.0 + jnp.exp(-x))


def _silu(x):
    return x * _sigmoid(x)


def _softplus(x):
    return jnp.maximum(x, 0.0) + jnp.log(1.0 + jnp.exp(-jnp.abs(x)))


def _mm(a, b):
    return jnp.dot(a.astype(BF16), b.astype(BF16), preferred_element_type=F32)


def _mm_nt(a, b):
    return lax.dot_general(a.astype(BF16), b.astype(BF16), (((1,), (1,)), ((), ())),
                           preferred_element_type=F32)


def _mm_split3(m01, x):
    x1 = x.astype(BF16)
    r1 = x - x1.astype(F32)
    x2 = r1.astype(BF16)
    x3 = (r1 - x2.astype(F32)).astype(BF16)
    dot = functools.partial(jnp.dot, preferred_element_type=F32)
    return (dot(m01, x3) + dot(m01, x2)) + dot(m01, x1)


def _lane_bcast(x, lane):
    return jnp.broadcast_to(x[:, lane:lane + 1], (x.shape[0], HEAD_DIM))


def _front_body(x_ref, g_ref, b_ref, wr_ref, wg_ref, wab_ref, pr_ref, pg_ref, pab_ref):
    h = _layer_norm(x_ref[...], g_ref[...], b_ref[...]).astype(BF16)
    pr_ref[...] = jnp.dot(h, wr_ref[...], preferred_element_type=F32)
    pg_ref[...] = jnp.dot(h, wg_ref[...], preferred_element_type=F32)
    pab_ref[...] = jnp.dot(h, wab_ref[...], preferred_element_type=F32)


def _front(x2d, ln_g, ln_b, w_ret, w_gdn, w_ab, tm):
    rows = x2d.shape[0]
    const = lambda i: (0, 0)
    tile = lambda i: (i, 0)
    return pl.pallas_call(
        _front_body,
        grid=(rows // tm,),
        in_specs=[
            pl.BlockSpec((tm, D_MODEL), tile),
            pl.BlockSpec((1, D_MODEL), const),
            pl.BlockSpec((1, D_MODEL), const),
            pl.BlockSpec((D_MODEL, 4 * GROUP_W), const),
            pl.BlockSpec((D_MODEL, 4 * GROUP_W), const),
            pl.BlockSpec((D_MODEL, HEAD_DIM), const),
        ],
        out_specs=[
            pl.BlockSpec((tm, 4 * GROUP_W), tile),
            pl.BlockSpec((tm, 4 * GROUP_W), tile),
            pl.BlockSpec((tm, HEAD_DIM), tile),
        ],
        out_shape=[
            jax.ShapeDtypeStruct((rows, 4 * GROUP_W), F32),
            jax.ShapeDtypeStruct((rows, 4 * GROUP_W), F32),
            jax.ShapeDtypeStruct((rows, HEAD_DIM), F32),
        ],
        compiler_params=_cparams(("parallel",)),
        name="front",
    )(x2d, ln_g, ln_b, w_ret, w_gdn, w_ab)


def _back_body(x_ref, mix_ref, eg_ref, eb_ref, wo_ref, g1_ref, b1_ref, wgu_ref, wd_ref,
               g2_ref, b2_ref, y_ref):
    h = _layer_norm(x_ref[...], eg_ref[...], eb_ref[...])
    mp = jnp.dot(mix_ref[...], wo_ref[...], preferred_element_type=F32)
    h1 = _layer_norm(ALPHA * h + mp, g1_ref[...], b1_ref[...])
    gu = jnp.dot(h1.astype(BF16), wgu_ref[...], preferred_element_type=F32)
    act = (_silu(gu[:, :D_FF]) * gu[:, D_FF:]).astype(BF16)
    ff = jnp.dot(act, wd_ref[...], preferred_element_type=F32)
    y_ref[...] = _layer_norm(ALPHA * h1 + ff, g2_ref[...], b2_ref[...])


def _back(x2d, mix, eg, eb, w_out, g1, b1, w_gu, w_down, g2, b2, tm):
    rows = x2d.shape[0]
    const = lambda i: (0, 0)
    tile = lambda i: (i, 0)
    single = pl.Buffered(1)
    vec = pl.BlockSpec((1, D_MODEL), const)
    return pl.pallas_call(
        _back_body,
        grid=(rows // tm,),
        in_specs=[
            pl.BlockSpec((tm, D_MODEL), tile),
            pl.BlockSpec((tm, D_MODEL), tile),
            vec, vec,
            pl.BlockSpec((D_MODEL, D_MODEL), const, pipeline_mode=single),
            vec, vec,
            pl.BlockSpec((D_MODEL, 2 * D_FF), const, pipeline_mode=single),
            pl.BlockSpec((D_FF, D_MODEL), const, pipeline_mode=single),
            vec, vec,
        ],
        out_specs=pl.BlockSpec((tm, D_MODEL), tile),
        out_shape=jax.ShapeDtypeStruct((rows, D_MODEL), F32),
        compiler_params=_cparams(("parallel",)),
        name="back",
    )(x2d, mix, eg, eb, w_out, g1, b1, w_gu, w_down, g2, b2)


def _iota2(shape, dim):
    return lax.broadcasted_iota(jnp.int32, shape, dim)


def _valid_rows(variant, shape):
    r = _iota2(shape, 0)
    if variant == "meta":
        return r < N_META
    if variant == "sample":
        return (r & (SAMPLE_GROUP - 1)) >= SAMPLE_GROUP - SAMPLE_TOKENS
    return None


def _mask_rows(valid, x):
    return x if valid is None else jnp.where(valid, x, 0.0)


def _tri_inverse(variant, nmat, row, col):
    eye = (row == col).astype(F32)
    if variant == "sample":
        n2 = _mm(nmat, nmat)
        p = eye + nmat
        return p + _mm(p, n2)
    base_log2 = 4
    d1 = jnp.where((row >> base_log2) == (col >> base_log2), nmat, 0.0)
    p = eye + d1
    d2 = _mm(d1, d1)
    p = p + _mm(p, d2)
    d4 = _mm(d2, d2)
    p = p + _mm(p, d4)
    d8 = _mm(d4, d4)
    p = p + _mm(p, d8)
    if variant == "meta":
        return p
    s = base_log2
    while (1 << s) < TILE:
        lower_left = (((row >> (s + 1)) == (col >> (s + 1))) & (((row >> s) & 1) == 1)
                      & (((col >> s) & 1) == 0))
        c = jnp.where(lower_left, nmat, 0.0)
        p = p + _mm(_mm(p, c), p)
        s += 1
    return p


def _prepare(variant, pret, pgdn, pab, cosf, sins, rsc, dmat_ref, cw_ref, alog, dtb, xc_ref,
             convpad):
    shape = (TILE, HEAD_DIM)
    row = _iota2(shape, 0)
    col = _iota2(shape, 1)
    valid = _valid_rows(variant, shape)
    if variant == "sample":
        same = (row >> 3) == (col >> 3)
        incl = same & (row >= col)
        strict = same & (row > col)
    else:
        incl = row >= col
        strict = row > col

    ret = []
    for h in range(N_HEADS):
        sl = lambda base: slice(base + h * HEAD_DIM, base + (h + 1) * HEAD_DIM)
        q = pret[:, sl(0)]
        k = pret[:, sl(GROUP_W)]
        v = pret[:, sl(2 * GROUP_W)]
        q = q * cosf + pltpu.roll(q, HEAD_DIM // 2, 1) * sins
        k = (k * cosf + pltpu.roll(k, HEAD_DIM // 2, 1) * sins) * (HEAD_DIM ** -0.5)
        q, k, v = (_mask_rows(valid, t) for t in (q, k, v))
        scores = _mm_nt(q, k) * dmat_ref[h]
        qd = q * _lane_bcast(rsc, h)
        kd = k * _lane_bcast(rsc, N_HEADS + h)
        ret.append(dict(scores=scores, qd=qd, kdT=kd.T, v=v, gate=pret[:, sl(3 * GROUP_W)]))

    x = pgdn[:, :CONV_COLS]
    if variant == "sample":
        x = jnp.where(_valid_rows(variant, x.shape), x, convpad)
    xc_ref[HALO:HALO + TILE, :] = x
    conv = None
    for w in range(CONV_W):
        off = HALO - (CONV_W - 1) + w
        term = xc_ref[off:off + TILE, :] * cw_ref[w:w + 1, :]
        conv = term if conv is None else conv + term
    conv = _silu(conv)

    g_all = -jnp.exp(alog) * _softplus(pab + dtb)
    beta_all = _sigmoid(pab)
    g_all = _mask_rows(valid, g_all)
    beta_all = _mask_rows(valid, beta_all)
    incl01 = incl.astype(BF16)
    if variant == "sample":
        both = _mm_split3(jnp.concatenate([incl01, same.astype(BF16)], 0), g_all)
        gcum, gseg = both[:TILE], both[TILE:]
    else:
        gcum = _mm_split3(incl01, g_all)
        gseg = jnp.broadcast_to(gcum[TILE - 1:TILE, :], shape)
    gam = jnp.exp(gcum)
    ktail = jnp.exp(gseg - gcum)
    cdr = jnp.exp(gseg)
    gcum_t = gcum.T

    gdn = []
    for h in range(N_HEADS):
        sl = lambda base: slice(base + h * HEAD_DIM, base + (h + 1) * HEAD_DIM)
        q = conv[:, sl(0)]
        k = conv[:, sl(GROUP_W)]
        v = conv[:, sl(2 * GROUP_W)]
        q = q * lax.rsqrt(jnp.sum(q * q, -1, keepdims=True) + RMS_EPS) * (HEAD_DIM ** -0.5)
        k = k * lax.rsqrt(jnp.sum(k * k, -1, keepdims=True) + RMS_EPS)
        q, k, v = (_mask_rows(valid, t) for t in (q, k, v))
        gi = _lane_bcast(gcum, h)
        gj = jnp.broadcast_to(gcum_t[h:h + 1, :], shape)
        dec = jnp.where(incl, jnp.exp(jnp.minimum(gi - gj, 0.0)), 0.0)
        bcol = _lane_bcast(beta_all, N_HEADS + h)
        gamc = _lane_bcast(gam, h)
        kk = _mm_nt(k, k)
        nmat = -(jnp.where(strict, kk * dec, 0.0) * bcol)
        tinv = _tri_inverse(variant, nmat, row, col)
        rhs = jnp.concatenate([v * bcol, k * (bcol * gamc)], 1)
        sol = _mm(tinv, rhs)
        attn = _mm_nt(q, k) * dec
        kt = k * _lane_bcast(ktail, h)
        gdn.append(dict(wv=sol[:, :HEAD_DIM], wk=sol[:, HEAD_DIM:], attn=attn, qg=q * gamc,
                        ktT=kt.T, cdr=cdr, gate=pgdn[:, sl(3 * GROUP_W)]))
    return ret, gdn


def _rms_gate(o, gate, weight=None):
    o = o * lax.rsqrt(jnp.mean(o * o, -1, keepdims=True) + RMS_EPS)
    if weight is not None:
        o = o * weight
    return (o * _silu(gate)).astype(BF16)


def _seq_mixer_body(variant, ret_cd, pr_ref, pg_ref, pab_ref, cos_ref, sin_ref, rsc_ref, dmat_ref,
                    cw_ref, alog_ref, dtb_ref, gnw_ref, s0r_ref, s0g_ref, halo0_ref,
                    mix_ref, sr_out, sg_out, sr_ref, sg_ref, xc_ref):
    n = pl.program_id(1)

    @pl.when(n == 0)
    def _():
        sr_ref[...] = s0r_ref[...]
        sg_ref[...] = s0g_ref[...]
        xc_ref[0:HALO, :] = halo0_ref[...]

    ret, gdn = _prepare(variant, pr_ref[...], pg_ref[...], pab_ref[...], cos_ref[...], sin_ref[...],
                        rsc_ref[...], dmat_ref, cw_ref, alog_ref[...], dtb_ref[...], xc_ref, None)
    xc_ref[0:HALO, :] = xc_ref[TILE:TILE + HALO, :]

    for h in range(N_HEADS):
        r = ret[h]
        s = sr_ref[h]
        o = _mm(r["scores"], r["v"]) + _mm(r["qd"], s)
        sr_ref[h] = ret_cd[h] * s + _mm(r["kdT"], r["v"])
        mix_ref[:, h * HEAD_DIM:(h + 1) * HEAD_DIM] = _rms_gate(o, r["gate"])

    for h in range(N_HEADS):
        d = gdn[h]
        s = sg_ref[h]
        both = _mm(jnp.concatenate([d["wk"], d["qg"]], 0), s)
        u = d["wv"] - both[:TILE]
        o = both[TILE:] + _mm(d["attn"], u)
        cd = jnp.broadcast_to(d["cdr"][0:1, h:h + 1], (HEAD_DIM, HEAD_DIM))
        sg_ref[h] = cd * s + _mm(d["ktT"], u)
        mix_ref[:, GROUP_W + h * HEAD_DIM:GROUP_W + (h + 1) * HEAD_DIM] = _rms_gate(
            o, d["gate"], gnw_ref[...])

    @pl.when(n == pl.num_programs(1) - 1)
    def _():
        sr_out[0] = sr_ref[...]
        sg_out[0] = sg_ref[...]


def _seq_mixer(variant, ret_cd, pr, pg, pab, cosf, sins, rsc, dmat, cw, alog, dtb, gnw, s0r, s0g,
               halo0, n_batch, n_chunks):
    rows = n_batch * n_chunks * TILE
    tile = lambda b, n: (b * n_chunks + n, 0)
    pos = lambda b, n: (n, 0)
    c2 = lambda b, n: (0, 0)
    c3 = lambda b, n: (0, 0, 0)
    state = pl.BlockSpec((1, N_HEADS, HEAD_DIM, HEAD_DIM), lambda b, n: (b, 0, 0, 0))
    return pl.pallas_call(
        functools.partial(_seq_mixer_body, variant, ret_cd),
        grid=(n_batch, n_chunks),
        in_specs=[
            pl.BlockSpec((TILE, 4 * GROUP_W), tile),
            pl.BlockSpec((TILE, 4 * GROUP_W), tile),
            pl.BlockSpec((TILE, HEAD_DIM), tile),
            pl.BlockSpec((TILE, HEAD_DIM), pos),
            pl.BlockSpec((TILE, HEAD_DIM), pos),
            pl.BlockSpec((TILE, HEAD_DIM), c2),
            pl.BlockSpec((N_HEADS, TILE, TILE), c3),
            pl.BlockSpec((CONV_W, CONV_COLS), c2),
            pl.BlockSpec((1, HEAD_DIM), c2),
            pl.BlockSpec((1, HEAD_DIM), c2),
            pl.BlockSpec((1, HEAD_DIM), c2),
            pl.BlockSpec((N_HEADS, HEAD_DIM, HEAD_DIM), c3),
            pl.BlockSpec((N_HEADS, HEAD_DIM, HEAD_DIM), c3),
            pl.BlockSpec((HALO, CONV_COLS), c2),
        ],
        out_specs=[pl.BlockSpec((TILE, 2 * GROUP_W), tile), state, state],
        out_shape=[
            jax.ShapeDtypeStruct((rows, 2 * GROUP_W), BF16),
            jax.ShapeDtypeStruct((n_batch, N_HEADS, HEAD_DIM, HEAD_DIM), F32),
            jax.ShapeDtypeStruct((n_batch, N_HEADS, HEAD_DIM, HEAD_DIM), F32),
        ],
        scratch_shapes=[
            pltpu.VMEM((N_HEADS, HEAD_DIM, HEAD_DIM), F32),
            pltpu.VMEM((N_HEADS, HEAD_DIM, HEAD_DIM), F32),
            pltpu.VMEM((HALO + TILE, CONV_COLS), F32),
        ],
        compiler_params=_cparams(("parallel", "arbitrary")),
        name="mixer_" + variant,
    )(pr, pg, pab, cosf, sins, rsc, dmat, cw, alog, dtb, gnw, s0r, s0g, halo0)


SAMPLE_PER_TILE = TILE // SAMPLE_GROUP


def _stack_by_group(xt):
    shape3 = (SAMPLE_PER_TILE, HEAD_DIM, TILE)
    keep = lax.broadcasted_iota(jnp.int32, shape3, 0) == (lax.broadcasted_iota(jnp.int32, shape3, 2) >> 3)
    stacked = jnp.where(keep, jnp.broadcast_to(xt[None], shape3), 0.0)
    return stacked.reshape(SAMPLE_PER_TILE * HEAD_DIM, TILE)


def _sample_mixer_body(ret_cd, pr_ref, pg_ref, pab_ref, cos_ref, sin_ref, rsc_ref, dmat_ref, cw_ref,
                       alog_ref, dtb_ref, gnw_ref, convpad_ref, sr_in, sg_in,
                       mix_ref, sr_out, sg_out,
                       xc_ref, qd_s, wk_s, qg_s, wv_s, cd_s, inter_s, u_s, qs_s):
    xc_ref[0:HALO, :] = jnp.zeros((HALO, CONV_COLS), F32)
    ret, gdn = _prepare("sample", pr_ref[...], pg_ref[...], pab_ref[...], cos_ref[...], sin_ref[...],
                        rsc_ref[...], dmat_ref, cw_ref, alog_ref[...], dtb_ref[...], xc_ref,
                        convpad_ref[...])
    for h in range(N_HEADS):
        qd_s[h] = ret[h]["qd"]
        wk_s[h] = gdn[h]["wk"]
        qg_s[h] = gdn[h]["qg"]
        wv_s[h] = gdn[h]["wv"]
    cd_s[...] = gdn[0]["cdr"]

    def per_batch(b, carry):
        rows = pl.ds(pl.multiple_of(b * SAMPLE_GROUP, SAMPLE_GROUP), SAMPLE_GROUP)
        cd_rows = cd_s[rows, :]
        for h in range(N_HEADS):
            inter_s[h, rows, :] = _mm(qd_s[h, rows, :], sr_in[b, h])
            s = sg_in[b, h]
            both = _mm(jnp.concatenate([wk_s[h, rows, :], qg_s[h, rows, :]], 0), s)
            u_s[h, rows, :] = wv_s[h, rows, :] - both[:SAMPLE_GROUP]
            qs_s[h, rows, :] = both[SAMPLE_GROUP:]
            cd = jnp.broadcast_to(cd_rows[SAMPLE_GROUP - 1:SAMPLE_GROUP, h:h + 1], (HEAD_DIM, HEAD_DIM))
            sg_out[b, h] = cd * s
        return carry

    lax.fori_loop(0, SAMPLE_PER_TILE, per_batch, 0)

    state_shape = (SAMPLE_PER_TILE, HEAD_DIM, HEAD_DIM)
    for h in range(N_HEADS):
        r = ret[h]
        o = _mm(r["scores"], r["v"]) + inter_s[h]
        upd = _mm(_stack_by_group(r["kdT"]), r["v"]).reshape(state_shape)
        sr_out[:, h] = ret_cd[h] * sr_in[:, h] + upd
        mix_ref[:, h * HEAD_DIM:(h + 1) * HEAD_DIM] = _rms_gate(o, r["gate"])

    for h in range(N_HEADS):
        d = gdn[h]
        u = u_s[h]
        o = qs_s[h] + _mm(d["attn"], u)
        upd = _mm(_stack_by_group(d["ktT"]), u).reshape(state_shape)
        sg_out[:, h] = sg_out[:, h] + upd
        mix_ref[:, GROUP_W + h * HEAD_DIM:GROUP_W + (h + 1) * HEAD_DIM] = _rms_gate(
            o, d["gate"], gnw_ref[...])


def _sample_mixer(ret_cd, pr, pg, pab, cosf, sins, rsc, dmat, cw, alog, dtb, gnw, convpad, sr, sg):
    n_tiles = pr.shape[0] // TILE
    tile = lambda i: (i, 0)
    c2 = lambda i: (0, 0)
    c3 = lambda i: (0, 0, 0)
    state = pl.BlockSpec((SAMPLE_PER_TILE, N_HEADS, HEAD_DIM, HEAD_DIM), lambda i: (i, 0, 0, 0))
    head_scratch = pltpu.VMEM((N_HEADS, TILE, HEAD_DIM), F32)
    return pl.pallas_call(
        functools.partial(_sample_mixer_body, ret_cd),
        grid=(n_tiles,),
        in_specs=[
            pl.BlockSpec((TILE, 4 * GROUP_W), tile),
            pl.BlockSpec((TILE, 4 * GROUP_W), tile),
            pl.BlockSpec((TILE, HEAD_DIM), tile),
            pl.BlockSpec((TILE, HEAD_DIM), c2),
            pl.BlockSpec((TILE, HEAD_DIM), c2),
            pl.BlockSpec((TILE, HEAD_DIM), c2),
            pl.BlockSpec((N_HEADS, TILE, TILE), c3),
            pl.BlockSpec((CONV_W, CONV_COLS), c2),
            pl.BlockSpec((1, HEAD_DIM), c2),
            pl.BlockSpec((1, HEAD_DIM), c2),
            pl.BlockSpec((1, HEAD_DIM), c2),
            pl.BlockSpec((TILE, CONV_COLS), tile),
            state, state,
        ],
        out_specs=[pl.BlockSpec((TILE, 2 * GROUP_W), tile), state, state],
        out_shape=[
            jax.ShapeDtypeStruct((pr.shape[0], 2 * GROUP_W), BF16),
            jax.ShapeDtypeStruct(sr.shape, F32),
            jax.ShapeDtypeStruct(sg.shape, F32),
        ],
        scratch_shapes=[pltpu.VMEM((HALO + TILE, CONV_COLS), F32)] + [head_scratch] * 4
                       + [pltpu.VMEM((TILE, HEAD_DIM), F32)] + [head_scratch] * 3,
        compiler_params=_cparams(("parallel",)),
        name="mixer_sample",
    )(pr, pg, pab, cosf, sins, rsc, dmat, cw, alog, dtb, gnw, convpad, sr, sg)


def _rotary_tables(pos):
    half = HEAD_DIM // 2
    inv = ROPE_BASE ** (-jnp.arange(half, dtype=F32) / half)
    ang = pos.astype(F32)[:, None] * inv[None, :]
    cos, sin = jnp.cos(ang), jnp.sin(ang)
    return jnp.concatenate([cos, cos], -1), jnp.concatenate([-sin, sin], -1)


def _retention_tables(seg, pos, valid, seg_len):
    log_gamma = jnp.log(1.0 - 2.0 ** (-5.0 - jnp.arange(N_HEADS, dtype=F32)))
    posf = pos.astype(F32)
    rel = posf[:, None] - posf[None, :]
    causal = (seg[:, None] == seg[None, :]) & (rel >= 0)
    dmat = jnp.where(causal[None], jnp.exp(log_gamma[:, None, None] * jnp.where(causal, rel, 0.0)[None]), 0.0)
    q_scale = jnp.exp(log_gamma[None, :] * (posf[:, None] + 1.0))
    k_scale = jnp.where(valid[:, None], jnp.exp(log_gamma[None, :] * (seg_len - 1.0 - posf[:, None])), 0.0)
    rsc = jnp.concatenate([q_scale, k_scale, jnp.zeros((TILE, HEAD_DIM - 2 * N_HEADS), F32)], -1)
    chunk_decay = tuple((1.0 - 2.0 ** (-5.0 - h)) ** seg_len for h in range(N_HEADS))
    return dmat, rsc, chunk_decay


def _pad_lanes(v):
    return jnp.pad(v.astype(F32), (0, HEAD_DIM - v.shape[0]))[None, :]


def kernel(x_prompt, x_sample, state_ret, state_gdn, state_conv, meta_tokens, emb_ln_g, emb_ln_b,
           w_in, conv_w, a_log, dt_bias, gdn_norm_w, w_out, ln1_g, ln1_b, w_gate_up, w_down,
           ln2_g, ln2_b):
    n_batch, seq, _ = x_prompt.shape
    dec_batch, dec_seq, _ = x_sample.shape
    assert seq % TILE == 0 and dec_seq == SAMPLE_TOKENS and N_META <= TILE
    n_chunks = seq // TILE
    layer = 0

    w_in_l = w_in[layer]
    w_ret = w_in_l[:, :4 * GROUP_W].astype(BF16)
    w_gdn = w_in_l[:, 4 * GROUP_W:8 * GROUP_W].astype(BF16)
    w_ab = jnp.pad(w_in_l[:, 8 * GROUP_W:], ((0, 0), (0, HEAD_DIM - 2 * N_HEADS))).astype(BF16)
    w_out_b = w_out[layer].astype(BF16)
    w_gu_b = w_gate_up[layer].astype(BF16)
    w_down_b = w_down[layer].astype(BF16)
    row = lambda v: v.astype(F32)[None, :]
    eg, eb = row(emb_ln_g), row(emb_ln_b)
    cw = conv_w[layer].astype(F32)
    alog, dtb, gnw = _pad_lanes(a_log[layer]), _pad_lanes(dt_bias[layer]), row(gdn_norm_w[layer])

    xp = x_prompt.reshape(n_batch * seq, D_MODEL)
    xm = jnp.pad(meta_tokens.astype(F32), ((0, TILE - N_META), (0, 0)))
    xs = jnp.pad(x_sample, ((0, 0), (SAMPLE_GROUP - dec_seq, 0), (0, 0))).reshape(
        dec_batch * SAMPLE_GROUP, D_MODEL)
    convpad = jnp.pad(state_conv[layer].astype(F32), ((0, 0), (1, SAMPLE_GROUP - CONV_W), (0, 0))).reshape(
        dec_batch * SAMPLE_GROUP, CONV_COLS)

    tile_idx = jnp.arange(TILE)
    cos_p, sin_p = _rotary_tables(N_META + jnp.arange(seq))
    cos_m, sin_m = _rotary_tables(tile_idx)
    tok = (tile_idx % SAMPLE_GROUP) - (SAMPLE_GROUP - dec_seq)
    cos_s, sin_s = _rotary_tables(PAST_LEN + jnp.maximum(tok, 0))
    zeros_i = jnp.zeros((TILE,), jnp.int32)
    all_valid = jnp.ones((TILE,), bool)
    dmat_p, rsc_p, cd_p = _retention_tables(zeros_i, tile_idx, all_valid, float(TILE))
    dmat_m, rsc_m, cd_m = _retention_tables(zeros_i, tile_idx, tile_idx < N_META, float(N_META))
    dmat_s, rsc_s, cd_s = _retention_tables(tile_idx // SAMPLE_GROUP, tok, tok >= 0, float(dec_seq))

    pr_p, pg_p, pab_p = _front(xp, eg, eb, w_ret, w_gdn, w_ab, 512)
    pr_m, pg_m, pab_m = _front(xm, eg, eb, w_ret, w_gdn, w_ab, TILE)
    pr_s, pg_s, pab_s = _front(xs, eg, eb, w_ret, w_gdn, w_ab, 512)

    zero_state = jnp.zeros((N_HEADS, HEAD_DIM, HEAD_DIM), F32)
    zero_halo = jnp.zeros((HALO, CONV_COLS), F32)
    _, sr_m, sg_m = _seq_mixer("meta", cd_m, pr_m, pg_m, pab_m, cos_m, sin_m, rsc_m, dmat_m, cw,
                               alog, dtb, gnw, zero_state, zero_state, zero_halo, 1, 1)
    halo_p = pg_m[N_META - HALO:N_META, :CONV_COLS]
    mix_p, sr_p, sg_p = _seq_mixer("prompt", cd_p, pr_p, pg_p, pab_p, cos_p, sin_p, rsc_p, dmat_p, cw,
                                   alog, dtb, gnw, sr_m[0], sg_m[0], halo_p, n_batch, n_chunks)
    mix_s, sr_s, sg_s = _sample_mixer(cd_s, pr_s, pg_s, pab_s, cos_s, sin_s, rsc_s, dmat_s, cw, alog,
                                      dtb, gnw, convpad, state_ret[layer].astype(F32),
                                      state_gdn[layer].astype(F32))

    back = functools.partial(_back, eg=eg, eb=eb, w_out=w_out_b, g1=row(ln1_g[layer]), b1=row(ln1_b[layer]),
                             w_gu=w_gu_b, w_down=w_down_b, g2=row(ln2_g[layer]), b2=row(ln2_b[layer]),
                             tm=256)
    y_p = back(xp, mix_p).reshape(n_batch, seq, D_MODEL)
    y_s = back(xs, mix_s).reshape(dec_batch, SAMPLE_GROUP, D_MODEL)[:, SAMPLE_GROUP - dec_seq:]

    conv_p = pg_p.reshape(n_batch, seq, 4 * GROUP_W)[:, seq - (CONV_W - 1):, :CONV_COLS]
    conv_s = pg_s.reshape(dec_batch, SAMPLE_GROUP, 4 * GROUP_W)[:, SAMPLE_GROUP - (CONV_W - 1):, :CONV_COLS]
    return (y_p, y_s, sr_p[None], sg_p[None], conv_p[None], sr_s[None], sg_s[None], conv_s[None])
```

```python
import functools

import jax
import jax.numpy as jnp
from jax import lax
from jax.experimental import pallas as pl
from jax.experimental.pallas import tpu as pltpu

F32 = jnp.float32
BF16 = jnp.bfloat16

D_MODEL = 1024
N_META = 16
N_HEADS = 4
HEAD_DIM = 128
GROUP_W = N_HEADS * HEAD_DIM
CONV_W = 4
CONV_COLS = 3 * GROUP_W
D_FF = 2816
PAST_LEN = 16384
ROPE_BASE = 10000.0
LN_EPS = 1e-5
RMS_EPS = 1e-6
ALPHA = 2.0 ** 0.25
TILE = 128
SAMPLE_GROUP = 8
SAMPLE_TOKENS = 4
HALO = 8
VMEM_LIMIT = 56 * 1024 * 1024


def _cparams(sem):
    return pltpu.CompilerParams(dimension_semantics=sem, vmem_limit_bytes=VMEM_LIMIT)


def _layer_norm(x, g, b):
    mu = jnp.mean(x, -1, keepdims=True)
    xc = x - mu
    var = jnp.mean(xc * xc, -1, keepdims=True)
    return xc * lax.rsqrt(var + LN_EPS) * g + b


def _sigmoid(x):
    return 1.0 / (1.0 + jnp.exp(-x))


def _silu(x):
    return x * _sigmoid(x)


def _softplus(x):
    return jnp.maximum(x, 0.0) + jnp.log(1.0 + jnp.exp(-jnp.abs(x)))


def _mm(a, b):
    return jnp.dot(a.astype(BF16), b.astype(BF16), preferred_element_type=F32)


def _mm_nt(a, b):
    return lax.dot_general(a.astype(BF16), b.astype(BF16), (((1,), (1,)), ((), ())),
                           preferred_element_type=F32)


def _mm_each(xs, ys):
    return [_mm(x, y) for x, y in zip(xs, ys)]


def _mm_split3(m01, x):
    x1 = x.astype(BF16)
    r1 = x - x1.astype(F32)
    x2 = r1.astype(BF16)
    x3 = (r1 - x2.astype(F32)).astype(BF16)
    dot = functools.partial(jnp.dot, preferred_element_type=F32)
    return (dot(m01, x3) + dot(m01, x2)) + dot(m01, x1)


def _lane_bcast(x, lane):
    return jnp.broadcast_to(x[:, lane:lane + 1], (x.shape[0], HEAD_DIM))


def _head_cols(x, base, h):
    return x[:, base + h * HEAD_DIM:base + (h + 1) * HEAD_DIM]


def _front_body(x_ref, g_ref, b_ref, wr_ref, wg_ref, wab_ref, pr_ref, pg_ref, pab_ref):
    h = _layer_norm(x_ref[...], g_ref[...], b_ref[...]).astype(BF16)
    pr_ref[...] = jnp.dot(h, wr_ref[...], preferred_element_type=F32)
    pg_ref[...] = jnp.dot(h, wg_ref[...], preferred_element_type=F32)
    pab_ref[...] = jnp.dot(h, wab_ref[...], preferred_element_type=F32)


def _front(x2d, ln_g, ln_b, w_ret, w_gdn, w_ab, tm):
    rows = x2d.shape[0]
    const = lambda i: (0, 0)
    tile = lambda i: (i, 0)
    return pl.pallas_call(
        _front_body,
        grid=(rows // tm,),
        in_specs=[
            pl.BlockSpec((tm, D_MODEL), tile),
            pl.BlockSpec((1, D_MODEL), const),
            pl.BlockSpec((1, D_MODEL), const),
            pl.BlockSpec((D_MODEL, 4 * GROUP_W), const),
            pl.BlockSpec((D_MODEL, 4 * GROUP_W), const),
            pl.BlockSpec((D_MODEL, HEAD_DIM), const),
        ],
        out_specs=[
            pl.BlockSpec((tm, 4 * GROUP_W), tile),
            pl.BlockSpec((tm, 4 * GROUP_W), tile),
            pl.BlockSpec((tm, HEAD_DIM), tile),
        ],
        out_shape=[
            jax.ShapeDtypeStruct((rows, 4 * GROUP_W), F32),
            jax.ShapeDtypeStruct((rows, 4 * GROUP_W), F32),
            jax.ShapeDtypeStruct((rows, HEAD_DIM), F32),
        ],
        compiler_params=_cparams(("parallel",)),
        name="front",
    )(x2d, ln_g, ln_b, w_ret, w_gdn, w_ab)


def _back_body(x_ref, mix_ref, eg_ref, eb_ref, wo_ref, g1_ref, b1_ref, wgu_ref, wd_ref,
               g2_ref, b2_ref, y_ref):
    h = _layer_norm(x_ref[...], eg_ref[...], eb_ref[...])
    mp = jnp.dot(mix_ref[...], wo_ref[...], preferred_element_type=F32)
    h1 = _layer_norm(ALPHA * h + mp, g1_ref[...], b1_ref[...])
    gu = jnp.dot(h1.astype(BF16), wgu_ref[...], preferred_element_type=F32)
    act = (_silu(gu[:, :D_FF]) * gu[:, D_FF:]).astype(BF16)
    ff = jnp.dot(act, wd_ref[...], preferred_element_type=F32)
    y_ref[...] = _layer_norm(ALPHA * h1 + ff, g2_ref[...], b2_ref[...])


def _back(x2d, mix, eg, eb, w_out, g1, b1, w_gu, w_down, g2, b2, tm):
    rows = x2d.shape[0]
    const = lambda i: (0, 0)
    tile = lambda i: (i, 0)
    single = pl.Buffered(1)
    vec = pl.BlockSpec((1, D_MODEL), const)
    return pl.pallas_call(
        _back_body,
        grid=(rows // tm,),
        in_specs=[
            pl.BlockSpec((tm, D_MODEL), tile),
            pl.BlockSpec((tm, D_MODEL), tile),
            vec, vec,
            pl.BlockSpec((D_MODEL, D_MODEL), const, pipeline_mode=single),
            vec, vec,
            pl.BlockSpec((D_MODEL, 2 * D_FF), const, pipeline_mode=single),
            pl.BlockSpec((D_FF, D_MODEL), const, pipeline_mode=single),
            vec, vec,
        ],
        out_specs=pl.BlockSpec((tm, D_MODEL), tile),
        out_shape=jax.ShapeDtypeStruct((rows, D_MODEL), F32),
        compiler_params=_cparams(("parallel",)),
        name="back",
    )(x2d, mix, eg, eb, w_out, g1, b1, w_gu, w_down, g2, b2)


def _iota2(shape, dim):
    return lax.broadcasted_iota(jnp.int32, shape, dim)


def _valid_rows(variant, shape):
    r = _iota2(shape, 0)
    if variant == "meta":
        return r < N_META
    if variant == "sample":
        return (r & (SAMPLE_GROUP - 1)) >= SAMPLE_GROUP - SAMPLE_TOKENS
    return None


def _mask_rows(valid, x):
    return x if valid is None else jnp.where(valid, x, 0.0)


def _tri_inverse(variant, nmats, row, col):
    eye = (row == col).astype(F32)
    if variant == "sample":
        n2 = _mm_each(nmats, nmats)
        ps = [eye + n for n in nmats]
        return [p + t for p, t in zip(ps, _mm_each(ps, n2))]
    base_log2 = 4
    in_block = (row >> base_log2) == (col >> base_log2)
    ds = [jnp.where(in_block, n, 0.0) for n in nmats]
    ps = [eye + d for d in ds]
    for _ in range(base_log2 - 1):
        ds = _mm_each(ds, ds)
        ps = [p + t for p, t in zip(ps, _mm_each(ps, ds))]
    if variant == "meta":
        return ps
    s = base_log2
    while (1 << s) < TILE:
        lower_left = (((row >> (s + 1)) == (col >> (s + 1))) & (((row >> s) & 1) == 1)
                      & (((col >> s) & 1) == 0))
        cs = [jnp.where(lower_left, n, 0.0) for n in nmats]
        ps = [p + t for p, t in zip(ps, _mm_each(_mm_each(ps, cs), ps))]
        s += 1
    return ps


def _prepare(variant, pret, pgdn, pab, cosf, sins, rsc, dmat_ref, cw_ref, alog, dtb, xc_ref,
             convpad):
    shape = (TILE, HEAD_DIM)
    heads = range(N_HEADS)
    row = _iota2(shape, 0)
    col = _iota2(shape, 1)
    valid = _valid_rows(variant, shape)
    if variant == "sample":
        same = (row >> 3) == (col >> 3)
        incl = same & (row >= col)
        strict = same & (row > col)
    else:
        incl = row >= col
        strict = row > col

    rot = lambda t: t * cosf + pltpu.roll(t, HEAD_DIM // 2, 1) * sins
    rq = [_mask_rows(valid, rot(_head_cols(pret, 0, h))) for h in heads]
    rk = [_mask_rows(valid, rot(_head_cols(pret, GROUP_W, h)) * (HEAD_DIM ** -0.5)) for h in heads]
    rv = [_mask_rows(valid, _head_cols(pret, 2 * GROUP_W, h)) for h in heads]
    ret = dict(
        scores=[_mm_nt(rq[h], rk[h]) * dmat_ref[h] for h in heads],
        qd=[rq[h] * _lane_bcast(rsc, h) for h in heads],
        kdT=[(rk[h] * _lane_bcast(rsc, N_HEADS + h)).T for h in heads],
        v=rv,
        gate=[_head_cols(pret, 3 * GROUP_W, h) for h in heads],
    )

    x = pgdn[:, :CONV_COLS]
    if variant == "sample":
        x = jnp.where(_valid_rows(variant, x.shape), x, convpad)
    xc_ref[HALO:HALO + TILE, :] = x
    conv = None
    for w in range(CONV_W):
        off = HALO - (CONV_W - 1) + w
        term = xc_ref[off:off + TILE, :] * cw_ref[w:w + 1, :]
        conv = term if conv is None else conv + term
    conv = _silu(conv)

    g_all = -jnp.exp(alog) * _softplus(pab + dtb)
    beta_all = _sigmoid(pab)
    g_all = _mask_rows(valid, g_all)
    beta_all = _mask_rows(valid, beta_all)
    incl01 = incl.astype(BF16)
    if variant == "sample":
        both = _mm_split3(jnp.concatenate([incl01, same.astype(BF16)], 0), g_all)
        gcum, gseg = both[:TILE], both[TILE:]
    else:
        gcum = _mm_split3(incl01, g_all)
        gseg = jnp.broadcast_to(gcum[TILE - 1:TILE, :], shape)
    gam = jnp.exp(gcum)
    ktail = jnp.exp(gseg - gcum)
    cdr = jnp.exp(gseg)
    gcum_t = gcum.T

    l2 = lambda t: t * lax.rsqrt(jnp.sum(t * t, -1, keepdims=True) + RMS_EPS)
    gq = [_mask_rows(valid, l2(_head_cols(conv, 0, h)) * (HEAD_DIM ** -0.5)) for h in heads]
    gk = [_mask_rows(valid, l2(_head_cols(conv, GROUP_W, h))) for h in heads]
    gv = [_mask_rows(valid, _head_cols(conv, 2 * GROUP_W, h)) for h in heads]
    dec = [jnp.where(incl, jnp.exp(jnp.minimum(
        _lane_bcast(gcum, h) - jnp.broadcast_to(gcum_t[h:h + 1, :], shape), 0.0)), 0.0) for h in heads]
    bcol = [_lane_bcast(beta_all, N_HEADS + h) for h in heads]
    gamc = [_lane_bcast(gam, h) for h in heads]
    kk = [_mm_nt(gk[h], gk[h]) for h in heads]
    qk = [_mm_nt(gq[h], gk[h]) for h in heads]
    nmats = [-(jnp.where(strict, kk[h] * dec[h], 0.0) * bcol[h]) for h in heads]
    tinv = _tri_inverse(variant, nmats, row, col)
    rhs = [jnp.concatenate([gv[h] * bcol[h], gk[h] * (bcol[h] * gamc[h])], 1) for h in heads]
    sol = _mm_each(tinv, rhs)
    gdn = dict(
        wv=[t[:, :HEAD_DIM] for t in sol],
        wk=[t[:, HEAD_DIM:] for t in sol],
        attn=[qk[h] * dec[h] for h in heads],
        qg=[gq[h] * gamc[h] for h in heads],
        ktT=[(gk[h] * _lane_bcast(ktail, h)).T for h in heads],
        cdr=cdr,
        gate=[_head_cols(pgdn, 3 * GROUP_W, h) for h in heads],
    )
    return ret, gdn


def _rms_gate(o, gate, weight=None):
    o = o * lax.rsqrt(jnp.mean(o * o, -1, keepdims=True) + RMS_EPS)
    if weight is not None:
        o = o * weight
    return (o * _silu(gate)).astype(BF16)


def _seq_mixer_body(variant, ret_cd, pr_ref, pg_ref, pab_ref, cos_ref, sin_ref, rsc_ref, dmat_ref,
                    cw_ref, alog_ref, dtb_ref, gnw_ref, s0r_ref, s0g_ref, halo0_ref,
                    mix_ref, sr_out, sg_out, sr_ref, sg_ref, xc_ref):
    n = pl.program_id(1)
    heads = range(N_HEADS)

    @pl.when(n == 0)
    def _():
        sr_ref[...] = s0r_ref[...]
        sg_ref[...] = s0g_ref[...]
        xc_ref[0:HALO, :] = halo0_ref[...]

    ret, gdn = _prepare(variant, pr_ref[...], pg_ref[...], pab_ref[...], cos_ref[...], sin_ref[...],
                        rsc_ref[...], dmat_ref, cw_ref, alog_ref[...], dtb_ref[...], xc_ref, None)
    xc_ref[0:HALO, :] = xc_ref[TILE:TILE + HALO, :]

    s_r = [sr_ref[h] for h in heads]
    s_g = [sg_ref[h] for h in heads]
    both = _mm_each([jnp.concatenate([gdn["wk"][h], gdn["qg"][h]], 0) for h in heads], s_g)
    inter = _mm_each(ret["qd"], s_r)
    intra = _mm_each(ret["scores"], ret["v"])
    u = [gdn["wv"][h] - both[h][:TILE] for h in heads]
    o_g = [both[h][TILE:] + t for h, t in zip(heads, _mm_each(gdn["attn"], u))]
    upd_g = _mm_each(gdn["ktT"], u)
    upd_r = _mm_each(ret["kdT"], ret["v"])
    for h in heads:
        cd = jnp.broadcast_to(gdn["cdr"][0:1, h:h + 1], (HEAD_DIM, HEAD_DIM))
        sg_ref[h] = cd * s_g[h] + upd_g[h]
        sr_ref[h] = ret_cd[h] * s_r[h] + upd_r[h]
    for h in heads:
        mix_ref[:, h * HEAD_DIM:(h + 1) * HEAD_DIM] = _rms_gate(intra[h] + inter[h], ret["gate"][h])
        mix_ref[:, GROUP_W + h * HEAD_DIM:GROUP_W + (h + 1) * HEAD_DIM] = _rms_gate(
            o_g[h], gdn["gate"][h], gnw_ref[...])

    @pl.when(n == pl.num_programs(1) - 1)
    def _():
        sr_out[0] = sr_ref[...]
        sg_out[0] = sg_ref[...]


def _seq_mixer(variant, ret_cd, pr, pg, pab, cosf, sins, rsc, dmat, cw, alog, dtb, gnw, s0r, s0g,
               halo0, n_batch, n_chunks):
    rows = n_batch * n_chunks * TILE
    tile = lambda b, n: (b * n_chunks + n, 0)
    pos = lambda b, n: (n, 0)
    c2 = lambda b, n: (0, 0)
    c3 = lambda b, n: (0, 0, 0)
    state = pl.BlockSpec((1, N_HEADS, HEAD_DIM, HEAD_DIM), lambda b, n: (b, 0, 0, 0))
    return pl.pallas_call(
        functools.partial(_seq_mixer_body, variant, ret_cd),
        grid=(n_batch, n_chunks),
        in_specs=[
            pl.BlockSpec((TILE, 4 * GROUP_W), tile),
            pl.BlockSpec((TILE, 4 * GROUP_W), tile),
            pl.BlockSpec((TILE, HEAD_DIM), tile),
            pl.BlockSpec((TILE, HEAD_DIM), pos),
            pl.BlockSpec((TILE, HEAD_DIM), pos),
            pl.BlockSpec((TILE, HEAD_DIM), c2),
            pl.BlockSpec((N_HEADS, TILE, TILE), c3),
            pl.BlockSpec((CONV_W, CONV_COLS), c2),
            pl.BlockSpec((1, HEAD_DIM), c2),
            pl.BlockSpec((1, HEAD_DIM), c2),
            pl.BlockSpec((1, HEAD_DIM), c2),
            pl.BlockSpec((N_HEADS, HEAD_DIM, HEAD_DIM), c3),
            pl.BlockSpec((N_HEADS, HEAD_DIM, HEAD_DIM), c3),
            pl.BlockSpec((HALO, CONV_COLS), c2),
        ],
        out_specs=[pl.BlockSpec((TILE, 2 * GROUP_W), tile), state, state],
        out_shape=[
            jax.ShapeDtypeStruct((rows, 2 * GROUP_W), BF16),
            jax.ShapeDtypeStruct((n_batch, N_HEADS, HEAD_DIM, HEAD_DIM), F32),
            jax.ShapeDtypeStruct((n_batch, N_HEADS, HEAD_DIM, HEAD_DIM), F32),
        ],
        scratch_shapes=[
            pltpu.VMEM((N_HEADS, HEAD_DIM, HEAD_DIM), F32),
            pltpu.VMEM((N_HEADS, HEAD_DIM, HEAD_DIM), F32),
            pltpu.VMEM((HALO + TILE, CONV_COLS), F32),
        ],
        compiler_params=_cparams(("parallel", "arbitrary")),
        name="mixer_" + variant,
    )(pr, pg, pab, cosf, sins, rsc, dmat, cw, alog, dtb, gnw, s0r, s0g, halo0)


SAMPLE_PER_TILE = TILE // SAMPLE_GROUP


def _stack_by_group(xt):
    shape3 = (SAMPLE_PER_TILE, HEAD_DIM, TILE)
    keep = lax.broadcasted_iota(jnp.int32, shape3, 0) == (lax.broadcasted_iota(jnp.int32, shape3, 2) >> 3)
    stacked = jnp.where(keep, jnp.broadcast_to(xt[None], shape3), 0.0)
    return stacked.reshape(SAMPLE_PER_TILE * HEAD_DIM, TILE)


def _sample_mixer_body(ret_cd, pr_ref, pg_ref, pab_ref, cos_ref, sin_ref, rsc_ref, dmat_ref, cw_ref,
                       alog_ref, dtb_ref, gnw_ref, convpad_ref, sr_in, sg_in,
                       mix_ref, sr_out, sg_out,
                       xc_ref, qd_s, wk_s, qg_s, wv_s, cd_s, inter_s, u_s, qs_s):
    heads = range(N_HEADS)
    xc_ref[0:HALO, :] = jnp.zeros((HALO, CONV_COLS), F32)
    ret, gdn = _prepare("sample", pr_ref[...], pg_ref[...], pab_ref[...], cos_ref[...], sin_ref[...],
                        rsc_ref[...], dmat_ref, cw_ref, alog_ref[...], dtb_ref[...], xc_ref,
                        convpad_ref[...])
    for h in heads:
        qd_s[h] = ret["qd"][h]
        wk_s[h] = gdn["wk"][h]
        qg_s[h] = gdn["qg"][h]
        wv_s[h] = gdn["wv"][h]
    cd_s[...] = gdn["cdr"]

    def per_batch(b, carry):
        rows = pl.ds(pl.multiple_of(b * SAMPLE_GROUP, SAMPLE_GROUP), SAMPLE_GROUP)
        cd_rows = cd_s[rows, :]
        for h in heads:
            inter_s[h, rows, :] = _mm(qd_s[h, rows, :], sr_in[b, h])
            s = sg_in[b, h]
            both = _mm(jnp.concatenate([wk_s[h, rows, :], qg_s[h, rows, :]], 0), s)
            u_s[h, rows, :] = wv_s[h, rows, :] - both[:SAMPLE_GROUP]
            qs_s[h, rows, :] = both[SAMPLE_GROUP:]
            cd = jnp.broadcast_to(cd_rows[SAMPLE_GROUP - 1:SAMPLE_GROUP, h:h + 1], (HEAD_DIM, HEAD_DIM))
            sg_out[b, h] = cd * s
        return carry

    lax.fori_loop(0, SAMPLE_PER_TILE, per_batch, 0)

    state_shape = (SAMPLE_PER_TILE, HEAD_DIM, HEAD_DIM)
    for h in heads:
        o = _mm(ret["scores"][h], ret["v"][h]) + inter_s[h]
        upd = _mm(_stack_by_group(ret["kdT"][h]), ret["v"][h]).reshape(state_shape)
        sr_out[:, h] = ret_cd[h] * sr_in[:, h] + upd
        mix_ref[:, h * HEAD_DIM:(h + 1) * HEAD_DIM] = _rms_gate(o, ret["gate"][h])

    for h in heads:
        u = u_s[h]
        o = qs_s[h] + _mm(gdn["attn"][h], u)
        upd = _mm(_stack_by_group(gdn["ktT"][h]), u).reshape(state_shape)
        sg_out[:, h] = sg_out[:, h] + upd
        mix_ref[:, GROUP_W + h * HEAD_DIM:GROUP_W + (h + 1) * HEAD_DIM] = _rms_gate(
            o, gdn["gate"][h], gnw_ref[...])


def _sample_mixer(ret_cd, pr, pg, pab, cosf, sins, rsc, dmat, cw, alog, dtb, gnw, convpad, sr, sg):
    n_tiles = pr.shape[0] // TILE
    tile = lambda i: (i, 0)
    c2 = lambda i: (0, 0)
    c3 = lambda i: (0, 0, 0)
    state = pl.BlockSpec((SAMPLE_PER_TILE, N_HEADS, HEAD_DIM, HEAD_DIM), lambda i: (i, 0, 0, 0))
    head_scratch = pltpu.VMEM((N_HEADS, TILE, HEAD_DIM), F32)
    return pl.pallas_call(
        functools.partial(_sample_mixer_body, ret_cd),
        grid=(n_tiles,),
        in_specs=[
            pl.BlockSpec((TILE, 4 * GROUP_W), tile),
            pl.BlockSpec((TILE, 4 * GROUP_W), tile),
            pl.BlockSpec((TILE, HEAD_DIM), tile),
            pl.BlockSpec((TILE, HEAD_DIM), c2),
            pl.BlockSpec((TILE, HEAD_DIM), c2),
            pl.BlockSpec((TILE, HEAD_DIM), c2),
            pl.BlockSpec((N_HEADS, TILE, TILE), c3),
            pl.BlockSpec((CONV_W, CONV_COLS), c2),
            pl.BlockSpec((1, HEAD_DIM), c2),
            pl.BlockSpec((1, HEAD_DIM), c2),
            pl.BlockSpec((1, HEAD_DIM), c2),
            pl.BlockSpec((TILE, CONV_COLS), tile),
            state, state,
        ],
        out_specs=[pl.BlockSpec((TILE, 2 * GROUP_W), tile), state, state],
        out_shape=[
            jax.ShapeDtypeStruct((pr.shape[0], 2 * GROUP_W), BF16),
            jax.ShapeDtypeStruct(sr.shape, F32),
            jax.ShapeDtypeStruct(sg.shape, F32),
        ],
        scratch_shapes=[pltpu.VMEM((HALO + TILE, CONV_COLS), F32)] + [head_scratch] * 4
                       + [pltpu.VMEM((TILE, HEAD_DIM), F32)] + [head_scratch] * 3,
        compiler_params=_cparams(("parallel",)),
        name="mixer_sample",
    )(pr, pg, pab, cosf, sins, rsc, dmat, cw, alog, dtb, gnw, convpad, sr, sg)


def _rotary_tables(pos):
    half = HEAD_DIM // 2
    inv = ROPE_BASE ** (-jnp.arange(half, dtype=F32) / half)
    ang = pos.astype(F32)[:, None] * inv[None, :]
    cos, sin = jnp.cos(ang), jnp.sin(ang)
    return jnp.concatenate([cos, cos], -1), jnp.concatenate([-sin, sin], -1)


def _retention_tables(seg, pos, valid, seg_len):
    log_gamma = jnp.log(1.0 - 2.0 ** (-5.0 - jnp.arange(N_HEADS, dtype=F32)))
    posf = pos.astype(F32)
    rel = posf[:, None] - posf[None, :]
    causal = (seg[:, None] == seg[None, :]) & (rel >= 0)
    dmat = jnp.where(causal[None], jnp.exp(log_gamma[:, None, None] * jnp.where(causal, rel, 0.0)[None]), 0.0)
    q_scale = jnp.exp(log_gamma[None, :] * (posf[:, None] + 1.0))
    k_scale = jnp.where(valid[:, None], jnp.exp(log_gamma[None, :] * (seg_len - 1.0 - posf[:, None])), 0.0)
    rsc = jnp.concatenate([q_scale, k_scale, jnp.zeros((TILE, HEAD_DIM - 2 * N_HEADS), F32)], -1)
    chunk_decay = tuple((1.0 - 2.0 ** (-5.0 - h)) ** seg_len for h in range(N_HEADS))
    return dmat, rsc, chunk_decay


def _pad_lanes(v):
    return jnp.pad(v.astype(F32), (0, HEAD_DIM - v.shape[0]))[None, :]


def kernel(x_prompt, x_sample, state_ret, state_gdn, state_conv, meta_tokens, emb_ln_g, emb_ln_b,
           w_in, conv_w, a_log, dt_bias, gdn_norm_w, w_out, ln1_g, ln1_b, w_gate_up, w_down,
           ln2_g, ln2_b):
    n_batch, seq, _ = x_prompt.shape
    dec_batch, dec_seq, _ = x_sample.shape
    assert seq % TILE == 0 and dec_seq == SAMPLE_TOKENS and N_META <= TILE
    n_chunks = seq // TILE
    layer = 0

    w_in_l = w_in[layer]
    w_ret = w_in_l[:, :4 * GROUP_W].astype(BF16)
    w_gdn = w_in_l[:, 4 * GROUP_W:8 * GROUP_W].astype(BF16)
    w_ab = jnp.pad(w_in_l[:, 8 * GROUP_W:], ((0, 0), (0, HEAD_DIM - 2 * N_HEADS))).astype(BF16)
    w_out_b = w_out[layer].astype(BF16)
    w_gu_b = w_gate_up[layer].astype(BF16)
    w_down_b = w_down[layer].astype(BF16)
    row = lambda v: v.astype(F32)[None, :]
    eg, eb = row(emb_ln_g), row(emb_ln_b)
    cw = conv_w[layer].astype(F32)
    alog, dtb, gnw = _pad_lanes(a_log[layer]), _pad_lanes(dt_bias[layer]), row(gdn_norm_w[layer])

    xp = x_prompt.reshape(n_batch * seq, D_MODEL)
    xm = jnp.pad(meta_tokens.astype(F32), ((0, TILE - N_META), (0, 0)))
    xs = jnp.pad(x_sample, ((0, 0), (SAMPLE_GROUP - dec_seq, 0), (0, 0))).reshape(
        dec_batch * SAMPLE_GROUP, D_MODEL)
    convpad = jnp.pad(state_conv[layer].astype(F32), ((0, 0), (1, SAMPLE_GROUP - CONV_W), (0, 0))).reshape(
        dec_batch * SAMPLE_GROUP, CONV_COLS)

    tile_idx = jnp.arange(TILE)
    cos_p, sin_p = _rotary_tables(N_META + jnp.arange(seq))
    cos_m, sin_m = _rotary_tables(tile_idx)
    tok = (tile_idx % SAMPLE_GROUP) - (SAMPLE_GROUP - dec_seq)
    cos_s, sin_s = _rotary_tables(PAST_LEN + jnp.maximum(tok, 0))
    zeros_i = jnp.zeros((TILE,), jnp.int32)
    all_valid = jnp.ones((TILE,), bool)
    dmat_p, rsc_p, cd_p = _retention_tables(zeros_i, tile_idx, all_valid, float(TILE))
    dmat_m, rsc_m, cd_m = _retention_tables(zeros_i, tile_idx, tile_idx < N_META, float(N_META))
    dmat_s, rsc_s, cd_s = _retention_tables(tile_idx // SAMPLE_GROUP, tok, tok >= 0, float(dec_seq))

    pr_p, pg_p, pab_p = _front(xp, eg, eb, w_ret, w_gdn, w_ab, 512)
    pr_m, pg_m, pab_m = _front(xm, eg, eb, w_ret, w_gdn, w_ab, TILE)
    pr_s, pg_s, pab_s = _front(xs, eg, eb, w_ret, w_gdn, w_ab, 512)

    zero_state = jnp.zeros((N_HEADS, HEAD_DIM, HEAD_DIM), F32)
    zero_halo = jnp.zeros((HALO, CONV_COLS), F32)
    _, sr_m, sg_m = _seq_mixer("meta", cd_m, pr_m, pg_m, pab_m, cos_m, sin_m, rsc_m, dmat_m, cw,
                               alog, dtb, gnw, zero_state, zero_state, zero_halo, 1, 1)
    halo_p = pg_m[N_META - HALO:N_META, :CONV_COLS]
    mix_p, sr_p, sg_p = _seq_mixer("prompt", cd_p, pr_p, pg_p, pab_p, cos_p, sin_p, rsc_p, dmat_p, cw,
                                   alog, dtb, gnw, sr_m[0], sg_m[0], halo_p, n_batch, n_chunks)
    mix_s, sr_s, sg_s = _sample_mixer(cd_s, pr_s, pg_s, pab_s, cos_s, sin_s, rsc_s, dmat_s, cw, alog,
                                      dtb, gnw, convpad, state_ret[layer].astype(F32),
                                      state_gdn[layer].astype(F32))

    back = functools.partial(_back, eg=eg, eb=eb, w_out=w_out_b, g1=row(ln1_g[layer]), b1=row(ln1_b[layer]),
                             w_gu=w_gu_b, w_down=w_down_b, g2=row(ln2_g[layer]), b2=row(ln2_b[layer]),
                             tm=256)
    y_p = back(xp, mix_p).reshape(n_batch, seq, D_MODEL)
    y_s = back(xs, mix_s).reshape(dec_batch, SAMPLE_GROUP, D_MODEL)[:, SAMPLE_GROUP - dec_seq:]

    conv_p = pg_p.reshape(n_batch, seq, 4 * GROUP_W)[:, seq - (CONV_W - 1):, :CONV_COLS]
    conv_s = pg_s.reshape(dec_batch, SAMPLE_GROUP, 4 * GROUP_W)[:, SAMPLE_GROUP - (CONV_W - 1):, :CONV_COLS]
    return (y_p, y_s, sr_p[None], sg_p[None], conv_p[None], sr_s[None], sg_s[None], conv_s[None])
```

```python
import functools

import jax
import jax.numpy as jnp
from jax import lax
from jax.experimental import pallas as pl
from jax.experimental.pallas import tpu as pltpu

F32 = jnp.float32
BF16 = jnp.bfloat16

D_MODEL = 1024
N_META = 16
N_HEADS = 4
HEAD_DIM = 128
GROUP_W = N_HEADS * HEAD_DIM
CONV_W = 4
CONV_COLS = 3 * GROUP_W
D_FF = 2816
PAST_LEN = 16384
ROPE_BASE = 10000.0
LN_EPS = 1e-5
RMS_EPS = 1e-6
ALPHA = 2.0 ** 0.25
TILE = 128
SAMPLE_GROUP = 8
SAMPLE_TOKENS = 4
HALO = 8
VMEM_LIMIT = 56 * 1024 * 1024


def _cparams(sem):
    return pltpu.CompilerParams(dimension_semantics=sem, vmem_limit_bytes=VMEM_LIMIT)


def _layer_norm(x, g, b):
    mu = jnp.mean(x, -1, keepdims=True)
    xc = x - mu
    var = jnp.mean(xc * xc, -1, keepdims=True)
    return xc * lax.rsqrt(var + LN_EPS) * g + b


def _sigmoid(x):
    return 1.0 / (1.0 + jnp.exp(-x))


def _silu(x):
    return x * _sigmoid(x)


def _softplus(x):
    return jnp.maximum(x, 0.0) + jnp.log(1.0 + jnp.exp(-jnp.abs(x)))


def _mm(a, b):
    return jnp.dot(a.astype(BF16), b.astype(BF16), preferred_element_type=F32)


def _mm_nt(a, b):
    return lax.dot_general(a.astype(BF16), b.astype(BF16), (((1,), (1,)), ((), ())),
                           preferred_element_type=F32)


def _mm_each(xs, ys):
    return [_mm(x, y) for x, y in zip(xs, ys)]


def _mm_split3(m01, x):
    x1 = x.astype(BF16)
    r1 = x - x1.astype(F32)
    x2 = r1.astype(BF16)
    x3 = (r1 - x2.astype(F32)).astype(BF16)
    dot = functools.partial(jnp.dot, preferred_element_type=F32)
    return (dot(m01, x3) + dot(m01, x2)) + dot(m01, x1)


def _lane_bcast(x, lane):
    return jnp.broadcast_to(x[:, lane:lane + 1], (x.shape[0], HEAD_DIM))


def _head_cols(x, base, h):
    return x[:, base + h * HEAD_DIM:base + (h + 1) * HEAD_DIM]


def _front_body(x_ref, g_ref, b_ref, wr_ref, wg_ref, wab_ref, pr_ref, pg_ref, pab_ref):
    h = _layer_norm(x_ref[...], g_ref[...], b_ref[...]).astype(BF16)
    pr_ref[...] = jnp.dot(h, wr_ref[...], preferred_element_type=F32)
    pg_ref[...] = jnp.dot(h, wg_ref[...], preferred_element_type=F32)
    pab_ref[...] = jnp.dot(h, wab_ref[...], preferred_element_type=F32)


def _front(x2d, ln_g, ln_b, w_ret, w_gdn, w_ab, tm):
    rows = x2d.shape[0]
    const = lambda i: (0, 0)
    tile = lambda i: (i, 0)
    return pl.pallas_call(
        _front_body,
        grid=(rows // tm,),
        in_specs=[
            pl.BlockSpec((tm, D_MODEL), tile),
            pl.BlockSpec((1, D_MODEL), const),
            pl.BlockSpec((1, D_MODEL), const),
            pl.BlockSpec((D_MODEL, 4 * GROUP_W), const),
            pl.BlockSpec((D_MODEL, 4 * GROUP_W), const),
            pl.BlockSpec((D_MODEL, HEAD_DIM), const),
        ],
        out_specs=[
            pl.BlockSpec((tm, 4 * GROUP_W), tile),
            pl.BlockSpec((tm, 4 * GROUP_W), tile),
            pl.BlockSpec((tm, HEAD_DIM), tile),
        ],
        out_shape=[
            jax.ShapeDtypeStruct((rows, 4 * GROUP_W), F32),
            jax.ShapeDtypeStruct((rows, 4 * GROUP_W), F32),
            jax.ShapeDtypeStruct((rows, HEAD_DIM), F32),
        ],
        compiler_params=_cparams(("parallel",)),
        name="front",
    )(x2d, ln_g, ln_b, w_ret, w_gdn, w_ab)


def _back_body(x_ref, mix_ref, eg_ref, eb_ref, wo_ref, g1_ref, b1_ref, wgu_ref, wd_ref,
               g2_ref, b2_ref, y_ref):
    h = _layer_norm(x_ref[...], eg_ref[...], eb_ref[...])
    mp = jnp.dot(mix_ref[...], wo_ref[...], preferred_element_type=F32)
    h1 = _layer_norm(ALPHA * h + mp, g1_ref[...], b1_ref[...])
    gu = jnp.dot(h1.astype(BF16), wgu_ref[...], preferred_element_type=F32)
    act = (_silu(gu[:, :D_FF]) * gu[:, D_FF:]).astype(BF16)
    ff = jnp.dot(act, wd_ref[...], preferred_element_type=F32)
    y_ref[...] = _layer_norm(ALPHA * h1 + ff, g2_ref[...], b2_ref[...])


def _back(x2d, mix, eg, eb, w_out, g1, b1, w_gu, w_down, g2, b2, tm):
    rows = x2d.shape[0]
    const = lambda i: (0, 0)
    tile = lambda i: (i, 0)
    single = pl.Buffered(1)
    vec = pl.BlockSpec((1, D_MODEL), const)
    return pl.pallas_call(
        _back_body,
        grid=(rows // tm,),
        in_specs=[
            pl.BlockSpec((tm, D_MODEL), tile),
            pl.BlockSpec((tm, D_MODEL), tile),
            vec, vec,
            pl.BlockSpec((D_MODEL, D_MODEL), const, pipeline_mode=single),
            vec, vec,
            pl.BlockSpec((D_MODEL, 2 * D_FF), const, pipeline_mode=single),
            pl.BlockSpec((D_FF, D_MODEL), const, pipeline_mode=single),
            vec, vec,
        ],
        out_specs=pl.BlockSpec((tm, D_MODEL), tile),
        out_shape=jax.ShapeDtypeStruct((rows, D_MODEL), F32),
        compiler_params=_cparams(("parallel",)),
        name="back",
    )(x2d, mix, eg, eb, w_out, g1, b1, w_gu, w_down, g2, b2)


def _iota2(shape, dim):
    return lax.broadcasted_iota(jnp.int32, shape, dim)


def _valid_rows(variant, shape):
    r = _iota2(shape, 0)
    if variant == "meta":
        return r < N_META
    if variant == "sample":
        return (r & (SAMPLE_GROUP - 1)) >= SAMPLE_GROUP - SAMPLE_TOKENS
    return None


def _mask_rows(valid, x):
    return x if valid is None else jnp.where(valid, x, 0.0)


def _tri_inverse(variant, nmats, row, col):
    eye = (row == col).astype(F32)
    if variant == "sample":
        n2 = _mm_each(nmats, nmats)
        ps = [eye + n for n in nmats]
        return [p + t for p, t in zip(ps, _mm_each(ps, n2))]
    base_log2 = 4
    in_block = (row >> base_log2) == (col >> base_log2)
    ds = [jnp.where(in_block, n, 0.0) for n in nmats]
    ps = [eye + d for d in ds]
    for _ in range(base_log2 - 1):
        ds = _mm_each(ds, ds)
        ps = [p + t for p, t in zip(ps, _mm_each(ps, ds))]
    if variant == "meta":
        return ps
    s = base_log2
    while (1 << s) < TILE:
        lower_left = (((row >> (s + 1)) == (col >> (s + 1))) & (((row >> s) & 1) == 1)
                      & (((col >> s) & 1) == 0))
        cs = [jnp.where(lower_left, n, 0.0) for n in nmats]
        ps = [p + t for p, t in zip(ps, _mm_each(_mm_each(ps, cs), ps))]
        s += 1
    return ps


def _tokenwise(variant, pret, pgdn, cosf, sins, rsc, cw_ref, xc_ref, convpad):
    rows = pret.shape[0]
    shape = (rows, HEAD_DIM)
    heads = range(N_HEADS)
    valid = _valid_rows(variant, shape)

    rot = lambda t: t * cosf + pltpu.roll(t, HEAD_DIM // 2, 1) * sins
    rq = [_mask_rows(valid, rot(_head_cols(pret, 0, h))) for h in heads]
    rk = [_mask_rows(valid, rot(_head_cols(pret, GROUP_W, h)) * (HEAD_DIM ** -0.5)) for h in heads]
    rv = [_mask_rows(valid, _head_cols(pret, 2 * GROUP_W, h)) for h in heads]
    qd = [rq[h] * _lane_bcast(rsc, h) for h in heads]
    kd = [rk[h] * _lane_bcast(rsc, N_HEADS + h) for h in heads]
    sgr = [_silu(_head_cols(pret, 3 * GROUP_W, h)) for h in heads]

    x = pgdn[:, :CONV_COLS]
    if variant == "sample":
        x = jnp.where(_valid_rows(variant, x.shape), x, convpad)
    xc_ref[HALO:HALO + rows, :] = x
    conv = None
    for w in range(CONV_W):
        off = HALO - (CONV_W - 1) + w
        term = xc_ref[off:off + rows, :] * cw_ref[w:w + 1, :]
        conv = term if conv is None else conv + term
    conv = _silu(conv)
    l2 = lambda t: t * lax.rsqrt(jnp.sum(t * t, -1, keepdims=True) + RMS_EPS)
    gq = [_mask_rows(valid, l2(_head_cols(conv, 0, h)) * (HEAD_DIM ** -0.5)) for h in heads]
    gk = [_mask_rows(valid, l2(_head_cols(conv, GROUP_W, h))) for h in heads]
    gv = [_mask_rows(valid, _head_cols(conv, 2 * GROUP_W, h)) for h in heads]
    sgz = [_silu(_head_cols(pgdn, 3 * GROUP_W, h)) for h in heads]
    return dict(rq=rq, rk=rk, rv=rv, qd=qd, kd=kd, sgr=sgr, gq=gq, gk=gk, gv=gv, sgz=sgz)


def _chunk_prep(variant, tok, pab, dmat_ref, alog, dtb):
    shape = (TILE, HEAD_DIM)
    heads = range(N_HEADS)
    row = _iota2(shape, 0)
    col = _iota2(shape, 1)
    valid = _valid_rows(variant, shape)
    if variant == "sample":
        same = (row >> 3) == (col >> 3)
        incl = same & (row >= col)
        strict = same & (row > col)
    else:
        incl = row >= col
        strict = row > col

    ret = dict(
        scores=[_mm_nt(tok["rq"][h], tok["rk"][h]) * dmat_ref[h] for h in heads],
        qd=tok["qd"], kdT=tok["kdT"], v=tok["rv"], gate=tok["sgr"],
    )

    g_all = -jnp.exp(alog) * _softplus(pab + dtb)
    beta_all = _sigmoid(pab)
    g_all = _mask_rows(valid, g_all)
    beta_all = _mask_rows(valid, beta_all)
    incl01 = incl.astype(BF16)
    if variant == "sample":
        both = _mm_split3(jnp.concatenate([incl01, same.astype(BF16)], 0), g_all)
        gcum, gseg = both[:TILE], both[TILE:]
    else:
        gcum = _mm_split3(incl01, g_all)
        gseg = jnp.broadcast_to(gcum[TILE - 1:TILE, :], shape)
    gam = jnp.exp(gcum)
    ktail = jnp.exp(gseg - gcum)
    cdr = jnp.exp(gseg)
    gcum_t = gcum.T

    gq, gk, gv = tok["gq"], tok["gk"], tok["gv"]
    dec = [jnp.where(incl, jnp.exp(jnp.minimum(
        _lane_bcast(gcum, h) - jnp.broadcast_to(gcum_t[h:h + 1, :], shape), 0.0)), 0.0) for h in heads]
    bcol = [_lane_bcast(beta_all, N_HEADS + h) for h in heads]
    gamc = [_lane_bcast(gam, h) for h in heads]
    kk = [_mm_nt(gk[h], gk[h]) for h in heads]
    qk = [_mm_nt(gq[h], gk[h]) for h in heads]
    nmats = [-(jnp.where(strict, kk[h] * dec[h], 0.0) * bcol[h]) for h in heads]
    tinv = _tri_inverse(variant, nmats, row, col)
    rhs = [jnp.concatenate([gv[h] * bcol[h], gk[h] * (bcol[h] * gamc[h])], 1) for h in heads]
    sol = _mm_each(tinv, rhs)
    gdn = dict(
        wv=[t[:, :HEAD_DIM] for t in sol],
        wk=[t[:, HEAD_DIM:] for t in sol],
        attn=[qk[h] * dec[h] for h in heads],
        qg=[gq[h] * gamc[h] for h in heads],
        ktT=[(gk[h] * _lane_bcast(ktail, h)).T for h in heads],
        cdr=cdr,
        gate=tok["sgz"],
    )
    return ret, gdn


def _rms_gate(o, gate, weight=None):
    o = o * lax.rsqrt(jnp.mean(o * o, -1, keepdims=True) + RMS_EPS)
    if weight is not None:
        o = o * weight
    return (o * gate).astype(BF16)


A_COLS = 5 * GROUP_W
B_COLS = 5 * GROUP_W
_STATE_SHAPE = (N_HEADS, HEAD_DIM, HEAD_DIM)


def _front_prompt_body(tm, x_ref, g_ref, b_ref, wr_ref, wg_ref, wab_ref, cos_ref, sin_ref, rsc_ref,
                       cw_ref, halo0_ref, a_ref, b_out_ref, pab_ref, tail_ref, xc_ref):
    @pl.when(pl.program_id(1) == 0)
    def _():
        xc_ref[0:HALO, :] = halo0_ref[...]

    h = _layer_norm(x_ref[...], g_ref[...], b_ref[...]).astype(BF16)
    pg = jnp.dot(h, wg_ref[...], preferred_element_type=F32)
    pr = jnp.dot(h, wr_ref[...], preferred_element_type=F32)
    pab_ref[...] = jnp.dot(h, wab_ref[...], preferred_element_type=F32)
    tok = _tokenwise("prompt", pr, pg, cos_ref[...], sin_ref[...], rsc_ref[...], cw_ref, xc_ref, None)
    tail = xc_ref[tm:tm + HALO, :]
    xc_ref[0:HALO, :] = tail
    tail_ref[0] = tail

    def put(ref, group, h, val):
        ref[:, group * GROUP_W + h * HEAD_DIM:group * GROUP_W + (h + 1) * HEAD_DIM] = val.astype(ref.dtype)

    for h in range(N_HEADS):
        for group, key in enumerate(("rq", "rk", "rv", "qd")):
            put(a_ref, group, h, tok[key][h])
        for c in range(tm // TILE):
            a_ref[c * TILE:(c + 1) * TILE, 4 * GROUP_W + h * HEAD_DIM:4 * GROUP_W + (h + 1) * HEAD_DIM] = (
                tok["kd"][h][c * TILE:(c + 1) * TILE, :].T.astype(BF16))
        for group, key in enumerate(("gq", "gk", "gv", "sgz", "sgr")):
            put(b_out_ref, group, h, tok[key][h])


def _front_prompt(x2d, ln_g, ln_b, w_ret, w_gdn, w_ab, cosf, sins, rsc, cw, halo0, n_batch, tm):
    rows = x2d.shape[0]
    tiles = rows // (n_batch * tm)
    c2 = lambda b, t: (0, 0)
    tile = lambda b, t: (b * tiles + t, 0)
    pos = lambda b, t: (t, 0)
    single = pl.Buffered(1)
    return pl.pallas_call(
        functools.partial(_front_prompt_body, tm),
        grid=(n_batch, tiles),
        in_specs=[
            pl.BlockSpec((tm, D_MODEL), tile),
            pl.BlockSpec((1, D_MODEL), c2),
            pl.BlockSpec((1, D_MODEL), c2),
            pl.BlockSpec((D_MODEL, 4 * GROUP_W), c2, pipeline_mode=single),
            pl.BlockSpec((D_MODEL, 4 * GROUP_W), c2, pipeline_mode=single),
            pl.BlockSpec((D_MODEL, HEAD_DIM), c2, pipeline_mode=single),
            pl.BlockSpec((tm, HEAD_DIM), pos),
            pl.BlockSpec((tm, HEAD_DIM), pos),
            pl.BlockSpec((tm, HEAD_DIM), c2),
            pl.BlockSpec((CONV_W, CONV_COLS), c2),
            pl.BlockSpec((HALO, CONV_COLS), c2),
        ],
        out_specs=[
            pl.BlockSpec((tm, A_COLS), tile),
            pl.BlockSpec((tm, B_COLS), tile),
            pl.BlockSpec((tm, HEAD_DIM), tile),
            pl.BlockSpec((1, HALO, CONV_COLS), lambda b, t: (b, 0, 0)),
        ],
        out_shape=[
            jax.ShapeDtypeStruct((rows, A_COLS), BF16),
            jax.ShapeDtypeStruct((rows, B_COLS), F32),
            jax.ShapeDtypeStruct((rows, HEAD_DIM), F32),
            jax.ShapeDtypeStruct((n_batch, HALO, CONV_COLS), F32),
        ],
        scratch_shapes=[pltpu.VMEM((HALO + tm, CONV_COLS), F32)],
        compiler_params=_cparams(("parallel", "arbitrary")),
        name="front_prompt",
    )(x2d, ln_g, ln_b, w_ret, w_gdn, w_ab, cosf, sins, rsc, cw, halo0)


def _seq_init(s0r_ref, s0g_ref, sr_ref, sg_ref):
    @pl.when(pl.program_id(1) == 0)
    def _():
        sr_ref[...] = s0r_ref[...]
        sg_ref[...] = s0g_ref[...]


def _seq_step(ret_cd, ret, gdn, gnw, mix_ref, sr_out, sg_out, sr_ref, sg_ref):
    heads = range(N_HEADS)
    s_r = [sr_ref[h] for h in heads]
    s_g = [sg_ref[h] for h in heads]
    both = _mm_each([jnp.concatenate([gdn["wk"][h], gdn["qg"][h]], 0) for h in heads], s_g)
    inter = _mm_each(ret["qd"], s_r)
    intra = _mm_each(ret["scores"], ret["v"])
    u = [gdn["wv"][h] - both[h][:TILE] for h in heads]
    o_g = [both[h][TILE:] + t for h, t in zip(heads, _mm_each(gdn["attn"], u))]
    upd_g = _mm_each(gdn["ktT"], u)
    upd_r = _mm_each(ret["kdT"], ret["v"])
    for h in heads:
        cd = jnp.broadcast_to(gdn["cdr"][0:1, h:h + 1], (HEAD_DIM, HEAD_DIM))
        sg_ref[h] = cd * s_g[h] + upd_g[h]
        sr_ref[h] = ret_cd[h] * s_r[h] + upd_r[h]
    for h in heads:
        mix_ref[:, h * HEAD_DIM:(h + 1) * HEAD_DIM] = _rms_gate(intra[h] + inter[h], ret["gate"][h])
        mix_ref[:, GROUP_W + h * HEAD_DIM:GROUP_W + (h + 1) * HEAD_DIM] = _rms_gate(
            o_g[h], gdn["gate"][h], gnw)

    @pl.when(pl.program_id(1) == pl.num_programs(1) - 1)
    def _():
        sr_out[0] = sr_ref[...]
        sg_out[0] = sg_ref[...]


def _meta_mixer_body(ret_cd, pr_ref, pg_ref, pab_ref, cos_ref, sin_ref, rsc_ref, dmat_ref,
                     cw_ref, alog_ref, dtb_ref, gnw_ref, s0r_ref, s0g_ref,
                     mix_ref, sr_out, sg_out, sr_ref, sg_ref, xc_ref):
    _seq_init(s0r_ref, s0g_ref, sr_ref, sg_ref)
    xc_ref[0:HALO, :] = jnp.zeros((HALO, CONV_COLS), F32)
    tok = _tokenwise("meta", pr_ref[...], pg_ref[...], cos_ref[...], sin_ref[...], rsc_ref[...],
                     cw_ref, xc_ref, None)
    tok["kdT"] = [t.T for t in tok["kd"]]
    ret, gdn = _chunk_prep("meta", tok, pab_ref[...], dmat_ref, alog_ref[...], dtb_ref[...])
    _seq_step(ret_cd, ret, gdn, gnw_ref[...], mix_ref, sr_out, sg_out, sr_ref, sg_ref)


def _prompt_mixer_body(ret_cd, a_ref, b_ref, pab_ref, dmat_ref, alog_ref, dtb_ref, gnw_ref,
                       s0r_ref, s0g_ref, mix_ref, sr_out, sg_out, sr_ref, sg_ref):
    _seq_init(s0r_ref, s0g_ref, sr_ref, sg_ref)
    heads = range(N_HEADS)
    piece = lambda ref, group: [ref[:, group * GROUP_W + h * HEAD_DIM:group * GROUP_W + (h + 1) * HEAD_DIM]
                                for h in heads]
    tok = dict(rq=piece(a_ref, 0), rk=piece(a_ref, 1), rv=piece(a_ref, 2), qd=piece(a_ref, 3),
               kdT=piece(a_ref, 4), gq=piece(b_ref, 0), gk=piece(b_ref, 1), gv=piece(b_ref, 2),
               sgz=piece(b_ref, 3), sgr=piece(b_ref, 4))
    ret, gdn = _chunk_prep("prompt", tok, pab_ref[...], dmat_ref, alog_ref[...], dtb_ref[...])
    _seq_step(ret_cd, ret, gdn, gnw_ref[...], mix_ref, sr_out, sg_out, sr_ref, sg_ref)


def _seq_out(n_batch, rows):
    state = pl.BlockSpec((1,) + _STATE_SHAPE, lambda b, n: (b, 0, 0, 0))
    out_specs = [pl.BlockSpec((TILE, 2 * GROUP_W), lambda b, n: (b * (rows // (n_batch * TILE)) + n, 0)),
                 state, state]
    out_shape = [
        jax.ShapeDtypeStruct((rows, 2 * GROUP_W), BF16),
        jax.ShapeDtypeStruct((n_batch,) + _STATE_SHAPE, F32),
        jax.ShapeDtypeStruct((n_batch,) + _STATE_SHAPE, F32),
    ]
    return out_specs, out_shape


def _meta_mixer(ret_cd, pr, pg, pab, cosf, sins, rsc, dmat, cw, alog, dtb, gnw, s0r, s0g):
    c2 = lambda b, n: (0, 0)
    c3 = lambda b, n: (0, 0, 0)
    out_specs, out_shape = _seq_out(1, TILE)
    return pl.pallas_call(
        functools.partial(_meta_mixer_body, ret_cd),
        grid=(1, 1),
        in_specs=[
            pl.BlockSpec((TILE, 4 * GROUP_W), c2),
            pl.BlockSpec((TILE, 4 * GROUP_W), c2),
            pl.BlockSpec((TILE, HEAD_DIM), c2),
            pl.BlockSpec((TILE, HEAD_DIM), c2),
            pl.BlockSpec((TILE, HEAD_DIM), c2),
            pl.BlockSpec((TILE, HEAD_DIM), c2),
            pl.BlockSpec((N_HEADS, TILE, TILE), c3),
            pl.BlockSpec((CONV_W, CONV_COLS), c2),
            pl.BlockSpec((1, HEAD_DIM), c2),
            pl.BlockSpec((1, HEAD_DIM), c2),
            pl.BlockSpec((1, HEAD_DIM), c2),
            pl.BlockSpec(_STATE_SHAPE, c3),
            pl.BlockSpec(_STATE_SHAPE, c3),
        ],
        out_specs=out_specs,
        out_shape=out_shape,
        scratch_shapes=[
            pltpu.VMEM(_STATE_SHAPE, F32),
            pltpu.VMEM(_STATE_SHAPE, F32),
            pltpu.VMEM((HALO + TILE, CONV_COLS), F32),
        ],
        compiler_params=_cparams(("parallel", "arbitrary")),
        name="mixer_meta",
    )(pr, pg, pab, cosf, sins, rsc, dmat, cw, alog, dtb, gnw, s0r, s0g)


def _prompt_mixer(ret_cd, a, b, pab, dmat, alog, dtb, gnw, s0r, s0g, n_batch, n_chunks):
    rows = n_batch * n_chunks * TILE
    tile = lambda b, n: (b * n_chunks + n, 0)
    c2 = lambda b, n: (0, 0)
    c3 = lambda b, n: (0, 0, 0)
    out_specs, out_shape = _seq_out(n_batch, rows)
    return pl.pallas_call(
        functools.partial(_prompt_mixer_body, ret_cd),
        grid=(n_batch, n_chunks),
        in_specs=[
            pl.BlockSpec((TILE, A_COLS), tile),
            pl.BlockSpec((TILE, B_COLS), tile),
            pl.BlockSpec((TILE, HEAD_DIM), tile),
            pl.BlockSpec((N_HEADS, TILE, TILE), c3),
            pl.BlockSpec((1, HEAD_DIM), c2),
            pl.BlockSpec((1, HEAD_DIM), c2),
            pl.BlockSpec((1, HEAD_DIM), c2),
            pl.BlockSpec(_STATE_SHAPE, c3),
            pl.BlockSpec(_STATE_SHAPE, c3),
        ],
        out_specs=out_specs,
        out_shape=out_shape,
        scratch_shapes=[pltpu.VMEM(_STATE_SHAPE, F32), pltpu.VMEM(_STATE_SHAPE, F32)],
        compiler_params=_cparams(("parallel", "arbitrary")),
        name="mixer_prompt",
    )(a, b, pab, dmat, alog, dtb, gnw, s0r, s0g)


SAMPLE_PER_TILE = TILE // SAMPLE_GROUP


def _stack_by_group(xt):
    shape3 = (SAMPLE_PER_TILE, HEAD_DIM, TILE)
    keep = lax.broadcasted_iota(jnp.int32, shape3, 0) == (lax.broadcasted_iota(jnp.int32, shape3, 2) >> 3)
    stacked = jnp.where(keep, jnp.broadcast_to(xt[None], shape3), 0.0)
    return stacked.reshape(SAMPLE_PER_TILE * HEAD_DIM, TILE)


def _sample_mixer_body(ret_cd, pr_ref, pg_ref, pab_ref, cos_ref, sin_ref, rsc_ref, dmat_ref, cw_ref,
                       alog_ref, dtb_ref, gnw_ref, convpad_ref, sr_in, sg_in,
                       mix_ref, sr_out, sg_out,
                       xc_ref, qd_s, wk_s, qg_s, wv_s, cd_s, inter_s, u_s, qs_s):
    heads = range(N_HEADS)
    xc_ref[0:HALO, :] = jnp.zeros((HALO, CONV_COLS), F32)
    tok = _tokenwise("sample", pr_ref[...], pg_ref[...], cos_ref[...], sin_ref[...], rsc_ref[...],
                     cw_ref, xc_ref, convpad_ref[...])
    tok["kdT"] = [t.T for t in tok["kd"]]
    ret, gdn = _chunk_prep("sample", tok, pab_ref[...], dmat_ref, alog_ref[...], dtb_ref[...])
    for h in heads:
        qd_s[h] = ret["qd"][h]
        wk_s[h] = gdn["wk"][h]
        qg_s[h] = gdn["qg"][h]
        wv_s[h] = gdn["wv"][h]
    cd_s[...] = gdn["cdr"]

    def per_batch(b, carry):
        rows = pl.ds(pl.multiple_of(b * SAMPLE_GROUP, SAMPLE_GROUP), SAMPLE_GROUP)
        cd_rows = cd_s[rows, :]
        for h in heads:
            inter_s[h, rows, :] = _mm(qd_s[h, rows, :], sr_in[b, h])
            s = sg_in[b, h]
            both = _mm(jnp.concatenate([wk_s[h, rows, :], qg_s[h, rows, :]], 0), s)
            u_s[h, rows, :] = wv_s[h, rows, :] - both[:SAMPLE_GROUP]
            qs_s[h, rows, :] = both[SAMPLE_GROUP:]
            cd = jnp.broadcast_to(cd_rows[SAMPLE_GROUP - 1:SAMPLE_GROUP, h:h + 1], (HEAD_DIM, HEAD_DIM))
            sg_out[b, h] = cd * s
        return carry

    lax.fori_loop(0, SAMPLE_PER_TILE, per_batch, 0)

    state_shape = (SAMPLE_PER_TILE, HEAD_DIM, HEAD_DIM)
    for h in heads:
        o = _mm(ret["scores"][h], ret["v"][h]) + inter_s[h]
        upd = _mm(_stack_by_group(ret["kdT"][h]), ret["v"][h]).reshape(state_shape)
        sr_out[:, h] = ret_cd[h] * sr_in[:, h] + upd
        mix_ref[:, h * HEAD_DIM:(h + 1) * HEAD_DIM] = _rms_gate(o, ret["gate"][h])

    for h in heads:
        u = u_s[h]
        o = qs_s[h] + _mm(gdn["attn"][h], u)
        upd = _mm(_stack_by_group(gdn["ktT"][h]), u).reshape(state_shape)
        sg_out[:, h] = sg_out[:, h] + upd
        mix_ref[:, GROUP_W + h * HEAD_DIM:GROUP_W + (h + 1) * HEAD_DIM] = _rms_gate(
            o, gdn["gate"][h], gnw_ref[...])


def _sample_mixer(ret_cd, pr, pg, pab, cosf, sins, rsc, dmat, cw, alog, dtb, gnw, convpad, sr, sg):
    n_tiles = pr.shape[0] // TILE
    tile = lambda i: (i, 0)
    c2 = lambda i: (0, 0)
    c3 = lambda i: (0, 0, 0)
    state = pl.BlockSpec((SAMPLE_PER_TILE,) + _STATE_SHAPE, lambda i: (i, 0, 0, 0))
    head_scratch = pltpu.VMEM((N_HEADS, TILE, HEAD_DIM), F32)
    return pl.pallas_call(
        functools.partial(_sample_mixer_body, ret_cd),
        grid=(n_tiles,),
        in_specs=[
            pl.BlockSpec((TILE, 4 * GROUP_W), tile),
            pl.BlockSpec((TILE, 4 * GROUP_W), tile),
            pl.BlockSpec((TILE, HEAD_DIM), tile),
            pl.BlockSpec((TILE, HEAD_DIM), c2),
            pl.BlockSpec((TILE, HEAD_DIM), c2),
            pl.BlockSpec((TILE, HEAD_DIM), c2),
            pl.BlockSpec((N_HEADS, TILE, TILE), c3),
            pl.BlockSpec((CONV_W, CONV_COLS), c2),
            pl.BlockSpec((1, HEAD_DIM), c2),
            pl.BlockSpec((1, HEAD_DIM), c2),
            pl.BlockSpec((1, HEAD_DIM), c2),
            pl.BlockSpec((TILE, CONV_COLS), tile),
            state, state,
        ],
        out_specs=[pl.BlockSpec((TILE, 2 * GROUP_W), tile), state, state],
        out_shape=[
            jax.ShapeDtypeStruct((pr.shape[0], 2 * GROUP_W), BF16),
            jax.ShapeDtypeStruct(sr.shape, F32),
            jax.ShapeDtypeStruct(sg.shape, F32),
        ],
        scratch_shapes=[pltpu.VMEM((HALO + TILE, CONV_COLS), F32)] + [head_scratch] * 4
                       + [pltpu.VMEM((TILE, HEAD_DIM), F32)] + [head_scratch] * 3,
        compiler_params=_cparams(("parallel",)),
        name="mixer_sample",
    )(pr, pg, pab, cosf, sins, rsc, dmat, cw, alog, dtb, gnw, convpad, sr, sg)


def _rotary_tables(pos):
    half = HEAD_DIM // 2
    inv = ROPE_BASE ** (-jnp.arange(half, dtype=F32) / half)
    ang = pos.astype(F32)[:, None] * inv[None, :]
    cos, sin = jnp.cos(ang), jnp.sin(ang)
    return jnp.concatenate([cos, cos], -1), jnp.concatenate([-sin, sin], -1)


def _retention_tables(seg, pos, valid, seg_len):
    log_gamma = jnp.log(1.0 - 2.0 ** (-5.0 - jnp.arange(N_HEADS, dtype=F32)))
    posf = pos.astype(F32)
    rel = posf[:, None] - posf[None, :]
    causal = (seg[:, None] == seg[None, :]) & (rel >= 0)
    dmat = jnp.where(causal[None], jnp.exp(log_gamma[:, None, None] * jnp.where(causal, rel, 0.0)[None]), 0.0)
    q_scale = jnp.exp(log_gamma[None, :] * (posf[:, None] + 1.0))
    k_scale = jnp.where(valid[:, None], jnp.exp(log_gamma[None, :] * (seg_len - 1.0 - posf[:, None])), 0.0)
    rsc = jnp.concatenate([q_scale, k_scale, jnp.zeros((TILE, HEAD_DIM - 2 * N_HEADS), F32)], -1)
    chunk_decay = tuple((1.0 - 2.0 ** (-5.0 - h)) ** seg_len for h in range(N_HEADS))
    return dmat, rsc, chunk_decay


def _pad_lanes(v):
    return jnp.pad(v.astype(F32), (0, HEAD_DIM - v.shape[0]))[None, :]


def kernel(x_prompt, x_sample, state_ret, state_gdn, state_conv, meta_tokens, emb_ln_g, emb_ln_b,
           w_in, conv_w, a_log, dt_bias, gdn_norm_w, w_out, ln1_g, ln1_b, w_gate_up, w_down,
           ln2_g, ln2_b):
    n_batch, seq, _ = x_prompt.shape
    dec_batch, dec_seq, _ = x_sample.shape
    assert seq % TILE == 0 and dec_seq == SAMPLE_TOKENS and N_META <= TILE
    n_chunks = seq // TILE
    layer = 0

    w_in_l = w_in[layer]
    w_ret = w_in_l[:, :4 * GROUP_W].astype(BF16)
    w_gdn = w_in_l[:, 4 * GROUP_W:8 * GROUP_W].astype(BF16)
    w_ab = jnp.pad(w_in_l[:, 8 * GROUP_W:], ((0, 0), (0, HEAD_DIM - 2 * N_HEADS))).astype(BF16)
    w_out_b = w_out[layer].astype(BF16)
    w_gu_b = w_gate_up[layer].astype(BF16)
    w_down_b = w_down[layer].astype(BF16)
    row = lambda v: v.astype(F32)[None, :]
    eg, eb = row(emb_ln_g), row(emb_ln_b)
    cw = conv_w[layer].astype(F32)
    alog, dtb, gnw = _pad_lanes(a_log[layer]), _pad_lanes(dt_bias[layer]), row(gdn_norm_w[layer])

    xp = x_prompt.reshape(n_batch * seq, D_MODEL)
    xm = jnp.pad(meta_tokens.astype(F32), ((0, TILE - N_META), (0, 0)))
    xs = jnp.pad(x_sample, ((0, 0), (SAMPLE_GROUP - dec_seq, 0), (0, 0))).reshape(
        dec_batch * SAMPLE_GROUP, D_MODEL)
    convpad = jnp.pad(state_conv[layer].astype(F32), ((0, 0), (1, SAMPLE_GROUP - CONV_W), (0, 0))).reshape(
        dec_batch * SAMPLE_GROUP, CONV_COLS)

    tile_idx = jnp.arange(TILE)
    cos_p, sin_p = _rotary_tables(N_META + jnp.arange(seq))
    cos_m, sin_m = _rotary_tables(tile_idx)
    tok = (tile_idx % SAMPLE_GROUP) - (SAMPLE_GROUP - dec_seq)
    cos_s, sin_s = _rotary_tables(PAST_LEN + jnp.maximum(tok, 0))
    zeros_i = jnp.zeros((TILE,), jnp.int32)
    all_valid = jnp.ones((TILE,), bool)
    dmat_p, rsc_p, cd_p = _retention_tables(zeros_i, tile_idx, all_valid, float(TILE))
    dmat_m, rsc_m, cd_m = _retention_tables(zeros_i, tile_idx, tile_idx < N_META, float(N_META))
    dmat_s, rsc_s, cd_s = _retention_tables(tile_idx // SAMPLE_GROUP, tok, tok >= 0, float(dec_seq))

    zero_state = jnp.zeros(_STATE_SHAPE, F32)
    pr_m, pg_m, pab_m = _front(xm, eg, eb, w_ret, w_gdn, w_ab, TILE)
    _, sr_m, sg_m = _meta_mixer(cd_m, pr_m, pg_m, pab_m, cos_m, sin_m, rsc_m, dmat_m, cw, alog, dtb, gnw,
                                zero_state, zero_state)
    halo_p = pg_m[N_META - HALO:N_META, :CONV_COLS]

    tm_front = 512
    a_p, b_p, pab_p, tail_p = _front_prompt(xp, eg, eb, w_ret, w_gdn, w_ab, cos_p, sin_p,
                                            jnp.tile(rsc_p, (tm_front // TILE, 1)), cw, halo_p,
                                            n_batch, tm_front)
    mix_p, sr_p, sg_p = _prompt_mixer(cd_p, a_p, b_p, pab_p, dmat_p, alog, dtb, gnw, sr_m[0], sg_m[0],
                                      n_batch, n_chunks)

    pr_s, pg_s, pab_s = _front(xs, eg, eb, w_ret, w_gdn, w_ab, 512)
    mix_s, sr_s, sg_s = _sample_mixer(cd_s, pr_s, pg_s, pab_s, cos_s, sin_s, rsc_s, dmat_s, cw, alog,
                                      dtb, gnw, convpad, state_ret[layer].astype(F32),
                                      state_gdn[layer].astype(F32))

    back = functools.partial(_back, eg=eg, eb=eb, w_out=w_out_b, g1=row(ln1_g[layer]), b1=row(ln1_b[layer]),
                             w_gu=w_gu_b, w_down=w_down_b, g2=row(ln2_g[layer]), b2=row(ln2_b[layer]),
                             tm=256)
    y_p = back(xp, mix_p).reshape(n_batch, seq, D_MODEL)
    y_s = back(xs, mix_s).reshape(dec_batch, SAMPLE_GROUP, D_MODEL)[:, SAMPLE_GROUP - dec_seq:]

    conv_p = tail_p[:, HALO - (CONV_W - 1):, :]
    conv_s = pg_s.reshape(dec_batch, SAMPLE_GROUP, 4 * GROUP_W)[:, SAMPLE_GROUP - (CONV_W - 1):, :CONV_COLS]
    return (y_p, y_s, sr_p[None], sg_p[None], conv_p[None], sr_s[None], sg_s[None], conv_s[None])
```

```python
import functools

import jax
import jax.numpy as jnp
from jax import lax
from jax.experimental import pallas as pl
from jax.experimental.pallas import tpu as pltpu

F32 = jnp.float32
BF16 = jnp.bfloat16

D_MODEL = 1024
N_META = 16
N_HEADS = 4
HEAD_DIM = 128
GROUP_W = N_HEADS * HEAD_DIM
CONV_W = 4
CONV_COLS = 3 * GROUP_W
D_FF = 2816
PAST_LEN = 16384
ROPE_BASE = 10000.0
LN_EPS = 1e-5
RMS_EPS = 1e-6
ALPHA = 2.0 ** 0.25
TILE = 128
SAMPLE_GROUP = 8
SAMPLE_TOKENS = 4
HALO = 8
MIXER_SEQS_PER_STEP = 4
BACK_PARTS = 2
FRONT_LAG = 2
VMEM_LIMIT = 56 * 1024 * 1024


def _cparams(sem):
    return pltpu.CompilerParams(dimension_semantics=sem, vmem_limit_bytes=VMEM_LIMIT)


def _layer_norm(x, g, b):
    mu = jnp.mean(x, -1, keepdims=True)
    xc = x - mu
    var = jnp.mean(xc * xc, -1, keepdims=True)
    return xc * lax.rsqrt(var + LN_EPS) * g + b


def _sigmoid(x):
    return 1.0 / (1.0 + jnp.exp(-x))


def _silu(x):
    return x * _sigmoid(x)


def _softplus(x):
    return jnp.maximum(x, 0.0) + jnp.log(1.0 + jnp.exp(-jnp.abs(x)))


def _mm(a, b):
    return jnp.dot(a.astype(BF16), b.astype(BF16), preferred_element_type=F32)


def _mm_nt(a, b):
    return lax.dot_general(a.astype(BF16), b.astype(BF16), (((1,), (1,)), ((), ())),
                           preferred_element_type=F32)


def _mm_each(xs, ys):
    return [_mm(x, y) for x, y in zip(xs, ys)]


def _mm_split3(m01, x):
    x1 = x.astype(BF16)
    r1 = x - x1.astype(F32)
    x2 = r1.astype(BF16)
    x3 = (r1 - x2.astype(F32)).astype(BF16)
    dot = functools.partial(jnp.dot, preferred_element_type=F32)
    return (dot(m01, x3) + dot(m01, x2)) + dot(m01, x1)


def _lane_bcast(x, lane):
    return jnp.broadcast_to(x[:, lane:lane + 1], (x.shape[0], HEAD_DIM))


def _head_cols(x, base, h):
    return x[:, base + h * HEAD_DIM:base + (h + 1) * HEAD_DIM]


def _front_body(x_ref, g_ref, b_ref, wr_ref, wg_ref, wab_ref, pr_ref, pg_ref, pab_ref):
    h = _layer_norm(x_ref[...], g_ref[...], b_ref[...]).astype(BF16)
    pr_ref[...] = jnp.dot(h, wr_ref[...], preferred_element_type=F32)
    pg_ref[...] = jnp.dot(h, wg_ref[...], preferred_element_type=F32)
    pab_ref[...] = jnp.dot(h, wab_ref[...], preferred_element_type=F32)


def _front(x2d, ln_g, ln_b, w_ret, w_gdn, w_ab, tm):
    rows = x2d.shape[0]
    const = lambda i: (0, 0)
    tile = lambda i: (i, 0)
    return pl.pallas_call(
        _front_body,
        grid=(rows // tm,),
        in_specs=[
            pl.BlockSpec((tm, D_MODEL), tile),
            pl.BlockSpec((1, D_MODEL), const),
            pl.BlockSpec((1, D_MODEL), const),
            pl.BlockSpec((D_MODEL, 4 * GROUP_W), const),
            pl.BlockSpec((D_MODEL, 4 * GROUP_W), const),
            pl.BlockSpec((D_MODEL, HEAD_DIM), const),
        ],
        out_specs=[
            pl.BlockSpec((tm, 4 * GROUP_W), tile),
            pl.BlockSpec((tm, 4 * GROUP_W), tile),
            pl.BlockSpec((tm, HEAD_DIM), tile),
        ],
        out_shape=[
            jax.ShapeDtypeStruct((rows, 4 * GROUP_W), F32),
            jax.ShapeDtypeStruct((rows, 4 * GROUP_W), F32),
            jax.ShapeDtypeStruct((rows, HEAD_DIM), F32),
        ],
        compiler_params=_cparams(("parallel",)),
        name="front",
    )(x2d, ln_g, ln_b, w_ret, w_gdn, w_ab)


def _back_body(x_ref, mix_ref, eg_ref, eb_ref, wo_ref, g1_ref, b1_ref, wgu_ref, wd_ref,
               g2_ref, b2_ref, y_ref):
    part = x_ref.shape[0] // BACK_PARTS
    parts = [slice(i * part, (i + 1) * part) for i in range(BACK_PARTS)]
    dot = functools.partial(jnp.dot, preferred_element_type=F32)
    h = [_layer_norm(x_ref[p, :], eg_ref[...], eb_ref[...]) for p in parts]
    mp = [dot(mix_ref[p, :], wo_ref[...]) for p in parts]
    h1 = [_layer_norm(ALPHA * a + b, g1_ref[...], b1_ref[...]) for a, b in zip(h, mp)]
    gu = [dot(a.astype(BF16), wgu_ref[...]) for a in h1]
    act = [(_silu(a[:, :D_FF]) * a[:, D_FF:]).astype(BF16) for a in gu]
    ff = [dot(a, wd_ref[...]) for a in act]
    for p, a, b in zip(parts, h1, ff):
        y_ref[p, :] = _layer_norm(ALPHA * a + b, g2_ref[...], b2_ref[...])


def _back(x2d, mix, eg, eb, w_out, g1, b1, w_gu, w_down, g2, b2, tm):
    rows = x2d.shape[0]
    const = lambda i: (0, 0)
    tile = lambda i: (i, 0)
    single = pl.Buffered(1)
    vec = pl.BlockSpec((1, D_MODEL), const)
    return pl.pallas_call(
        _back_body,
        grid=(rows // tm,),
        in_specs=[
            pl.BlockSpec((tm, D_MODEL), tile),
            pl.BlockSpec((tm, D_MODEL), tile),
            vec, vec,
            pl.BlockSpec((D_MODEL, D_MODEL), const, pipeline_mode=single),
            vec, vec,
            pl.BlockSpec((D_MODEL, 2 * D_FF), const, pipeline_mode=single),
            pl.BlockSpec((D_FF, D_MODEL), const, pipeline_mode=single),
            vec, vec,
        ],
        out_specs=pl.BlockSpec((tm, D_MODEL), tile),
        out_shape=jax.ShapeDtypeStruct((rows, D_MODEL), F32),
        compiler_params=_cparams(("parallel",)),
        name="back",
    )(x2d, mix, eg, eb, w_out, g1, b1, w_gu, w_down, g2, b2)


def _iota2(shape, dim):
    return lax.broadcasted_iota(jnp.int32, shape, dim)


def _valid_rows(variant, shape):
    r = _iota2(shape, 0)
    if variant == "meta":
        return r < N_META
    if variant == "sample":
        return (r & (SAMPLE_GROUP - 1)) >= SAMPLE_GROUP - SAMPLE_TOKENS
    return None


def _mask_rows(valid, x):
    return x if valid is None else jnp.where(valid, x, 0.0)


def _tri_inverse(variant, nmats, row, col):
    eye = (row == col).astype(F32)
    if variant == "sample":
        n2 = _mm_each(nmats, nmats)
        ps = [eye + n for n in nmats]
        return [p + t for p, t in zip(ps, _mm_each(ps, n2))]
    base_log2 = 4
    in_block = (row >> base_log2) == (col >> base_log2)
    ds = [jnp.where(in_block, n, 0.0) for n in nmats]
    ps = [eye + d for d in ds]
    for _ in range(base_log2 - 1):
        ds = _mm_each(ds, ds)
        ps = [p + t for p, t in zip(ps, _mm_each(ps, ds))]
    if variant == "meta":
        return ps
    s = base_log2
    while (1 << s) < TILE:
        lower_left = (((row >> (s + 1)) == (col >> (s + 1))) & (((row >> s) & 1) == 1)
                      & (((col >> s) & 1) == 0))
        cs = [jnp.where(lower_left, n, 0.0) for n in nmats]
        ps = [p + t for p, t in zip(ps, _mm_each(_mm_each(ps, cs), ps))]
        s += 1
    return ps


def _ret_head_ops(valid, q, k, v, gate, cosf, sins, q_scale, k_scale):
    rot = lambda t: t * cosf + pltpu.roll(t, HEAD_DIM // 2, 1) * sins
    rq = _mask_rows(valid, rot(q))
    rk = _mask_rows(valid, rot(k) * (HEAD_DIM ** -0.5))
    return dict(rq=rq, rk=rk, rv=_mask_rows(valid, v), qd=rq * q_scale, kd=rk * k_scale, sgr=_silu(gate))


def _short_conv(x, cw_ref, xc_ref, col0, row0=0):
    rows, ncols = x.shape
    cols = slice(col0, col0 + ncols)
    xc_ref[HALO + row0:HALO + row0 + rows, cols] = x
    conv = None
    for w in range(CONV_W):
        off = HALO + row0 - (CONV_W - 1) + w
        term = xc_ref[off:off + rows, cols] * cw_ref[w:w + 1, cols]
        conv = term if conv is None else conv + term
    return _silu(conv)


def _l2_normalize(t):
    return t * lax.rsqrt(jnp.sum(t * t, -1, keepdims=True) + RMS_EPS)


def _tokenwise(variant, pret, pgdn, cosf, sins, rsc, cw_ref, xc_ref, convpad):
    rows = pret.shape[0]
    heads = range(N_HEADS)
    valid = _valid_rows(variant, (rows, HEAD_DIM))
    ret = [_ret_head_ops(valid, *(_head_cols(pret, g * GROUP_W, h) for g in range(4)), cosf, sins,
                         _lane_bcast(rsc, h), _lane_bcast(rsc, N_HEADS + h)) for h in heads]
    x = pgdn[:, :CONV_COLS]
    if variant == "sample":
        x = jnp.where(_valid_rows(variant, x.shape), x, convpad)
    conv = _short_conv(x, cw_ref, xc_ref, 0)
    gq = [_mask_rows(valid, _l2_normalize(_head_cols(conv, 0, h)) * (HEAD_DIM ** -0.5)) for h in heads]
    gk = [_mask_rows(valid, _l2_normalize(_head_cols(conv, GROUP_W, h))) for h in heads]
    gv = [_mask_rows(valid, _head_cols(conv, 2 * GROUP_W, h)) for h in heads]
    sgz = [_silu(_head_cols(pgdn, 3 * GROUP_W, h)) for h in heads]
    tok = {key: [r[key] for r in ret] for key in ret[0]}
    tok.update(gq=gq, gk=gk, gv=gv, sgz=sgz)
    return tok


def _chunk_prep(variant, toks, pabs, dmat_ref, alog, dtb):
    shape = (TILE, HEAD_DIM)
    tiles = range(len(toks))
    items = [(j, h) for j in tiles for h in range(N_HEADS)]
    pick = lambda key: [toks[j][key][h] for j, h in items]
    row = _iota2(shape, 0)
    col = _iota2(shape, 1)
    valid = _valid_rows(variant, shape)
    if variant == "sample":
        same = (row >> 3) == (col >> 3)
        incl = same & (row >= col)
        strict = same & (row > col)
    else:
        incl = row >= col
        strict = row > col

    ret = dict(
        items=items,
        scores=[_mm_nt(toks[j]["rq"][h], toks[j]["rk"][h]) * dmat_ref[h] for j, h in items],
        qd=pick("qd"), kdT=pick("kdT"), v=pick("rv"), gate=pick("sgr"),
    )

    incl01 = incl.astype(BF16)
    beta_all, gcum, gam, ktail, cdr, gcum_t = [], [], [], [], [], []
    for j in tiles:
        g_all = _mask_rows(valid, -jnp.exp(alog) * _softplus(pabs[j] + dtb))
        beta_all.append(_mask_rows(valid, _sigmoid(pabs[j])))
        if variant == "sample":
            both = _mm_split3(jnp.concatenate([incl01, same.astype(BF16)], 0), g_all)
            gc, gseg = both[:TILE], both[TILE:]
        else:
            gc = _mm_split3(incl01, g_all)
            gseg = jnp.broadcast_to(gc[TILE - 1:TILE, :], shape)
        gcum.append(gc)
        gam.append(jnp.exp(gc))
        ktail.append(jnp.exp(gseg - gc))
        cdr.append(jnp.exp(gseg))
        gcum_t.append(gc.T)

    gq, gk, gv = pick("gq"), pick("gk"), pick("gv")
    n_items = range(len(items))
    dec = [jnp.where(incl, jnp.exp(jnp.minimum(
        _lane_bcast(gcum[j], h) - jnp.broadcast_to(gcum_t[j][h:h + 1, :], shape), 0.0)), 0.0)
        for j, h in items]
    bcol = [_lane_bcast(beta_all[j], N_HEADS + h) for j, h in items]
    gamc = [_lane_bcast(gam[j], h) for j, h in items]
    kk = [_mm_nt(gk[i], gk[i]) for i in n_items]
    qk = [_mm_nt(gq[i], gk[i]) for i in n_items]
    nmats = [-(jnp.where(strict, kk[i] * dec[i], 0.0) * bcol[i]) for i in n_items]
    tinv = _tri_inverse(variant, nmats, row, col)
    rhs = [jnp.concatenate([gv[i] * bcol[i], gk[i] * (bcol[i] * gamc[i])], 1) for i in n_items]
    sol = _mm_each(tinv, rhs)
    gdn = dict(
        items=items,
        wv=[t[:, :HEAD_DIM] for t in sol],
        wk=[t[:, HEAD_DIM:] for t in sol],
        attn=[qk[i] * dec[i] for i in n_items],
        qg=[gq[i] * gamc[i] for i in n_items],
        ktT=[(gk[i] * _lane_bcast(ktail[j], h)).T for i, (j, h) in enumerate(items)],
        cdr=cdr,
        gate=pick("sgz"),
    )
    return ret, gdn


def _rms_gate(o, gate, weight=None):
    o = o * lax.rsqrt(jnp.mean(o * o, -1, keepdims=True) + RMS_EPS)
    if weight is not None:
        o = o * weight
    return (o * gate).astype(BF16)


A_COLS = 5 * GROUP_W
B_COLS = 5 * GROUP_W
_STATE_SHAPE = (N_HEADS, HEAD_DIM, HEAD_DIM)


def _front_prompt_body(tm, x_ref, g_ref, b_ref, w_ref, cos_ref, sin_ref, rsc_ref, cw_ref, halo0_ref,
                       a_ref, b_out_ref, pab_ref, tail_ref, xc_ref):
    @pl.when(pl.program_id(1) == 0)
    def _():
        xc_ref[0:HALO, :] = halo0_ref[...]

    def put(ref, r, group, h, val):
        ref[r * TILE:(r + 1) * TILE,
            group * GROUP_W + h * HEAD_DIM:group * GROUP_W + (h + 1) * HEAD_DIM] = val.astype(ref.dtype)

    def finish_ret_head(h, r, val):
        rows = slice(r * TILE, (r + 1) * TILE)
        rsc = rsc_ref[rows, :]
        ops = _ret_head_ops(None, *(_head_cols(val, 0, g) for g in range(4)), cos_ref[rows, :],
                            sin_ref[rows, :], _lane_bcast(rsc, h), _lane_bcast(rsc, N_HEADS + h))
        for group, key in enumerate(("rq", "rk", "rv", "qd")):
            put(a_ref, r, group, h, ops[key])
        put(a_ref, r, 4, h, ops["kd"].T)
        put(b_out_ref, r, 4, h, ops["sgr"])

    def finish_conv_group(group, r, val):
        conv = _short_conv(val, cw_ref, xc_ref, group * GROUP_W, r * TILE)
        for h in range(N_HEADS):
            t = _head_cols(conv, 0, h)
            if group == 0:
                t = _l2_normalize(t) * (HEAD_DIM ** -0.5)
            elif group == 1:
                t = _l2_normalize(t)
            put(b_out_ref, r, group, h, t)

    def finish_gate(r, val):
        for h in range(N_HEADS):
            put(b_out_ref, r, 3, h, _silu(_head_cols(val, 0, h)))

    def finish_ab(r, val):
        pab_ref[r * TILE:(r + 1) * TILE, :] = val

    ret_group = lambda h: (h * GROUP_W, GROUP_W, functools.partial(finish_ret_head, h))
    gdn_group = lambda g: ((N_HEADS + g) * GROUP_W, GROUP_W, functools.partial(finish_conv_group, g))
    plan = [ret_group(0), gdn_group(0), ret_group(1), gdn_group(1), ret_group(2), gdn_group(2),
            ret_group(3), (7 * GROUP_W, GROUP_W, finish_gate), (8 * GROUP_W, HEAD_DIM, finish_ab)]
    pending = []
    for r in range(tm // TILE):
        hidden = _layer_norm(x_ref[r * TILE:(r + 1) * TILE, :], g_ref[...], b_ref[...]).astype(BF16)
        for col, width, finish in plan:
            val = jnp.dot(hidden, w_ref[:, col:col + width], preferred_element_type=F32)
            pending.append(functools.partial(finish, r, val))
            if len(pending) > FRONT_LAG:
                pending.pop(0)()
    for finish in pending:
        finish()
    tail = xc_ref[tm:tm + HALO, :]
    xc_ref[0:HALO, :] = tail
    tail_ref[0] = tail


def _front_prompt(x2d, ln_g, ln_b, w_all, cosf, sins, rsc, cw, halo0, n_batch, tm):
    rows = x2d.shape[0]
    tiles = rows // (n_batch * tm)
    c2 = lambda b, t: (0, 0)
    tile = lambda b, t: (b * tiles + t, 0)
    pos = lambda b, t: (t, 0)
    return pl.pallas_call(
        functools.partial(_front_prompt_body, tm),
        grid=(n_batch, tiles),
        in_specs=[
            pl.BlockSpec((tm, D_MODEL), tile),
            pl.BlockSpec((1, D_MODEL), c2),
            pl.BlockSpec((1, D_MODEL), c2),
            pl.BlockSpec((D_MODEL, 8 * GROUP_W + HEAD_DIM), c2, pipeline_mode=pl.Buffered(1)),
            pl.BlockSpec((tm, HEAD_DIM), pos),
            pl.BlockSpec((tm, HEAD_DIM), pos),
            pl.BlockSpec((tm, HEAD_DIM), c2),
            pl.BlockSpec((CONV_W, CONV_COLS), c2),
            pl.BlockSpec((HALO, CONV_COLS), c2),
        ],
        out_specs=[
            pl.BlockSpec((tm, A_COLS), tile),
            pl.BlockSpec((tm, B_COLS), tile),
            pl.BlockSpec((tm, HEAD_DIM), tile),
            pl.BlockSpec((1, HALO, CONV_COLS), lambda b, t: (b, 0, 0)),
        ],
        out_shape=[
            jax.ShapeDtypeStruct((rows, A_COLS), BF16),
            jax.ShapeDtypeStruct((rows, B_COLS), F32),
            jax.ShapeDtypeStruct((rows, HEAD_DIM), F32),
            jax.ShapeDtypeStruct((n_batch, HALO, CONV_COLS), F32),
        ],
        scratch_shapes=[pltpu.VMEM((HALO + tm, CONV_COLS), F32)],
        compiler_params=_cparams(("parallel", "arbitrary")),
        name="front_prompt",
    )(x2d, ln_g, ln_b, w_all, cosf, sins, rsc, cw, halo0)


def _seq_init(s0r_ref, s0g_ref, sr_ref, sg_ref):
    @pl.when(pl.program_id(1) == 0)
    def _():
        for j in range(sr_ref.shape[0]):
            sr_ref[j] = s0r_ref[...]
            sg_ref[j] = s0g_ref[...]


def _seq_step(ret_cd, ret, gdn, gnw, mix_ref, sr_out, sg_out, sr_ref, sg_ref):
    items = ret["items"]
    idx = range(len(items))
    s_r = [sr_ref[j, h] for j, h in items]
    s_g = [sg_ref[j, h] for j, h in items]
    both = _mm_each([jnp.concatenate([gdn["wk"][i], gdn["qg"][i]], 0) for i in idx], s_g)
    inter = _mm_each(ret["qd"], s_r)
    intra = _mm_each(ret["scores"], ret["v"])
    u = [gdn["wv"][i] - both[i][:TILE] for i in idx]
    o_g = [both[i][TILE:] + t for i, t in zip(idx, _mm_each(gdn["attn"], u))]
    upd_g = _mm_each(gdn["ktT"], u)
    upd_r = _mm_each(ret["kdT"], ret["v"])
    for i, (j, h) in enumerate(items):
        cd = jnp.broadcast_to(gdn["cdr"][j][0:1, h:h + 1], (HEAD_DIM, HEAD_DIM))
        sg_ref[j, h] = cd * s_g[i] + upd_g[i]
        sr_ref[j, h] = ret_cd[h] * s_r[i] + upd_r[i]
    for i, (j, h) in enumerate(items):
        mix_ref[j, :, h * HEAD_DIM:(h + 1) * HEAD_DIM] = _rms_gate(intra[i] + inter[i], ret["gate"][i])
        mix_ref[j, :, GROUP_W + h * HEAD_DIM:GROUP_W + (h + 1) * HEAD_DIM] = _rms_gate(
            o_g[i], gdn["gate"][i], gnw)

    @pl.when(pl.program_id(1) == pl.num_programs(1) - 1)
    def _():
        sr_out[...] = sr_ref[...]
        sg_out[...] = sg_ref[...]


def _meta_mixer_body(ret_cd, pr_ref, pg_ref, pab_ref, cos_ref, sin_ref, rsc_ref, dmat_ref,
                     cw_ref, alog_ref, dtb_ref, gnw_ref, s0r_ref, s0g_ref,
                     mix_ref, sr_out, sg_out, sr_ref, sg_ref, xc_ref):
    _seq_init(s0r_ref, s0g_ref, sr_ref, sg_ref)
    xc_ref[0:HALO, :] = jnp.zeros((HALO, CONV_COLS), F32)
    tok = _tokenwise("meta", pr_ref[...], pg_ref[...], cos_ref[...], sin_ref[...], rsc_ref[...],
                     cw_ref, xc_ref, None)
    tok["kdT"] = [t.T for t in tok["kd"]]
    ret, gdn = _chunk_prep("meta", [tok], [pab_ref[...]], dmat_ref, alog_ref[...], dtb_ref[...])
    _seq_step(ret_cd, ret, gdn, gnw_ref[...], mix_ref, sr_out, sg_out, sr_ref, sg_ref)


def _prompt_mixer_body(ret_cd, a_ref, b_ref, pab_ref, dmat_ref, alog_ref, dtb_ref, gnw_ref,
                       s0r_ref, s0g_ref, mix_ref, sr_out, sg_out, sr_ref, sg_ref):
    _seq_init(s0r_ref, s0g_ref, sr_ref, sg_ref)
    heads = range(N_HEADS)
    tiles = range(a_ref.shape[0])
    piece = lambda ref, j, group: [
        ref[j, :, group * GROUP_W + h * HEAD_DIM:group * GROUP_W + (h + 1) * HEAD_DIM] for h in heads]
    toks = [dict(rq=piece(a_ref, j, 0), rk=piece(a_ref, j, 1), rv=piece(a_ref, j, 2),
                 qd=piece(a_ref, j, 3), kdT=piece(a_ref, j, 4), gq=piece(b_ref, j, 0),
                 gk=piece(b_ref, j, 1), gv=piece(b_ref, j, 2), sgz=piece(b_ref, j, 3),
                 sgr=piece(b_ref, j, 4)) for j in tiles]
    ret, gdn = _chunk_prep("prompt", toks, [pab_ref[j] for j in tiles], dmat_ref, alog_ref[...],
                           dtb_ref[...])
    _seq_step(ret_cd, ret, gdn, gnw_ref[...], mix_ref, sr_out, sg_out, sr_ref, sg_ref)


def _seq_out(n_batch, n_chunks, nb):
    state = pl.BlockSpec((nb,) + _STATE_SHAPE, lambda b, n: (b, 0, 0, 0))
    out_specs = [pl.BlockSpec((nb, TILE, 2 * GROUP_W), lambda b, n: (b, n, 0)), state, state]
    out_shape = [
        jax.ShapeDtypeStruct((n_batch, n_chunks * TILE, 2 * GROUP_W), BF16),
        jax.ShapeDtypeStruct((n_batch,) + _STATE_SHAPE, F32),
        jax.ShapeDtypeStruct((n_batch,) + _STATE_SHAPE, F32),
    ]
    scratch = [pltpu.VMEM((nb,) + _STATE_SHAPE, F32), pltpu.VMEM((nb,) + _STATE_SHAPE, F32)]
    return out_specs, out_shape, scratch


def _meta_mixer(ret_cd, pr, pg, pab, cosf, sins, rsc, dmat, cw, alog, dtb, gnw, s0r, s0g):
    c2 = lambda b, n: (0, 0)
    c3 = lambda b, n: (0, 0, 0)
    out_specs, out_shape, state_scratch = _seq_out(1, 1, 1)
    return pl.pallas_call(
        functools.partial(_meta_mixer_body, ret_cd),
        grid=(1, 1),
        in_specs=[
            pl.BlockSpec((TILE, 4 * GROUP_W), c2),
            pl.BlockSpec((TILE, 4 * GROUP_W), c2),
            pl.BlockSpec((TILE, HEAD_DIM), c2),
            pl.BlockSpec((TILE, HEAD_DIM), c2),
            pl.BlockSpec((TILE, HEAD_DIM), c2),
            pl.BlockSpec((TILE, HEAD_DIM), c2),
            pl.BlockSpec((N_HEADS, TILE, TILE), c3),
            pl.BlockSpec((CONV_W, CONV_COLS), c2),
            pl.BlockSpec((1, HEAD_DIM), c2),
            pl.BlockSpec((1, HEAD_DIM), c2),
            pl.BlockSpec((1, HEAD_DIM), c2),
            pl.BlockSpec(_STATE_SHAPE, c3),
            pl.BlockSpec(_STATE_SHAPE, c3),
        ],
        out_specs=out_specs,
        out_shape=out_shape,
        scratch_shapes=state_scratch + [pltpu.VMEM((HALO + TILE, CONV_COLS), F32)],
        compiler_params=_cparams(("parallel", "arbitrary")),
        name="mixer_meta",
    )(pr, pg, pab, cosf, sins, rsc, dmat, cw, alog, dtb, gnw, s0r, s0g)


def _prompt_mixer(ret_cd, a, b, pab, dmat, alog, dtb, gnw, s0r, s0g, nb):
    n_batch, seq, _ = a.shape
    n_chunks = seq // TILE
    tile = lambda b, n: (b, n, 0)
    c2 = lambda b, n: (0, 0)
    c3 = lambda b, n: (0, 0, 0)
    out_specs, out_shape, state_scratch = _seq_out(n_batch, n_chunks, nb)
    return pl.pallas_call(
        functools.partial(_prompt_mixer_body, ret_cd),
        grid=(n_batch // nb, n_chunks),
        in_specs=[
            pl.BlockSpec((nb, TILE, A_COLS), tile),
            pl.BlockSpec((nb, TILE, B_COLS), tile),
            pl.BlockSpec((nb, TILE, HEAD_DIM), tile),
            pl.BlockSpec((N_HEADS, TILE, TILE), c3),
            pl.BlockSpec((1, HEAD_DIM), c2),
            pl.BlockSpec((1, HEAD_DIM), c2),
            pl.BlockSpec((1, HEAD_DIM), c2),
            pl.BlockSpec(_STATE_SHAPE, c3),
            pl.BlockSpec(_STATE_SHAPE, c3),
        ],
        out_specs=out_specs,
        out_shape=out_shape,
        scratch_shapes=state_scratch,
        compiler_params=_cparams(("parallel", "arbitrary")),
        name="mixer_prompt",
    )(a, b, pab, dmat, alog, dtb, gnw, s0r, s0g)


SAMPLE_PER_TILE = TILE // SAMPLE_GROUP


def _stack_by_group(xt):
    shape3 = (SAMPLE_PER_TILE, HEAD_DIM, TILE)
    keep = lax.broadcasted_iota(jnp.int32, shape3, 0) == (lax.broadcasted_iota(jnp.int32, shape3, 2) >> 3)
    stacked = jnp.where(keep, jnp.broadcast_to(xt[None], shape3), 0.0)
    return stacked.reshape(SAMPLE_PER_TILE * HEAD_DIM, TILE)


def _sample_mixer_body(ret_cd, pr_ref, pg_ref, pab_ref, cos_ref, sin_ref, rsc_ref, dmat_ref, cw_ref,
                       alog_ref, dtb_ref, gnw_ref, convpad_ref, sr_in, sg_in,
                       mix_ref, sr_out, sg_out,
                       xc_ref, qd_s, wk_s, qg_s, wv_s, cd_s, inter_s, u_s, qs_s):
    heads = range(N_HEADS)
    xc_ref[0:HALO, :] = jnp.zeros((HALO, CONV_COLS), F32)
    tok = _tokenwise("sample", pr_ref[...], pg_ref[...], cos_ref[...], sin_ref[...], rsc_ref[...],
                     cw_ref, xc_ref, convpad_ref[...])
    tok["kdT"] = [t.T for t in tok["kd"]]
    ret, gdn = _chunk_prep("sample", [tok], [pab_ref[...]], dmat_ref, alog_ref[...], dtb_ref[...])
    for h in heads:
        qd_s[h] = ret["qd"][h]
        wk_s[h] = gdn["wk"][h]
        qg_s[h] = gdn["qg"][h]
        wv_s[h] = gdn["wv"][h]
    cd_s[...] = gdn["cdr"][0]

    def per_batch(b, carry):
        rows = pl.ds(pl.multiple_of(b * SAMPLE_GROUP, SAMPLE_GROUP), SAMPLE_GROUP)
        cd_rows = cd_s[rows, :]
        for h in heads:
            inter_s[h, rows, :] = _mm(qd_s[h, rows, :], sr_in[b, h])
            s = sg_in[b, h]
            both = _mm(jnp.concatenate([wk_s[h, rows, :], qg_s[h, rows, :]], 0), s)
            u_s[h, rows, :] = wv_s[h, rows, :] - both[:SAMPLE_GROUP]
            qs_s[h, rows, :] = both[SAMPLE_GROUP:]
            cd = jnp.broadcast_to(cd_rows[SAMPLE_GROUP - 1:SAMPLE_GROUP, h:h + 1], (HEAD_DIM, HEAD_DIM))
            sg_out[b, h] = cd * s
        return carry

    lax.fori_loop(0, SAMPLE_PER_TILE, per_batch, 0)

    state_shape = (SAMPLE_PER_TILE, HEAD_DIM, HEAD_DIM)
    for h in heads:
        o = _mm(ret["scores"][h], ret["v"][h]) + inter_s[h]
        upd = _mm(_stack_by_group(ret["kdT"][h]), ret["v"][h]).reshape(state_shape)
        sr_out[:, h] = ret_cd[h] * sr_in[:, h] + upd
        mix_ref[:, h * HEAD_DIM:(h + 1) * HEAD_DIM] = _rms_gate(o, ret["gate"][h])

    for h in heads:
        u = u_s[h]
        o = qs_s[h] + _mm(gdn["attn"][h], u)
        upd = _mm(_stack_by_group(gdn["ktT"][h]), u).reshape(state_shape)
        sg_out[:, h] = sg_out[:, h] + upd
        mix_ref[:, GROUP_W + h * HEAD_DIM:GROUP_W + (h + 1) * HEAD_DIM] = _rms_gate(
            o, gdn["gate"][h], gnw_ref[...])


def _sample_mixer(ret_cd, pr, pg, pab, cosf, sins, rsc, dmat, cw, alog, dtb, gnw, convpad, sr, sg):
    n_tiles = pr.shape[0] // TILE
    tile = lambda i: (i, 0)
    c2 = lambda i: (0, 0)
    c3 = lambda i: (0, 0, 0)
    state = pl.BlockSpec((SAMPLE_PER_TILE,) + _STATE_SHAPE, lambda i: (i, 0, 0, 0))
    head_scratch = pltpu.VMEM((N_HEADS, TILE, HEAD_DIM), F32)
    return pl.pallas_call(
        functools.partial(_sample_mixer_body, ret_cd),
        grid=(n_tiles,),
        in_specs=[
            pl.BlockSpec((TILE, 4 * GROUP_W), tile),
            pl.BlockSpec((TILE, 4 * GROUP_W), tile),
            pl.BlockSpec((TILE, HEAD_DIM), tile),
            pl.BlockSpec((TILE, HEAD_DIM), c2),
            pl.BlockSpec((TILE, HEAD_DIM), c2),
            pl.BlockSpec((TILE, HEAD_DIM), c2),
            pl.BlockSpec((N_HEADS, TILE, TILE), c3),
            pl.BlockSpec((CONV_W, CONV_COLS), c2),
            pl.BlockSpec((1, HEAD_DIM), c2),
            pl.BlockSpec((1, HEAD_DIM), c2),
            pl.BlockSpec((1, HEAD_DIM), c2),
            pl.BlockSpec((TILE, CONV_COLS), tile),
            state, state,
        ],
        out_specs=[pl.BlockSpec((TILE, 2 * GROUP_W), tile), state, state],
        out_shape=[
            jax.ShapeDtypeStruct((pr.shape[0], 2 * GROUP_W), BF16),
            jax.ShapeDtypeStruct(sr.shape, F32),
            jax.ShapeDtypeStruct(sg.shape, F32),
        ],
        scratch_shapes=[pltpu.VMEM((HALO + TILE, CONV_COLS), F32)] + [head_scratch] * 4
                       + [pltpu.VMEM((TILE, HEAD_DIM), F32)] + [head_scratch] * 3,
        compiler_params=_cparams(("parallel",)),
        name="mixer_sample",
    )(pr, pg, pab, cosf, sins, rsc, dmat, cw, alog, dtb, gnw, convpad, sr, sg)


def _rotary_tables(pos):
    half = HEAD_DIM // 2
    inv = ROPE_BASE ** (-jnp.arange(half, dtype=F32) / half)
    ang = pos.astype(F32)[:, None] * inv[None, :]
    cos, sin = jnp.cos(ang), jnp.sin(ang)
    return jnp.concatenate([cos, cos], -1), jnp.concatenate([-sin, sin], -1)


def _retention_tables(seg, pos, valid, seg_len):
    log_gamma = jnp.log(1.0 - 2.0 ** (-5.0 - jnp.arange(N_HEADS, dtype=F32)))
    posf = pos.astype(F32)
    rel = posf[:, None] - posf[None, :]
    causal = (seg[:, None] == seg[None, :]) & (rel >= 0)
    dmat = jnp.where(causal[None], jnp.exp(log_gamma[:, None, None] * jnp.where(causal, rel, 0.0)[None]), 0.0)
    q_scale = jnp.exp(log_gamma[None, :] * (posf[:, None] + 1.0))
    k_scale = jnp.where(valid[:, None], jnp.exp(log_gamma[None, :] * (seg_len - 1.0 - posf[:, None])), 0.0)
    rsc = jnp.concatenate([q_scale, k_scale, jnp.zeros((TILE, HEAD_DIM - 2 * N_HEADS), F32)], -1)
    chunk_decay = tuple((1.0 - 2.0 ** (-5.0 - h)) ** seg_len for h in range(N_HEADS))
    return dmat, rsc, chunk_decay


def _pad_lanes(v):
    return jnp.pad(v.astype(F32), (0, HEAD_DIM - v.shape[0]))[None, :]


def kernel(x_prompt, x_sample, state_ret, state_gdn, state_conv, meta_tokens, emb_ln_g, emb_ln_b,
           w_in, conv_w, a_log, dt_bias, gdn_norm_w, w_out, ln1_g, ln1_b, w_gate_up, w_down,
           ln2_g, ln2_b):
    n_batch, seq, _ = x_prompt.shape
    dec_batch, dec_seq, _ = x_sample.shape
    assert seq % TILE == 0 and dec_seq == SAMPLE_TOKENS and N_META <= TILE
    n_chunks = seq // TILE
    layer = 0

    w_in_l = w_in[layer]
    w_ret = w_in_l[:, :4 * GROUP_W].astype(BF16)
    w_gdn = w_in_l[:, 4 * GROUP_W:8 * GROUP_W].astype(BF16)
    w_ab = jnp.pad(w_in_l[:, 8 * GROUP_W:], ((0, 0), (0, HEAD_DIM - 2 * N_HEADS))).astype(BF16)
    w_out_b = w_out[layer].astype(BF16)
    w_gu_b = w_gate_up[layer].astype(BF16)
    w_down_b = w_down[layer].astype(BF16)
    row = lambda v: v.astype(F32)[None, :]
    eg, eb = row(emb_ln_g), row(emb_ln_b)
    cw = conv_w[layer].astype(F32)
    alog, dtb, gnw = _pad_lanes(a_log[layer]), _pad_lanes(dt_bias[layer]), row(gdn_norm_w[layer])

    xp = x_prompt.reshape(n_batch * seq, D_MODEL)
    xm = jnp.pad(meta_tokens.astype(F32), ((0, TILE - N_META), (0, 0)))
    xs = jnp.pad(x_sample, ((0, 0), (SAMPLE_GROUP - dec_seq, 0), (0, 0))).reshape(
        dec_batch * SAMPLE_GROUP, D_MODEL)
    convpad = jnp.pad(state_conv[layer].astype(F32), ((0, 0), (1, SAMPLE_GROUP - CONV_W), (0, 0))).reshape(
        dec_batch * SAMPLE_GROUP, CONV_COLS)

    tile_idx = jnp.arange(TILE)
    cos_p, sin_p = _rotary_tables(N_META + jnp.arange(seq))
    cos_m, sin_m = _rotary_tables(tile_idx)
    tok = (tile_idx % SAMPLE_GROUP) - (SAMPLE_GROUP - dec_seq)
    cos_s, sin_s = _rotary_tables(PAST_LEN + jnp.maximum(tok, 0))
    zeros_i = jnp.zeros((TILE,), jnp.int32)
    all_valid = jnp.ones((TILE,), bool)
    dmat_p, rsc_p, cd_p = _retention_tables(zeros_i, tile_idx, all_valid, float(TILE))
    dmat_m, rsc_m, cd_m = _retention_tables(zeros_i, tile_idx, tile_idx < N_META, float(N_META))
    dmat_s, rsc_s, cd_s = _retention_tables(tile_idx // SAMPLE_GROUP, tok, tok >= 0, float(dec_seq))

    zero_state = jnp.zeros(_STATE_SHAPE, F32)
    pr_m, pg_m, pab_m = _front(xm, eg, eb, w_ret, w_gdn, w_ab, TILE)
    _, sr_m, sg_m = _meta_mixer(cd_m, pr_m, pg_m, pab_m, cos_m, sin_m, rsc_m, dmat_m, cw, alog, dtb, gnw,
                                zero_state, zero_state)
    halo_p = pg_m[N_META - HALO:N_META, :CONV_COLS]

    tm_front = 512
    w_ret_by_head = w_ret.reshape(D_MODEL, 4, N_HEADS, HEAD_DIM).transpose(0, 2, 1, 3).reshape(
        D_MODEL, 4 * GROUP_W)
    w_all = jnp.concatenate([w_ret_by_head, w_gdn, w_ab], axis=1)
    a_p, b_p, pab_p, tail_p = _front_prompt(xp, eg, eb, w_all, cos_p, sin_p,
                                            jnp.tile(rsc_p, (tm_front // TILE, 1)), cw, halo_p,
                                            n_batch, tm_front)
    per_seq = lambda t: t.reshape(n_batch, seq, t.shape[-1])
    mix_p, sr_p, sg_p = _prompt_mixer(cd_p, per_seq(a_p), per_seq(b_p), per_seq(pab_p), dmat_p, alog, dtb,
                                      gnw, sr_m[0], sg_m[0], MIXER_SEQS_PER_STEP)
    mix_p = mix_p.reshape(n_batch * seq, 2 * GROUP_W)

    pr_s, pg_s, pab_s = _front(xs, eg, eb, w_ret, w_gdn, w_ab, 512)
    mix_s, sr_s, sg_s = _sample_mixer(cd_s, pr_s, pg_s, pab_s, cos_s, sin_s, rsc_s, dmat_s, cw, alog,
                                      dtb, gnw, convpad, state_ret[layer].astype(F32),
                                      state_gdn[layer].astype(F32))

    back = functools.partial(_back, eg=eg, eb=eb, w_out=w_out_b, g1=row(ln1_g[layer]), b1=row(ln1_b[layer]),
                             w_gu=w_gu_b, w_down=w_down_b, g2=row(ln2_g[layer]), b2=row(ln2_b[layer]),
                             tm=512)
    y_p = back(xp, mix_p).reshape(n_batch, seq, D_MODEL)
    y_s = back(xs, mix_s).reshape(dec_batch, SAMPLE_GROUP, D_MODEL)[:, SAMPLE_GROUP - dec_seq:]

    conv_p = tail_p[:, HALO - (CONV_W - 1):, :]
    conv_s = pg_s.reshape(dec_batch, SAMPLE_GROUP, 4 * GROUP_W)[:, SAMPLE_GROUP - (CONV_W - 1):, :CONV_COLS]
    return (y_p, y_s, sr_p[None], sg_p[None], conv_p[None], sr_s[None], sg_s[None], conv_s[None])
```

```python
import functools

import jax
import jax.numpy as jnp
from jax import lax
from jax.experimental import pallas as pl
from jax.experimental.pallas import tpu as pltpu

F32 = jnp.float32
BF16 = jnp.bfloat16

D_MODEL = 1024
N_META = 16
N_HEADS = 4
HEAD_DIM = 128
GROUP_W = N_HEADS * HEAD_DIM
CONV_W = 4
CONV_COLS = 3 * GROUP_W
D_FF = 2816
PAST_LEN = 16384
ROPE_BASE = 10000.0
LN_EPS = 1e-5
RMS_EPS = 1e-6
ALPHA = 2.0 ** 0.25
TILE = 128
SAMPLE_GROUP = 8
SAMPLE_TOKENS = 4
HALO = 8
MIXER_SEQS_PER_STEP = 4
BACK_PARTS = 2
FRONT_LAG = 2
VMEM_LIMIT = 56 * 1024 * 1024


def _cparams(sem):
    return pltpu.CompilerParams(dimension_semantics=sem, vmem_limit_bytes=VMEM_LIMIT)


def _layer_norm(x, g, b):
    mu = jnp.mean(x, -1, keepdims=True)
    xc = x - mu
    var = jnp.mean(xc * xc, -1, keepdims=True)
    return xc * lax.rsqrt(var + LN_EPS) * g + b


def _sigmoid(x):
    return 1.0 / (1.0 + jnp.exp(-x))


def _silu(x):
    return x * _sigmoid(x)


def _softplus(x):
    return jnp.maximum(x, 0.0) + jnp.log(1.0 + jnp.exp(-jnp.abs(x)))


def _mm(a, b):
    return jnp.dot(a.astype(BF16), b.astype(BF16), preferred_element_type=F32)


def _mm_nt(a, b):
    return lax.dot_general(a.astype(BF16), b.astype(BF16), (((1,), (1,)), ((), ())),
                           preferred_element_type=F32)


def _mm_each(xs, ys):
    return [_mm(x, y) for x, y in zip(xs, ys)]


def _mm_split3(m01, x):
    x1 = x.astype(BF16)
    r1 = x - x1.astype(F32)
    x2 = r1.astype(BF16)
    x3 = (r1 - x2.astype(F32)).astype(BF16)
    dot = functools.partial(jnp.dot, preferred_element_type=F32)
    return (dot(m01, x3) + dot(m01, x2)) + dot(m01, x1)


def _lane_bcast(x, lane):
    return jnp.broadcast_to(x[:, lane:lane + 1], (x.shape[0], HEAD_DIM))


def _head_cols(x, base, h):
    return x[:, base + h * HEAD_DIM:base + (h + 1) * HEAD_DIM]


def _front_body(x_ref, g_ref, b_ref, w_ref, wab_ref, pr_ref, pg_ref, pab_ref):
    h = _layer_norm(x_ref[...], g_ref[...], b_ref[...]).astype(BF16)
    pr_ref[...] = jnp.dot(h, w_ref[:, :4 * GROUP_W], preferred_element_type=F32)
    pg_ref[...] = jnp.dot(h, w_ref[:, 4 * GROUP_W:], preferred_element_type=F32)
    pab_ref[...] = jnp.dot(h, wab_ref[...], preferred_element_type=F32)


def _front(x2d, ln_g, ln_b, w_main, w_ab, tm):
    rows = x2d.shape[0]
    const = lambda i: (0, 0)
    tile = lambda i: (i, 0)
    return pl.pallas_call(
        _front_body,
        grid=(rows // tm,),
        in_specs=[
            pl.BlockSpec((tm, D_MODEL), tile),
            pl.BlockSpec((1, D_MODEL), const),
            pl.BlockSpec((1, D_MODEL), const),
            pl.BlockSpec((D_MODEL, 8 * GROUP_W), const),
            pl.BlockSpec((D_MODEL, HEAD_DIM), const),
        ],
        out_specs=[
            pl.BlockSpec((tm, 4 * GROUP_W), tile),
            pl.BlockSpec((tm, 4 * GROUP_W), tile),
            pl.BlockSpec((tm, HEAD_DIM), tile),
        ],
        out_shape=[
            jax.ShapeDtypeStruct((rows, 4 * GROUP_W), F32),
            jax.ShapeDtypeStruct((rows, 4 * GROUP_W), F32),
            jax.ShapeDtypeStruct((rows, HEAD_DIM), F32),
        ],
        compiler_params=_cparams(("parallel",)),
        name="front",
    )(x2d, ln_g, ln_b, w_main, w_ab)


def _back_body(x_ref, mix_ref, eg_ref, eb_ref, wo_ref, g1_ref, b1_ref, wgu_ref, wd_ref,
               g2_ref, b2_ref, y_ref):
    part = x_ref.shape[0] // BACK_PARTS
    parts = [slice(i * part, (i + 1) * part) for i in range(BACK_PARTS)]
    dot = functools.partial(jnp.dot, preferred_element_type=F32)
    h = [_layer_norm(x_ref[p, :], eg_ref[...], eb_ref[...]) for p in parts]
    mp = [dot(mix_ref[p, :], wo_ref[...]) for p in parts]
    h1 = [_layer_norm(ALPHA * a + b, g1_ref[...], b1_ref[...]) for a, b in zip(h, mp)]
    gu = [dot(a.astype(BF16), wgu_ref[...]) for a in h1]
    act = [(_silu(a[:, :D_FF]) * a[:, D_FF:]).astype(BF16) for a in gu]
    ff = [dot(a, wd_ref[...]) for a in act]
    for p, a, b in zip(parts, h1, ff):
        y_ref[p, :] = _layer_norm(ALPHA * a + b, g2_ref[...], b2_ref[...])


def _back(x2d, mix, eg, eb, w_out, g1, b1, w_gu, w_down, g2, b2, tm):
    rows = x2d.shape[0]
    const = lambda i: (0, 0)
    tile = lambda i: (i, 0)
    single = pl.Buffered(1)
    vec = pl.BlockSpec((1, D_MODEL), const)
    return pl.pallas_call(
        _back_body,
        grid=(rows // tm,),
        in_specs=[
            pl.BlockSpec((tm, D_MODEL), tile),
            pl.BlockSpec((tm, D_MODEL), tile),
            vec, vec,
            pl.BlockSpec((D_MODEL, D_MODEL), const, pipeline_mode=single),
            vec, vec,
            pl.BlockSpec((D_MODEL, 2 * D_FF), const, pipeline_mode=single),
            pl.BlockSpec((D_FF, D_MODEL), const, pipeline_mode=single),
            vec, vec,
        ],
        out_specs=pl.BlockSpec((tm, D_MODEL), tile),
        out_shape=jax.ShapeDtypeStruct((rows, D_MODEL), F32),
        compiler_params=_cparams(("parallel",)),
        name="back",
    )(x2d, mix, eg, eb, w_out, g1, b1, w_gu, w_down, g2, b2)


def _iota2(shape, dim):
    return lax.broadcasted_iota(jnp.int32, shape, dim)


def _valid_rows(variant, shape):
    r = _iota2(shape, 0)
    if variant == "meta":
        return r < N_META
    if variant == "sample":
        return (r & (SAMPLE_GROUP - 1)) >= SAMPLE_GROUP - SAMPLE_TOKENS
    return None


def _mask_rows(valid, x):
    return x if valid is None else jnp.where(valid, x, 0.0)


def _tri_inverse(variant, nmats, row, col):
    eye = (row == col).astype(F32)
    if variant == "sample":
        n2 = _mm_each(nmats, nmats)
        ps = [eye + n for n in nmats]
        return [p + t for p, t in zip(ps, _mm_each(ps, n2))]
    base_log2 = 3
    in_block = (row >> base_log2) == (col >> base_log2)
    ds = [jnp.where(in_block, n, 0.0) for n in nmats]
    ps = [eye + d for d in ds]
    for _ in range(base_log2 - 1):
        ds = _mm_each(ds, ds)
        ps = [p + t for p, t in zip(ps, _mm_each(ps, ds))]
    live_rows = N_META if variant == "meta" else TILE
    s = base_log2
    while (1 << s) < live_rows:
        size = 1 << s
        lower_left = (((row >> (s + 1)) == (col >> (s + 1))) & (((row >> s) & 1) == 1)
                      & (((col >> s) & 1) == 0))
        cs = [jnp.where(lower_left, n, 0.0) for n in nmats]
        lower = [slice(start + size, start + 2 * size) for start in range(0, TILE, 2 * size)]
        upper = [slice(start, start + size) for start in range(0, TILE, 2 * size)]
        p_low = [jnp.concatenate([p[rows, :] for rows in lower], 0) for p in ps]
        fix = _mm_each(_mm_each(p_low, cs), ps)
        merged = []
        for p, f in zip(ps, fix):
            pieces = []
            for i, (up, lo) in enumerate(zip(upper, lower)):
                pieces += [p[up, :], p[lo, :] + f[i * size:(i + 1) * size, :]]
            merged.append(jnp.concatenate(pieces, 0))
        ps = merged
        s += 1
    return ps


def _rotary(t, cosf, sins):
    return t * cosf + pltpu.roll(t, HEAD_DIM // 2, 1) * sins


def _ret_head_ops(valid, q, k, v, gate, cosf, sins, q_scale, k_scale):
    rq = _mask_rows(valid, _rotary(q, cosf, sins))
    rk = _mask_rows(valid, _rotary(k, cosf, sins) * (HEAD_DIM ** -0.5))
    return dict(rq=rq, rk=rk, rv=_mask_rows(valid, v), qd=rq * q_scale, kd=rk * k_scale, sgr=_silu(gate))


def _short_conv(x, cw_ref, xc_ref, col0, row0=0):
    rows, ncols = x.shape
    cols = slice(col0, col0 + ncols)
    xc_ref[HALO + row0:HALO + row0 + rows, cols] = x
    conv = None
    for w in range(CONV_W):
        off = HALO + row0 - (CONV_W - 1) + w
        term = xc_ref[off:off + rows, cols] * cw_ref[w:w + 1, cols]
        conv = term if conv is None else conv + term
    return _silu(conv)


def _l2_normalize(t):
    return t * lax.rsqrt(jnp.sum(t * t, -1, keepdims=True) + RMS_EPS)


def _tokenwise(variant, pret, pgdn, cosf, sins, rsc, cw_ref, xc_ref, convpad):
    rows = pret.shape[0]
    heads = range(N_HEADS)
    valid = _valid_rows(variant, (rows, HEAD_DIM))
    ret = [_ret_head_ops(valid, *(_head_cols(pret, g * GROUP_W, h) for g in range(4)), cosf, sins,
                         _lane_bcast(rsc, h), _lane_bcast(rsc, N_HEADS + h)) for h in heads]
    x = pgdn[:, :CONV_COLS]
    if variant == "sample":
        x = jnp.where(_valid_rows(variant, x.shape), x, convpad)
    conv = _short_conv(x, cw_ref, xc_ref, 0)
    gq = [_mask_rows(valid, _l2_normalize(_head_cols(conv, 0, h)) * (HEAD_DIM ** -0.5)) for h in heads]
    gk = [_mask_rows(valid, _l2_normalize(_head_cols(conv, GROUP_W, h))) for h in heads]
    gv = [_mask_rows(valid, _head_cols(conv, 2 * GROUP_W, h)) for h in heads]
    sgz = [_silu(_head_cols(pgdn, 3 * GROUP_W, h)) for h in heads]
    tok = {key: [r[key] for r in ret] for key in ret[0]}
    tok.update(gq=gq, gk=gk, gv=gv, sgz=sgz)
    return tok


def _chunk_prep(variant, toks, pabs, dmat_ref, alog, dtb):
    shape = (TILE, HEAD_DIM)
    tiles = range(len(toks))
    items = [(j, h) for j in tiles for h in range(N_HEADS)]
    pick = lambda key: [toks[j][key][h] for j, h in items]
    row = _iota2(shape, 0)
    col = _iota2(shape, 1)
    valid = _valid_rows(variant, shape)
    if variant == "sample":
        same = (row >> 3) == (col >> 3)
        incl = same & (row >= col)
        strict = same & (row > col)
    else:
        incl = row >= col
        strict = row > col

    ret = dict(
        items=items,
        scores=[_mm_nt(toks[j]["rq"][h], toks[j]["rk"][h]) * dmat_ref[h] for j, h in items],
        qd=pick("qd"), kdT=pick("kdT"), v=pick("rv"), gate=pick("sgr"),
    )

    incl01 = incl.astype(BF16)
    beta_all, gcum, gam, ktail, cdr, gcum_t = [], [], [], [], [], []
    for j in tiles:
        g_all = _mask_rows(valid, -jnp.exp(alog) * _softplus(pabs[j] + dtb))
        beta_all.append(_mask_rows(valid, _sigmoid(pabs[j])))
        if variant == "sample":
            both = _mm_split3(jnp.concatenate([incl01, same.astype(BF16)], 0), g_all)
            gc, gseg = both[:TILE], both[TILE:]
        else:
            gc = _mm_split3(incl01, g_all)
            gseg = jnp.broadcast_to(gc[TILE - 1:TILE, :], shape)
        gcum.append(gc)
        gam.append(jnp.exp(gc))
        ktail.append(jnp.exp(gseg - gc))
        cdr.append(jnp.exp(gseg))
        gcum_t.append(gc.T)

    gq, gk, gv = pick("gq"), pick("gk"), pick("gv")
    n_items = range(len(items))
    dec = [jnp.where(incl, jnp.exp(jnp.minimum(
        _lane_bcast(gcum[j], h) - jnp.broadcast_to(gcum_t[j][h:h + 1, :], shape), 0.0)), 0.0)
        for j, h in items]
    bcol = [_lane_bcast(beta_all[j], N_HEADS + h) for j, h in items]
    gamc = [_lane_bcast(gam[j], h) for j, h in items]
    kk = [_mm_nt(gk[i], gk[i]) for i in n_items]
    qk = [_mm_nt(gq[i], gk[i]) for i in n_items]
    nmats = [-(jnp.where(strict, kk[i] * dec[i], 0.0) * bcol[i]) for i in n_items]
    tinv = _tri_inverse(variant, nmats, row, col)
    rhs = [jnp.concatenate([gv[i] * bcol[i], gk[i] * (bcol[i] * gamc[i])], 1) for i in n_items]
    sol = _mm_each(tinv, rhs)
    gdn = dict(
        items=items,
        wv=[t[:, :HEAD_DIM] for t in sol],
        wk=[t[:, HEAD_DIM:] for t in sol],
        attn=[qk[i] * dec[i] for i in n_items],
        qg=[gq[i] * gamc[i] for i in n_items],
        ktT=[(gk[i] * _lane_bcast(ktail[j], h)).T for i, (j, h) in enumerate(items)],
        cdr=cdr,
        gate=pick("sgz"),
    )
    return ret, gdn


def _rms_gate(o, gate, weight=None):
    o = o * lax.rsqrt(jnp.mean(o * o, -1, keepdims=True) + RMS_EPS)
    if weight is not None:
        o = o * weight
    return (o * gate).astype(BF16)


A_COLS = 5 * GROUP_W
B_COLS = 5 * GROUP_W
_STATE_SHAPE = (N_HEADS, HEAD_DIM, HEAD_DIM)


def _front_prompt_body(tm, x_ref, g_ref, b_ref, w_ref, cos_ref, sin_ref, rsc_ref,
                       cw_ref, halo0_ref, a_ref, b_out_ref, pab_ref, tail_ref, xc_ref):
    @pl.when(pl.program_id(1) == 0)
    def _():
        xc_ref[0:HALO, :] = halo0_ref[...]

    def put(ref, r, group, h, val):
        ref[r * TILE:(r + 1) * TILE,
            group * GROUP_W + h * HEAD_DIM:group * GROUP_W + (h + 1) * HEAD_DIM] = val.astype(ref.dtype)

    def finish_ret_head(h, r, val):
        rows = slice(r * TILE, (r + 1) * TILE)
        rsc = rsc_ref[rows, :]
        ops = _ret_head_ops(None, *(_head_cols(val, 0, g) for g in range(4)), cos_ref[rows, :],
                            sin_ref[rows, :], _lane_bcast(rsc, h), _lane_bcast(rsc, N_HEADS + h))
        for group, key in enumerate(("rq", "rk", "rv", "qd")):
            put(a_ref, r, group, h, ops[key])
        put(a_ref, r, 4, h, ops["kd"].T)
        put(b_out_ref, r, 4, h, ops["sgr"])

    def finish_conv_group(group, r, val):
        conv = _short_conv(val, cw_ref, xc_ref, group * GROUP_W, r * TILE)
        for h in range(N_HEADS):
            t = _head_cols(conv, 0, h)
            if group == 0:
                t = _l2_normalize(t) * (HEAD_DIM ** -0.5)
            elif group == 1:
                t = _l2_normalize(t)
            put(b_out_ref, r, group, h, t)

    def finish_gate(r, val):
        for h in range(N_HEADS):
            put(b_out_ref, r, 3, h, _silu(_head_cols(val, 0, h)))

    def finish_ab(r, val):
        pab_ref[r * TILE:(r + 1) * TILE, :] = val

    ret_group = lambda h: (h * GROUP_W, GROUP_W, functools.partial(finish_ret_head, h))
    gdn_group = lambda g: ((N_HEADS + g) * GROUP_W, GROUP_W, functools.partial(finish_conv_group, g))
    plan = [ret_group(0), gdn_group(0), ret_group(1), gdn_group(1), ret_group(2), gdn_group(2),
            ret_group(3), (7 * GROUP_W, GROUP_W, finish_gate), (8 * GROUP_W, HEAD_DIM, finish_ab)]
    pending = []
    for r in range(tm // TILE):
        hidden = _layer_norm(x_ref[r * TILE:(r + 1) * TILE, :], g_ref[...], b_ref[...]).astype(BF16)
        for col, width, finish in plan:
            val = jnp.dot(hidden, w_ref[:, col:col + width], preferred_element_type=F32)
            pending.append(functools.partial(finish, r, val))
            if len(pending) > FRONT_LAG:
                pending.pop(0)()
    for finish in pending:
        finish()
    tail = xc_ref[tm:tm + HALO, :]
    xc_ref[0:HALO, :] = tail
    tail_ref[0] = tail


def _front_prompt(x2d, ln_g, ln_b, w_all, cosf, sins, rsc, cw, halo0, n_batch, tm):
    rows = x2d.shape[0]
    tiles = rows // (n_batch * tm)
    c2 = lambda b, t: (0, 0)
    tile = lambda b, t: (b * tiles + t, 0)
    pos = lambda b, t: (t, 0)
    return pl.pallas_call(
        functools.partial(_front_prompt_body, tm),
        grid=(n_batch, tiles),
        in_specs=[
            pl.BlockSpec((tm, D_MODEL), tile),
            pl.BlockSpec((1, D_MODEL), c2),
            pl.BlockSpec((1, D_MODEL), c2),
            pl.BlockSpec((D_MODEL, 8 * GROUP_W + HEAD_DIM), c2, pipeline_mode=pl.Buffered(1)),
            pl.BlockSpec((tm, HEAD_DIM), pos),
            pl.BlockSpec((tm, HEAD_DIM), pos),
            pl.BlockSpec((tm, HEAD_DIM), c2),
            pl.BlockSpec((CONV_W, CONV_COLS), c2),
            pl.BlockSpec((HALO, CONV_COLS), c2),
        ],
        out_specs=[
            pl.BlockSpec((tm, A_COLS), tile),
            pl.BlockSpec((tm, B_COLS), tile),
            pl.BlockSpec((tm, HEAD_DIM), tile),
            pl.BlockSpec((1, HALO, CONV_COLS), lambda b, t: (b, 0, 0)),
        ],
        out_shape=[
            jax.ShapeDtypeStruct((rows, A_COLS), BF16),
            jax.ShapeDtypeStruct((rows, B_COLS), F32),
            jax.ShapeDtypeStruct((rows, HEAD_DIM), F32),
            jax.ShapeDtypeStruct((n_batch, HALO, CONV_COLS), F32),
        ],
        scratch_shapes=[pltpu.VMEM((HALO + tm, CONV_COLS), F32)],
        compiler_params=_cparams(("parallel", "arbitrary")),
        name="front_prompt",
    )(x2d, ln_g, ln_b, w_all, cosf, sins, rsc, cw, halo0)


def _seq_init(s0r_ref, s0g_ref, sr_ref, sg_ref):
    @pl.when(pl.program_id(1) == 0)
    def _():
        for j in range(sr_ref.shape[0]):
            sr_ref[j] = s0r_ref[...]
            sg_ref[j] = s0g_ref[...]


def _seq_step(ret_cd, ret, gdn, gnw, mix_ref, sr_out, sg_out, sr_ref, sg_ref):
    items = ret["items"]
    idx = range(len(items))
    s_r = [sr_ref[j, h] for j, h in items]
    s_g = [sg_ref[j, h] for j, h in items]
    bf = lambda t: t.astype(BF16)
    cat = lambda a, b, axis: jnp.concatenate([bf(a), bf(b)], axis)
    wk_s = _mm_each(gdn["wk"], s_g)
    o_r = _mm_each([cat(ret["scores"][i], ret["qd"][i], 1) for i in idx],
                   [cat(ret["v"][i], s_r[i], 0) for i in idx])
    u = [gdn["wv"][i] - wk_s[i] for i in idx]
    o_g = _mm_each([cat(gdn["qg"][i], gdn["attn"][i], 1) for i in idx],
                   [cat(s_g[i], u[i], 0) for i in idx])
    upd_g = _mm_each(gdn["ktT"], u)
    upd_r = _mm_each(ret["kdT"], ret["v"])
    for i, (j, h) in enumerate(items):
        cd = jnp.broadcast_to(gdn["cdr"][j][0:1, h:h + 1], (HEAD_DIM, HEAD_DIM))
        sg_ref[j, h] = cd * s_g[i] + upd_g[i]
        sr_ref[j, h] = ret_cd[h] * s_r[i] + upd_r[i]
    for i, (j, h) in enumerate(items):
        mix_ref[j, :, h * HEAD_DIM:(h + 1) * HEAD_DIM] = _rms_gate(o_r[i], ret["gate"][i])
        mix_ref[j, :, GROUP_W + h * HEAD_DIM:GROUP_W + (h + 1) * HEAD_DIM] = _rms_gate(
            o_g[i], gdn["gate"][i], gnw)

    @pl.when(pl.program_id(1) == pl.num_programs(1) - 1)
    def _():
        sr_out[...] = sr_ref[...]
        sg_out[...] = sg_ref[...]


def _meta_mixer_body(ret_cd, pr_ref, pg_ref, pab_ref, cos_ref, sin_ref, rsc_ref, dmat_ref,
                     cw_ref, alog_ref, dtb_ref, gnw_ref, s0r_ref, s0g_ref,
                     mix_ref, sr_out, sg_out, sr_ref, sg_ref, xc_ref):
    _seq_init(s0r_ref, s0g_ref, sr_ref, sg_ref)
    xc_ref[0:HALO, :] = jnp.zeros((HALO, CONV_COLS), F32)
    tok = _tokenwise("meta", pr_ref[...], pg_ref[...], cos_ref[...], sin_ref[...], rsc_ref[...],
                     cw_ref, xc_ref, None)
    tok["kdT"] = [t.T for t in tok["kd"]]
    ret, gdn = _chunk_prep("meta", [tok], [pab_ref[...]], dmat_ref, alog_ref[...], dtb_ref[...])
    _seq_step(ret_cd, ret, gdn, gnw_ref[...], mix_ref, sr_out, sg_out, sr_ref, sg_ref)


def _prompt_mixer_body(ret_cd, a_ref, b_ref, pab_ref, dmat_ref, alog_ref, dtb_ref, gnw_ref,
                       s0r_ref, s0g_ref, mix_ref, sr_out, sg_out, sr_ref, sg_ref):
    _seq_init(s0r_ref, s0g_ref, sr_ref, sg_ref)
    heads = range(N_HEADS)
    tiles = range(a_ref.shape[0])
    piece = lambda ref, j, group: [
        ref[j, :, group * GROUP_W + h * HEAD_DIM:group * GROUP_W + (h + 1) * HEAD_DIM] for h in heads]
    toks = [dict(rq=piece(a_ref, j, 0), rk=piece(a_ref, j, 1), rv=piece(a_ref, j, 2),
                 qd=piece(a_ref, j, 3), kdT=piece(a_ref, j, 4), gq=piece(b_ref, j, 0),
                 gk=piece(b_ref, j, 1), gv=piece(b_ref, j, 2), sgz=piece(b_ref, j, 3),
                 sgr=piece(b_ref, j, 4)) for j in tiles]
    ret, gdn = _chunk_prep("prompt", toks, [pab_ref[j] for j in tiles], dmat_ref, alog_ref[...],
                           dtb_ref[...])
    _seq_step(ret_cd, ret, gdn, gnw_ref[...], mix_ref, sr_out, sg_out, sr_ref, sg_ref)


def _seq_out(n_batch, n_chunks, nb):
    state = pl.BlockSpec((nb,) + _STATE_SHAPE, lambda b, n: (b, 0, 0, 0))
    out_specs = [pl.BlockSpec((nb, TILE, 2 * GROUP_W), lambda b, n: (b, n, 0)), state, state]
    out_shape = [
        jax.ShapeDtypeStruct((n_batch, n_chunks * TILE, 2 * GROUP_W), BF16),
        jax.ShapeDtypeStruct((n_batch,) + _STATE_SHAPE, F32),
        jax.ShapeDtypeStruct((n_batch,) + _STATE_SHAPE, F32),
    ]
    scratch = [pltpu.VMEM((nb,) + _STATE_SHAPE, F32), pltpu.VMEM((nb,) + _STATE_SHAPE, F32)]
    return out_specs, out_shape, scratch


def _meta_mixer(ret_cd, pr, pg, pab, cosf, sins, rsc, dmat, cw, alog, dtb, gnw, s0r, s0g):
    c2 = lambda b, n: (0, 0)
    c3 = lambda b, n: (0, 0, 0)
    out_specs, out_shape, state_scratch = _seq_out(1, 1, 1)
    return pl.pallas_call(
        functools.partial(_meta_mixer_body, ret_cd),
        grid=(1, 1),
        in_specs=[
            pl.BlockSpec((TILE, 4 * GROUP_W), c2),
            pl.BlockSpec((TILE, 4 * GROUP_W), c2),
            pl.BlockSpec((TILE, HEAD_DIM), c2),
            pl.BlockSpec((TILE, HEAD_DIM), c2),
            pl.BlockSpec((TILE, HEAD_DIM), c2),
            pl.BlockSpec((TILE, HEAD_DIM), c2),
            pl.BlockSpec((N_HEADS, TILE, TILE), c3),
            pl.BlockSpec((CONV_W, CONV_COLS), c2),
            pl.BlockSpec((1, HEAD_DIM), c2),
            pl.BlockSpec((1, HEAD_DIM), c2),
            pl.BlockSpec((1, HEAD_DIM), c2),
            pl.BlockSpec(_STATE_SHAPE, c3),
            pl.BlockSpec(_STATE_SHAPE, c3),
        ],
        out_specs=out_specs,
        out_shape=out_shape,
        scratch_shapes=state_scratch + [pltpu.VMEM((HALO + TILE, CONV_COLS), F32)],
        compiler_params=_cparams(("parallel", "arbitrary")),
        name="mixer_meta",
    )(pr, pg, pab, cosf, sins, rsc, dmat, cw, alog, dtb, gnw, s0r, s0g)


def _prompt_mixer(ret_cd, a, b, pab, dmat, alog, dtb, gnw, s0r, s0g, nb):
    n_batch, seq, _ = a.shape
    n_chunks = seq // TILE
    tile = lambda b, n: (b, n, 0)
    c2 = lambda b, n: (0, 0)
    c3 = lambda b, n: (0, 0, 0)
    out_specs, out_shape, state_scratch = _seq_out(n_batch, n_chunks, nb)
    return pl.pallas_call(
        functools.partial(_prompt_mixer_body, ret_cd),
        grid=(n_batch // nb, n_chunks),
        in_specs=[
            pl.BlockSpec((nb, TILE, A_COLS), tile),
            pl.BlockSpec((nb, TILE, B_COLS), tile),
            pl.BlockSpec((nb, TILE, HEAD_DIM), tile),
            pl.BlockSpec((N_HEADS, TILE, TILE), c3),
            pl.BlockSpec((1, HEAD_DIM), c2),
            pl.BlockSpec((1, HEAD_DIM), c2),
            pl.BlockSpec((1, HEAD_DIM), c2),
            pl.BlockSpec(_STATE_SHAPE, c3),
            pl.BlockSpec(_STATE_SHAPE, c3),
        ],
        out_specs=out_specs,
        out_shape=out_shape,
        scratch_shapes=state_scratch,
        compiler_params=_cparams(("parallel", "arbitrary")),
        name="mixer_prompt",
    )(a, b, pab, dmat, alog, dtb, gnw, s0r, s0g)


SAMPLE_PER_TILE = TILE // SAMPLE_GROUP


def _stack_by_group(xt):
    shape3 = (SAMPLE_PER_TILE, HEAD_DIM, TILE)
    keep = lax.broadcasted_iota(jnp.int32, shape3, 0) == (lax.broadcasted_iota(jnp.int32, shape3, 2) >> 3)
    stacked = jnp.where(keep, jnp.broadcast_to(xt[None], shape3), 0.0)
    return stacked.reshape(SAMPLE_PER_TILE * HEAD_DIM, TILE)


def _sample_mixer_body(ret_cd, pr_ref, pg_ref, pab_ref, cos_ref, sin_ref, rsc_ref, dmat_ref, cw_ref,
                       alog_ref, dtb_ref, gnw_ref, convpad_ref, sr_in, sg_in,
                       mix_ref, sr_out, sg_out,
                       xc_ref, qd_s, wk_s, qg_s, wv_s, cd_s, inter_s, u_s, qs_s):
    heads = range(N_HEADS)
    xc_ref[0:HALO, :] = jnp.zeros((HALO, CONV_COLS), F32)
    tok = _tokenwise("sample", pr_ref[...], pg_ref[...], cos_ref[...], sin_ref[...], rsc_ref[...],
                     cw_ref, xc_ref, convpad_ref[...])
    tok["kdT"] = [t.T for t in tok["kd"]]
    ret, gdn = _chunk_prep("sample", [tok], [pab_ref[...]], dmat_ref, alog_ref[...], dtb_ref[...])
    for h in heads:
        qd_s[h] = ret["qd"][h]
        wk_s[h] = gdn["wk"][h]
        qg_s[h] = gdn["qg"][h]
        wv_s[h] = gdn["wv"][h]
    cd_s[...] = gdn["cdr"][0]

    def per_batch(b, carry):
        rows = pl.ds(pl.multiple_of(b * SAMPLE_GROUP, SAMPLE_GROUP), SAMPLE_GROUP)
        cd_rows = cd_s[rows, :]
        for h in heads:
            inter_s[h, rows, :] = _mm(qd_s[h, rows, :], sr_in[b, h])
            s = sg_in[b, h]
            both = _mm(jnp.concatenate([wk_s[h, rows, :], qg_s[h, rows, :]], 0), s)
            u_s[h, rows, :] = wv_s[h, rows, :] - both[:SAMPLE_GROUP]
            qs_s[h, rows, :] = both[SAMPLE_GROUP:]
            cd = jnp.broadcast_to(cd_rows[SAMPLE_GROUP - 1:SAMPLE_GROUP, h:h + 1], (HEAD_DIM, HEAD_DIM))
            sg_out[b, h] = cd * s
        return carry

    lax.fori_loop(0, SAMPLE_PER_TILE, per_batch, 0)

    out_rows = SAMPLE_PER_TILE * SAMPLE_TOKENS
    out_row = _iota2((out_rows, TILE), 0)
    token_row = ((out_row >> 2) << 3) + (SAMPLE_GROUP - SAMPLE_TOKENS) + (out_row & (SAMPLE_TOKENS - 1))
    select = (_iota2((out_rows, TILE), 1) == token_row).astype(BF16)
    compact = lambda t: jnp.dot(select, t, preferred_element_type=F32).astype(BF16)

    state_shape = (SAMPLE_PER_TILE, HEAD_DIM, HEAD_DIM)
    for h in heads:
        o = _mm(ret["scores"][h], ret["v"][h]) + inter_s[h]
        upd = _mm(_stack_by_group(ret["kdT"][h]), ret["v"][h]).reshape(state_shape)
        sr_out[:, h] = ret_cd[h] * sr_in[:, h] + upd
        mix_ref[:, h * HEAD_DIM:(h + 1) * HEAD_DIM] = compact(_rms_gate(o, ret["gate"][h]))

    for h in heads:
        u = u_s[h]
        o = qs_s[h] + _mm(gdn["attn"][h], u)
        upd = _mm(_stack_by_group(gdn["ktT"][h]), u).reshape(state_shape)
        sg_out[:, h] = sg_out[:, h] + upd
        mix_ref[:, GROUP_W + h * HEAD_DIM:GROUP_W + (h + 1) * HEAD_DIM] = compact(_rms_gate(
            o, gdn["gate"][h], gnw_ref[...]))


def _sample_mixer(ret_cd, pr, pg, pab, cosf, sins, rsc, dmat, cw, alog, dtb, gnw, convpad, sr, sg):
    n_tiles = pr.shape[0] // TILE
    tile = lambda i: (i, 0)
    c2 = lambda i: (0, 0)
    c3 = lambda i: (0, 0, 0)
    state = pl.BlockSpec((SAMPLE_PER_TILE,) + _STATE_SHAPE, lambda i: (i, 0, 0, 0))
    head_scratch = pltpu.VMEM((N_HEADS, TILE, HEAD_DIM), F32)
    return pl.pallas_call(
        functools.partial(_sample_mixer_body, ret_cd),
        grid=(n_tiles,),
        in_specs=[
            pl.BlockSpec((TILE, 4 * GROUP_W), tile),
            pl.BlockSpec((TILE, 4 * GROUP_W), tile),
            pl.BlockSpec((TILE, HEAD_DIM), tile),
            pl.BlockSpec((TILE, HEAD_DIM), c2),
            pl.BlockSpec((TILE, HEAD_DIM), c2),
            pl.BlockSpec((TILE, HEAD_DIM), c2),
            pl.BlockSpec((N_HEADS, TILE, TILE), c3),
            pl.BlockSpec((CONV_W, CONV_COLS), c2),
            pl.BlockSpec((1, HEAD_DIM), c2),
            pl.BlockSpec((1, HEAD_DIM), c2),
            pl.BlockSpec((1, HEAD_DIM), c2),
            pl.BlockSpec((TILE, CONV_COLS), tile),
            state, state,
        ],
        out_specs=[pl.BlockSpec((SAMPLE_PER_TILE * SAMPLE_TOKENS, 2 * GROUP_W), tile), state, state],
        out_shape=[
            jax.ShapeDtypeStruct((n_tiles * SAMPLE_PER_TILE * SAMPLE_TOKENS, 2 * GROUP_W), BF16),
            jax.ShapeDtypeStruct(sr.shape, F32),
            jax.ShapeDtypeStruct(sg.shape, F32),
        ],
        scratch_shapes=[pltpu.VMEM((HALO + TILE, CONV_COLS), F32)] + [head_scratch] * 4
                       + [pltpu.VMEM((TILE, HEAD_DIM), F32)] + [head_scratch] * 3,
        compiler_params=_cparams(("parallel",)),
        name="mixer_sample",
    )(pr, pg, pab, cosf, sins, rsc, dmat, cw, alog, dtb, gnw, convpad, sr, sg)


def _rotary_tables(pos):
    half = HEAD_DIM // 2
    inv = ROPE_BASE ** (-jnp.arange(half, dtype=F32) / half)
    ang = pos.astype(F32)[:, None] * inv[None, :]
    cos, sin = jnp.cos(ang), jnp.sin(ang)
    return jnp.concatenate([cos, cos], -1), jnp.concatenate([-sin, sin], -1)


def _retention_tables(seg, pos, valid, seg_len):
    log_gamma = jnp.log(1.0 - 2.0 ** (-5.0 - jnp.arange(N_HEADS, dtype=F32)))
    posf = pos.astype(F32)
    rel = posf[:, None] - posf[None, :]
    causal = (seg[:, None] == seg[None, :]) & (rel >= 0)
    dmat = jnp.where(causal[None], jnp.exp(log_gamma[:, None, None] * jnp.where(causal, rel, 0.0)[None]), 0.0)
    q_scale = jnp.exp(log_gamma[None, :] * (posf[:, None] + 1.0))
    k_scale = jnp.where(valid[:, None], jnp.exp(log_gamma[None, :] * (seg_len - 1.0 - posf[:, None])), 0.0)
    rsc = jnp.concatenate([q_scale, k_scale, jnp.zeros((TILE, HEAD_DIM - 2 * N_HEADS), F32)], -1)
    chunk_decay = tuple((1.0 - 2.0 ** (-5.0 - h)) ** seg_len for h in range(N_HEADS))
    return dmat, rsc, chunk_decay


def _pad_lanes(v):
    return jnp.pad(v.astype(F32), (0, HEAD_DIM - v.shape[0]))[None, :]


def kernel(x_prompt, x_sample, state_ret, state_gdn, state_conv, meta_tokens, emb_ln_g, emb_ln_b,
           w_in, conv_w, a_log, dt_bias, gdn_norm_w, w_out, ln1_g, ln1_b, w_gate_up, w_down,
           ln2_g, ln2_b):
    n_batch, seq, _ = x_prompt.shape
    dec_batch, dec_seq, _ = x_sample.shape
    assert seq % TILE == 0 and dec_seq == SAMPLE_TOKENS and N_META <= TILE
    n_chunks = seq // TILE
    layer = 0

    w_in_l = w_in[layer]
    w_main = w_in_l[:, :8 * GROUP_W].astype(BF16)
    w_ab =jnp.pad(w_in_l[:, 8 * GROUP_W:], ((0, 0), (0, HEAD_DIM - 2 * N_HEADS))).astype(BF16)
    w_out_b = w_out[layer].astype(BF16)
    w_gu_b = w_gate_up[layer].astype(BF16)
    w_down_b = w_down[layer].astype(BF16)
    row = lambda v: v.astype(F32)[None, :]
    eg, eb = row(emb_ln_g), row(emb_ln_b)
    cw = conv_w[layer].astype(F32)
    alog, dtb, gnw = _pad_lanes(a_log[layer]), _pad_lanes(dt_bias[layer]), row(gdn_norm_w[layer])

    xp = x_prompt.reshape(n_batch * seq, D_MODEL)
    xm = jnp.pad(meta_tokens.astype(F32), ((0, TILE - N_META), (0, 0)))
    xs = jnp.pad(x_sample, ((0, 0), (SAMPLE_GROUP - dec_seq, 0), (0, 0))).reshape(
        dec_batch * SAMPLE_GROUP, D_MODEL)
    convpad = jnp.pad(state_conv[layer].astype(F32), ((0, 0), (1, SAMPLE_GROUP - CONV_W), (0, 0))).reshape(
        dec_batch * SAMPLE_GROUP, CONV_COLS)

    tile_idx = jnp.arange(TILE)
    cos_p, sin_p = _rotary_tables(N_META + jnp.arange(seq))
    cos_m, sin_m = _rotary_tables(tile_idx)
    tok = (tile_idx % SAMPLE_GROUP) - (SAMPLE_GROUP - dec_seq)
    cos_s, sin_s = _rotary_tables(PAST_LEN + jnp.maximum(tok, 0))
    zeros_i = jnp.zeros((TILE,), jnp.int32)
    all_valid = jnp.ones((TILE,), bool)
    dmat_p, rsc_p, cd_p = _retention_tables(zeros_i, tile_idx, all_valid, float(TILE))
    dmat_m, rsc_m, cd_m = _retention_tables(zeros_i, tile_idx, tile_idx < N_META, float(N_META))
    dmat_s, rsc_s, cd_s = _retention_tables(tile_idx // SAMPLE_GROUP, tok, tok >= 0, float(dec_seq))

    zero_state = jnp.zeros(_STATE_SHAPE, F32)
    pr_m, pg_m, pab_m = _front(xm, eg, eb, w_main, w_ab, TILE)
    _, sr_m, sg_m = _meta_mixer(cd_m, pr_m, pg_m, pab_m, cos_m, sin_m, rsc_m, dmat_m, cw, alog, dtb, gnw,
                                zero_state, zero_state)
    halo_p = pg_m[N_META - HALO:N_META, :CONV_COLS]

    tm_front = 512
    w_ret_by_head = w_main[:, :4 * GROUP_W].reshape(D_MODEL, 4, N_HEADS, HEAD_DIM).transpose(
        0, 2, 1, 3).reshape(D_MODEL, 4 * GROUP_W)
    w_all = jnp.concatenate([w_ret_by_head, w_main[:, 4 * GROUP_W:], w_ab], axis=1)
    a_p, b_p, pab_p, tail_p = _front_prompt(xp, eg, eb, w_all, cos_p, sin_p,
                                            jnp.tile(rsc_p, (tm_front // TILE, 1)), cw, halo_p,
                                            n_batch, tm_front)
    per_seq = lambda t: t.reshape(n_batch, seq, t.shape[-1])
    mix_p, sr_p, sg_p = _prompt_mixer(cd_p, per_seq(a_p), per_seq(b_p), per_seq(pab_p), dmat_p, alog, dtb,
                                      gnw, sr_m[0], sg_m[0], MIXER_SEQS_PER_STEP)
    mix_p = mix_p.reshape(n_batch * seq, 2 * GROUP_W)

    pr_s, pg_s, pab_s = _front(xs, eg, eb, w_main, w_ab, 512)
    mix_s, sr_s, sg_s = _sample_mixer(cd_s, pr_s, pg_s, pab_s, cos_s, sin_s, rsc_s, dmat_s, cw, alog,
                                      dtb, gnw, convpad, state_ret[layer].astype(F32),
                                      state_gdn[layer].astype(F32))

    back = functools.partial(_back, eg=eg, eb=eb, w_out=w_out_b, g1=row(ln1_g[layer]), b1=row(ln1_b[layer]),
                             w_gu=w_gu_b, w_down=w_down_b, g2=row(ln2_g[layer]), b2=row(ln2_b[layer]),
                             tm=512)
    y_p = back(xp, mix_p).reshape(n_batch, seq, D_MODEL)
    y_s = back(x_sample.reshape(dec_batch * dec_seq, D_MODEL), mix_s).reshape(dec_batch, dec_seq, D_MODEL)

    conv_p = tail_p[:, HALO - (CONV_W - 1):, :]
    conv_s = pg_s.reshape(dec_batch, SAMPLE_GROUP, 4 * GROUP_W)[:, SAMPLE_GROUP - (CONV_W - 1):, :CONV_COLS]
    return (y_p, y_s, sr_p[None], sg_p[None], conv_p[None], sr_s[None], sg_s[None], conv_s[None])
```

```python
import functools

import numpy as np
import jax
import jax.numpy as jnp
from jax import lax
from jax.experimental import pallas as pl
from jax.experimental.pallas import tpu as pltpu

F32 = jnp.float32
BF16 = jnp.bfloat16

D_MODEL = 1024
N_META = 16
N_HEADS = 4
HEAD_DIM = 128
GROUP_W = N_HEADS * HEAD_DIM
CONV_W = 4
CONV_COLS = 3 * GROUP_W
D_FF = 2816
PAST_LEN = 16384
ROPE_BASE = 10000.0
LN_EPS = 1e-5
RMS_EPS = 1e-6
ALPHA = 2.0 ** 0.25
TILE = 128
SAMPLE_GROUP = 8
SAMPLE_TOKENS = 4
HALO = 8
MIXER_SEQS_PER_STEP = 4
BACK_PARTS = 2
FRONT_LAG = 2
VMEM_LIMIT = 56 * 1024 * 1024


def _cparams(sem):
    return pltpu.CompilerParams(dimension_semantics=sem, vmem_limit_bytes=VMEM_LIMIT)


def _layer_norm(x, g, b):
    mu = jnp.mean(x, -1, keepdims=True)
    xc = x - mu
    var = jnp.mean(xc * xc, -1, keepdims=True)
    return xc * lax.rsqrt(var + LN_EPS) * g + b


def _sigmoid(x):
    return 1.0 / (1.0 + jnp.exp(-x))


def _silu(x):
    return x * _sigmoid(x)


def _softplus(x):
    return jnp.maximum(x, 0.0) + jnp.log(1.0 + jnp.exp(-jnp.abs(x)))


def _mm(a, b):
    return jnp.dot(a.astype(BF16), b.astype(BF16), preferred_element_type=F32)


def _mm_nt(a, b):
    return lax.dot_general(a.astype(BF16), b.astype(BF16), (((1,), (1,)), ((), ())),
                           preferred_element_type=F32)


def _mm_each(xs, ys):
    return [_mm(x, y) for x, y in zip(xs, ys)]


def _mm_split3(m01, x):
    x1 = x.astype(BF16)
    r1 = x - x1.astype(F32)
    x2 = r1.astype(BF16)
    x3 = (r1 - x2.astype(F32)).astype(BF16)
    dot = functools.partial(jnp.dot, preferred_element_type=F32)
    return (dot(m01, x3) + dot(m01, x2)) + dot(m01, x1)


def _lane_bcast(x, lane):
    return jnp.broadcast_to(x[:, lane:lane + 1], (x.shape[0], HEAD_DIM))


def _head_cols(x, base, h):
    return x[:, base + h * HEAD_DIM:base + (h + 1) * HEAD_DIM]


W_ALL_COLS = 9 * GROUP_W


def _weight_prep_body(q_ref, k_ref, v_ref, g_ref, wab_ref, o_ref):
    j = pl.program_id(0)

    @pl.when(j < 2 * N_HEADS)
    def _():
        for piece, ref in enumerate((q_ref, k_ref, v_ref, g_ref)):
            o_ref[:, piece * HEAD_DIM:(piece + 1) * HEAD_DIM] = ref[...].astype(BF16)

    @pl.when(j == 2 * N_HEADS)
    def _():
        o_ref[:, :HEAD_DIM] = wab_ref[...]
        o_ref[:, HEAD_DIM:] = jnp.zeros((D_MODEL, GROUP_W - HEAD_DIM), BF16)


def _weight_prep(w_in_l, w_ab):
    def piece_spec(piece):
        def index(j):
            ret_block = piece * N_HEADS + j
            gdn_block = N_HEADS * j + piece
            return (0, jnp.where(j < N_HEADS, ret_block, jnp.where(j < 2 * N_HEADS, gdn_block, 0)))
        return pl.BlockSpec((D_MODEL, HEAD_DIM), index)
    return pl.pallas_call(
        _weight_prep_body,
        grid=(2 * N_HEADS + 1,),
        in_specs=[piece_spec(p) for p in range(4)] + [pl.BlockSpec((D_MODEL, HEAD_DIM), lambda j: (0, 0))],
        out_specs=pl.BlockSpec((D_MODEL, GROUP_W), lambda j: (0, j)),
        out_shape=jax.ShapeDtypeStruct((D_MODEL, W_ALL_COLS), BF16),
        compiler_params=_cparams(("arbitrary",)),
        name="weight_prep",
    )(w_in_l, w_in_l, w_in_l, w_in_l, w_ab)


def _front_body(x_ref, g_ref, b_ref, w_ref, pr_ref, pg_ref, pab_ref):
    h = _layer_norm(x_ref[...], g_ref[...], b_ref[...]).astype(BF16)
    pr_ref[...] = jnp.dot(h, w_ref[:, :4 * GROUP_W], preferred_element_type=F32)
    pg_ref[...] = jnp.dot(h, w_ref[:, 4 * GROUP_W:8 * GROUP_W], preferred_element_type=F32)
    pab_ref[...] = jnp.dot(h, w_ref[:, 8 * GROUP_W:8 * GROUP_W + HEAD_DIM], preferred_element_type=F32)


def _front(x2d, ln_g, ln_b, w_all, tm):
    rows = x2d.shape[0]
    const = lambda i: (0, 0)
    tile = lambda i: (i, 0)
    return pl.pallas_call(
        _front_body,
        grid=(rows // tm,),
        in_specs=[
            pl.BlockSpec((tm, D_MODEL), tile),
            pl.BlockSpec((1, D_MODEL), const),
            pl.BlockSpec((1, D_MODEL), const),
            pl.BlockSpec((D_MODEL, W_ALL_COLS), const),
        ],
        out_specs=[
            pl.BlockSpec((tm, 4 * GROUP_W), tile),
            pl.BlockSpec((tm, 4 * GROUP_W), tile),
            pl.BlockSpec((tm, HEAD_DIM), tile),
        ],
        out_shape=[
            jax.ShapeDtypeStruct((rows, 4 * GROUP_W), F32),
            jax.ShapeDtypeStruct((rows, 4 * GROUP_W), F32),
            jax.ShapeDtypeStruct((rows, HEAD_DIM), F32),
        ],
        compiler_params=_cparams(("parallel",)),
        name="front",
    )(x2d, ln_g, ln_b, w_all)


def _back_body(x_ref, mix_ref, eg_ref, eb_ref, wo_ref, g1_ref, b1_ref, wgu_ref, wd_ref,
               g2_ref, b2_ref, y_ref):
    part = x_ref.shape[0] // BACK_PARTS
    parts = [slice(i * part, (i + 1) * part) for i in range(BACK_PARTS)]
    dot = functools.partial(jnp.dot, preferred_element_type=F32)
    h = [_layer_norm(x_ref[p, :], eg_ref[...], eb_ref[...]) for p in parts]
    mp = [dot(mix_ref[p, :], wo_ref[...]) for p in parts]
    h1 = [_layer_norm(ALPHA * a + b, g1_ref[...], b1_ref[...]) for a, b in zip(h, mp)]
    gu = [dot(a.astype(BF16), wgu_ref[...]) for a in h1]
    act = [(_silu(a[:, :D_FF]) * a[:, D_FF:]).astype(BF16) for a in gu]
    ff = [dot(a, wd_ref[...]) for a in act]
    for p, a, b in zip(parts, h1, ff):
        y_ref[p, :] = _layer_norm(ALPHA * a + b, g2_ref[...], b2_ref[...])


def _back(x2d, mix, eg, eb, w_out, g1, b1, w_gu, w_down, g2, b2, tm):
    rows = x2d.shape[0]
    const = lambda i: (0, 0)
    tile = lambda i: (i, 0)
    single = pl.Buffered(1)
    vec = pl.BlockSpec((1, D_MODEL), const)
    return pl.pallas_call(
        _back_body,
        grid=(rows // tm,),
        in_specs=[
            pl.BlockSpec((tm, D_MODEL), tile),
            pl.BlockSpec((tm, D_MODEL), tile),
            vec, vec,
            pl.BlockSpec((D_MODEL, D_MODEL), const, pipeline_mode=single),
            vec, vec,
            pl.BlockSpec((D_MODEL, 2 * D_FF), const, pipeline_mode=single),
            pl.BlockSpec((D_FF, D_MODEL), const, pipeline_mode=single),
            vec, vec,
        ],
        out_specs=pl.BlockSpec((tm, D_MODEL), tile),
        out_shape=jax.ShapeDtypeStruct((rows, D_MODEL), F32),
        compiler_params=_cparams(("parallel",)),
        name="back",
    )(x2d, mix, eg, eb, w_out, g1, b1, w_gu, w_down, g2, b2)


def _iota2(shape, dim):
    return lax.broadcasted_iota(jnp.int32, shape, dim)


def _valid_rows(variant, shape):
    r = _iota2(shape, 0)
    if variant == "meta":
        return r < N_META
    if variant == "sample":
        return (r & (SAMPLE_GROUP - 1)) >= SAMPLE_GROUP - SAMPLE_TOKENS
    return None


def _mask_rows(valid, x):
    return x if valid is None else jnp.where(valid, x, 0.0)


def _tri_inverse(variant, nmats, row, col):
    eye = (row == col).astype(F32)
    if variant == "sample":
        n2 = _mm_each(nmats, nmats)
        ps = [eye + n for n in nmats]
        return [p + t for p, t in zip(ps, _mm_each(ps, n2))]
    base_log2 = 3
    in_block = (row >> base_log2) == (col >> base_log2)
    ds = [jnp.where(in_block, n, 0.0) for n in nmats]
    ps = [eye + d for d in ds]
    for _ in range(base_log2 - 1):
        ds = _mm_each(ds, ds)
        ps = [p + t for p, t in zip(ps, _mm_each(ps, ds))]
    live_rows = N_META if variant == "meta" else TILE
    s = base_log2
    while (1 << s) < live_rows:
        size = 1 << s
        lower_left = (((row >> (s + 1)) == (col >> (s + 1))) & (((row >> s) & 1) == 1)
                      & (((col >> s) & 1) == 0))
        cs = [jnp.where(lower_left, n, 0.0) for n in nmats]
        lower = [slice(start + size, start + 2 * size) for start in range(0, TILE, 2 * size)]
        upper = [slice(start, start + size) for start in range(0, TILE, 2 * size)]
        p_low = [jnp.concatenate([p[rows, :] for rows in lower], 0) for p in ps]
        fix = _mm_each(_mm_each(p_low, cs), ps)
        merged = []
        for p, f in zip(ps, fix):
            pieces = []
            for i, (up, lo) in enumerate(zip(upper, lower)):
                pieces += [p[up, :], p[lo, :] + f[i * size:(i + 1) * size, :]]
            merged.append(jnp.concatenate(pieces, 0))
        ps = merged
        s += 1
    return ps


def _rotary(t, cosf, sins):
    return t * cosf + pltpu.roll(t, HEAD_DIM // 2, 1) * sins


def _ret_head_ops(valid, q, k, v, gate, cosf, sins, q_scale, k_scale):
    rq = _mask_rows(valid, _rotary(q, cosf, sins))
    rk = _mask_rows(valid, _rotary(k, cosf, sins) * (HEAD_DIM ** -0.5))
    return dict(rq=rq, rk=rk, rv=_mask_rows(valid, v), qd=rq * q_scale, kd=rk * k_scale, sgr=_silu(gate))


def _short_conv(x, cw_ref, xc_ref, col0, row0=0):
    rows, ncols = x.shape
    cols = slice(col0, col0 + ncols)
    xc_ref[HALO + row0:HALO + row0 + rows, cols] = x
    conv = None
    for w in range(CONV_W):
        off = HALO + row0 - (CONV_W - 1) + w
        term = xc_ref[off:off + rows, cols] * cw_ref[w:w + 1, cols]
        conv = term if conv is None else conv + term
    return _silu(conv)


def _l2_normalize(t):
    return t * lax.rsqrt(jnp.sum(t * t, -1, keepdims=True) + RMS_EPS)


def _tokenwise(variant, pret, pgdn, cosf, sins, rsc, cw_ref, xc_ref, convpad):
    rows = pret.shape[0]
    heads = range(N_HEADS)
    valid = _valid_rows(variant, (rows, HEAD_DIM))
    ret = [_ret_head_ops(valid, *(_head_cols(pret, h * GROUP_W, g) for g in range(4)), cosf, sins,
                         _lane_bcast(rsc, h), _lane_bcast(rsc, N_HEADS + h)) for h in heads]
    x = pgdn[:, :CONV_COLS]
    if variant == "sample":
        x = jnp.where(_valid_rows(variant, x.shape), x, convpad)
    conv = _short_conv(x, cw_ref, xc_ref, 0)
    gq = [_mask_rows(valid, _l2_normalize(_head_cols(conv, 0, h)) * (HEAD_DIM ** -0.5)) for h in heads]
    gk = [_mask_rows(valid, _l2_normalize(_head_cols(conv, GROUP_W, h))) for h in heads]
    gv = [_mask_rows(valid, _head_cols(conv, 2 * GROUP_W, h)) for h in heads]
    sgz = [_silu(_head_cols(pgdn, 3 * GROUP_W, h)) for h in heads]
    tok = {key: [r[key] for r in ret] for key in ret[0]}
    tok.update(gq=gq, gk=gk, gv=gv, sgz=sgz)
    return tok


def _chunk_prep(variant, toks, pabs, dmat_ref, alog, dtb):
    shape = (TILE, HEAD_DIM)
    tiles = range(len(toks))
    items = [(j, h) for j in tiles for h in range(N_HEADS)]
    pick = lambda key: [toks[j][key][h] for j, h in items]
    row = _iota2(shape, 0)
    col = _iota2(shape, 1)
    valid = _valid_rows(variant, shape)
    if variant == "sample":
        same = (row >> 3) == (col >> 3)
        incl = same & (row >= col)
        strict = same & (row > col)
    else:
        incl = row >= col
        strict = row > col

    ret = dict(
        items=items,
        scores=[_mm_nt(toks[j]["rq"][h], toks[j]["rk"][h]) * dmat_ref[h] for j, h in items],
        qd=pick("qd"), kdT=pick("kdT"), v=pick("rv"), gate=pick("sgr"),
    )

    incl01 = incl.astype(BF16)
    beta_all, gcum, gam, ktail, cdr, gcum_t = [], [], [], [], [], []
    for j in tiles:
        g_all = _mask_rows(valid, -jnp.exp(alog) * _softplus(pabs[j] + dtb))
        beta_all.append(_mask_rows(valid, _sigmoid(pabs[j])))
        if variant == "sample":
            both = _mm_split3(jnp.concatenate([incl01, same.astype(BF16)], 0), g_all)
            gc, gseg = both[:TILE], both[TILE:]
        else:
            gc = _mm_split3(incl01, g_all)
            gseg = jnp.broadcast_to(gc[TILE - 1:TILE, :], shape)
        gcum.append(gc)
        gam.append(jnp.exp(gc))
        ktail.append(jnp.exp(gseg - gc))
        cdr.append(jnp.exp(gseg))
        gcum_t.append(gc.T)

    gq, gk, gv = pick("gq"), pick("gk"), pick("gv")
    n_items = range(len(items))
    dec = [jnp.where(incl, jnp.exp(jnp.minimum(
        _lane_bcast(gcum[j], h) - jnp.broadcast_to(gcum_t[j][h:h + 1, :], shape), 0.0)), 0.0)
        for j, h in items]
    bcol = [_lane_bcast(beta_all[j], N_HEADS + h) for j, h in items]
    gamc = [_lane_bcast(gam[j], h) for j, h in items]
    kk = [_mm_nt(gk[i], gk[i]) for i in n_items]
    qk = [_mm_nt(gq[i], gk[i]) for i in n_items]
    nmats = [-(jnp.where(strict, kk[i] * dec[i], 0.0) * bcol[i]) for i in n_items]
    tinv = _tri_inverse(variant, nmats, row, col)
    rhs = [jnp.concatenate([gv[i] * bcol[i], gk[i] * (bcol[i] * gamc[i])], 1) for i in n_items]
    sol = _mm_each(tinv, rhs)
    gdn = dict(
        items=items,
        wv=[t[:, :HEAD_DIM] for t in sol],
        wk=[t[:, HEAD_DIM:] for t in sol],
        attn=[qk[i] * dec[i] for i in n_items],
        qg=[gq[i] * gamc[i] for i in n_items],
        ktT=[(gk[i] * _lane_bcast(ktail[j], h)).T for i, (j, h) in enumerate(items)],
        cdr=cdr,
        gate=pick("sgz"),
    )
    return ret, gdn


def _rms_gate(o, gate, weight=None):
    o = o * lax.rsqrt(jnp.mean(o * o, -1, keepdims=True) + RMS_EPS)
    if weight is not None:
        o = o * weight
    return (o * gate).astype(BF16)


A_COLS = 7 * GROUP_W
B_COLS = 3 * GROUP_W
_STATE_SHAPE = (N_HEADS, HEAD_DIM, HEAD_DIM)


def _front_prompt_body(tm, x_ref, g_ref, b_ref, w_ref, cos_ref, sin_ref, rsc_ref,
                       cw_ref, halo0_ref, a_ref, b_out_ref, pab_ref, tail_ref, xc_ref):
    @pl.when(pl.program_id(1) == 0)
    def _():
        xc_ref[0:HALO, :] = halo0_ref[...]

    def put(ref, r, group, h, val):
        ref[r * TILE:(r + 1) * TILE,
            group * GROUP_W + h * HEAD_DIM:group * GROUP_W + (h + 1) * HEAD_DIM] = val.astype(ref.dtype)

    def finish_ret_head(h, r, val):
        rows = slice(r * TILE, (r + 1) * TILE)
        rsc = rsc_ref[rows, :]
        ops = _ret_head_ops(None, *(_head_cols(val, 0, g) for g in range(4)), cos_ref[rows, :],
                            sin_ref[rows, :], _lane_bcast(rsc, h), _lane_bcast(rsc, N_HEADS + h))
        for group, key in enumerate(("rq", "rk", "rv", "qd")):
            put(a_ref, r, group, h, ops[key])
        put(a_ref, r, 4, h, ops["kd"].T)
        put(a_ref, r, 6, h, ops["sgr"])

    def finish_conv_group(group, r, val):
        conv = _short_conv(val, cw_ref, xc_ref, group * GROUP_W, r * TILE)
        for h in range(N_HEADS):
            t = _head_cols(conv, 0, h)
            if group == 0:
                t = _l2_normalize(t) * (HEAD_DIM ** -0.5)
            elif group == 1:
                t = _l2_normalize(t)
            put(b_out_ref, r, group, h, t)

    def finish_gate(r, val):
        for h in range(N_HEADS):
            put(a_ref, r, 5, h, _silu(_head_cols(val, 0, h)))

    def finish_ab(r, val):
        pab_ref[r * TILE:(r + 1) * TILE, :] = val

    ret_group = lambda h: (h * GROUP_W, GROUP_W, functools.partial(finish_ret_head, h))
    gdn_group = lambda g: ((N_HEADS + g) * GROUP_W, GROUP_W, functools.partial(finish_conv_group, g))
    plan = [ret_group(0), gdn_group(0), ret_group(1), gdn_group(1), ret_group(2), gdn_group(2),
            ret_group(3), (7 * GROUP_W, GROUP_W, finish_gate), (8 * GROUP_W, HEAD_DIM, finish_ab)]
    pending = []
    for r in range(tm // TILE):
        hidden = _layer_norm(x_ref[r * TILE:(r + 1) * TILE, :], g_ref[...], b_ref[...]).astype(BF16)
        for col, width, finish in plan:
            val = jnp.dot(hidden, w_ref[:, col:col + width], preferred_element_type=F32)
            pending.append(functools.partial(finish, r, val))
            if len(pending) > FRONT_LAG:
                pending.pop(0)()
    for finish in pending:
        finish()
    tail = xc_ref[tm:tm + HALO, :]
    xc_ref[0:HALO, :] = tail
    tail_ref[0] = tail


def _front_prompt(x2d, ln_g, ln_b, w_all, cosf, sins, rsc, cw, halo0, n_batch, tm):
    rows = x2d.shape[0]
    tiles = rows // (n_batch * tm)
    c2 = lambda b, t: (0, 0)
    tile = lambda b, t: (b * tiles + t, 0)
    pos = lambda b, t: (t, 0)
    return pl.pallas_call(
        functools.partial(_front_prompt_body, tm),
        grid=(n_batch, tiles),
        in_specs=[
            pl.BlockSpec((tm, D_MODEL), tile),
            pl.BlockSpec((1, D_MODEL), c2),
            pl.BlockSpec((1, D_MODEL), c2),
            pl.BlockSpec((D_MODEL, W_ALL_COLS), c2, pipeline_mode=pl.Buffered(1)),
            pl.BlockSpec((tm, HEAD_DIM), pos),
            pl.BlockSpec((tm, HEAD_DIM), pos),
            pl.BlockSpec((tm, HEAD_DIM), c2),
            pl.BlockSpec((CONV_W, CONV_COLS), c2),
            pl.BlockSpec((HALO, CONV_COLS), c2),
        ],
        out_specs=[
            pl.BlockSpec((tm, A_COLS), tile),
            pl.BlockSpec((tm, B_COLS), tile),
            pl.BlockSpec((tm, HEAD_DIM), tile),
            pl.BlockSpec((1, HALO, CONV_COLS), lambda b, t: (b, 0, 0)),
        ],
        out_shape=[
            jax.ShapeDtypeStruct((rows, A_COLS), BF16),
            jax.ShapeDtypeStruct((rows, B_COLS), BF16),
            jax.ShapeDtypeStruct((rows, HEAD_DIM), F32),
            jax.ShapeDtypeStruct((n_batch, HALO, CONV_COLS), F32),
        ],
        scratch_shapes=[pltpu.VMEM((HALO + tm, CONV_COLS), F32)],
        compiler_params=_cparams(("parallel", "arbitrary")),
        name="front_prompt",
    )(x2d, ln_g, ln_b, w_all, cosf, sins, rsc, cw, halo0)


def _seq_init(s0r_ref, s0g_ref, sr_ref, sg_ref):
    @pl.when(pl.program_id(1) == 0)
    def _():
        for j in range(sr_ref.shape[0]):
            sr_ref[j] = s0r_ref[...]
            sg_ref[j] = s0g_ref[...]


def _seq_step(ret_cd, ret, gdn, gnw, mix_ref, sr_out, sg_out, sr_ref, sg_ref):
    items = ret["items"]
    idx = range(len(items))
    s_r = [sr_ref[j, h] for j, h in items]
    s_g = [sg_ref[j, h] for j, h in items]
    bf = lambda t: t.astype(BF16)
    cat = lambda a, b, axis: jnp.concatenate([bf(a), bf(b)], axis)
    wk_s = _mm_each(gdn["wk"], s_g)
    o_r = _mm_each([cat(ret["scores"][i], ret["qd"][i], 1) for i in idx],
                   [cat(ret["v"][i], s_r[i], 0) for i in idx])
    u = [gdn["wv"][i] - wk_s[i] for i in idx]
    o_g = _mm_each([cat(gdn["qg"][i], gdn["attn"][i], 1) for i in idx],
                   [cat(s_g[i], u[i], 0) for i in idx])
    upd_g = _mm_each(gdn["ktT"], u)
    upd_r = _mm_each(ret["kdT"], ret["v"])
    for i, (j, h) in enumerate(items):
        cd = jnp.broadcast_to(gdn["cdr"][j][0:1, h:h + 1], (HEAD_DIM, HEAD_DIM))
        sg_ref[j, h] = cd * s_g[i] + upd_g[i]
        sr_ref[j, h] = ret_cd[h] * s_r[i] + upd_r[i]
    for i, (j, h) in enumerate(items):
        mix_ref[j, :, h * HEAD_DIM:(h + 1) * HEAD_DIM] = _rms_gate(o_r[i], ret["gate"][i])
        mix_ref[j, :, GROUP_W + h * HEAD_DIM:GROUP_W + (h + 1) * HEAD_DIM] = _rms_gate(
            o_g[i], gdn["gate"][i], gnw)

    @pl.when(pl.program_id(1) == pl.num_programs(1) - 1)
    def _():
        sr_out[...] = sr_ref[...]
        sg_out[...] = sg_ref[...]


def _meta_mixer_body(ret_cd, pr_ref, pg_ref, pab_ref, cos_ref, sin_ref, rsc_ref, dmat_ref,
                     cw_ref, alog_ref, dtb_ref, gnw_ref, s0r_ref, s0g_ref,
                     mix_ref, sr_out, sg_out, sr_ref, sg_ref, xc_ref):
    _seq_init(s0r_ref, s0g_ref, sr_ref, sg_ref)
    xc_ref[0:HALO, :] = jnp.zeros((HALO, CONV_COLS), F32)
    tok = _tokenwise("meta", pr_ref[...], pg_ref[...], cos_ref[...], sin_ref[...], rsc_ref[...],
                     cw_ref, xc_ref, None)
    tok["kdT"] = [t.T for t in tok["kd"]]
    ret, gdn = _chunk_prep("meta", [tok], [pab_ref[...]], dmat_ref, alog_ref[...], dtb_ref[...])
    _seq_step(ret_cd, ret, gdn, gnw_ref[...], mix_ref, sr_out, sg_out, sr_ref, sg_ref)


def _prompt_mixer_body(ret_cd, a_ref, b_ref, pab_ref, dmat_ref, alog_ref, dtb_ref, gnw_ref,
                       s0r_ref, s0g_ref, mix_ref, sr_out, sg_out, sr_ref, sg_ref):
    _seq_init(s0r_ref, s0g_ref, sr_ref, sg_ref)
    heads = range(N_HEADS)
    tiles = range(a_ref.shape[0])
    piece = lambda ref, j, group: [
        ref[j, :, group * GROUP_W + h * HEAD_DIM:group * GROUP_W + (h + 1) * HEAD_DIM] for h in heads]
    toks = [dict(rq=piece(a_ref, j, 0), rk=piece(a_ref, j, 1), rv=piece(a_ref, j, 2),
                 qd=piece(a_ref, j, 3), kdT=piece(a_ref, j, 4), gq=piece(b_ref, j, 0),
                 gk=piece(b_ref, j, 1), gv=piece(b_ref, j, 2), sgz=piece(a_ref, j, 5),
                 sgr=piece(a_ref, j, 6)) for j in tiles]
    ret, gdn = _chunk_prep("prompt", toks, [pab_ref[j] for j in tiles], dmat_ref, alog_ref[...],
                           dtb_ref[...])
    _seq_step(ret_cd, ret, gdn, gnw_ref[...], mix_ref, sr_out, sg_out, sr_ref, sg_ref)


def _seq_out(n_batch, n_chunks, nb):
    state = pl.BlockSpec((nb,) + _STATE_SHAPE, lambda b, n: (b, 0, 0, 0))
    out_specs = [pl.BlockSpec((nb, TILE, 2 * GROUP_W), lambda b, n: (b, n, 0)), state, state]
    out_shape = [
        jax.ShapeDtypeStruct((n_batch, n_chunks * TILE, 2 * GROUP_W), BF16),
        jax.ShapeDtypeStruct((n_batch,) + _STATE_SHAPE, F32),
        jax.ShapeDtypeStruct((n_batch,) + _STATE_SHAPE, F32),
    ]
    scratch = [pltpu.VMEM((nb,) + _STATE_SHAPE, F32), pltpu.VMEM((nb,) + _STATE_SHAPE, F32)]
    return out_specs, out_shape, scratch


def _meta_mixer(ret_cd, pr, pg, pab, cosf, sins, rsc, dmat, cw, alog, dtb, gnw, s0r, s0g):
    c2 = lambda b, n: (0, 0)
    c3 = lambda b, n: (0, 0, 0)
    out_specs, out_shape, state_scratch = _seq_out(1, 1, 1)
    return pl.pallas_call(
        functools.partial(_meta_mixer_body, ret_cd),
        grid=(1, 1),
        in_specs=[
            pl.BlockSpec((TILE, 4 * GROUP_W), c2),
            pl.BlockSpec((TILE, 4 * GROUP_W), c2),
            pl.BlockSpec((TILE, HEAD_DIM), c2),
            pl.BlockSpec((TILE, HEAD_DIM), c2),
            pl.BlockSpec((TILE, HEAD_DIM), c2),
            pl.BlockSpec((TILE, HEAD_DIM), c2),
            pl.BlockSpec((N_HEADS, TILE, TILE), c3),
            pl.BlockSpec((CONV_W, CONV_COLS), c2),
            pl.BlockSpec((1, HEAD_DIM), c2),
            pl.BlockSpec((1, HEAD_DIM), c2),
            pl.BlockSpec((1, HEAD_DIM), c2),
            pl.BlockSpec(_STATE_SHAPE, c3),
            pl.BlockSpec(_STATE_SHAPE, c3),
        ],
        out_specs=out_specs,
        out_shape=out_shape,
        scratch_shapes=state_scratch + [pltpu.VMEM((HALO + TILE, CONV_COLS), F32)],
        compiler_params=_cparams(("parallel", "arbitrary")),
        name="mixer_meta",
    )(pr, pg, pab, cosf, sins, rsc, dmat, cw, alog, dtb, gnw, s0r, s0g)


def _prompt_mixer(ret_cd, a, b, pab, dmat, alog, dtb, gnw, s0r, s0g, nb):
    n_batch, seq, _ = a.shape
    n_chunks = seq // TILE
    tile = lambda b, n: (b, n, 0)
    c2 = lambda b, n: (0, 0)
    c3 = lambda b, n: (0, 0, 0)
    out_specs, out_shape, state_scratch = _seq_out(n_batch, n_chunks, nb)
    return pl.pallas_call(
        functools.partial(_prompt_mixer_body, ret_cd),
        grid=(n_batch // nb, n_chunks),
        in_specs=[
            pl.BlockSpec((nb, TILE, A_COLS), tile),
            pl.BlockSpec((nb, TILE, B_COLS), tile),
            pl.BlockSpec((nb, TILE, HEAD_DIM), tile),
            pl.BlockSpec((N_HEADS, TILE, TILE), c3),
            pl.BlockSpec((1, HEAD_DIM), c2),
            pl.BlockSpec((1, HEAD_DIM), c2),
            pl.BlockSpec((1, HEAD_DIM), c2),
            pl.BlockSpec(_STATE_SHAPE, c3),
            pl.BlockSpec(_STATE_SHAPE, c3),
        ],
        out_specs=out_specs,
        out_shape=out_shape,
        scratch_shapes=state_scratch,
        compiler_params=_cparams(("parallel", "arbitrary")),
        name="mixer_prompt",
    )(a, b, pab, dmat, alog, dtb, gnw, s0r, s0g)


SAMPLE_PER_TILE = TILE // SAMPLE_GROUP


def _stack_by_group(xt):
    shape3 = (SAMPLE_PER_TILE, HEAD_DIM, TILE)
    keep = lax.broadcasted_iota(jnp.int32, shape3, 0) == (lax.broadcasted_iota(jnp.int32, shape3, 2) >> 3)
    stacked = jnp.where(keep, jnp.broadcast_to(xt[None], shape3), 0.0)
    return stacked.reshape(SAMPLE_PER_TILE * HEAD_DIM, TILE)


def _sample_mixer_body(ret_cd, pr_ref, pg_ref, pab_ref, cos_ref, sin_ref, rsc_ref, dmat_ref, cw_ref,
                       alog_ref, dtb_ref, gnw_ref, convpad_ref, sr_in, sg_in,
                       mix_ref, sr_out, sg_out,
                       xc_ref, qd_s, wk_s, qg_s, wv_s, cd_s, inter_s, u_s, qs_s):
    heads = range(N_HEADS)
    xc_ref[0:HALO, :] = jnp.zeros((HALO, CONV_COLS), F32)
    tok = _tokenwise("sample", pr_ref[...], pg_ref[...], cos_ref[...], sin_ref[...], rsc_ref[...],
                     cw_ref, xc_ref, convpad_ref[...])
    tok["kdT"] = [t.T for t in tok["kd"]]
    ret, gdn = _chunk_prep("sample", [tok], [pab_ref[...]], dmat_ref, alog_ref[...], dtb_ref[...])
    for h in heads:
        qd_s[h] = ret["qd"][h]
        wk_s[h] = gdn["wk"][h]
        qg_s[h] = gdn["qg"][h]
        wv_s[h] = gdn["wv"][h]
    cd_s[...] = gdn["cdr"][0]

    def per_batch(b, carry):
        rows = pl.ds(pl.multiple_of(b * SAMPLE_GROUP, SAMPLE_GROUP), SAMPLE_GROUP)
        cd_rows = cd_s[rows, :]
        for h in heads:
            inter_s[h, rows, :] = _mm(qd_s[h, rows, :], sr_in[b, h])
            s = sg_in[b, h]
            both = _mm(jnp.concatenate([wk_s[h, rows, :], qg_s[h, rows, :]], 0), s)
            u_s[h, rows, :] = wv_s[h, rows, :] - both[:SAMPLE_GROUP]
            qs_s[h, rows, :] = both[SAMPLE_GROUP:]
            cd = jnp.broadcast_to(cd_rows[SAMPLE_GROUP - 1:SAMPLE_GROUP, h:h + 1], (HEAD_DIM, HEAD_DIM))
            sg_out[b, h] = cd * s
        return carry

    lax.fori_loop(0, SAMPLE_PER_TILE, per_batch, 0)

    out_rows = SAMPLE_PER_TILE * SAMPLE_TOKENS
    out_row = _iota2((out_rows, TILE), 0)
    token_row = ((out_row >> 2) << 3) + (SAMPLE_GROUP - SAMPLE_TOKENS) + (out_row & (SAMPLE_TOKENS - 1))
    select = (_iota2((out_rows, TILE), 1) == token_row).astype(BF16)
    compact = lambda t: jnp.dot(select, t, preferred_element_type=F32).astype(BF16)

    state_shape = (SAMPLE_PER_TILE, HEAD_DIM, HEAD_DIM)
    for h in heads:
        o = _mm(ret["scores"][h], ret["v"][h]) + inter_s[h]
        upd = _mm(_stack_by_group(ret["kdT"][h]), ret["v"][h]).reshape(state_shape)
        sr_out[:, h] = ret_cd[h] * sr_in[:, h] + upd
        mix_ref[:, h * HEAD_DIM:(h + 1) * HEAD_DIM] = compact(_rms_gate(o, ret["gate"][h]))

    for h in heads:
        u = u_s[h]
        o = qs_s[h] + _mm(gdn["attn"][h], u)
        upd = _mm(_stack_by_group(gdn["ktT"][h]), u).reshape(state_shape)
        sg_out[:, h] = sg_out[:, h] + upd
        mix_ref[:, GROUP_W + h * HEAD_DIM:GROUP_W + (h + 1) * HEAD_DIM] = compact(_rms_gate(
            o, gdn["gate"][h], gnw_ref[...]))


def _sample_mixer(ret_cd, pr, pg, pab, cosf, sins, rsc, dmat, cw, alog, dtb, gnw, convpad, sr, sg):
    n_tiles = pr.shape[0] // TILE
    tile = lambda i: (i, 0)
    c2 = lambda i: (0, 0)
    c3 = lambda i: (0, 0, 0)
    state = pl.BlockSpec((SAMPLE_PER_TILE,) + _STATE_SHAPE, lambda i: (i, 0, 0, 0))
    head_scratch = pltpu.VMEM((N_HEADS, TILE, HEAD_DIM), F32)
    return pl.pallas_call(
        functools.partial(_sample_mixer_body, ret_cd),
        grid=(n_tiles,),
        in_specs=[
            pl.BlockSpec((TILE, 4 * GROUP_W), tile),
            pl.BlockSpec((TILE, 4 * GROUP_W), tile),
            pl.BlockSpec((TILE, HEAD_DIM), tile),
            pl.BlockSpec((TILE, HEAD_DIM), c2),
            pl.BlockSpec((TILE, HEAD_DIM), c2),
            pl.BlockSpec((TILE, HEAD_DIM), c2),
            pl.BlockSpec((N_HEADS, TILE, TILE), c3),
            pl.BlockSpec((CONV_W, CONV_COLS), c2),
            pl.BlockSpec((1, HEAD_DIM), c2),
            pl.BlockSpec((1, HEAD_DIM), c2),
            pl.BlockSpec((1, HEAD_DIM), c2),
            pl.BlockSpec((TILE, CONV_COLS), tile),
            state, state,
        ],
        out_specs=[pl.BlockSpec((SAMPLE_PER_TILE * SAMPLE_TOKENS, 2 * GROUP_W), tile), state, state],
        out_shape=[
            jax.ShapeDtypeStruct((n_tiles * SAMPLE_PER_TILE * SAMPLE_TOKENS, 2 * GROUP_W), BF16),
            jax.ShapeDtypeStruct(sr.shape, F32),
            jax.ShapeDtypeStruct(sg.shape, F32),
        ],
        scratch_shapes=[pltpu.VMEM((HALO + TILE, CONV_COLS), F32)] + [head_scratch] * 4
                       + [pltpu.VMEM((TILE, HEAD_DIM), F32)] + [head_scratch] * 3,
        compiler_params=_cparams(("parallel",)),
        name="mixer_sample",
    )(pr, pg, pab, cosf, sins, rsc, dmat, cw, alog, dtb, gnw, convpad, sr, sg)


def _rotary_tables(pos):
    half = HEAD_DIM // 2
    inv = ROPE_BASE ** (-np.arange(half, dtype=np.float64) / half)
    ang = np.asarray(pos, np.float64)[:, None] * inv[None, :]
    cos, sin = np.cos(ang), np.sin(ang)
    return (jnp.asarray(np.concatenate([cos, cos], -1), F32),
            jnp.asarray(np.concatenate([-sin, sin], -1), F32))


def _retention_tables(seg, pos, valid, seg_len):
    gamma = 1.0 - 2.0 ** (-5.0 - np.arange(N_HEADS, dtype=np.float64))
    posf = np.asarray(pos, np.float64)
    rel = posf[:, None] - posf[None, :]
    causal = (seg[:, None] == seg[None, :]) & (rel >= 0)
    dmat = np.where(causal[None], gamma[:, None, None] ** np.where(causal, rel, 0.0)[None], 0.0)
    q_scale = gamma[None, :] ** (posf[:, None] + 1.0)
    k_scale = np.where(valid[:, None], gamma[None, :] ** (seg_len - 1.0 - posf[:, None]), 0.0)
    rsc = np.concatenate([q_scale, k_scale, np.zeros((TILE, HEAD_DIM - 2 * N_HEADS))], -1)
    chunk_decay = tuple(float(g ** seg_len) for g in gamma)
    return jnp.asarray(dmat, F32), jnp.asarray(rsc, F32), chunk_decay


def _pad_lanes(v):
    return jnp.pad(v.astype(F32), (0, HEAD_DIM - v.shape[0]))[None, :]


def kernel(x_prompt, x_sample, state_ret, state_gdn, state_conv, meta_tokens, emb_ln_g, emb_ln_b,
           w_in, conv_w, a_log, dt_bias, gdn_norm_w, w_out, ln1_g, ln1_b, w_gate_up, w_down,
           ln2_g, ln2_b):
    n_batch, seq, _ = x_prompt.shape
    dec_batch, dec_seq, _ = x_sample.shape
    assert seq % TILE == 0 and dec_seq == SAMPLE_TOKENS and N_META <= TILE
    n_chunks = seq // TILE
    layer = 0

    w_in_l = w_in[layer]
    w_ab = jnp.pad(w_in_l[:, 8 * GROUP_W:], ((0, 0), (0, HEAD_DIM - 2 * N_HEADS))).astype(BF16)
    w_all = _weight_prep(w_in_l, w_ab)
    w_out_b = w_out[layer].astype(BF16)
    w_gu_b = w_gate_up[layer].astype(BF16)
    w_down_b = w_down[layer].astype(BF16)
    row = lambda v: v.astype(F32)[None, :]
    eg, eb = row(emb_ln_g), row(emb_ln_b)
    cw = conv_w[layer].astype(F32)
    alog, dtb, gnw = _pad_lanes(a_log[layer]), _pad_lanes(dt_bias[layer]), row(gdn_norm_w[layer])

    xp = x_prompt.reshape(n_batch * seq, D_MODEL)
    xm = jnp.pad(meta_tokens.astype(F32), ((0, TILE - N_META), (0, 0)))
    xs = jnp.pad(x_sample, ((0, 0), (SAMPLE_GROUP - dec_seq, 0), (0, 0))).reshape(
        dec_batch * SAMPLE_GROUP, D_MODEL)
    convpad = jnp.pad(state_conv[layer].astype(F32), ((0, 0), (1, SAMPLE_GROUP - CONV_W), (0, 0))).reshape(
        dec_batch * SAMPLE_GROUP, CONV_COLS)

    tile_idx = np.arange(TILE)
    cos_p, sin_p = _rotary_tables(N_META + np.arange(seq))
    cos_m, sin_m = _rotary_tables(tile_idx)
    tok = (tile_idx % SAMPLE_GROUP) - (SAMPLE_GROUP - dec_seq)
    cos_s, sin_s = _rotary_tables(PAST_LEN + np.maximum(tok, 0))
    zeros_i = np.zeros((TILE,), np.int32)
    all_valid = np.ones((TILE,), bool)
    dmat_p, rsc_p, cd_p = _retention_tables(zeros_i, tile_idx, all_valid, float(TILE))
    dmat_m, rsc_m, cd_m = _retention_tables(zeros_i, tile_idx, tile_idx < N_META, float(N_META))
    dmat_s, rsc_s, cd_s = _retention_tables(tile_idx // SAMPLE_GROUP, tok, tok >= 0, float(dec_seq))

    zero_state = jnp.zeros(_STATE_SHAPE, F32)
    pr_m, pg_m, pab_m = _front(xm, eg, eb, w_all, TILE)
    _, sr_m, sg_m = _meta_mixer(cd_m, pr_m, pg_m, pab_m, cos_m, sin_m, rsc_m, dmat_m, cw, alog, dtb, gnw,
                                zero_state, zero_state)
    halo_p = pg_m[N_META - HALO:N_META, :CONV_COLS]

    tm_front = 512
    a_p, b_p, pab_p, tail_p = _front_prompt(xp, eg, eb, w_all, cos_p, sin_p,
                                            jnp.tile(rsc_p, (tm_front // TILE, 1)), cw, halo_p,
                                            n_batch, tm_front)
    per_seq = lambda t: t.reshape(n_batch, seq, t.shape[-1])
    mix_p, sr_p, sg_p = _prompt_mixer(cd_p, per_seq(a_p), per_seq(b_p), per_seq(pab_p), dmat_p, alog, dtb,
                                      gnw, sr_m[0], sg_m[0], MIXER_SEQS_PER_STEP)
    mix_p = mix_p.reshape(n_batch * seq, 2 * GROUP_W)

    pr_s, pg_s, pab_s = _front(xs, eg, eb, w_all, 512)
    mix_s, sr_s, sg_s = _sample_mixer(cd_s, pr_s, pg_s, pab_s, cos_s, sin_s, rsc_s, dmat_s, cw, alog,
                                      dtb, gnw, convpad, state_ret[layer].astype(F32),
                                      state_gdn[layer].astype(F32))

    back = functools.partial(_back, eg=eg, eb=eb, w_out=w_out_b, g1=row(ln1_g[layer]), b1=row(ln1_b[layer]),
                             w_gu=w_gu_b, w_down=w_down_b, g2=row(ln2_g[layer]), b2=row(ln2_b[layer]),
                             tm=512)
    y_p = back(xp, mix_p).reshape(n_batch, seq, D_MODEL)
    y_s = back(x_sample.reshape(dec_batch * dec_seq, D_MODEL), mix_s).reshape(dec_batch, dec_seq, D_MODEL)

    conv_p = tail_p[:, HALO - (CONV_W - 1):, :]
    conv_s = pg_s.reshape(dec_batch, SAMPLE_GROUP, 4 * GROUP_W)[:, SAMPLE_GROUP - (CONV_W - 1):, :CONV_COLS]
    return (y_p, y_s, sr_p[None], sg_p[None], conv_p[None], sr_s[None], sg_s[None], conv_s[None])
```

```python
import functools

import numpy as np
import jax
import jax.numpy as jnp
from jax import lax
from jax.experimental import pallas as pl
from jax.experimental.pallas import tpu as pltpu

F32 = jnp.float32
BF16 = jnp.bfloat16

D_MODEL = 1024
N_META = 16
N_HEADS = 4
HEAD_DIM = 128
GROUP_W = N_HEADS * HEAD_DIM
CONV_W = 4
CONV_COLS = 3 * GROUP_W
D_FF = 2816
PAST_LEN = 16384
ROPE_BASE = 10000.0
LN_EPS = 1e-5
RMS_EPS = 1e-6
ALPHA = 2.0 ** 0.25
TILE = 128
SAMPLE_GROUP = 8
SAMPLE_TOKENS = 4
HALO = 8
MIXER_SEQS_PER_STEP = 4
BACK_PARTS = 2
FRONT_LAG = 2
VMEM_LIMIT = 56 * 1024 * 1024


def _cparams(sem):
    return pltpu.CompilerParams(dimension_semantics=sem, vmem_limit_bytes=VMEM_LIMIT)


def _layer_norm(x, g, b):
    mu = jnp.mean(x, -1, keepdims=True)
    xc = x - mu
    var = jnp.mean(xc * xc, -1, keepdims=True)
    return xc * lax.rsqrt(var + LN_EPS) * g + b


def _sigmoid(x):
    return 1.0 / (1.0 + jnp.exp(-x))


def _silu(x):
    return x * _sigmoid(x)


def _softplus(x):
    return jnp.maximum(x, 0.0) + jnp.log(1.0 + jnp.exp(-jnp.abs(x)))


def _mm(a, b):
    return jnp.dot(a.astype(BF16), b.astype(BF16), preferred_element_type=F32)


def _mm_nt(a, b):
    return lax.dot_general(a.astype(BF16), b.astype(BF16), (((1,), (1,)), ((), ())),
                           preferred_element_type=F32)


def _mm_each(xs, ys):
    return [_mm(x, y) for x, y in zip(xs, ys)]


def _mm_split3(m01, x):
    x1 = x.astype(BF16)
    r1 = x - x1.astype(F32)
    x2 = r1.astype(BF16)
    x3 = (r1 - x2.astype(F32)).astype(BF16)
    dot = functools.partial(jnp.dot, preferred_element_type=F32)
    return (dot(m01, x3) + dot(m01, x2)) + dot(m01, x1)


def _lane_bcast(x, lane):
    return jnp.broadcast_to(x[:, lane:lane + 1], (x.shape[0], HEAD_DIM))


def _head_cols(x, base, h):
    return x[:, base + h * HEAD_DIM:base + (h + 1) * HEAD_DIM]


W_ALL_COLS = 9 * GROUP_W


AB_ROWS = 8


def _weight_prep_body(q_ref, k_ref, v_ref, g_ref, ab_ref, o_ref):
    j = pl.program_id(0)

    @pl.when(j < 2 * N_HEADS)
    def _():
        for piece, ref in enumerate((q_ref, k_ref, v_ref, g_ref)):
            o_ref[:, piece * HEAD_DIM:(piece + 1) * HEAD_DIM] = ref[...].T.astype(BF16)

    @pl.when(j == 2 * N_HEADS)
    def _():
        ab = jnp.concatenate([ab_ref[...], jnp.zeros((HEAD_DIM - AB_ROWS, D_MODEL), F32)], 0)
        o_ref[:, :HEAD_DIM] = ab.T.astype(BF16)
        o_ref[:, HEAD_DIM:] = jnp.zeros((D_MODEL, GROUP_W - HEAD_DIM), BF16)


def _weight_prep(w_in_t):
    assert w_in_t.shape == (8 * GROUP_W + AB_ROWS, D_MODEL)
    def piece_spec(piece):
        def index(j):
            ret_block = piece * N_HEADS + j
            gdn_block = N_HEADS * j + piece
            return (jnp.where(j < N_HEADS, ret_block, jnp.where(j < 2 * N_HEADS, gdn_block, 0)), 0)
        return pl.BlockSpec((HEAD_DIM, D_MODEL), index)
    ab_spec = pl.BlockSpec((AB_ROWS, D_MODEL), lambda j: (8 * GROUP_W // AB_ROWS, 0))
    return pl.pallas_call(
        _weight_prep_body,
        grid=(2 * N_HEADS + 1,),
        in_specs=[piece_spec(p) for p in range(4)] + [ab_spec],
        out_specs=pl.BlockSpec((D_MODEL, GROUP_W), lambda j: (0, j)),
        out_shape=jax.ShapeDtypeStruct((D_MODEL, W_ALL_COLS), BF16),
        compiler_params=_cparams(("arbitrary",)),
        name="weight_prep",
    )(w_in_t, w_in_t, w_in_t, w_in_t, w_in_t)


def _front_body(x_ref, g_ref, b_ref, w_ref, pr_ref, pg_ref, pab_ref):
    h = _layer_norm(x_ref[...], g_ref[...], b_ref[...]).astype(BF16)
    pr_ref[...] = jnp.dot(h, w_ref[:, :4 * GROUP_W], preferred_element_type=F32)
    pg_ref[...] = jnp.dot(h, w_ref[:, 4 * GROUP_W:8 * GROUP_W], preferred_element_type=F32)
    pab_ref[...] = jnp.dot(h, w_ref[:, 8 * GROUP_W:8 * GROUP_W + HEAD_DIM], preferred_element_type=F32)


def _front(x2d, ln_g, ln_b, w_all, tm):
    rows = x2d.shape[0]
    const = lambda i: (0, 0)
    tile = lambda i: (i, 0)
    return pl.pallas_call(
        _front_body,
        grid=(rows // tm,),
        in_specs=[
            pl.BlockSpec((tm, D_MODEL), tile),
            pl.BlockSpec((1, D_MODEL), const),
            pl.BlockSpec((1, D_MODEL), const),
            pl.BlockSpec((D_MODEL, W_ALL_COLS), const),
        ],
        out_specs=[
            pl.BlockSpec((tm, 4 * GROUP_W), tile),
            pl.BlockSpec((tm, 4 * GROUP_W), tile),
            pl.BlockSpec((tm, HEAD_DIM), tile),
        ],
        out_shape=[
            jax.ShapeDtypeStruct((rows, 4 * GROUP_W), F32),
            jax.ShapeDtypeStruct((rows, 4 * GROUP_W), F32),
            jax.ShapeDtypeStruct((rows, HEAD_DIM), F32),
        ],
        compiler_params=_cparams(("parallel",)),
        name="front",
    )(x2d, ln_g, ln_b, w_all)


def _back_body(x_ref, mix_ref, eg_ref, eb_ref, wo_ref, g1_ref, b1_ref, wgu_ref, wd_ref,
               g2_ref, b2_ref, y_ref):
    part = x_ref.shape[0] // BACK_PARTS
    parts = [slice(i * part, (i + 1) * part) for i in range(BACK_PARTS)]
    dot = functools.partial(jnp.dot, preferred_element_type=F32)
    h = [_layer_norm(x_ref[p, :], eg_ref[...], eb_ref[...]) for p in parts]
    mp = [dot(mix_ref[p, :], wo_ref[...]) for p in parts]
    h1 = [_layer_norm(ALPHA * a + b, g1_ref[...], b1_ref[...]) for a, b in zip(h, mp)]
    gu = [dot(a.astype(BF16), wgu_ref[...]) for a in h1]
    act = [(_silu(a[:, :D_FF]) * a[:, D_FF:]).astype(BF16) for a in gu]
    ff = [dot(a, wd_ref[...]) for a in act]
    for p, a, b in zip(parts, h1, ff):
        y_ref[p, :] = _layer_norm(ALPHA * a + b, g2_ref[...], b2_ref[...])


def _back(x2d, mix, eg, eb, w_out, g1, b1, w_gu, w_down, g2, b2, tm):
    rows = x2d.shape[0]
    const = lambda i: (0, 0)
    tile = lambda i: (i, 0)
    single = pl.Buffered(1)
    vec = pl.BlockSpec((1, D_MODEL), const)
    return pl.pallas_call(
        _back_body,
        grid=(rows // tm,),
        in_specs=[
            pl.BlockSpec((tm, D_MODEL), tile),
            pl.BlockSpec((tm, D_MODEL), tile),
            vec, vec,
            pl.BlockSpec((D_MODEL, D_MODEL), const, pipeline_mode=single),
            vec, vec,
            pl.BlockSpec((D_MODEL, 2 * D_FF), const, pipeline_mode=single),
            pl.BlockSpec((D_FF, D_MODEL), const, pipeline_mode=single),
            vec, vec,
        ],
        out_specs=pl.BlockSpec((tm, D_MODEL), tile),
        out_shape=jax.ShapeDtypeStruct((rows, D_MODEL), F32),
        compiler_params=_cparams(("parallel",)),
        name="back",
    )(x2d, mix, eg, eb, w_out, g1, b1, w_gu, w_down, g2, b2)


def _iota2(shape, dim):
    return lax.broadcasted_iota(jnp.int32, shape, dim)


def _valid_rows(variant, shape):
    r = _iota2(shape, 0)
    if variant == "meta":
        return r < N_META
    if variant == "sample":
        return (r & (SAMPLE_GROUP - 1)) >= SAMPLE_GROUP - SAMPLE_TOKENS
    return None


def _mask_rows(valid, x):
    return x if valid is None else jnp.where(valid, x, 0.0)


def _tri_inverse(variant, nmats, row, col):
    eye = (row == col).astype(F32)
    if variant == "sample":
        n2 = _mm_each(nmats, nmats)
        ps = [eye + n for n in nmats]
        return [p + t for p, t in zip(ps, _mm_each(ps, n2))]
    base_log2 = 3
    in_block = (row >> base_log2) == (col >> base_log2)
    ds = [jnp.where(in_block, n, 0.0) for n in nmats]
    ps = [eye + d for d in ds]
    for _ in range(base_log2 - 1):
        ds = _mm_each(ds, ds)
        ps = [p + t for p, t in zip(ps, _mm_each(ps, ds))]
    live_rows = N_META if variant == "meta" else TILE
    s = base_log2
    while (1 << s) < live_rows:
        size = 1 << s
        lower_left = (((row >> (s + 1)) == (col >> (s + 1))) & (((row >> s) & 1) == 1)
                      & (((col >> s) & 1) == 0))
        cs = [jnp.where(lower_left, n, 0.0) for n in nmats]
        lower = [slice(start + size, start + 2 * size) for start in range(0, TILE, 2 * size)]
        upper = [slice(start, start + size) for start in range(0, TILE, 2 * size)]
        p_low = [jnp.concatenate([p[rows, :] for rows in lower], 0) for p in ps]
        fix = _mm_each(_mm_each(p_low, cs), ps)
        merged = []
        for p, f in zip(ps, fix):
            pieces = []
            for i, (up, lo) in enumerate(zip(upper, lower)):
                pieces += [p[up, :], p[lo, :] + f[i * size:(i + 1) * size, :]]
            merged.append(jnp.concatenate(pieces, 0))
        ps = merged
        s += 1
    return ps


def _rotary(t, cosf, sins):
    return t * cosf + pltpu.roll(t, HEAD_DIM // 2, 1) * sins


def _ret_head_ops(valid, q, k, v, gate, cosf, sins, q_scale, k_scale):
    rq = _mask_rows(valid, _rotary(q, cosf, sins))
    rk = _mask_rows(valid, _rotary(k, cosf, sins) * (HEAD_DIM ** -0.5))
    return dict(rq=rq, rk=rk, rv=_mask_rows(valid, v), qd=rq * q_scale, kd=rk * k_scale, sgr=_silu(gate))


def _short_conv(x, cw_ref, xc_ref, col0, row0=0):
    rows, ncols = x.shape
    cols = slice(col0, col0 + ncols)
    xc_ref[HALO + row0:HALO + row0 + rows, cols] = x
    conv = None
    for w in range(CONV_W):
        off = HALO + row0 - (CONV_W - 1) + w
        term = xc_ref[off:off + rows, cols] * cw_ref[w:w + 1, cols]
        conv = term if conv is None else conv + term
    return _silu(conv)


def _l2_normalize(t):
    return t * lax.rsqrt(jnp.sum(t * t, -1, keepdims=True) + RMS_EPS)


def _tokenwise(variant, pret, pgdn, cosf, sins, rsc, cw_ref, xc_ref, convpad):
    rows = pret.shape[0]
    heads = range(N_HEADS)
    valid = _valid_rows(variant, (rows, HEAD_DIM))
    ret = [_ret_head_ops(valid, *(_head_cols(pret, h * GROUP_W, g) for g in range(4)), cosf, sins,
                         _lane_bcast(rsc, h), _lane_bcast(rsc, N_HEADS + h)) for h in heads]
    x = pgdn[:, :CONV_COLS]
    if variant == "sample":
        x = jnp.where(_valid_rows(variant, x.shape), x, convpad)
    conv = _short_conv(x, cw_ref, xc_ref, 0)
    gq = [_mask_rows(valid, _l2_normalize(_head_cols(conv, 0, h)) * (HEAD_DIM ** -0.5)) for h in heads]
    gk = [_mask_rows(valid, _l2_normalize(_head_cols(conv, GROUP_W, h))) for h in heads]
    gv = [_mask_rows(valid, _head_cols(conv, 2 * GROUP_W, h)) for h in heads]
    sgz = [_silu(_head_cols(pgdn, 3 * GROUP_W, h)) for h in heads]
    tok = {key: [r[key] for r in ret] for key in ret[0]}
    tok.update(gq=gq, gk=gk, gv=gv, sgz=sgz)
    return tok


def _chunk_prep(variant, toks, pabs, dmat_ref, alog, dtb):
    shape = (TILE, HEAD_DIM)
    tiles = range(len(toks))
    items = [(j, h) for j in tiles for h in range(N_HEADS)]
    pick = lambda key: [toks[j][key][h] for j, h in items]
    row = _iota2(shape, 0)
    col = _iota2(shape, 1)
    valid = _valid_rows(variant, shape)
    if variant == "sample":
        same = (row >> 3) == (col >> 3)
        incl = same & (row >= col)
        strict = same & (row > col)
    else:
        incl = row >= col
        strict = row > col

    ret = dict(
        items=items,
        scores=[_mm_nt(toks[j]["rq"][h], toks[j]["rk"][h]) * dmat_ref[h] for j, h in items],
        qd=pick("qd"), kdT=pick("kdT"), v=pick("rv"), gate=pick("sgr"),
    )

    incl01 = incl.astype(BF16)
    beta_all, gcum, gam, ktail, cdr, gcum_t = [], [], [], [], [], []
    for j in tiles:
        g_all = _mask_rows(valid, -jnp.exp(alog) * _softplus(pabs[j] + dtb))
        beta_all.append(_mask_rows(valid, _sigmoid(pabs[j])))
        if variant == "sample":
            both = _mm_split3(jnp.concatenate([incl01, same.astype(BF16)], 0), g_all)
            gc, gseg = both[:TILE], both[TILE:]
        else:
            gc = _mm_split3(incl01, g_all)
            gseg = jnp.broadcast_to(gc[TILE - 1:TILE, :], shape)
        gcum.append(gc)
        gam.append(jnp.exp(gc))
        ktail.append(jnp.exp(gseg - gc))
        cdr.append(jnp.exp(gseg))
        gcum_t.append(gc.T)

    gq, gk, gv = pick("gq"), pick("gk"), pick("gv")
    n_items = range(len(items))
    dec = [jnp.where(incl, jnp.exp(jnp.minimum(
        _lane_bcast(gcum[j], h) - jnp.broadcast_to(gcum_t[j][h:h + 1, :], shape), 0.0)), 0.0)
        for j, h in items]
    bcol = [_lane_bcast(beta_all[j], N_HEADS + h) for j, h in items]
    gamc = [_lane_bcast(gam[j], h) for j, h in items]
    kk = [_mm_nt(gk[i], gk[i]) for i in n_items]
    qk = [_mm_nt(gq[i], gk[i]) for i in n_items]
    nmats = [-(jnp.where(strict, kk[i] * dec[i], 0.0) * bcol[i]) for i in n_items]
    tinv = _tri_inverse(variant, nmats, row, col)
    rhs = [jnp.concatenate([gv[i] * bcol[i], gk[i] * (bcol[i] * gamc[i])], 1) for i in n_items]
    sol = _mm_each(tinv, rhs)
    gdn = dict(
        items=items,
        wv=[t[:, :HEAD_DIM] for t in sol],
        wk=[t[:, HEAD_DIM:] for t in sol],
        attn=[qk[i] * dec[i] for i in n_items],
        qg=[gq[i] * gamc[i] for i in n_items],
        ktT=[(gk[i] * _lane_bcast(ktail[j], h)).T for i, (j, h) in enumerate(items)],
        cdr=cdr,
        gate=pick("sgz"),
    )
    return ret, gdn


def _rms_gate(o, gate, weight=None):
    o = o * lax.rsqrt(jnp.mean(o * o, -1, keepdims=True) + RMS_EPS)
    if weight is not None:
        o = o * weight
    return (o * gate).astype(BF16)


A_COLS = 7 * GROUP_W
B_COLS = 3 * GROUP_W
_STATE_SHAPE = (N_HEADS, HEAD_DIM, HEAD_DIM)


def _front_prompt_body(tm, x_ref, g_ref, b_ref, w_ref, cos_ref, sin_ref, rsc_ref,
                       cw_ref, halo0_ref, a_ref, b_out_ref, pab_ref, tail_ref, xc_ref):
    @pl.when(pl.program_id(1) == 0)
    def _():
        xc_ref[0:HALO, :] = halo0_ref[...]

    def put(ref, r, group, h, val):
        ref[r * TILE:(r + 1) * TILE,
            group * GROUP_W + h * HEAD_DIM:group * GROUP_W + (h + 1) * HEAD_DIM] = val.astype(ref.dtype)

    def finish_ret_head(h, r, val):
        rows = slice(r * TILE, (r + 1) * TILE)
        rsc = rsc_ref[rows, :]
        ops = _ret_head_ops(None, *(_head_cols(val, 0, g) for g in range(4)), cos_ref[rows, :],
                            sin_ref[rows, :], _lane_bcast(rsc, h), _lane_bcast(rsc, N_HEADS + h))
        for group, key in enumerate(("rq", "rk", "rv", "qd")):
            put(a_ref, r, group, h, ops[key])
        put(a_ref, r, 4, h, ops["kd"].T)
        put(a_ref, r, 6, h, ops["sgr"])

    def finish_conv_group(group, r, val):
        conv = _short_conv(val, cw_ref, xc_ref, group * GROUP_W, r * TILE)
        for h in range(N_HEADS):
            t = _head_cols(conv, 0, h)
            if group == 0:
                t = _l2_normalize(t) * (HEAD_DIM ** -0.5)
            elif group == 1:
                t = _l2_normalize(t)
            put(b_out_ref, r, group, h, t)

    def finish_gate(r, val):
        for h in range(N_HEADS):
            put(a_ref, r, 5, h, _silu(_head_cols(val, 0, h)))

    def finish_ab(r, val):
        pab_ref[r * TILE:(r + 1) * TILE, :] = val

    ret_group = lambda h: (h * GROUP_W, GROUP_W, functools.partial(finish_ret_head, h))
    gdn_group = lambda g: ((N_HEADS + g) * GROUP_W, GROUP_W, functools.partial(finish_conv_group, g))
    plan = [ret_group(0), gdn_group(0), ret_group(1), gdn_group(1), ret_group(2), gdn_group(2),
            ret_group(3), (7 * GROUP_W, GROUP_W, finish_gate), (8 * GROUP_W, HEAD_DIM, finish_ab)]
    pending = []
    for r in range(tm // TILE):
        hidden = _layer_norm(x_ref[r * TILE:(r + 1) * TILE, :], g_ref[...], b_ref[...]).astype(BF16)
        for col, width, finish in plan:
            val = jnp.dot(hidden, w_ref[:, col:col + width], preferred_element_type=F32)
            pending.append(functools.partial(finish, r, val))
            if len(pending) > FRONT_LAG:
                pending.pop(0)()
    for finish in pending:
        finish()
    tail = xc_ref[tm:tm + HALO, :]
    xc_ref[0:HALO, :] = tail
    tail_ref[0] = tail


def _front_prompt(x2d, ln_g, ln_b, w_all, cosf, sins, rsc, cw, halo0, n_batch, tm):
    rows = x2d.shape[0]
    tiles = rows // (n_batch * tm)
    c2 = lambda b, t: (0, 0)
    tile = lambda b, t: (b * tiles + t, 0)
    pos = lambda b, t: (t, 0)
    return pl.pallas_call(
        functools.partial(_front_prompt_body, tm),
        grid=(n_batch, tiles),
        in_specs=[
            pl.BlockSpec((tm, D_MODEL), tile),
            pl.BlockSpec((1, D_MODEL), c2),
            pl.BlockSpec((1, D_MODEL), c2),
            pl.BlockSpec((D_MODEL, W_ALL_COLS), c2, pipeline_mode=pl.Buffered(1)),
            pl.BlockSpec((tm, HEAD_DIM), pos),
            pl.BlockSpec((tm, HEAD_DIM), pos),
            pl.BlockSpec((tm, HEAD_DIM), c2),
            pl.BlockSpec((CONV_W, CONV_COLS), c2),
            pl.BlockSpec((HALO, CONV_COLS), c2),
        ],
        out_specs=[
            pl.BlockSpec((tm, A_COLS), tile),
            pl.BlockSpec((tm, B_COLS), tile),
            pl.BlockSpec((tm, HEAD_DIM), tile),
            pl.BlockSpec((1, HALO, CONV_COLS), lambda b, t: (b, 0, 0)),
        ],
        out_shape=[
            jax.ShapeDtypeStruct((rows, A_COLS), BF16),
            jax.ShapeDtypeStruct((rows, B_COLS), F32),
            jax.ShapeDtypeStruct((rows, HEAD_DIM), F32),
            jax.ShapeDtypeStruct((n_batch, HALO, CONV_COLS), F32),
        ],
        scratch_shapes=[pltpu.VMEM((HALO + tm, CONV_COLS), F32)],
        compiler_params=_cparams(("parallel", "arbitrary")),
        name="front_prompt",
    )(x2d, ln_g, ln_b, w_all, cosf, sins, rsc, cw, halo0)


def _seq_init(s0r_ref, s0g_ref, sr_ref, sg_ref):
    @pl.when(pl.program_id(1) == 0)
    def _():
        for j in range(sr_ref.shape[0]):
            sr_ref[j] = s0r_ref[...]
            sg_ref[j] = s0g_ref[...]


def _seq_step(ret_cd, ret, gdn, gnw, mix_ref, sr_out, sg_out, sr_ref, sg_ref):
    items = ret["items"]
    idx = range(len(items))
    s_r = [sr_ref[j, h] for j, h in items]
    s_g = [sg_ref[j, h] for j, h in items]
    bf = lambda t: t.astype(BF16)
    cat = lambda a, b, axis: jnp.concatenate([bf(a), bf(b)], axis)
    wk_s = _mm_each(gdn["wk"], s_g)
    o_r = _mm_each([cat(ret["scores"][i], ret["qd"][i], 1) for i in idx],
                   [cat(ret["v"][i], s_r[i], 0) for i in idx])
    u = [gdn["wv"][i] - wk_s[i] for i in idx]
    o_g = _mm_each([cat(gdn["qg"][i], gdn["attn"][i], 1) for i in idx],
                   [cat(s_g[i], u[i], 0) for i in idx])
    upd_g = _mm_each(gdn["ktT"], u)
    upd_r = _mm_each(ret["kdT"], ret["v"])
    for i, (j, h) in enumerate(items):
        cd = jnp.broadcast_to(gdn["cdr"][j][0:1, h:h + 1], (HEAD_DIM, HEAD_DIM))
        sg_ref[j, h] = cd * s_g[i] + upd_g[i]
        sr_ref[j, h] = ret_cd[h] * s_r[i] + upd_r[i]
    for i, (j, h) in enumerate(items):
        mix_ref[j, :, h * HEAD_DIM:(h + 1) * HEAD_DIM] = _rms_gate(o_r[i], ret["gate"][i])
        mix_ref[j, :, GROUP_W + h * HEAD_DIM:GROUP_W + (h + 1) * HEAD_DIM] = _rms_gate(
            o_g[i], gdn["gate"][i], gnw)

    @pl.when(pl.program_id(1) == pl.num_programs(1) - 1)
    def _():
        sr_out[...] = sr_ref[...]
        sg_out[...] = sg_ref[...]


def _meta_mixer_body(ret_cd, pr_ref, pg_ref, pab_ref, cos_ref, sin_ref, rsc_ref, dmat_ref,
                     cw_ref, alog_ref, dtb_ref, gnw_ref, s0r_ref, s0g_ref,
                     mix_ref, sr_out, sg_out, sr_ref, sg_ref, xc_ref):
    _seq_init(s0r_ref, s0g_ref, sr_ref, sg_ref)
    xc_ref[0:HALO, :] = jnp.zeros((HALO, CONV_COLS), F32)
    tok = _tokenwise("meta", pr_ref[...], pg_ref[...], cos_ref[...], sin_ref[...], rsc_ref[...],
                     cw_ref, xc_ref, None)
    tok["kdT"] = [t.T for t in tok["kd"]]
    ret, gdn = _chunk_prep("meta", [tok], [pab_ref[...]], dmat_ref, alog_ref[...], dtb_ref[...])
    _seq_step(ret_cd, ret, gdn, gnw_ref[...], mix_ref, sr_out, sg_out, sr_ref, sg_ref)


def _prompt_mixer_body(ret_cd, a_ref, b_ref, pab_ref, dmat_ref, alog_ref, dtb_ref, gnw_ref,
                       s0r_ref, s0g_ref, mix_ref, sr_out, sg_out, sr_ref, sg_ref):
    _seq_init(s0r_ref, s0g_ref, sr_ref, sg_ref)
    heads = range(N_HEADS)
    tiles = range(a_ref.shape[0])
    piece = lambda ref, j, group: [
        ref[j, :, group * GROUP_W + h * HEAD_DIM:group * GROUP_W + (h + 1) * HEAD_DIM] for h in heads]
    toks = [dict(rq=piece(a_ref, j, 0), rk=piece(a_ref, j, 1), rv=piece(a_ref, j, 2),
                 qd=piece(a_ref, j, 3), kdT=piece(a_ref, j, 4), gq=piece(b_ref, j, 0),
                 gk=piece(b_ref, j, 1), gv=piece(b_ref, j, 2), sgz=piece(a_ref, j, 5),
                 sgr=piece(a_ref, j, 6)) for j in tiles]
    ret, gdn = _chunk_prep("prompt", toks, [pab_ref[j] for j in tiles], dmat_ref, alog_ref[...],
                           dtb_ref[...])
    _seq_step(ret_cd, ret, gdn, gnw_ref[...], mix_ref, sr_out, sg_out, sr_ref, sg_ref)


def _seq_out(n_batch, n_chunks, nb):
    state = pl.BlockSpec((nb,) + _STATE_SHAPE, lambda b, n: (b, 0, 0, 0))
    out_specs = [pl.BlockSpec((nb, TILE, 2 * GROUP_W), lambda b, n: (b, n, 0)), state, state]
    out_shape = [
        jax.ShapeDtypeStruct((n_batch, n_chunks * TILE, 2 * GROUP_W), BF16),
        jax.ShapeDtypeStruct((n_batch,) + _STATE_SHAPE, F32),
        jax.ShapeDtypeStruct((n_batch,) + _STATE_SHAPE, F32),
    ]
    scratch = [pltpu.VMEM((nb,) + _STATE_SHAPE, F32), pltpu.VMEM((nb,) + _STATE_SHAPE, F32)]
    return out_specs, out_shape, scratch


def _meta_mixer(ret_cd, pr, pg, pab, cosf, sins, rsc, dmat, cw, alog, dtb, gnw, s0r, s0g):
    c2 = lambda b, n: (0, 0)
    c3 = lambda b, n: (0, 0, 0)
    out_specs, out_shape, state_scratch = _seq_out(1, 1, 1)
    return pl.pallas_call(
        functools.partial(_meta_mixer_body, ret_cd),
        grid=(1, 1),
        in_specs=[
            pl.BlockSpec((TILE, 4 * GROUP_W), c2),
            pl.BlockSpec((TILE, 4 * GROUP_W), c2),
            pl.BlockSpec((TILE, HEAD_DIM), c2),
            pl.BlockSpec((TILE, HEAD_DIM), c2),
            pl.BlockSpec((TILE, HEAD_DIM), c2),
            pl.BlockSpec((TILE, HEAD_DIM), c2),
            pl.BlockSpec((N_HEADS, TILE, TILE), c3),
            pl.BlockSpec((CONV_W, CONV_COLS), c2),
            pl.BlockSpec((1, HEAD_DIM), c2),
            pl.BlockSpec((1, HEAD_DIM), c2),
            pl.BlockSpec((1, HEAD_DIM), c2),
            pl.BlockSpec(_STATE_SHAPE, c3),
            pl.BlockSpec(_STATE_SHAPE, c3),
        ],
        out_specs=out_specs,
        out_shape=out_shape,
        scratch_shapes=state_scratch + [pltpu.VMEM((HALO + TILE, CONV_COLS), F32)],
        compiler_params=_cparams(("parallel", "arbitrary")),
        name="mixer_meta",
    )(pr, pg, pab, cosf, sins, rsc, dmat, cw, alog, dtb, gnw, s0r, s0g)


def _prompt_mixer(ret_cd, a, b, pab, dmat, alog, dtb, gnw, s0r, s0g, nb):
    n_batch, seq, _ = a.shape
    n_chunks = seq // TILE
    tile = lambda b, n: (b, n, 0)
    c2 = lambda b, n: (0, 0)
    c3 = lambda b, n: (0, 0, 0)
    out_specs, out_shape, state_scratch = _seq_out(n_batch, n_chunks, nb)
    return pl.pallas_call(
        functools.partial(_prompt_mixer_body, ret_cd),
        grid=(n_batch // nb, n_chunks),
        in_specs=[
            pl.BlockSpec((nb, TILE, A_COLS), tile),
            pl.BlockSpec((nb, TILE, B_COLS), tile),
            pl.BlockSpec((nb, TILE, HEAD_DIM), tile),
            pl.BlockSpec((N_HEADS, TILE, TILE), c3),
            pl.BlockSpec((1, HEAD_DIM), c2),
            pl.BlockSpec((1, HEAD_DIM), c2),
            pl.BlockSpec((1, HEAD_DIM), c2),
            pl.BlockSpec(_STATE_SHAPE, c3),
            pl.BlockSpec(_STATE_SHAPE, c3),
        ],
        out_specs=out_specs,
        out_shape=out_shape,
        scratch_shapes=state_scratch,
        compiler_params=_cparams(("parallel", "arbitrary")),
        name="mixer_prompt",
    )(a, b, pab, dmat, alog, dtb, gnw, s0r, s0g)


SAMPLE_PER_TILE = TILE // SAMPLE_GROUP


def _stack_by_group(xt):
    shape3 = (SAMPLE_PER_TILE, HEAD_DIM, TILE)
    keep = lax.broadcasted_iota(jnp.int32, shape3, 0) == (lax.broadcasted_iota(jnp.int32, shape3, 2) >> 3)
    stacked = jnp.where(keep, jnp.broadcast_to(xt[None], shape3), 0.0)
    return stacked.reshape(SAMPLE_PER_TILE * HEAD_DIM, TILE)


def _sample_mixer_body(ret_cd, pr_ref, pg_ref, pab_ref, cos_ref, sin_ref, rsc_ref, dmat_ref, cw_ref,
                       alog_ref, dtb_ref, gnw_ref, convpad_ref, sr_in, sg_in,
                       mix_ref, sr_out, sg_out,
                       xc_ref, qd_s, wk_s, qg_s, wv_s, cd_s, inter_s, u_s, qs_s):
    heads = range(N_HEADS)
    xc_ref[0:HALO, :] = jnp.zeros((HALO, CONV_COLS), F32)
    tok = _tokenwise("sample", pr_ref[...], pg_ref[...], cos_ref[...], sin_ref[...], rsc_ref[...],
                     cw_ref, xc_ref, convpad_ref[...])
    tok["kdT"] = [t.T for t in tok["kd"]]
    ret, gdn = _chunk_prep("sample", [tok], [pab_ref[...]], dmat_ref, alog_ref[...], dtb_ref[...])
    for h in heads:
        qd_s[h] = ret["qd"][h]
        wk_s[h] = gdn["wk"][h]
        qg_s[h] = gdn["qg"][h]
        wv_s[h] = gdn["wv"][h]
    cd_s[...] = gdn["cdr"][0]

    def per_batch(b, carry):
        rows = pl.ds(pl.multiple_of(b * SAMPLE_GROUP, SAMPLE_GROUP), SAMPLE_GROUP)
        cd_rows = cd_s[rows, :]
        for h in heads:
            inter_s[h, rows, :] = _mm(qd_s[h, rows, :], sr_in[b, h])
            s = sg_in[b, h]
            both = _mm(jnp.concatenate([wk_s[h, rows, :], qg_s[h, rows, :]], 0), s)
            u_s[h, rows, :] = wv_s[h, rows, :] - both[:SAMPLE_GROUP]
            qs_s[h, rows, :] = both[SAMPLE_GROUP:]
            cd = jnp.broadcast_to(cd_rows[SAMPLE_GROUP - 1:SAMPLE_GROUP, h:h + 1], (HEAD_DIM, HEAD_DIM))
            sg_out[b, h] = cd * s
        return carry

    lax.fori_loop(0, SAMPLE_PER_TILE, per_batch, 0)

    out_rows = SAMPLE_PER_TILE * SAMPLE_TOKENS
    out_row = _iota2((out_rows, TILE), 0)
    token_row = ((out_row >> 2) << 3) + (SAMPLE_GROUP - SAMPLE_TOKENS) + (out_row & (SAMPLE_TOKENS - 1))
    select = (_iota2((out_rows, TILE), 1) == token_row).astype(BF16)
    compact = lambda t: jnp.dot(select, t, preferred_element_type=F32).astype(BF16)

    state_shape = (SAMPLE_PER_TILE, HEAD_DIM, HEAD_DIM)
    for h in heads:
        o = _mm(ret["scores"][h], ret["v"][h]) + inter_s[h]
        upd = _mm(_stack_by_group(ret["kdT"][h]), ret["v"][h]).reshape(state_shape)
        sr_out[:, h] = ret_cd[h] * sr_in[:, h] + upd
        mix_ref[:, h * HEAD_DIM:(h + 1) * HEAD_DIM] = compact(_rms_gate(o, ret["gate"][h]))

    for h in heads:
        u = u_s[h]
        o = qs_s[h] + _mm(gdn["attn"][h], u)
        upd = _mm(_stack_by_group(gdn["ktT"][h]), u).reshape(state_shape)
        sg_out[:, h] = sg_out[:, h] + upd
        mix_ref[:, GROUP_W + h * HEAD_DIM:GROUP_W + (h + 1) * HEAD_DIM] = compact(_rms_gate(
            o, gdn["gate"][h], gnw_ref[...]))


def _sample_mixer(ret_cd, pr, pg, pab, cosf, sins, rsc, dmat, cw, alog, dtb, gnw, convpad, sr, sg):
    n_tiles = pr.shape[0] // TILE
    tile = lambda i: (i, 0)
    c2 = lambda i: (0, 0)
    c3 = lambda i: (0, 0, 0)
    state = pl.BlockSpec((SAMPLE_PER_TILE,) + _STATE_SHAPE, lambda i: (i, 0, 0, 0))
    head_scratch = pltpu.VMEM((N_HEADS, TILE, HEAD_DIM), F32)
    return pl.pallas_call(
        functools.partial(_sample_mixer_body, ret_cd),
        grid=(n_tiles,),
        in_specs=[
            pl.BlockSpec((TILE, 4 * GROUP_W), tile),
            pl.BlockSpec((TILE, 4 * GROUP_W), tile),
            pl.BlockSpec((TILE, HEAD_DIM), tile),
            pl.BlockSpec((TILE, HEAD_DIM), c2),
            pl.BlockSpec((TILE, HEAD_DIM), c2),
            pl.BlockSpec((TILE, HEAD_DIM), c2),
            pl.BlockSpec((N_HEADS, TILE, TILE), c3),
            pl.BlockSpec((CONV_W, CONV_COLS), c2),
            pl.BlockSpec((1, HEAD_DIM), c2),
            pl.BlockSpec((1, HEAD_DIM), c2),
            pl.BlockSpec((1, HEAD_DIM), c2),
            pl.BlockSpec((TILE, CONV_COLS), tile),
            state, state,
        ],
        out_specs=[pl.BlockSpec((SAMPLE_PER_TILE * SAMPLE_TOKENS, 2 * GROUP_W), tile), state, state],
        out_shape=[
            jax.ShapeDtypeStruct((n_tiles * SAMPLE_PER_TILE * SAMPLE_TOKENS, 2 * GROUP_W), BF16),
            jax.ShapeDtypeStruct(sr.shape, F32),
            jax.ShapeDtypeStruct(sg.shape, F32),
        ],
        scratch_shapes=[pltpu.VMEM((HALO + TILE, CONV_COLS), F32)] + [head_scratch] * 4
                       + [pltpu.VMEM((TILE, HEAD_DIM), F32)] + [head_scratch] * 3,
        compiler_params=_cparams(("parallel",)),
        name="mixer_sample",
    )(pr, pg, pab, cosf, sins, rsc, dmat, cw, alog, dtb, gnw, convpad, sr, sg)


def _rotary_tables(pos):
    half = HEAD_DIM // 2
    inv = ROPE_BASE ** (-np.arange(half, dtype=np.float64) / half)
    ang = np.asarray(pos, np.float64)[:, None] * inv[None, :]
    cos, sin = np.cos(ang), np.sin(ang)
    return (jnp.asarray(np.concatenate([cos, cos], -1), F32),
            jnp.asarray(np.concatenate([-sin, sin], -1), F32))


def _retention_tables(seg, pos, valid, seg_len):
    gamma = 1.0 - 2.0 ** (-5.0 - np.arange(N_HEADS, dtype=np.float64))
    posf = np.asarray(pos, np.float64)
    rel = posf[:, None] - posf[None, :]
    causal = (seg[:, None] == seg[None, :]) & (rel >= 0)
    dmat = np.where(causal[None], gamma[:, None, None] ** np.where(causal, rel, 0.0)[None], 0.0)
    q_scale = gamma[None, :] ** (posf[:, None] + 1.0)
    k_scale = np.where(valid[:, None], gamma[None, :] ** (seg_len - 1.0 - posf[:, None]), 0.0)
    rsc = np.concatenate([q_scale, k_scale, np.zeros((TILE, HEAD_DIM - 2 * N_HEADS))], -1)
    chunk_decay = tuple(float(g ** seg_len) for g in gamma)
    return jnp.asarray(dmat, F32), jnp.asarray(rsc, F32), chunk_decay


def _pad_lanes(v):
    return jnp.pad(v.astype(F32), (0, HEAD_DIM - v.shape[0]))[None, :]


def kernel(x_prompt, x_sample, state_ret, state_gdn, state_conv, meta_tokens, emb_ln_g, emb_ln_b,
           w_in, conv_w, a_log, dt_bias, gdn_norm_w, w_out, ln1_g, ln1_b, w_gate_up, w_down,
           ln2_g, ln2_b):
    n_batch, seq, _ = x_prompt.shape
    dec_batch, dec_seq, _ = x_sample.shape
    assert seq % TILE == 0 and dec_seq == SAMPLE_TOKENS and N_META <= TILE
    n_chunks = seq // TILE
    layer = 0

    w_in_l = w_in[layer]
    w_all = _weight_prep(jnp.swapaxes(w_in_l, 0, 1))
    w_out_b = w_out[layer].astype(BF16)
    w_gu_b = w_gate_up[layer].astype(BF16)
    w_down_b = w_down[layer].astype(BF16)
    row = lambda v: v.astype(F32)[None, :]
    eg, eb = row(emb_ln_g), row(emb_ln_b)
    cw = conv_w[layer].astype(F32)
    alog, dtb, gnw = _pad_lanes(a_log[layer]), _pad_lanes(dt_bias[layer]), row(gdn_norm_w[layer])

    xp = x_prompt.reshape(n_batch * seq, D_MODEL)
    xm = jnp.pad(meta_tokens.astype(F32), ((0, TILE - N_META), (0, 0)))
    xs = jnp.pad(x_sample, ((0, 0), (SAMPLE_GROUP - dec_seq, 0), (0, 0))).reshape(
        dec_batch * SAMPLE_GROUP, D_MODEL)
    convpad = jnp.pad(state_conv[layer].astype(F32), ((0, 0), (1, SAMPLE_GROUP - CONV_W), (0, 0))).reshape(
        dec_batch * SAMPLE_GROUP, CONV_COLS)

    tile_idx = np.arange(TILE)
    cos_p, sin_p = _rotary_tables(N_META + np.arange(seq))
    cos_m, sin_m = _rotary_tables(tile_idx)
    tok = (tile_idx % SAMPLE_GROUP) - (SAMPLE_GROUP - dec_seq)
    cos_s, sin_s = _rotary_tables(PAST_LEN + np.maximum(tok, 0))
    zeros_i = np.zeros((TILE,), np.int32)
    all_valid = np.ones((TILE,), bool)
    dmat_p, rsc_p, cd_p = _retention_tables(zeros_i, tile_idx, all_valid, float(TILE))
    dmat_m, rsc_m, cd_m = _retention_tables(zeros_i, tile_idx, tile_idx < N_META, float(N_META))
    dmat_s, rsc_s, cd_s = _retention_tables(tile_idx // SAMPLE_GROUP, tok, tok >= 0, float(dec_seq))

    zero_state = jnp.zeros(_STATE_SHAPE, F32)
    pr_m, pg_m, pab_m = _front(xm, eg, eb, w_all, TILE)
    _, sr_m, sg_m = _meta_mixer(cd_m, pr_m, pg_m, pab_m, cos_m, sin_m, rsc_m, dmat_m, cw, alog, dtb, gnw,
                                zero_state, zero_state)
    halo_p = pg_m[N_META - HALO:N_META, :CONV_COLS]

    tm_front = 512
    a_p, b_p, pab_p, tail_p = _front_prompt(xp, eg, eb, w_all, cos_p, sin_p,
                                            jnp.tile(rsc_p, (tm_front // TILE, 1)), cw, halo_p,
                                            n_batch, tm_front)
    per_seq = lambda t: t.reshape(n_batch, seq, t.shape[-1])
    mix_p, sr_p, sg_p = _prompt_mixer(cd_p, per_seq(a_p), per_seq(b_p), per_seq(pab_p), dmat_p, alog, dtb,
                                      gnw, sr_m[0], sg_m[0], MIXER_SEQS_PER_STEP)
    mix_p = mix_p.reshape(n_batch * seq, 2 * GROUP_W)

    pr_s, pg_s, pab_s = _front(xs, eg, eb, w_all, 512)
    mix_s, sr_s, sg_s = _sample_mixer(cd_s, pr_s, pg_s, pab_s, cos_s, sin_s, rsc_s, dmat_s, cw, alog,
                                      dtb, gnw, convpad, state_ret[layer].astype(F32),
                                      state_gdn[layer].astype(F32))

    back = functools.partial(_back, eg=eg, eb=eb, w_out=w_out_b, g1=row(ln1_g[layer]), b1=row(ln1_b[layer]),
                             w_gu=w_gu_b, w_down=w_down_b, g2=row(ln2_g[layer]), b2=row(ln2_b[layer]),
                             tm=512)
    y_p = back(xp, mix_p).reshape(n_batch, seq, D_MODEL)
    y_s = back(x_sample.reshape(dec_batch * dec_seq, D_MODEL), mix_s).reshape(dec_batch, dec_seq, D_MODEL)

    conv_p = tail_p[:, HALO - (CONV_W - 1):, :]
    conv_s = pg_s.reshape(dec_batch, SAMPLE_GROUP, 4 * GROUP_W)[:, SAMPLE_GROUP - (CONV_W - 1):, :CONV_COLS]
    return (y_p, y_s, sr_p[None], sg_p[None], conv_p[None], sr_s[None], sg_s[None], conv_s[None])
```

```python
import functools

import numpy as np
import jax
import jax.numpy as jnp
from jax import lax
from jax.experimental import pallas as pl
from jax.experimental.pallas import tpu as pltpu

F32 = jnp.float32
BF16 = jnp.bfloat16

D_MODEL = 1024
N_META = 16
N_HEADS = 4
HEAD_DIM = 128
GROUP_W = N_HEADS * HEAD_DIM
CONV_W = 4
CONV_COLS = 3 * GROUP_W
D_FF = 2816
PAST_LEN = 16384
ROPE_BASE = 10000.0
LN_EPS = 1e-5
RMS_EPS = 1e-6
ALPHA = 2.0 ** 0.25
TILE = 128
SAMPLE_GROUP = 8
SAMPLE_TOKENS = 4
HALO = 8
MIXER_SEQS_PER_STEP = 4
BACK_PARTS = 2
FRONT_LAG = 2
VMEM_LIMIT = 56 * 1024 * 1024


def _cparams(sem):
    return pltpu.CompilerParams(dimension_semantics=sem, vmem_limit_bytes=VMEM_LIMIT)


def _layer_norm(x, g, b):
    mu = jnp.mean(x, -1, keepdims=True)
    xc = x - mu
    var = jnp.mean(xc * xc, -1, keepdims=True)
    return xc * lax.rsqrt(var + LN_EPS) * g + b


def _sigmoid(x):
    return 1.0 / (1.0 + jnp.exp(-x))


def _silu(x):
    return x * _sigmoid(x)


def _softplus(x):
    return jnp.maximum(x, 0.0) + jnp.log(1.0 + jnp.exp(-jnp.abs(x)))


def _mm(a, b):
    return jnp.dot(a.astype(BF16), b.astype(BF16), preferred_element_type=F32)


def _mm_nt(a, b):
    return lax.dot_general(a.astype(BF16), b.astype(BF16), (((1,), (1,)), ((), ())),
                           preferred_element_type=F32)


def _mm_each(xs, ys):
    return [_mm(x, y) for x, y in zip(xs, ys)]


def _mm_split3(m01, x):
    x1 = x.astype(BF16)
    r1 = x - x1.astype(F32)
    x2 = r1.astype(BF16)
    x3 = (r1 - x2.astype(F32)).astype(BF16)
    dot = functools.partial(jnp.dot, preferred_element_type=F32)
    return (dot(m01, x3) + dot(m01, x2)) + dot(m01, x1)


def _lane_bcast(x, lane):
    return jnp.broadcast_to(x[:, lane:lane + 1], (x.shape[0], HEAD_DIM))


def _head_cols(x, base, h):
    return x[:, base + h * HEAD_DIM:base + (h + 1) * HEAD_DIM]


W_ALL_COLS = 9 * GROUP_W


AB_ROWS = 8


def _weight_prep_body(q_ref, k_ref, v_ref, g_ref, ab_ref, o_ref):
    j = pl.program_id(0)

    @pl.when(j < 2 * N_HEADS)
    def _():
        for piece, ref in enumerate((q_ref, k_ref, v_ref, g_ref)):
            o_ref[:, piece * HEAD_DIM:(piece + 1) * HEAD_DIM] = ref[...].T.astype(BF16)

    @pl.when(j == 2 * N_HEADS)
    def _():
        ab = jnp.concatenate([ab_ref[...], jnp.zeros((HEAD_DIM - AB_ROWS, D_MODEL), F32)], 0)
        o_ref[:, :HEAD_DIM] = ab.T.astype(BF16)
        o_ref[:, HEAD_DIM:] = jnp.zeros((D_MODEL, GROUP_W - HEAD_DIM), BF16)


def _weight_prep(w_in_t):
    assert w_in_t.shape == (8 * GROUP_W + AB_ROWS, D_MODEL)
    def piece_spec(piece):
        def index(j):
            ret_block = piece * N_HEADS + j
            gdn_block = N_HEADS * j + piece
            return (jnp.where(j < N_HEADS, ret_block, jnp.where(j < 2 * N_HEADS, gdn_block, 0)), 0)
        return pl.BlockSpec((HEAD_DIM, D_MODEL), index)
    ab_spec = pl.BlockSpec((AB_ROWS, D_MODEL), lambda j: (8 * GROUP_W // AB_ROWS, 0))
    return pl.pallas_call(
        _weight_prep_body,
        grid=(2 * N_HEADS + 1,),
        in_specs=[piece_spec(p) for p in range(4)] + [ab_spec],
        out_specs=pl.BlockSpec((D_MODEL, GROUP_W), lambda j: (0, j)),
        out_shape=jax.ShapeDtypeStruct((D_MODEL, W_ALL_COLS), BF16),
        compiler_params=_cparams(("arbitrary",)),
        name="weight_prep",
    )(w_in_t, w_in_t, w_in_t, w_in_t, w_in_t)


def _front_body(x_ref, g_ref, b_ref, w_ref, pr_ref, pg_ref, pab_ref):
    h = _layer_norm(x_ref[...], g_ref[...], b_ref[...]).astype(BF16)
    pr_ref[...] = jnp.dot(h, w_ref[:, :4 * GROUP_W], preferred_element_type=F32)
    pg_ref[...] = jnp.dot(h, w_ref[:, 4 * GROUP_W:8 * GROUP_W], preferred_element_type=F32)
    pab_ref[...] = jnp.dot(h, w_ref[:, 8 * GROUP_W:8 * GROUP_W + HEAD_DIM], preferred_element_type=F32)


def _front(x2d, ln_g, ln_b, w_all, tm):
    rows = x2d.shape[0]
    const = lambda i: (0, 0)
    tile = lambda i: (i, 0)
    return pl.pallas_call(
        _front_body,
        grid=(rows // tm,),
        in_specs=[
            pl.BlockSpec((tm, D_MODEL), tile),
            pl.BlockSpec((1, D_MODEL), const),
            pl.BlockSpec((1, D_MODEL), const),
            pl.BlockSpec((D_MODEL, W_ALL_COLS), const),
        ],
        out_specs=[
            pl.BlockSpec((tm, 4 * GROUP_W), tile),
            pl.BlockSpec((tm, 4 * GROUP_W), tile),
            pl.BlockSpec((tm, HEAD_DIM), tile),
        ],
        out_shape=[
            jax.ShapeDtypeStruct((rows, 4 * GROUP_W), F32),
            jax.ShapeDtypeStruct((rows, 4 * GROUP_W), F32),
            jax.ShapeDtypeStruct((rows, HEAD_DIM), F32),
        ],
        compiler_params=_cparams(("parallel",)),
        name="front",
    )(x2d, ln_g, ln_b, w_all)


def _back_body(x_ref, mix_ref, eg_ref, eb_ref, wo_ref, g1_ref, b1_ref, wgu_ref, wd_ref,
               g2_ref, b2_ref, y_ref):
    part = x_ref.shape[0] // BACK_PARTS
    parts = [slice(i * part, (i + 1) * part) for i in range(BACK_PARTS)]
    dot = functools.partial(jnp.dot, preferred_element_type=F32)
    h = [_layer_norm(x_ref[p, :], eg_ref[...], eb_ref[...]) for p in parts]
    mp = [dot(mix_ref[p, :], wo_ref[...]) for p in parts]
    h1 = [_layer_norm(ALPHA * a + b, g1_ref[...], b1_ref[...]) for a, b in zip(h, mp)]
    gu = [dot(a.astype(BF16), wgu_ref[...]) for a in h1]
    act = [(_silu(a[:, :D_FF]) * a[:, D_FF:]).astype(BF16) for a in gu]
    ff = [dot(a, wd_ref[...]) for a in act]
    for p, a, b in zip(parts, h1, ff):
        y_ref[p, :] = _layer_norm(ALPHA * a + b, g2_ref[...], b2_ref[...])


def _back(x2d, mix, eg, eb, w_out, g1, b1, w_gu, w_down, g2, b2, tm):
    rows = x2d.shape[0]
    const = lambda i: (0, 0)
    tile = lambda i: (i, 0)
    single = pl.Buffered(1)
    vec = pl.BlockSpec((1, D_MODEL), const)
    return pl.pallas_call(
        _back_body,
        grid=(rows // tm,),
        in_specs=[
            pl.BlockSpec((tm, D_MODEL), tile),
            pl.BlockSpec((tm, D_MODEL), tile),
            vec, vec,
            pl.BlockSpec((D_MODEL, D_MODEL), const, pipeline_mode=single),
            vec, vec,
            pl.BlockSpec((D_MODEL, 2 * D_FF), const, pipeline_mode=single),
            pl.BlockSpec((D_FF, D_MODEL), const, pipeline_mode=single),
            vec, vec,
        ],
        out_specs=pl.BlockSpec((tm, D_MODEL), tile),
        out_shape=jax.ShapeDtypeStruct((rows, D_MODEL), F32),
        compiler_params=_cparams(("parallel",)),
        name="back",
    )(x2d, mix, eg, eb, w_out, g1, b1, w_gu, w_down, g2, b2)


def _iota2(shape, dim):
    return lax.broadcasted_iota(jnp.int32, shape, dim)


def _valid_rows(variant, shape):
    r = _iota2(shape, 0)
    if variant == "meta":
        return r < N_META
    if variant == "sample":
        return (r & (SAMPLE_GROUP - 1)) >= SAMPLE_GROUP - SAMPLE_TOKENS
    return None


def _mask_rows(valid, x):
    return x if valid is None else jnp.where(valid, x, 0.0)


def _tri_inverse(variant, nmats, row, col, between=None):
    between = between or (lambda: None)
    eye = (row == col).astype(F32)
    if variant == "sample":
        n2 = _mm_each(nmats, nmats)
        ps = [eye + n for n in nmats]
        return [p + t for p, t in zip(ps, _mm_each(ps, n2))]
    base_log2 = 3
    in_block = (row >> base_log2) == (col >> base_log2)
    ds = [jnp.where(in_block, n, 0.0) for n in nmats]
    ps = [eye + d for d in ds]
    for _ in range(base_log2 - 1):
        ds = _mm_each(ds, ds)
        between()
        ps = [p + t for p, t in zip(ps, _mm_each(ps, ds))]
        between()
    live_rows = N_META if variant == "meta" else TILE
    s = base_log2
    while (1 << s) < live_rows:
        size = 1 << s
        lower_left = (((row >> (s + 1)) == (col >> (s + 1))) & (((row >> s) & 1) == 1)
                      & (((col >> s) & 1) == 0))
        cs = [jnp.where(lower_left, n, 0.0) for n in nmats]
        lower = [slice(start + size, start + 2 * size) for start in range(0, TILE, 2 * size)]
        upper = [slice(start, start + size) for start in range(0, TILE, 2 * size)]
        p_low = [jnp.concatenate([p[rows, :] for rows in lower], 0) for p in ps]
        half = _mm_each(p_low, cs)
        between()
        fix = _mm_each(half, ps)
        between()
        merged = []
        for p, f in zip(ps, fix):
            pieces = []
            for i, (up, lo) in enumerate(zip(upper, lower)):
                pieces += [p[up, :], p[lo, :] + f[i * size:(i + 1) * size, :]]
            merged.append(jnp.concatenate(pieces, 0))
        ps = merged
        s += 1
    return ps


def _rotary(t, cosf, sins):
    return t * cosf + pltpu.roll(t, HEAD_DIM // 2, 1) * sins


def _ret_head_ops(valid, q, k, v, gate, cosf, sins, q_scale, k_scale):
    rq = _mask_rows(valid, _rotary(q, cosf, sins))
    rk = _mask_rows(valid, _rotary(k, cosf, sins) * (HEAD_DIM ** -0.5))
    return dict(rq=rq, rk=rk, rv=_mask_rows(valid, v), qd=rq * q_scale, kd=rk * k_scale, sgr=_silu(gate))


def _short_conv(x, cw_ref, xc_ref, col0, row0=0):
    rows, ncols = x.shape
    cols = slice(col0, col0 + ncols)
    xc_ref[HALO + row0:HALO + row0 + rows, cols] = x
    conv = None
    for w in range(CONV_W):
        off = HALO + row0 - (CONV_W - 1) + w
        term = xc_ref[off:off + rows, cols] * cw_ref[w:w + 1, cols]
        conv = term if conv is None else conv + term
    return _silu(conv)


def _l2_normalize(t):
    return t * lax.rsqrt(jnp.sum(t * t, -1, keepdims=True) + RMS_EPS)


def _tokenwise(variant, pret, pgdn, cosf, sins, rsc, cw_ref, xc_ref, convpad):
    rows = pret.shape[0]
    heads = range(N_HEADS)
    valid = _valid_rows(variant, (rows, HEAD_DIM))
    ret = [_ret_head_ops(valid, *(_head_cols(pret, h * GROUP_W, g) for g in range(4)), cosf, sins,
                         _lane_bcast(rsc, h), _lane_bcast(rsc, N_HEADS + h)) for h in heads]
    x = pgdn[:, :CONV_COLS]
    if variant == "sample":
        x = jnp.where(_valid_rows(variant, x.shape), x, convpad)
    conv = _short_conv(x, cw_ref, xc_ref, 0)
    gq = [_mask_rows(valid, _l2_normalize(_head_cols(conv, 0, h)) * (HEAD_DIM ** -0.5)) for h in heads]
    gk = [_mask_rows(valid, _l2_normalize(_head_cols(conv, GROUP_W, h))) for h in heads]
    gv = [_mask_rows(valid, _head_cols(conv, 2 * GROUP_W, h)) for h in heads]
    sgz = [_silu(_head_cols(pgdn, 3 * GROUP_W, h)) for h in heads]
    tok = {key: [r[key] for r in ret] for key in ret[0]}
    tok.update(gq=gq, gk=gk, gv=gv, sgz=sgz)
    return tok


def _chunk_prep(variant, toks, pabs, dmat_ref, alog, dtb, between=None):
    shape = (TILE, HEAD_DIM)
    tiles = range(len(toks))
    items = [(j, h) for j in tiles for h in range(N_HEADS)]
    pick = lambda key: [toks[j][key][h] for j, h in items]
    row = _iota2(shape, 0)
    col = _iota2(shape, 1)
    valid = _valid_rows(variant, shape)
    if variant == "sample":
        same = (row >> 3) == (col >> 3)
        incl = same & (row >= col)
        strict = same & (row > col)
    else:
        incl = row >= col
        strict = row > col

    incl01 = incl.astype(BF16)
    beta_all, gcum, gam, ktail, cdr, gcum_t = [], [], [], [], [], []
    for j in tiles:
        g_all = _mask_rows(valid, -jnp.exp(alog) * _softplus(pabs[j] + dtb))
        beta_all.append(_mask_rows(valid, _sigmoid(pabs[j])))
        if variant == "sample":
            both = _mm_split3(jnp.concatenate([incl01, same.astype(BF16)], 0), g_all)
            gc, gseg = both[:TILE], both[TILE:]
        else:
            gc = _mm_split3(incl01, g_all)
            gseg = jnp.broadcast_to(gc[TILE - 1:TILE, :], shape)
        gcum.append(gc)
        gam.append(jnp.exp(gc))
        ktail.append(jnp.exp(gseg - gc))
        cdr.append(jnp.exp(gseg))
        gcum_t.append(gc.T)

    gq, gk, gv = pick("gq"), pick("gk"), pick("gv")
    n_items = range(len(items))
    dec = [jnp.where(incl, jnp.exp(jnp.minimum(
        _lane_bcast(gcum[j], h) - jnp.broadcast_to(gcum_t[j][h:h + 1, :], shape), 0.0)), 0.0)
        for j, h in items]
    bcol = [_lane_bcast(beta_all[j], N_HEADS + h) for j, h in items]
    gamc = [_lane_bcast(gam[j], h) for j, h in items]
    kk = [_mm_nt(gk[i], gk[i]) for i in n_items]
    qk = [_mm_nt(gq[i], gk[i]) for i in n_items]
    nmats = [-(jnp.where(strict, kk[i] * dec[i], 0.0) * bcol[i]) for i in n_items]
    tinv = _tri_inverse(variant, nmats, row, col, between)
    rhs = [jnp.concatenate([gv[i] * bcol[i], gk[i] * (bcol[i] * gamc[i])], 1) for i in n_items]
    sol = _mm_each(tinv, rhs)
    while between is not None and between():
        pass
    ret = dict(
        items=items,
        scores=[_mm_nt(toks[j]["rq"][h], toks[j]["rk"][h]) * dmat_ref[h] for j, h in items],
        qd=pick("qd"), kdT=pick("kdT"), v=pick("rv"), gate=pick("sgr"),
    )
    gdn = dict(
        items=items,
        wv=[t[:, :HEAD_DIM] for t in sol],
        wk=[t[:, HEAD_DIM:] for t in sol],
        attn=[qk[i] * dec[i] for i in n_items],
        qg=[gq[i] * gamc[i] for i in n_items],
        ktT=[(gk[i] * _lane_bcast(ktail[j], h)).T for i, (j, h) in enumerate(items)],
        cdr=cdr,
        gate=pick("sgz"),
    )
    return ret, gdn


def _rms_gate(o, gate, weight=None):
    o = o * lax.rsqrt(jnp.mean(o * o, -1, keepdims=True) + RMS_EPS)
    if weight is not None:
        o = o * weight
    return (o * gate).astype(BF16)


A_COLS = 7 * GROUP_W
B_COLS = 3 * GROUP_W
_STATE_SHAPE = (N_HEADS, HEAD_DIM, HEAD_DIM)


def _front_prompt_body(tm, x_ref, g_ref, b_ref, w_ref, cos_ref, sin_ref, rsc_ref,
                       cw_ref, halo0_ref, a_ref, b_out_ref, pab_ref, tail_ref, xc_ref):
    @pl.when(pl.program_id(1) == 0)
    def _():
        xc_ref[0:HALO, :] = halo0_ref[...]

    def put(ref, r, group, h, val):
        ref[r * TILE:(r + 1) * TILE,
            group * GROUP_W + h * HEAD_DIM:group * GROUP_W + (h + 1) * HEAD_DIM] = val.astype(ref.dtype)

    def finish_ret_head(h, r, val):
        rows = slice(r * TILE, (r + 1) * TILE)
        rsc = rsc_ref[rows, :]
        ops = _ret_head_ops(None, *(_head_cols(val, 0, g) for g in range(4)), cos_ref[rows, :],
                            sin_ref[rows, :], _lane_bcast(rsc, h), _lane_bcast(rsc, N_HEADS + h))
        for group, key in enumerate(("rq", "rk", "rv", "qd")):
            put(a_ref, r, group, h, ops[key])
        put(a_ref, r, 4, h, ops["kd"].T)
        put(a_ref, r, 6, h, ops["sgr"])

    def finish_conv_group(group, r, val):
        conv = _short_conv(val, cw_ref, xc_ref, group * GROUP_W, r * TILE)
        for h in range(N_HEADS):
            t = _head_cols(conv, 0, h)
            if group == 0:
                t = _l2_normalize(t) * (HEAD_DIM ** -0.5)
            elif group == 1:
                t = _l2_normalize(t)
            put(b_out_ref, r, group, h, t)

    def finish_gate(r, val):
        for h in range(N_HEADS):
            put(a_ref, r, 5, h, _silu(_head_cols(val, 0, h)))

    def finish_ab(r, val):
        pab_ref[r * TILE:(r + 1) * TILE, :] = val

    ret_group = lambda h: (h * GROUP_W, GROUP_W, functools.partial(finish_ret_head, h))
    gdn_group = lambda g: ((N_HEADS + g) * GROUP_W, GROUP_W, functools.partial(finish_conv_group, g))
    plan = [ret_group(0), gdn_group(0), ret_group(1), gdn_group(1), ret_group(2), gdn_group(2),
            ret_group(3), (7 * GROUP_W, GROUP_W, finish_gate), (8 * GROUP_W, HEAD_DIM, finish_ab)]
    pending = []
    for r in range(tm // TILE):
        hidden = _layer_norm(x_ref[r * TILE:(r + 1) * TILE, :], g_ref[...], b_ref[...]).astype(BF16)
        for col, width, finish in plan:
            val = jnp.dot(hidden, w_ref[:, col:col + width], preferred_element_type=F32)
            pending.append(functools.partial(finish, r, val))
            if len(pending) > FRONT_LAG:
                pending.pop(0)()
    for finish in pending:
        finish()
    tail = xc_ref[tm:tm + HALO, :]
    xc_ref[0:HALO, :] = tail
    tail_ref[0] = tail


def _front_prompt(x2d, ln_g, ln_b, w_all, cosf, sins, rsc, cw, halo0, n_batch, tm):
    rows = x2d.shape[0]
    tiles = rows // (n_batch * tm)
    c2 = lambda b, t: (0, 0)
    tile = lambda b, t: (b * tiles + t, 0)
    pos = lambda b, t: (t, 0)
    return pl.pallas_call(
        functools.partial(_front_prompt_body, tm),
        grid=(n_batch, tiles),
        in_specs=[
            pl.BlockSpec((tm, D_MODEL), tile),
            pl.BlockSpec((1, D_MODEL), c2),
            pl.BlockSpec((1, D_MODEL), c2),
            pl.BlockSpec((D_MODEL, W_ALL_COLS), c2, pipeline_mode=pl.Buffered(1)),
            pl.BlockSpec((tm, HEAD_DIM), pos),
            pl.BlockSpec((tm, HEAD_DIM), pos),
            pl.BlockSpec((tm, HEAD_DIM), c2),
            pl.BlockSpec((CONV_W, CONV_COLS), c2),
            pl.BlockSpec((HALO, CONV_COLS), c2),
        ],
        out_specs=[
            pl.BlockSpec((tm, A_COLS), tile),
            pl.BlockSpec((tm, B_COLS), tile),
            pl.BlockSpec((tm, HEAD_DIM), tile),
            pl.BlockSpec((1, HALO, CONV_COLS), lambda b, t: (b, 0, 0)),
        ],
        out_shape=[
            jax.ShapeDtypeStruct((rows, A_COLS), BF16),
            jax.ShapeDtypeStruct((rows, B_COLS), F32),
            jax.ShapeDtypeStruct((rows, HEAD_DIM), F32),
            jax.ShapeDtypeStruct((n_batch, HALO, CONV_COLS), F32),
        ],
        scratch_shapes=[pltpu.VMEM((HALO + tm, CONV_COLS), F32)],
        compiler_params=_cparams(("parallel", "arbitrary")),
        name="front_prompt",
    )(x2d, ln_g, ln_b, w_all, cosf, sins, rsc, cw, halo0)


def _seq_init(s0r_ref, s0g_ref, sr_ref, sg_ref):
    @pl.when(pl.program_id(1) == 0)
    def _():
        for j in range(sr_ref.shape[0]):
            sr_ref[j] = s0r_ref[...]
            sg_ref[j] = s0g_ref[...]


def _seq_step(ret_cd, ret, gdn, gnw, mix_ref, sr_out, sg_out, sr_ref, sg_ref):
    items = ret["items"]
    idx = range(len(items))
    s_r = [sr_ref[j, h] for j, h in items]
    s_g = [sg_ref[j, h] for j, h in items]
    bf = lambda t: t.astype(BF16)
    cat = lambda a, b, axis: jnp.concatenate([bf(a), bf(b)], axis)
    wk_s = _mm_each(gdn["wk"], s_g)
    o_r = _mm_each([cat(ret["scores"][i], ret["qd"][i], 1) for i in idx],
                   [cat(ret["v"][i], s_r[i], 0) for i in idx])
    u = [gdn["wv"][i] - wk_s[i] for i in idx]
    o_g = _mm_each([cat(gdn["qg"][i], gdn["attn"][i], 1) for i in idx],
                   [cat(s_g[i], u[i], 0) for i in idx])
    upd_g = _mm_each(gdn["ktT"], u)
    upd_r = _mm_each(ret["kdT"], ret["v"])
    for i, (j, h) in enumerate(items):
        cd = jnp.broadcast_to(gdn["cdr"][j][0:1, h:h + 1], (HEAD_DIM, HEAD_DIM))
        sg_ref[j, h] = cd * s_g[i] + upd_g[i]
        sr_ref[j, h] = ret_cd[h] * s_r[i] + upd_r[i]
    for i, (j, h) in enumerate(items):
        mix_ref[j, :, h * HEAD_DIM:(h + 1) * HEAD_DIM] = _rms_gate(o_r[i], ret["gate"][i])
        mix_ref[j, :, GROUP_W + h * HEAD_DIM:GROUP_W + (h + 1) * HEAD_DIM] = _rms_gate(
            o_g[i], gdn["gate"][i], gnw)

    @pl.when(pl.program_id(1) == pl.num_programs(1) - 1)
    def _():
        sr_out[...] = sr_ref[...]
        sg_out[...] = sg_ref[...]


def _meta_mixer_body(ret_cd, pr_ref, pg_ref, pab_ref, cos_ref, sin_ref, rsc_ref, dmat_ref,
                     cw_ref, alog_ref, dtb_ref, gnw_ref, s0r_ref, s0g_ref,
                     mix_ref, sr_out, sg_out, sr_ref, sg_ref, xc_ref):
    _seq_init(s0r_ref, s0g_ref, sr_ref, sg_ref)
    xc_ref[0:HALO, :] = jnp.zeros((HALO, CONV_COLS), F32)
    tok = _tokenwise("meta", pr_ref[...], pg_ref[...], cos_ref[...], sin_ref[...], rsc_ref[...],
                     cw_ref, xc_ref, None)
    tok["kdT"] = [t.T for t in tok["kd"]]
    ret, gdn = _chunk_prep("meta", [tok], [pab_ref[...]], dmat_ref, alog_ref[...], dtb_ref[...])
    _seq_step(ret_cd, ret, gdn, gnw_ref[...], mix_ref, sr_out, sg_out, sr_ref, sg_ref)


def _prompt_mixer_body(ret_cd, a_ref, b_ref, pab_ref, dmat_ref, alog_ref, dtb_ref, gnw_ref,
                       s0r_ref, s0g_ref, mix_ref, sr_out, sg_out, sr_ref, sg_ref):
    _seq_init(s0r_ref, s0g_ref, sr_ref, sg_ref)
    heads = range(N_HEADS)
    tiles = range(a_ref.shape[0])
    piece = lambda ref, j, group: [
        ref[j, :, group * GROUP_W + h * HEAD_DIM:group * GROUP_W + (h + 1) * HEAD_DIM] for h in heads]
    toks = [dict(rq=piece(a_ref, j, 0), rk=piece(a_ref, j, 1), rv=piece(a_ref, j, 2),
                 qd=piece(a_ref, j, 3), kdT=piece(a_ref, j, 4), gq=piece(b_ref, j, 0),
                 gk=piece(b_ref, j, 1), gv=piece(b_ref, j, 2), sgz=piece(a_ref, j, 5),
                 sgr=piece(a_ref, j, 6)) for j in tiles]
    ret, gdn = _chunk_prep("prompt", toks, [pab_ref[j] for j in tiles], dmat_ref, alog_ref[...],
                           dtb_ref[...])
    _seq_step(ret_cd, ret, gdn, gnw_ref[...], mix_ref, sr_out, sg_out, sr_ref, sg_ref)


def _seq_out(n_batch, n_chunks, nb):
    state = pl.BlockSpec((nb,) + _STATE_SHAPE, lambda b, n: (b, 0, 0, 0))
    out_specs = [pl.BlockSpec((nb, TILE, 2 * GROUP_W), lambda b, n: (b, n, 0)), state, state]
    out_shape = [
        jax.ShapeDtypeStruct((n_batch, n_chunks * TILE, 2 * GROUP_W), BF16),
        jax.ShapeDtypeStruct((n_batch,) + _STATE_SHAPE, F32),
        jax.ShapeDtypeStruct((n_batch,) + _STATE_SHAPE, F32),
    ]
    scratch = [pltpu.VMEM((nb,) + _STATE_SHAPE, F32), pltpu.VMEM((nb,) + _STATE_SHAPE, F32)]
    return out_specs, out_shape, scratch


def _meta_mixer(ret_cd, pr, pg, pab, cosf, sins, rsc, dmat, cw, alog, dtb, gnw, s0r, s0g):
    c2 = lambda b, n: (0, 0)
    c3 = lambda b, n: (0, 0, 0)
    out_specs, out_shape, state_scratch = _seq_out(1, 1, 1)
    return pl.pallas_call(
        functools.partial(_meta_mixer_body, ret_cd),
        grid=(1, 1),
        in_specs=[
            pl.BlockSpec((TILE, 4 * GROUP_W), c2),
            pl.BlockSpec((TILE, 4 * GROUP_W), c2),
            pl.BlockSpec((TILE, HEAD_DIM), c2),
            pl.BlockSpec((TILE, HEAD_DIM), c2),
            pl.BlockSpec((TILE, HEAD_DIM), c2),
            pl.BlockSpec((TILE, HEAD_DIM), c2),
            pl.BlockSpec((N_HEADS, TILE, TILE), c3),
            pl.BlockSpec((CONV_W, CONV_COLS), c2),
            pl.BlockSpec((1, HEAD_DIM), c2),
            pl.BlockSpec((1, HEAD_DIM), c2),
            pl.BlockSpec((1, HEAD_DIM), c2),
            pl.BlockSpec(_STATE_SHAPE, c3),
            pl.BlockSpec(_STATE_SHAPE, c3),
        ],
        out_specs=out_specs,
        out_shape=out_shape,
        scratch_shapes=state_scratch + [pltpu.VMEM((HALO + TILE, CONV_COLS), F32)],
        compiler_params=_cparams(("parallel", "arbitrary")),
        name="mixer_meta",
    )(pr, pg, pab, cosf, sins, rsc, dmat, cw, alog, dtb, gnw, s0r, s0g)


def _prompt_mixer(ret_cd, a, b, pab, dmat, alog, dtb, gnw, s0r, s0g, nb):
    n_batch, seq, _ = a.shape
    n_chunks = seq // TILE
    tile = lambda b, n: (b, n, 0)
    c2 = lambda b, n: (0, 0)
    c3 = lambda b, n: (0, 0, 0)
    out_specs, out_shape, state_scratch = _seq_out(n_batch, n_chunks, nb)
    return pl.pallas_call(
        functools.partial(_prompt_mixer_body, ret_cd),
        grid=(n_batch // nb, n_chunks),
        in_specs=[
            pl.BlockSpec((nb, TILE, A_COLS), tile),
            pl.BlockSpec((nb, TILE, B_COLS), tile),
            pl.BlockSpec((nb, TILE, HEAD_DIM), tile),
            pl.BlockSpec((N_HEADS, TILE, TILE), c3),
            pl.BlockSpec((1, HEAD_DIM), c2),
            pl.BlockSpec((1, HEAD_DIM), c2),
            pl.BlockSpec((1, HEAD_DIM), c2),
            pl.BlockSpec(_STATE_SHAPE, c3),
            pl.BlockSpec(_STATE_SHAPE, c3),
        ],
        out_specs=out_specs,
        out_shape=out_shape,
        scratch_shapes=state_scratch,
        compiler_params=_cparams(("parallel", "arbitrary")),
        name="mixer_prompt",
    )(a, b, pab, dmat, alog, dtb, gnw, s0r, s0g)


def _prompt_fused_body(ret_cd, x_ref, g_ref, b_ref, w_ref, cos_ref, sin_ref, rsc_ref, cw_ref, halo0_ref,
                       dmat_ref, alog_ref, dtb_ref, gnw_ref, s0r_ref, s0g_ref,
                       mix_ref, sr_out, sg_out, tail_ref, sr_ref, sg_ref, xc_ref):
    nb = x_ref.shape[0]
    seqs, heads = range(nb), range(N_HEADS)
    _seq_init(s0r_ref, s0g_ref, sr_ref, sg_ref)

    @pl.when(pl.program_id(1) == 0)
    def _():
        for j in seqs:
            xc_ref[j, 0:HALO, :] = halo0_ref[...]

    hidden = _layer_norm(x_ref[...].reshape(nb * TILE, D_MODEL), g_ref[...], b_ref[...]).astype(BF16)
    cosf, sins, rsc = cos_ref[...], sin_ref[...], rsc_ref[...]
    keys = ("rq", "rk", "rv", "qd", "kdT", "sgr", "gq", "gk", "gv", "sgz")
    toks = [{key: [None] * N_HEADS for key in keys} for _ in seqs]
    pabs = [None] * nb
    rows = lambda j: slice(j * TILE, (j + 1) * TILE)

    def project(group, width=GROUP_W):
        return jnp.dot(hidden, w_ref[:, group * GROUP_W:group * GROUP_W + width], preferred_element_type=F32)

    def conv_group(group, key):
        val = project(N_HEADS + group)
        for j in seqs:
            conv = _short_conv(val[rows(j), :], cw_ref, xc_ref.at[j], group * GROUP_W)
            for h in heads:
                t = _head_cols(conv, 0, h)
                if key == "gq":
                    t = _l2_normalize(t) * (HEAD_DIM ** -0.5)
                elif key == "gk":
                    t = _l2_normalize(t)
                toks[j][key][h] = t

    def ab_group():
        val = project(2 * N_HEADS, HEAD_DIM)
        for j in seqs:
            pabs[j] = val[rows(j), :]

    def ret_head(h):
        val = project(h)
        for j in seqs:
            part = val[rows(j), :]
            ops = _ret_head_ops(None, *(_head_cols(part, 0, g) for g in range(4)), cosf, sins,
                                _lane_bcast(rsc, h), _lane_bcast(rsc, N_HEADS + h))
            for key in ("rq", "rk", "rv", "qd", "sgr"):
                toks[j][key][h] = ops[key]
            toks[j]["kdT"][h] = ops["kd"].T

    def gate_group():
        val = project(2 * N_HEADS - 1)
        for j in seqs:
            for h in heads:
                toks[j]["sgz"][h] = _silu(_head_cols(val[rows(j), :], 0, h))

    conv_group(1, "gk")
    conv_group(0, "gq")
    ab_group()
    conv_group(2, "gv")
    for j in seqs:
        tail = xc_ref[j, TILE:TILE + HALO, :]
        xc_ref[j, 0:HALO, :] = tail
        tail_ref[j] = tail

    fillers = [functools.partial(ret_head, h) for h in heads] + [gate_group]

    def between():
        if not fillers:
            return False
        fillers.pop(0)()
        return True

    ret, gdn = _chunk_prep("prompt", toks, pabs, dmat_ref, alog_ref[...], dtb_ref[...], between)
    _seq_step(ret_cd, ret, gdn, gnw_ref[...], mix_ref, sr_out, sg_out, sr_ref, sg_ref)


def _prompt_fused(ret_cd, x, ln_g, ln_b, w_all, cosf, sins, rsc, cw, halo0, dmat, alog, dtb, gnw, s0r, s0g, nb):
    n_batch, seq, _ = x.shape
    n_chunks = seq // TILE
    tile = lambda b, n: (b, n, 0)
    pos = lambda b, n: (n, 0)
    c2 = lambda b, n: (0, 0)
    c3 = lambda b, n: (0, 0, 0)
    out_specs, out_shape, state_scratch = _seq_out(n_batch, n_chunks, nb)
    return pl.pallas_call(
        functools.partial(_prompt_fused_body, ret_cd),
        grid=(n_batch // nb, n_chunks),
        in_specs=[
            pl.BlockSpec((nb, TILE, D_MODEL), tile),
            pl.BlockSpec((1, D_MODEL), c2),
            pl.BlockSpec((1, D_MODEL), c2),
            pl.BlockSpec((D_MODEL, W_ALL_COLS), c2, pipeline_mode=pl.Buffered(1)),
            pl.BlockSpec((TILE, HEAD_DIM), pos),
            pl.BlockSpec((TILE, HEAD_DIM), pos),
            pl.BlockSpec((TILE, HEAD_DIM), c2),
            pl.BlockSpec((CONV_W, CONV_COLS), c2),
            pl.BlockSpec((HALO, CONV_COLS), c2),
            pl.BlockSpec((N_HEADS, TILE, TILE), c3),
            pl.BlockSpec((1, HEAD_DIM), c2),
            pl.BlockSpec((1, HEAD_DIM), c2),
            pl.BlockSpec((1, HEAD_DIM), c2),
            pl.BlockSpec(_STATE_SHAPE, c3),
            pl.BlockSpec(_STATE_SHAPE, c3),
        ],
        out_specs=out_specs + [pl.BlockSpec((nb, HALO, CONV_COLS), lambda b, n: (b, 0, 0))],
        out_shape=out_shape + [jax.ShapeDtypeStruct((n_batch, HALO, CONV_COLS), F32)],
        scratch_shapes=state_scratch + [pltpu.VMEM((nb, HALO + TILE, CONV_COLS), F32)],
        compiler_params=_cparams(("parallel", "arbitrary")),
        name="prompt_fused",
    )(x, ln_g, ln_b, w_all, cosf, sins, rsc, cw, halo0, dmat, alog, dtb, gnw, s0r, s0g)


SAMPLE_PER_TILE = TILE // SAMPLE_GROUP


def _stack_by_group(xt):
    shape3 = (SAMPLE_PER_TILE, HEAD_DIM, TILE)
    keep = lax.broadcasted_iota(jnp.int32, shape3, 0) == (lax.broadcasted_iota(jnp.int32, shape3, 2) >> 3)
    stacked = jnp.where(keep, jnp.broadcast_to(xt[None], shape3), 0.0)
    return stacked.reshape(SAMPLE_PER_TILE * HEAD_DIM, TILE)


def _sample_mixer_body(ret_cd, pr_ref, pg_ref, pab_ref, cos_ref, sin_ref, rsc_ref, dmat_ref, cw_ref,
                       alog_ref, dtb_ref, gnw_ref, convpad_ref, sr_in, sg_in,
                       mix_ref, sr_out, sg_out,
                       xc_ref, qd_s, wk_s, qg_s, wv_s, cd_s, inter_s, u_s, qs_s):
    heads = range(N_HEADS)
    xc_ref[0:HALO, :] = jnp.zeros((HALO, CONV_COLS), F32)
    tok = _tokenwise("sample", pr_ref[...], pg_ref[...], cos_ref[...], sin_ref[...], rsc_ref[...],
                     cw_ref, xc_ref, convpad_ref[...])
    tok["kdT"] = [t.T for t in tok["kd"]]
    ret, gdn = _chunk_prep("sample", [tok], [pab_ref[...]], dmat_ref, alog_ref[...], dtb_ref[...])
    for h in heads:
        qd_s[h] = ret["qd"][h]
        wk_s[h] = gdn["wk"][h]
        qg_s[h] = gdn["qg"][h]
        wv_s[h] = gdn["wv"][h]
    cd_s[...] = gdn["cdr"][0]

    def per_batch(b, carry):
        rows = pl.ds(pl.multiple_of(b * SAMPLE_GROUP, SAMPLE_GROUP), SAMPLE_GROUP)
        cd_rows = cd_s[rows, :]
        for h in heads:
            inter_s[h, rows, :] = _mm(qd_s[h, rows, :], sr_in[b, h])
            s = sg_in[b, h]
            both = _mm(jnp.concatenate([wk_s[h, rows, :], qg_s[h, rows, :]], 0), s)
            u_s[h, rows, :] = wv_s[h, rows, :] - both[:SAMPLE_GROUP]
            qs_s[h, rows, :] = both[SAMPLE_GROUP:]
            cd = jnp.broadcast_to(cd_rows[SAMPLE_GROUP - 1:SAMPLE_GROUP, h:h + 1], (HEAD_DIM, HEAD_DIM))
            sg_out[b, h] = cd * s
        return carry

    lax.fori_loop(0, SAMPLE_PER_TILE, per_batch, 0)

    out_rows = SAMPLE_PER_TILE * SAMPLE_TOKENS
    out_row = _iota2((out_rows, TILE), 0)
    token_row = ((out_row >> 2) << 3) + (SAMPLE_GROUP - SAMPLE_TOKENS) + (out_row & (SAMPLE_TOKENS - 1))
    select = (_iota2((out_rows, TILE), 1) == token_row).astype(BF16)
    compact = lambda t: jnp.dot(select, t, preferred_element_type=F32).astype(BF16)

    state_shape = (SAMPLE_PER_TILE, HEAD_DIM, HEAD_DIM)
    for h in heads:
        o = _mm(ret["scores"][h], ret["v"][h]) + inter_s[h]
        upd = _mm(_stack_by_group(ret["kdT"][h]), ret["v"][h]).reshape(state_shape)
        sr_out[:, h] = ret_cd[h] * sr_in[:, h] + upd
        mix_ref[:, h * HEAD_DIM:(h + 1) * HEAD_DIM] = compact(_rms_gate(o, ret["gate"][h]))

    for h in heads:
        u = u_s[h]
        o = qs_s[h] + _mm(gdn["attn"][h], u)
        upd = _mm(_stack_by_group(gdn["ktT"][h]), u).reshape(state_shape)
        sg_out[:, h] = sg_out[:, h] + upd
        mix_ref[:, GROUP_W + h * HEAD_DIM:GROUP_W + (h + 1) * HEAD_DIM] = compact(_rms_gate(
            o, gdn["gate"][h], gnw_ref[...]))


def _sample_mixer(ret_cd, pr, pg, pab, cosf, sins, rsc, dmat, cw, alog, dtb, gnw, convpad, sr, sg):
    n_tiles = pr.shape[0] // TILE
    tile = lambda i: (i, 0)
    c2 = lambda i: (0, 0)
    c3 = lambda i: (0, 0, 0)
    state = pl.BlockSpec((SAMPLE_PER_TILE,) + _STATE_SHAPE, lambda i: (i, 0, 0, 0))
    head_scratch = pltpu.VMEM((N_HEADS, TILE, HEAD_DIM), F32)
    return pl.pallas_call(
        functools.partial(_sample_mixer_body, ret_cd),
        grid=(n_tiles,),
        in_specs=[
            pl.BlockSpec((TILE, 4 * GROUP_W), tile),
            pl.BlockSpec((TILE, 4 * GROUP_W), tile),
            pl.BlockSpec((TILE, HEAD_DIM), tile),
            pl.BlockSpec((TILE, HEAD_DIM), c2),
            pl.BlockSpec((TILE, HEAD_DIM), c2),
            pl.BlockSpec((TILE, HEAD_DIM), c2),
            pl.BlockSpec((N_HEADS, TILE, TILE), c3),
            pl.BlockSpec((CONV_W, CONV_COLS), c2),
            pl.BlockSpec((1, HEAD_DIM), c2),
            pl.BlockSpec((1, HEAD_DIM), c2),
            pl.BlockSpec((1, HEAD_DIM), c2),
            pl.BlockSpec((TILE, CONV_COLS), tile),
            state, state,
        ],
        out_specs=[pl.BlockSpec((SAMPLE_PER_TILE * SAMPLE_TOKENS, 2 * GROUP_W), tile), state, state],
        out_shape=[
            jax.ShapeDtypeStruct((n_tiles * SAMPLE_PER_TILE * SAMPLE_TOKENS, 2 * GROUP_W), BF16),
            jax.ShapeDtypeStruct(sr.shape, F32),
            jax.ShapeDtypeStruct(sg.shape, F32),
        ],
        scratch_shapes=[pltpu.VMEM((HALO + TILE, CONV_COLS), F32)] + [head_scratch] * 4
                       + [pltpu.VMEM((TILE, HEAD_DIM), F32)] + [head_scratch] * 3,
        compiler_params=_cparams(("parallel",)),
        name="mixer_sample",
    )(pr, pg, pab, cosf, sins, rsc, dmat, cw, alog, dtb, gnw, convpad, sr, sg)


def _rotary_tables(pos):
    half = HEAD_DIM // 2
    inv = ROPE_BASE ** (-np.arange(half, dtype=np.float64) / half)
    ang = np.asarray(pos, np.float64)[:, None] * inv[None, :]
    cos, sin = np.cos(ang), np.sin(ang)
    return (jnp.asarray(np.concatenate([cos, cos], -1), F32),
            jnp.asarray(np.concatenate([-sin, sin], -1), F32))


def _retention_tables(seg, pos, valid, seg_len):
    gamma = 1.0 - 2.0 ** (-5.0 - np.arange(N_HEADS, dtype=np.float64))
    posf = np.asarray(pos, np.float64)
    rel = posf[:, None] - posf[None, :]
    causal = (seg[:, None] == seg[None, :]) & (rel >= 0)
    dmat = np.where(causal[None], gamma[:, None, None] ** np.where(causal, rel, 0.0)[None], 0.0)
    q_scale = gamma[None, :] ** (posf[:, None] + 1.0)
    k_scale = np.where(valid[:, None], gamma[None, :] ** (seg_len - 1.0 - posf[:, None]), 0.0)
    rsc = np.concatenate([q_scale, k_scale, np.zeros((TILE, HEAD_DIM - 2 * N_HEADS))], -1)
    chunk_decay = tuple(float(g ** seg_len) for g in gamma)
    return jnp.asarray(dmat, F32), jnp.asarray(rsc, F32), chunk_decay


def _pad_lanes(v):
    return jnp.pad(v.astype(F32), (0, HEAD_DIM - v.shape[0]))[None, :]


def kernel(x_prompt, x_sample, state_ret, state_gdn, state_conv, meta_tokens, emb_ln_g, emb_ln_b,
           w_in, conv_w, a_log, dt_bias, gdn_norm_w, w_out, ln1_g, ln1_b, w_gate_up, w_down,
           ln2_g, ln2_b):
    n_batch, seq, _ = x_prompt.shape
    dec_batch, dec_seq, _ = x_sample.shape
    assert seq % TILE == 0 and dec_seq == SAMPLE_TOKENS and N_META <= TILE
    n_chunks = seq // TILE
    layer = 0

    w_in_l = w_in[layer]
    w_all = _weight_prep(jnp.swapaxes(w_in_l, 0, 1))
    w_out_b = w_out[layer].astype(BF16)
    w_gu_b = w_gate_up[layer].astype(BF16)
    w_down_b = w_down[layer].astype(BF16)
    row = lambda v: v.astype(F32)[None, :]
    eg, eb = row(emb_ln_g), row(emb_ln_b)
    cw = conv_w[layer].astype(F32)
    alog, dtb, gnw = _pad_lanes(a_log[layer]), _pad_lanes(dt_bias[layer]), row(gdn_norm_w[layer])

    xp = x_prompt.reshape(n_batch * seq, D_MODEL)
    xm = jnp.pad(meta_tokens.astype(F32), ((0, TILE - N_META), (0, 0)))
    xs = jnp.pad(x_sample, ((0, 0), (SAMPLE_GROUP - dec_seq, 0), (0, 0))).reshape(
        dec_batch * SAMPLE_GROUP, D_MODEL)
    convpad = jnp.pad(state_conv[layer].astype(F32), ((0, 0), (1, SAMPLE_GROUP - CONV_W), (0, 0))).reshape(
        dec_batch * SAMPLE_GROUP, CONV_COLS)

    tile_idx = np.arange(TILE)
    cos_p, sin_p = _rotary_tables(N_META + np.arange(seq))
    cos_m, sin_m = _rotary_tables(tile_idx)
    tok = (tile_idx % SAMPLE_GROUP) - (SAMPLE_GROUP - dec_seq)
    cos_s, sin_s = _rotary_tables(PAST_LEN + np.maximum(tok, 0))
    zeros_i = np.zeros((TILE,), np.int32)
    all_valid = np.ones((TILE,), bool)
    dmat_p, rsc_p, cd_p = _retention_tables(zeros_i, tile_idx, all_valid, float(TILE))
    dmat_m, rsc_m, cd_m = _retention_tables(zeros_i, tile_idx, tile_idx < N_META, float(N_META))
    dmat_s, rsc_s, cd_s = _retention_tables(tile_idx // SAMPLE_GROUP, tok, tok >= 0, float(dec_seq))

    zero_state = jnp.zeros(_STATE_SHAPE, F32)
    pr_m, pg_m, pab_m = _front(xm, eg, eb, w_all, TILE)
    _, sr_m, sg_m = _meta_mixer(cd_m, pr_m, pg_m, pab_m, cos_m, sin_m, rsc_m, dmat_m, cw, alog, dtb, gnw,
                                zero_state, zero_state)
    halo_p = pg_m[N_META - HALO:N_META, :CONV_COLS]

    mix_p, sr_p, sg_p, tail_p = _prompt_fused(cd_p, x_prompt, eg, eb, w_all, cos_p, sin_p, rsc_p, cw, halo_p,
                                              dmat_p, alog, dtb, gnw, sr_m[0], sg_m[0], MIXER_SEQS_PER_STEP)
    mix_p = mix_p.reshape(n_batch * seq, 2 * GROUP_W)

    pr_s, pg_s, pab_s = _front(xs, eg, eb, w_all, 512)
    mix_s, sr_s, sg_s = _sample_mixer(cd_s, pr_s, pg_s, pab_s, cos_s, sin_s, rsc_s, dmat_s, cw, alog,
                                      dtb, gnw, convpad, state_ret[layer].astype(F32),
                                      state_gdn[layer].astype(F32))

    back = functools.partial(_back, eg=eg, eb=eb, w_out=w_out_b, g1=row(ln1_g[layer]), b1=row(ln1_b[layer]),
                             w_gu=w_gu_b, w_down=w_down_b, g2=row(ln2_g[layer]), b2=row(ln2_b[layer]),
                             tm=512)
    y_p = back(xp, mix_p).reshape(n_batch, seq, D_MODEL)
    y_s = back(x_sample.reshape(dec_batch * dec_seq, D_MODEL), mix_s).reshape(dec_batch, dec_seq, D_MODEL)

    conv_p = tail_p[:, HALO - (CONV_W - 1):, :]
    conv_s = pg_s.reshape(dec_batch, SAMPLE_GROUP, 4 * GROUP_W)[:, SAMPLE_GROUP - (CONV_W - 1):, :CONV_COLS]
    return (y_p, y_s, sr_p[None], sg_p[None], conv_p[None], sr_s[None], sg_s[None], conv_s[None])
```

```python
import functools

import numpy as np
import jax
import jax.numpy as jnp
from jax import lax
from jax.experimental import pallas as pl
from jax.experimental.pallas import tpu as pltpu

F32 = jnp.float32
BF16 = jnp.bfloat16

D_MODEL = 1024
N_META = 16
N_HEADS = 4
HEAD_DIM = 128
GROUP_W = N_HEADS * HEAD_DIM
CONV_W = 4
CONV_COLS = 3 * GROUP_W
D_FF = 2816
PAST_LEN = 16384
ROPE_BASE = 10000.0
LN_EPS = 1e-5
RMS_EPS = 1e-6
ALPHA = 2.0 ** 0.25
TILE = 128
SAMPLE_GROUP = 8
SAMPLE_TOKENS = 4
HALO = 8
MIXER_SEQS_PER_STEP = 4
BACK_PARTS = 2
W_ALL_COLS = 9 * GROUP_W
AB_ROWS = 2 * N_HEADS
VMEM_LIMIT = 56 * 1024 * 1024


def _cparams(sem):
    return pltpu.CompilerParams(dimension_semantics=sem, vmem_limit_bytes=VMEM_LIMIT)


def _layer_norm(x, g, b):
    mu = jnp.mean(x, -1, keepdims=True)
    xc = x - mu
    var = jnp.mean(xc * xc, -1, keepdims=True)
    return xc * lax.rsqrt(var + LN_EPS) * g + b


def _sigmoid(x):
    return 1.0 / (1.0 + jnp.exp(-x))


def _silu(x):
    return x * _sigmoid(x)


def _softplus(x):
    return jnp.maximum(x, 0.0) + jnp.log(1.0 + jnp.exp(-jnp.abs(x)))


def _mm(a, b):
    return jnp.dot(a.astype(BF16), b.astype(BF16), preferred_element_type=F32)


def _mm_nt(a, b):
    return lax.dot_general(a.astype(BF16), b.astype(BF16), (((1,), (1,)), ((), ())),
                           preferred_element_type=F32)


def _mm_each(xs, ys):
    return [_mm(x, y) for x, y in zip(xs, ys)]


def _mm_nt_each(xs, ys):
    return [_mm_nt(x, y) for x, y in zip(xs, ys)]


def _mm_split3(m01, x):
    x1 = x.astype(BF16)
    r1 = x - x1.astype(F32)
    x2 = r1.astype(BF16)
    x3 = (r1 - x2.astype(F32)).astype(BF16)
    dot = functools.partial(jnp.dot, preferred_element_type=F32)
    return (dot(m01, x3) + dot(m01, x2)) + dot(m01, x1)


def _lane_bcast(x, lane):
    return jnp.broadcast_to(x[:, lane:lane + 1], (x.shape[0], HEAD_DIM))


def _head_cols(x, base, h):
    return x[:, base + h * HEAD_DIM:base + (h + 1) * HEAD_DIM]


def _weight_prep_body(q_ref, k_ref, v_ref, g_ref, ab_ref, o_ref):
    j = pl.program_id(0)

    @pl.when(j < 2 * N_HEADS)
    def _():
        for piece, ref in enumerate((q_ref, k_ref, v_ref, g_ref)):
            o_ref[:, piece * HEAD_DIM:(piece + 1) * HEAD_DIM] = ref[...].T.astype(BF16)

    @pl.when(j == 2 * N_HEADS)
    def _():
        ab = jnp.concatenate([ab_ref[...], jnp.zeros((HEAD_DIM - AB_ROWS, D_MODEL), F32)], 0)
        o_ref[:, :HEAD_DIM] = ab.T.astype(BF16)
        o_ref[:, HEAD_DIM:] = jnp.zeros((D_MODEL, GROUP_W - HEAD_DIM), BF16)


def _weight_prep(w_in_t):
    assert w_in_t.shape == (8 * GROUP_W + AB_ROWS, D_MODEL)

    def piece_spec(piece):
        def index(j):
            ret_block = piece * N_HEADS + j
            gdn_block = N_HEADS * j + piece
            return (jnp.where(j < N_HEADS, ret_block, jnp.where(j < 2 * N_HEADS, gdn_block, 0)), 0)
        return pl.BlockSpec((HEAD_DIM, D_MODEL), index)

    ab_spec = pl.BlockSpec((AB_ROWS, D_MODEL), lambda j: (8 * GROUP_W // AB_ROWS, 0))
    return pl.pallas_call(
        _weight_prep_body,
        grid=(2 * N_HEADS + 1,),
        in_specs=[piece_spec(p) for p in range(4)] + [ab_spec],
        out_specs=pl.BlockSpec((D_MODEL, GROUP_W), lambda j: (0, j)),
        out_shape=jax.ShapeDtypeStruct((D_MODEL, W_ALL_COLS), BF16),
        compiler_params=_cparams(("arbitrary",)),
        name="weight_prep",
    )(w_in_t, w_in_t, w_in_t, w_in_t, w_in_t)


def _front_body(x_ref, g_ref, b_ref, w_ref, pr_ref, pg_ref, pab_ref):
    h = _layer_norm(x_ref[...], g_ref[...], b_ref[...]).astype(BF16)
    pr_ref[...] = jnp.dot(h, w_ref[:, :4 * GROUP_W], preferred_element_type=F32)
    pg_ref[...] = jnp.dot(h, w_ref[:, 4 * GROUP_W:8 * GROUP_W], preferred_element_type=F32)
    pab_ref[...] = jnp.dot(h, w_ref[:, 8 * GROUP_W:8 * GROUP_W + HEAD_DIM], preferred_element_type=F32)


def _front(x2d, ln_g, ln_b, w_all, tm):
    rows = x2d.shape[0]
    const = lambda i: (0, 0)
    tile = lambda i: (i, 0)
    return pl.pallas_call(
        _front_body,
        grid=(rows // tm,),
        in_specs=[
            pl.BlockSpec((tm, D_MODEL), tile),
            pl.BlockSpec((1, D_MODEL), const),
            pl.BlockSpec((1, D_MODEL), const),
            pl.BlockSpec((D_MODEL, W_ALL_COLS), const),
        ],
        out_specs=[
            pl.BlockSpec((tm, 4 * GROUP_W), tile),
            pl.BlockSpec((tm, 4 * GROUP_W), tile),
            pl.BlockSpec((tm, HEAD_DIM), tile),
        ],
        out_shape=[
            jax.ShapeDtypeStruct((rows, 4 * GROUP_W), F32),
            jax.ShapeDtypeStruct((rows, 4 * GROUP_W), F32),
            jax.ShapeDtypeStruct((rows, HEAD_DIM), F32),
        ],
        compiler_params=_cparams(("parallel",)),
        name="front",
    )(x2d, ln_g, ln_b, w_all)


def _back_body(x_ref, mix_ref, eg_ref, eb_ref, wo_ref, g1_ref, b1_ref, wgu_ref, wd_ref,
               g2_ref, b2_ref, y_ref):
    part = x_ref.shape[0] // BACK_PARTS
    parts = [slice(i * part, (i + 1) * part) for i in range(BACK_PARTS)]
    dot = functools.partial(jnp.dot, preferred_element_type=F32)
    h = [_layer_norm(x_ref[p, :], eg_ref[...], eb_ref[...]) for p in parts]
    mp = [dot(mix_ref[p, :], wo_ref[...]) for p in parts]
    h1 = [_layer_norm(ALPHA * a + b, g1_ref[...], b1_ref[...]) for a, b in zip(h, mp)]
    gu = [dot(a.astype(BF16), wgu_ref[...]) for a in h1]
    act = [(_silu(a[:, :D_FF]) * a[:, D_FF:]).astype(BF16) for a in gu]
    ff = [dot(a, wd_ref[...]) for a in act]
    for p, a, b in zip(parts, h1, ff):
        y_ref[p, :] = _layer_norm(ALPHA * a + b, g2_ref[...], b2_ref[...])


def _back(x2d, mix, eg, eb, w_out, g1, b1, w_gu, w_down, g2, b2, tm):
    rows = x2d.shape[0]
    const = lambda i: (0, 0)
    tile = lambda i: (i, 0)
    single = pl.Buffered(1)
    vec = pl.BlockSpec((1, D_MODEL), const)
    return pl.pallas_call(
        _back_body,
        grid=(rows // tm,),
        in_specs=[
            pl.BlockSpec((tm, D_MODEL), tile),
            pl.BlockSpec((tm, D_MODEL), tile),
            vec, vec,
            pl.BlockSpec((D_MODEL, D_MODEL), const, pipeline_mode=single),
            vec, vec,
            pl.BlockSpec((D_MODEL, 2 * D_FF), const, pipeline_mode=single),
            pl.BlockSpec((D_FF, D_MODEL), const, pipeline_mode=single),
            vec, vec,
        ],
        out_specs=pl.BlockSpec((tm, D_MODEL), tile),
        out_shape=jax.ShapeDtypeStruct((rows, D_MODEL), F32),
        compiler_params=_cparams(("parallel",)),
        name="back",
    )(x2d, mix, eg, eb, w_out, g1, b1, w_gu, w_down, g2, b2)


def _iota2(shape, dim):
    return lax.broadcasted_iota(jnp.int32, shape, dim)


def _valid_rows(variant, shape):
    r = _iota2(shape, 0)
    if variant == "meta":
        return r < N_META
    if variant == "sample":
        return (r & (SAMPLE_GROUP - 1)) >= SAMPLE_GROUP - SAMPLE_TOKENS
    return None


def _mask_rows(valid, x):
    return x if valid is None else jnp.where(valid, x, 0.0)


def _tri_inverse(variant, nmats, row, col, between=None):
    between = between or (lambda: None)
    eye = (row == col).astype(F32)
    if variant == "sample":
        n2 = _mm_each(nmats, nmats)
        ps = [eye + n for n in nmats]
        return [p + t for p, t in zip(ps, _mm_each(ps, n2))]
    base_log2 = 3
    in_block = (row >> base_log2) == (col >> base_log2)
    ds = [jnp.where(in_block, n, 0.0) for n in nmats]
    ps = [eye + d for d in ds]
    for _ in range(base_log2 - 1):
        ds = _mm_each(ds, ds)
        between()
        ps = [p + t for p, t in zip(ps, _mm_each(ps, ds))]
        between()
    live_rows = N_META if variant == "meta" else TILE
    s = base_log2
    while (1 << s) < live_rows:
        size = 1 << s
        lower_left = (((row >> (s + 1)) == (col >> (s + 1))) & (((row >> s) & 1) == 1)
                      & (((col >> s) & 1) == 0))
        cs = [jnp.where(lower_left, n, 0.0) for n in nmats]
        lower = [slice(start + size, start + 2 * size) for start in range(0, TILE, 2 * size)]
        upper = [slice(start, start + size) for start in range(0, TILE, 2 * size)]
        p_low = [jnp.concatenate([p[rows, :] for rows in lower], 0) for p in ps]
        half = _mm_each(p_low, cs)
        between()
        fix = _mm_each(half, ps)
        between()
        merged = []
        for p, f in zip(ps, fix):
            pieces = []
            for i, (up, lo) in enumerate(zip(upper, lower)):
                pieces += [p[up, :], p[lo, :] + f[i * size:(i + 1) * size, :]]
            merged.append(jnp.concatenate(pieces, 0))
        ps = merged
        s += 1
    return ps


def _rotary(t, cosf, sins):
    return t * cosf + pltpu.roll(t, HEAD_DIM // 2, 1) * sins


def _ret_head_ops(valid, q, k, v, gate, cosf, sins, q_scale, k_scale):
    rq = _mask_rows(valid, _rotary(q, cosf, sins))
    rk = _mask_rows(valid, _rotary(k, cosf, sins) * (HEAD_DIM ** -0.5))
    return dict(rq=rq, rk=rk, rv=_mask_rows(valid, v), qd=rq * q_scale, kd=rk * k_scale, sgr=_silu(gate))


def _short_conv(x, halo, cw_ref, col0):
    rows, ncols = x.shape
    cols = slice(col0, col0 + ncols)
    n = rows // HALO
    full = jnp.concatenate([halo, x], 0).reshape(n + 1, HALO, ncols)
    sub = lax.broadcasted_iota(jnp.int32, (n, HALO, ncols), 1)
    conv = full[1:] * cw_ref[CONV_W - 1:CONV_W, cols]
    for shift in range(1, CONV_W):
        rolled = pltpu.roll(full, shift, 1)
        shifted = jnp.where(sub < shift, rolled[:-1], rolled[1:])
        conv = conv + shifted * cw_ref[CONV_W - 1 - shift:CONV_W - shift, cols]
    return _silu(conv).reshape(rows, ncols)


def _l2_normalize(t):
    return t * lax.rsqrt(jnp.sum(t * t, -1, keepdims=True) + RMS_EPS)


def _tokenwise(variant, pret, pgdn, cosf, sins, rsc, cw_ref, convpad):
    rows = pret.shape[0]
    heads = range(N_HEADS)
    valid = _valid_rows(variant, (rows, HEAD_DIM))
    ret = [_ret_head_ops(valid, *(_head_cols(pret, h * GROUP_W, g) for g in range(4)), cosf, sins,
                         _lane_bcast(rsc, h), _lane_bcast(rsc, N_HEADS + h)) for h in heads]
    x = pgdn[:, :CONV_COLS]
    if variant == "sample":
        x = jnp.where(_valid_rows(variant, x.shape), x, convpad)
    conv = _short_conv(x, jnp.zeros((HALO, CONV_COLS), F32), cw_ref, 0)
    gq = [_mask_rows(valid, _l2_normalize(_head_cols(conv, 0, h)) * (HEAD_DIM ** -0.5)) for h in heads]
    gk = [_mask_rows(valid, _l2_normalize(_head_cols(conv, GROUP_W, h))) for h in heads]
    gv = [_mask_rows(valid, _head_cols(conv, 2 * GROUP_W, h)) for h in heads]
    sgz = [_silu(_head_cols(pgdn, 3 * GROUP_W, h)) for h in heads]
    tok = {key: [r[key] for r in ret] for key in ret[0]}
    tok.update(gq=gq, gk=gk, gv=gv, sgz=sgz)
    return tok


def _chunk_prep(variant, toks, pabs, dmat_ref, alog, dtb, between=None):
    shape = (TILE, HEAD_DIM)
    tiles = range(len(toks))
    items = [(j, h) for j in tiles for h in range(N_HEADS)]
    pick = lambda key: [toks[j][key][h] for j, h in items]
    row = _iota2(shape, 0)
    col = _iota2(shape, 1)
    valid = _valid_rows(variant, shape)
    if variant == "sample":
        same = (row >> 3) == (col >> 3)
        incl = same & (row >= col)
        strict = same & (row > col)
    else:
        incl = row >= col
        strict = row > col

    incl01 = incl.astype(BF16)
    beta_all, gcum, gam, ktail, cdr, gcum_t = [], [], [], [], [], []
    for j in tiles:
        g_all = _mask_rows(valid, -jnp.exp(alog) * _softplus(pabs[j] + dtb))
        beta_all.append(_mask_rows(valid, _sigmoid(pabs[j])))
        if variant == "sample":
            both = _mm_split3(jnp.concatenate([incl01, same.astype(BF16)], 0), g_all)
            gc, gseg = both[:TILE], both[TILE:]
        else:
            gc = _mm_split3(incl01, g_all)
            gseg = jnp.broadcast_to(gc[TILE - 1:TILE, :], shape)
        gcum.append(gc)
        gam.append(jnp.exp(gc))
        ktail.append(jnp.exp(gseg - gc))
        cdr.append(jnp.exp(gseg))
        gcum_t.append(gc.T)

    gq, gk, gv = pick("gq"), pick("gk"), pick("gv")
    n_items = range(len(items))
    dec = [jnp.where(incl, jnp.exp(jnp.minimum(
        _lane_bcast(gcum[j], h) - jnp.broadcast_to(gcum_t[j][h:h + 1, :], shape), 0.0)), 0.0)
        for j, h in items]
    bcol = [_lane_bcast(beta_all[j], N_HEADS + h) for j, h in items]
    gamc = [_lane_bcast(gam[j], h) for j, h in items]
    kk = _mm_nt_each(gk, gk)
    qk = _mm_nt_each(gq, gk)
    nmats = [-(jnp.where(strict, kk[i] * dec[i], 0.0) * bcol[i]) for i in n_items]
    tinv = _tri_inverse(variant, nmats, row, col, between)
    rhs = [jnp.concatenate([gv[i] * bcol[i], gk[i] * (bcol[i] * gamc[i])], 1) for i in n_items]
    sol = _mm_each(tinv, rhs)
    while between is not None and between():
        pass
    ret = dict(
        items=items,
        scores=[qk_r * dmat_ref[h] for qk_r, (_, h) in zip(
            _mm_nt_each(pick("rq"), pick("rk")), items)],
        qd=pick("qd"), kdT=pick("kdT"), v=pick("rv"), gate=pick("sgr"),
    )
    gdn = dict(
        items=items,
        wv=[t[:, :HEAD_DIM] for t in sol],
        wk=[t[:, HEAD_DIM:] for t in sol],
        attn=[qk[i] * dec[i] for i in n_items],
        qg=[gq[i] * gamc[i] for i in n_items],
        ktT=[(gk[i] * _lane_bcast(ktail[j], h)).T for i, (j, h) in enumerate(items)],
        cdr=cdr,
        gate=pick("sgz"),
    )
    return ret, gdn


def _rms_gate(o, gate, weight=None):
    o = o * lax.rsqrt(jnp.mean(o * o, -1, keepdims=True) + RMS_EPS)
    if weight is not None:
        o = o * weight
    return (o * gate).astype(BF16)


_STATE_SHAPE = (N_HEADS, HEAD_DIM, HEAD_DIM)


def _seq_init(s0r_ref, s0g_ref, sr_ref, sg_ref):
    @pl.when(pl.program_id(1) == 0)
    def _():
        for j in range(sr_ref.shape[0]):
            sr_ref[j] = s0r_ref[...]
            sg_ref[j] = s0g_ref[...]


def _seq_step(ret_cd, ret, gdn, gnw, mix_ref, sr_out, sg_out, sr_ref, sg_ref):
    items = ret["items"]
    idx = range(len(items))
    s_r = [sr_ref[j, h] for j, h in items]
    s_g = [sg_ref[j, h] for j, h in items]
    bf = lambda t: t.astype(BF16)
    cat = lambda a, b, axis: jnp.concatenate([bf(a), bf(b)], axis)
    wk_s = _mm_each(gdn["wk"], s_g)
    o_r = _mm_each([cat(ret["scores"][i], ret["qd"][i], 1) for i in idx],
                   [cat(ret["v"][i], s_r[i], 0) for i in idx])
    u = [gdn["wv"][i] - wk_s[i] for i in idx]
    o_g = _mm_each([cat(gdn["qg"][i], gdn["attn"][i], 1) for i in idx],
                   [cat(s_g[i], u[i], 0) for i in idx])
    upd_g = _mm_each(gdn["ktT"], u)
    upd_r = _mm_each(ret["kdT"], ret["v"])
    for i, (j, h) in enumerate(items):
        cd = jnp.broadcast_to(gdn["cdr"][j][0:1, h:h + 1], (HEAD_DIM, HEAD_DIM))
        sg_ref[j, h] = cd * s_g[i] + upd_g[i]
        sr_ref[j, h] = ret_cd[h] * s_r[i] + upd_r[i]
    for i, (j, h) in enumerate(items):
        mix_ref[j, :, h * HEAD_DIM:(h + 1) * HEAD_DIM] = _rms_gate(o_r[i], ret["gate"][i])
        mix_ref[j, :, GROUP_W + h * HEAD_DIM:GROUP_W + (h + 1) * HEAD_DIM] = _rms_gate(
            o_g[i], gdn["gate"][i], gnw)

    @pl.when(pl.program_id(1) == pl.num_programs(1) - 1)
    def _():
        sr_out[...] = sr_ref[...]
        sg_out[...] = sg_ref[...]


def _seq_out(n_batch, n_chunks, nb):
    state = pl.BlockSpec((nb,) + _STATE_SHAPE, lambda b, n: (b, 0, 0, 0))
    out_specs = [pl.BlockSpec((nb, TILE, 2 * GROUP_W), lambda b, n: (b, n, 0)), state, state]
    out_shape = [
        jax.ShapeDtypeStruct((n_batch, n_chunks * TILE, 2 * GROUP_W), BF16),
        jax.ShapeDtypeStruct((n_batch,) + _STATE_SHAPE, F32),
        jax.ShapeDtypeStruct((n_batch,) + _STATE_SHAPE, F32),
    ]
    scratch = [pltpu.VMEM((nb,) + _STATE_SHAPE, F32), pltpu.VMEM((nb,) + _STATE_SHAPE, F32)]
    return out_specs, out_shape, scratch


def _meta_mixer_body(ret_cd, pr_ref, pg_ref, pab_ref, cos_ref, sin_ref, rsc_ref, dmat_ref,
                     cw_ref, alog_ref, dtb_ref, gnw_ref, s0r_ref, s0g_ref,
                     mix_ref, sr_out, sg_out, sr_ref, sg_ref):
    _seq_init(s0r_ref, s0g_ref, sr_ref, sg_ref)
    tok = _tokenwise("meta", pr_ref[...], pg_ref[...], cos_ref[...], sin_ref[...], rsc_ref[...],
                     cw_ref, None)
    tok["kdT"] = [t.T for t in tok["kd"]]
    ret, gdn = _chunk_prep("meta", [tok], [pab_ref[...]], dmat_ref, alog_ref[...], dtb_ref[...])
    _seq_step(ret_cd, ret, gdn, gnw_ref[...], mix_ref, sr_out, sg_out, sr_ref, sg_ref)


def _meta_mixer(ret_cd, pr, pg, pab, cosf, sins, rsc, dmat, cw, alog, dtb, gnw, s0r, s0g):
    c2 = lambda b, n: (0, 0)
    c3 = lambda b, n: (0, 0, 0)
    out_specs, out_shape, state_scratch = _seq_out(1, 1, 1)
    return pl.pallas_call(
        functools.partial(_meta_mixer_body, ret_cd),
        grid=(1, 1),
        in_specs=[
            pl.BlockSpec((TILE, 4 * GROUP_W), c2),
            pl.BlockSpec((TILE, 4 * GROUP_W), c2),
            pl.BlockSpec((TILE, HEAD_DIM), c2),
            pl.BlockSpec((TILE, HEAD_DIM), c2),
            pl.BlockSpec((TILE, HEAD_DIM), c2),
            pl.BlockSpec((TILE, HEAD_DIM), c2),
            pl.BlockSpec((N_HEADS, TILE, TILE), c3),
            pl.BlockSpec((CONV_W, CONV_COLS), c2),
            pl.BlockSpec((1, HEAD_DIM), c2),
            pl.BlockSpec((1, HEAD_DIM), c2),
            pl.BlockSpec((1, HEAD_DIM), c2),
            pl.BlockSpec(_STATE_SHAPE, c3),
            pl.BlockSpec(_STATE_SHAPE, c3),
        ],
        out_specs=out_specs,
        out_shape=out_shape,
        scratch_shapes=state_scratch,
        compiler_params=_cparams(("parallel", "arbitrary")),
        name="mixer_meta",
    )(pr, pg, pab, cosf, sins, rsc, dmat, cw, alog, dtb, gnw, s0r, s0g)


def _prompt_fused_body(ret_cd, x_ref, g_ref, b_ref, w_ref, cos_ref, sin_ref, rsc_ref, cw_ref, halo0_ref,
                       dmat_ref, alog_ref, dtb_ref, gnw_ref, s0r_ref, s0g_ref,
                       mix_ref, sr_out, sg_out, tail_ref, sr_ref, sg_ref, halo_ref):
    nb = x_ref.shape[0]
    seqs, heads = range(nb), range(N_HEADS)
    _seq_init(s0r_ref, s0g_ref, sr_ref, sg_ref)

    @pl.when(pl.program_id(1) == 0)
    def _():
        for j in seqs:
            halo_ref[j] = halo0_ref[...]

    hidden = _layer_norm(x_ref[...].reshape(nb * TILE, D_MODEL), g_ref[...], b_ref[...]).astype(BF16)
    cosf, sins, rsc = cos_ref[...], sin_ref[...], rsc_ref[...]
    keys = ("rq", "rk", "rv", "qd", "kdT", "sgr", "gq", "gk", "gv", "sgz")
    toks = [{key: [None] * N_HEADS for key in keys} for _ in seqs]
    pabs = [None] * nb
    rows = lambda j: slice(j * TILE, (j + 1) * TILE)

    def project(group, width=GROUP_W):
        return jnp.dot(hidden, w_ref[:, group * GROUP_W:group * GROUP_W + width], preferred_element_type=F32)

    def conv_group(group, key):
        val = project(N_HEADS + group)
        cols = slice(group * GROUP_W, (group + 1) * GROUP_W)
        for j in seqs:
            part = val[rows(j), :]
            conv = _short_conv(part, halo_ref[j, :, cols], cw_ref, group * GROUP_W)
            halo_ref[j, :, cols] = part[TILE - HALO:, :]
            for h in heads:
                t = _head_cols(conv, 0, h)
                if key == "gq":
                    t = _l2_normalize(t) * (HEAD_DIM ** -0.5)
                elif key == "gk":
                    t = _l2_normalize(t)
                toks[j][key][h] = t

    def ab_group():
        val = project(2 * N_HEADS, HEAD_DIM)
        for j in seqs:
            pabs[j] = val[rows(j), :]

    def ret_head(h):
        val = project(h)
        for j in seqs:
            part = val[rows(j), :]
            ops = _ret_head_ops(None, *(_head_cols(part, 0, g) for g in range(4)), cosf, sins,
                                _lane_bcast(rsc, h), _lane_bcast(rsc, N_HEADS + h))
            for key in ("rq", "rk", "rv", "qd", "sgr"):
                toks[j][key][h] = ops[key]
            toks[j]["kdT"][h] = ops["kd"].T

    def gate_group():
        val = project(2 * N_HEADS - 1)
        for j in seqs:
            for h in heads:
                toks[j]["sgz"][h] = _silu(_head_cols(val[rows(j), :], 0, h))

    ab_group()
    conv_group(1, "gk")
    conv_group(0, "gq")
    conv_group(2, "gv")
    for j in seqs:
        tail_ref[j] = halo_ref[j]

    fillers = [functools.partial(ret_head, h) for h in heads] + [gate_group]

    def between():
        if not fillers:
            return False
        fillers.pop(0)()
        return True

    ret, gdn = _chunk_prep("prompt", toks, pabs, dmat_ref, alog_ref[...], dtb_ref[...], between)
    _seq_step(ret_cd, ret, gdn, gnw_ref[...], mix_ref, sr_out, sg_out, sr_ref, sg_ref)


def _prompt_fused(ret_cd, x, ln_g, ln_b, w_all, cosf, sins, rsc, cw, halo0, dmat, alog, dtb, gnw, s0r, s0g, nb):
    n_batch, seq, _ = x.shape
    n_chunks = seq // TILE
    tile = lambda b, n: (b, n, 0)
    pos = lambda b, n: (n, 0)
    c2 = lambda b, n: (0, 0)
    c3 = lambda b, n: (0, 0, 0)
    out_specs, out_shape, state_scratch = _seq_out(n_batch, n_chunks, nb)
    return pl.pallas_call(
        functools.partial(_prompt_fused_body, ret_cd),
        grid=(n_batch // nb, n_chunks),
        in_specs=[
            pl.BlockSpec((nb, TILE, D_MODEL), tile),
            pl.BlockSpec((1, D_MODEL), c2),
            pl.BlockSpec((1, D_MODEL), c2),
            pl.BlockSpec((D_MODEL, W_ALL_COLS), c2, pipeline_mode=pl.Buffered(1)),
            pl.BlockSpec((TILE, HEAD_DIM), pos),
            pl.BlockSpec((TILE, HEAD_DIM), pos),
            pl.BlockSpec((TILE, HEAD_DIM), c2),
            pl.BlockSpec((CONV_W, CONV_COLS), c2),
            pl.BlockSpec((HALO, CONV_COLS), c2),
            pl.BlockSpec((N_HEADS, TILE, TILE), c3),
            pl.BlockSpec((1, HEAD_DIM), c2),
            pl.BlockSpec((1, HEAD_DIM), c2),
            pl.BlockSpec((1, HEAD_DIM), c2),
            pl.BlockSpec(_STATE_SHAPE, c3),
            pl.BlockSpec(_STATE_SHAPE, c3),
        ],
        out_specs=out_specs + [pl.BlockSpec((nb, HALO, CONV_COLS), lambda b, n: (b, 0, 0))],
        out_shape=out_shape + [jax.ShapeDtypeStruct((n_batch, HALO, CONV_COLS), F32)],
        scratch_shapes=state_scratch + [pltpu.VMEM((nb, HALO, CONV_COLS), F32)],
        compiler_params=_cparams(("parallel", "arbitrary")),
        name="prompt_fused",
    )(x, ln_g, ln_b, w_all, cosf, sins, rsc, cw, halo0, dmat, alog, dtb, gnw, s0r, s0g)


SAMPLE_PER_TILE = TILE // SAMPLE_GROUP


def _stack_by_group(xt):
    shape3 = (SAMPLE_PER_TILE, HEAD_DIM, TILE)
    keep = lax.broadcasted_iota(jnp.int32, shape3, 0) == (lax.broadcasted_iota(jnp.int32, shape3, 2) >> 3)
    stacked = jnp.where(keep, jnp.broadcast_to(xt[None], shape3), 0.0)
    return stacked.reshape(SAMPLE_PER_TILE * HEAD_DIM, TILE)


def _sample_mixer_body(ret_cd, pr_ref, pg_ref, pab_ref, cos_ref, sin_ref, rsc_ref, dmat_ref, cw_ref,
                       alog_ref, dtb_ref, gnw_ref, convpad_ref, sr_in, sg_in,
                       mix_ref, sr_out, sg_out,
                       qd_s, wk_s, qg_s, wv_s, cd_s, inter_s, u_s, qs_s):
    heads = range(N_HEADS)
    tok = _tokenwise("sample", pr_ref[...], pg_ref[...], cos_ref[...], sin_ref[...], rsc_ref[...],
                     cw_ref, convpad_ref[...])
    tok["kdT"] = [t.T for t in tok["kd"]]
    ret, gdn = _chunk_prep("sample", [tok], [pab_ref[...]], dmat_ref, alog_ref[...], dtb_ref[...])
    for h in heads:
        qd_s[h] = ret["qd"][h]
        wk_s[h] = gdn["wk"][h]
        qg_s[h] = gdn["qg"][h]
        wv_s[h] = gdn["wv"][h]
    cd_s[...] = gdn["cdr"][0]

    def per_batch(b, carry):
        rows = pl.ds(pl.multiple_of(b * SAMPLE_GROUP, SAMPLE_GROUP), SAMPLE_GROUP)
        cd_rows = cd_s[rows, :]
        for h in heads:
            inter_s[h, rows, :] = _mm(qd_s[h, rows, :], sr_in[b, h])
            s = sg_in[b, h]
            both = _mm(jnp.concatenate([wk_s[h, rows, :], qg_s[h, rows, :]], 0), s)
            u_s[h, rows, :] = wv_s[h, rows, :] - both[:SAMPLE_GROUP]
            qs_s[h, rows, :] = both[SAMPLE_GROUP:]
            cd = jnp.broadcast_to(cd_rows[SAMPLE_GROUP - 1:SAMPLE_GROUP, h:h + 1], (HEAD_DIM, HEAD_DIM))
            sg_out[b, h] = cd * s
        return carry

    lax.fori_loop(0, SAMPLE_PER_TILE, per_batch, 0)

    out_rows = SAMPLE_PER_TILE * SAMPLE_TOKENS
    out_row = _iota2((out_rows, TILE), 0)
    token_row = ((out_row >> 2) << 3) + (SAMPLE_GROUP - SAMPLE_TOKENS) + (out_row & (SAMPLE_TOKENS - 1))
    select = (_iota2((out_rows, TILE), 1) == token_row).astype(BF16)
    compact = lambda t: jnp.dot(select, t, preferred_element_type=F32).astype(BF16)

    state_shape = (SAMPLE_PER_TILE, HEAD_DIM, HEAD_DIM)
    for h in heads:
        o = _mm(ret["scores"][h], ret["v"][h]) + inter_s[h]
        upd = _mm(_stack_by_group(ret["kdT"][h]), ret["v"][h]).reshape(state_shape)
        sr_out[:, h] = ret_cd[h] * sr_in[:, h] + upd
        mix_ref[:, h * HEAD_DIM:(h + 1) * HEAD_DIM] = compact(_rms_gate(o, ret["gate"][h]))

    for h in heads:
        u = u_s[h]
        o = qs_s[h] + _mm(gdn["attn"][h], u)
        upd = _mm(_stack_by_group(gdn["ktT"][h]), u).reshape(state_shape)
        sg_out[:, h] = sg_out[:, h] + upd
        mix_ref[:, GROUP_W + h * HEAD_DIM:GROUP_W + (h + 1) * HEAD_DIM] = compact(_rms_gate(
            o, gdn["gate"][h], gnw_ref[...]))


def _sample_mixer(ret_cd, pr, pg, pab, cosf, sins, rsc, dmat, cw, alog, dtb, gnw, convpad, sr, sg):
    n_tiles = pr.shape[0] // TILE
    tile = lambda i: (i, 0)
    c2 = lambda i: (0, 0)
    c3 = lambda i: (0, 0, 0)
    state = pl.BlockSpec((SAMPLE_PER_TILE,) + _STATE_SHAPE, lambda i: (i, 0, 0, 0))
    head_scratch = pltpu.VMEM((N_HEADS, TILE, HEAD_DIM), F32)
    return pl.pallas_call(
        functools.partial(_sample_mixer_body, ret_cd),
        grid=(n_tiles,),
        in_specs=[
            pl.BlockSpec((TILE, 4 * GROUP_W), tile),
            pl.BlockSpec((TILE, 4 * GROUP_W), tile),
            pl.BlockSpec((TILE, HEAD_DIM), tile),
            pl.BlockSpec((TILE, HEAD_DIM), c2),
            pl.BlockSpec((TILE, HEAD_DIM), c2),
            pl.BlockSpec((TILE, HEAD_DIM), c2),
            pl.BlockSpec((N_HEADS, TILE, TILE), c3),
            pl.BlockSpec((CONV_W, CONV_COLS), c2),
            pl.BlockSpec((1, HEAD_DIM), c2),
            pl.BlockSpec((1, HEAD_DIM), c2),
            pl.BlockSpec((1, HEAD_DIM), c2),
            pl.BlockSpec((TILE, CONV_COLS), tile),
            state, state,
        ],
        out_specs=[pl.BlockSpec((SAMPLE_PER_TILE * SAMPLE_TOKENS, 2 * GROUP_W), tile), state, state],
        out_shape=[
            jax.ShapeDtypeStruct((n_tiles * SAMPLE_PER_TILE * SAMPLE_TOKENS, 2 * GROUP_W), BF16),
            jax.ShapeDtypeStruct(sr.shape, F32),
            jax.ShapeDtypeStruct(sg.shape, F32),
        ],
        scratch_shapes=[head_scratch] * 4 + [pltpu.VMEM((TILE, HEAD_DIM), F32)] + [head_scratch] * 3,
        compiler_params=_cparams(("parallel",)),
        name="mixer_sample",
    )(pr, pg, pab, cosf, sins, rsc, dmat, cw, alog, dtb, gnw, convpad, sr, sg)


def _rotary_tables(pos):
    half = HEAD_DIM // 2
    inv = ROPE_BASE ** (-np.arange(half, dtype=np.float64) / half)
    ang = np.asarray(pos, np.float64)[:, None] * inv[None, :]
    cos, sin = np.cos(ang), np.sin(ang)
    return (jnp.asarray(np.concatenate([cos, cos], -1), F32),
            jnp.asarray(np.concatenate([-sin, sin], -1), F32))


def _retention_tables(seg, pos, valid, seg_len):
    gamma = 1.0 - 2.0 ** (-5.0 - np.arange(N_HEADS, dtype=np.float64))
    posf = np.asarray(pos, np.float64)
    rel = posf[:, None] - posf[None, :]
    causal = (seg[:, None] == seg[None, :]) & (rel >= 0)
    dmat = np.where(causal[None], gamma[:, None, None] ** np.where(causal, rel, 0.0)[None], 0.0)
    q_scale = gamma[None, :] ** (posf[:, None] + 1.0)
    k_scale = np.where(valid[:, None], gamma[None, :] ** (seg_len - 1.0 - posf[:, None]), 0.0)
    rsc = np.concatenate([q_scale, k_scale, np.zeros((TILE, HEAD_DIM - 2 * N_HEADS))], -1)
    chunk_decay = tuple(float(g ** seg_len) for g in gamma)
    return jnp.asarray(dmat, F32), jnp.asarray(rsc, F32), chunk_decay


def _pad_lanes(v):
    return jnp.pad(v.astype(F32), (0, HEAD_DIM - v.shape[0]))[None, :]


def kernel(x_prompt, x_sample, state_ret, state_gdn, state_conv, meta_tokens, emb_ln_g, emb_ln_b,
           w_in, conv_w, a_log, dt_bias, gdn_norm_w, w_out, ln1_g, ln1_b, w_gate_up, w_down,
           ln2_g, ln2_b):
    n_batch, seq, _ = x_prompt.shape
    dec_batch, dec_seq, _ = x_sample.shape
    assert seq % TILE == 0 and dec_seq == SAMPLE_TOKENS and N_META <= TILE
    assert n_batch % MIXER_SEQS_PER_STEP == 0
    layer = 0

    w_all = _weight_prep(jnp.swapaxes(w_in[layer], 0, 1))
    w_out_b = w_out[layer].astype(BF16)
    w_gu_b = w_gate_up[layer].astype(BF16)
    w_down_b = w_down[layer].astype(BF16)
    row = lambda v: v.astype(F32)[None, :]
    eg, eb = row(emb_ln_g), row(emb_ln_b)
    cw = conv_w[layer].astype(F32)
    alog, dtb, gnw = _pad_lanes(a_log[layer]), _pad_lanes(dt_bias[layer]), row(gdn_norm_w[layer])

    xp = x_prompt.reshape(n_batch * seq, D_MODEL)
    xm = jnp.pad(meta_tokens.astype(F32), ((0, TILE - N_META), (0, 0)))
    xs = jnp.pad(x_sample, ((0, 0), (SAMPLE_GROUP - dec_seq, 0), (0, 0))).reshape(
        dec_batch * SAMPLE_GROUP, D_MODEL)
    convpad = jnp.pad(state_conv[layer].astype(F32), ((0, 0), (1, SAMPLE_GROUP - CONV_W), (0, 0))).reshape(
        dec_batch * SAMPLE_GROUP, CONV_COLS)

    tile_idx = np.arange(TILE)
    cos_p, sin_p = _rotary_tables(N_META + np.arange(seq))
    cos_m, sin_m = _rotary_tables(tile_idx)
    tok = (tile_idx % SAMPLE_GROUP) - (SAMPLE_GROUP - dec_seq)
    cos_s, sin_s = _rotary_tables(PAST_LEN + np.maximum(tok, 0))
    zeros_i = np.zeros((TILE,), np.int32)
    all_valid = np.ones((TILE,), bool)
    dmat_p, rsc_p, cd_p = _retention_tables(zeros_i, tile_idx, all_valid, float(TILE))
    dmat_m, rsc_m, cd_m = _retention_tables(zeros_i, tile_idx, tile_idx < N_META, float(N_META))
    dmat_s, rsc_s, cd_s = _retention_tables(tile_idx // SAMPLE_GROUP, tok, tok >= 0, float(dec_seq))

    zero_state = jnp.zeros(_STATE_SHAPE, F32)
    pr_m, pg_m, pab_m = _front(xm, eg, eb, w_all, TILE)
    _, sr_m, sg_m = _meta_mixer(cd_m, pr_m, pg_m, pab_m, cos_m, sin_m, rsc_m, dmat_m, cw, alog, dtb, gnw,
                                zero_state, zero_state)
    halo_p = pg_m[N_META - HALO:N_META, :CONV_COLS]

    mix_p, sr_p, sg_p, tail_p = _prompt_fused(cd_p, x_prompt, eg, eb, w_all, cos_p, sin_p, rsc_p, cw, halo_p,
                                              dmat_p, alog, dtb, gnw, sr_m[0], sg_m[0], MIXER_SEQS_PER_STEP)
    mix_p = mix_p.reshape(n_batch * seq, 2 * GROUP_W)

    pr_s, pg_s, pab_s = _front(xs, eg, eb, w_all, 512)
    mix_s, sr_s, sg_s = _sample_mixer(cd_s, pr_s, pg_s, pab_s, cos_s, sin_s, rsc_s, dmat_s, cw, alog,
                                      dtb, gnw, convpad, state_ret[layer].astype(F32),
                                      state_gdn[layer].astype(F32))

    back = functools.partial(_back, eg=eg, eb=eb, w_out=w_out_b, g1=row(ln1_g[layer]), b1=row(ln1_b[layer]),
                             w_gu=w_gu_b, w_down=w_down_b, g2=row(ln2_g[layer]), b2=row(ln2_b[layer]),
                             tm=512)
    y_p = back(xp, mix_p).reshape(n_batch, seq, D_MODEL)
    y_s = back(x_sample.reshape(dec_batch * dec_seq, D_MODEL), mix_s).reshape(dec_batch, dec_seq, D_MODEL)

    conv_p = tail_p[:, HALO - (CONV_W - 1):, :]
    conv_s = pg_s.reshape(dec_batch, SAMPLE_GROUP, 4 * GROUP_W)[:, SAMPLE_GROUP - (CONV_W - 1):, :CONV_COLS]
    return (y_p, y_s, sr_p[None], sg_p[None], conv_p[None], sr_s[None], sg_s[None], conv_s[None])
```

```python
import functools

import numpy as np
import jax
import jax.numpy as jnp
from jax import lax
from jax.experimental import pallas as pl
from jax.experimental.pallas import tpu as pltpu

F32 = jnp.float32
BF16 = jnp.bfloat16

D_MODEL = 1024
N_META = 16
N_HEADS = 4
HEAD_DIM = 128
GROUP_W = N_HEADS * HEAD_DIM
CONV_W = 4
CONV_COLS = 3 * GROUP_W
D_FF = 2816
PAST_LEN = 16384
ROPE_BASE = 10000.0
LN_EPS = 1e-5
RMS_EPS = 1e-6
ALPHA = 2.0 ** 0.25
TILE = 128
SAMPLE_GROUP = 8
SAMPLE_TOKENS = 4
HALO = 8
MIXER_SEQS_PER_STEP = 4
BACK_PARTS = 2
W_ALL_COLS = 9 * GROUP_W
AB_ROWS = 2 * N_HEADS
VMEM_LIMIT = 56 * 1024 * 1024


def _cparams(sem):
    return pltpu.CompilerParams(dimension_semantics=sem, vmem_limit_bytes=VMEM_LIMIT)


def _layer_norm(x, g, b):
    mu = jnp.mean(x, -1, keepdims=True)
    xc = x - mu
    var = jnp.mean(xc * xc, -1, keepdims=True)
    return xc * lax.rsqrt(var + LN_EPS) * g + b


def _sigmoid(x):
    return 1.0 / (1.0 + jnp.exp(-x))


def _silu(x):
    return x * _sigmoid(x)


def _softplus(x):
    return jnp.maximum(x, 0.0) + jnp.log(1.0 + jnp.exp(-jnp.abs(x)))


def _mm(a, b):
    return jnp.dot(a.astype(BF16), b.astype(BF16), preferred_element_type=F32)


def _mm_nt(a, b):
    return lax.dot_general(a.astype(BF16), b.astype(BF16), (((1,), (1,)), ((), ())),
                           preferred_element_type=F32)


def _mm_each(xs, ys):
    return [_mm(x, y) for x, y in zip(xs, ys)]


def _mm_nt_each(xs, ys):
    return [_mm_nt(x, y) for x, y in zip(xs, ys)]


def _mm_split3(m01, x):
    x1 = x.astype(BF16)
    r1 = x - x1.astype(F32)
    x2 = r1.astype(BF16)
    x3 = (r1 - x2.astype(F32)).astype(BF16)
    dot = functools.partial(jnp.dot, preferred_element_type=F32)
    return (dot(m01, x3) + dot(m01, x2)) + dot(m01, x1)


def _lane_bcast(x, lane):
    return jnp.broadcast_to(x[:, lane:lane + 1], (x.shape[0], HEAD_DIM))


def _head_cols(x, base, h):
    return x[:, base + h * HEAD_DIM:base + (h + 1) * HEAD_DIM]


def _weight_prep_body(q_ref, k_ref, v_ref, g_ref, ab_ref, o_ref):
    j = pl.program_id(0)

    @pl.when(j < 2 * N_HEADS)
    def _():
        for piece, ref in enumerate((q_ref, k_ref, v_ref, g_ref)):
            o_ref[:, piece * HEAD_DIM:(piece + 1) * HEAD_DIM] = ref[...].T.astype(BF16)

    @pl.when(j == 2 * N_HEADS)
    def _():
        ab = jnp.concatenate([ab_ref[...], jnp.zeros((HEAD_DIM - AB_ROWS, D_MODEL), F32)], 0)
        o_ref[:, :HEAD_DIM] = ab.T.astype(BF16)
        o_ref[:, HEAD_DIM:] = jnp.zeros((D_MODEL, GROUP_W - HEAD_DIM), BF16)


def _weight_prep(w_in_t):
    assert w_in_t.shape == (8 * GROUP_W + AB_ROWS, D_MODEL)

    def piece_spec(piece):
        def index(j):
            ret_block = piece * N_HEADS + j
            gdn_block = N_HEADS * j + piece
            return (jnp.where(j < N_HEADS, ret_block, jnp.where(j < 2 * N_HEADS, gdn_block, 0)), 0)
        return pl.BlockSpec((HEAD_DIM, D_MODEL), index)

    ab_spec = pl.BlockSpec((AB_ROWS, D_MODEL), lambda j: (8 * GROUP_W // AB_ROWS, 0))
    return pl.pallas_call(
        _weight_prep_body,
        grid=(2 * N_HEADS + 1,),
        in_specs=[piece_spec(p) for p in range(4)] + [ab_spec],
        out_specs=pl.BlockSpec((D_MODEL, GROUP_W), lambda j: (0, j)),
        out_shape=jax.ShapeDtypeStruct((D_MODEL, W_ALL_COLS), BF16),
        compiler_params=_cparams(("arbitrary",)),
        name="weight_prep",
    )(w_in_t, w_in_t, w_in_t, w_in_t, w_in_t)


def _front_body(x_ref, g_ref, b_ref, w_ref, pr_ref, pg_ref, pab_ref):
    h = _layer_norm(x_ref[...], g_ref[...], b_ref[...]).astype(BF16)
    pr_ref[...] = jnp.dot(h, w_ref[:, :4 * GROUP_W], preferred_element_type=F32)
    pg_ref[...] = jnp.dot(h, w_ref[:, 4 * GROUP_W:8 * GROUP_W], preferred_element_type=F32)
    pab_ref[...] = jnp.dot(h, w_ref[:, 8 * GROUP_W:8 * GROUP_W + HEAD_DIM], preferred_element_type=F32)


def _front(x2d, ln_g, ln_b, w_all, tm):
    rows = x2d.shape[0]
    const = lambda i: (0, 0)
    tile = lambda i: (i, 0)
    return pl.pallas_call(
        _front_body,
        grid=(rows // tm,),
        in_specs=[
            pl.BlockSpec((tm, D_MODEL), tile),
            pl.BlockSpec((1, D_MODEL), const),
            pl.BlockSpec((1, D_MODEL), const),
            pl.BlockSpec((D_MODEL, W_ALL_COLS), const),
        ],
        out_specs=[
            pl.BlockSpec((tm, 4 * GROUP_W), tile),
            pl.BlockSpec((tm, 4 * GROUP_W), tile),
            pl.BlockSpec((tm, HEAD_DIM), tile),
        ],
        out_shape=[
            jax.ShapeDtypeStruct((rows, 4 * GROUP_W), F32),
            jax.ShapeDtypeStruct((rows, 4 * GROUP_W), F32),
            jax.ShapeDtypeStruct((rows, HEAD_DIM), F32),
        ],
        compiler_params=_cparams(("parallel",)),
        name="front",
    )(x2d, ln_g, ln_b, w_all)


def _back_body(n_main, x_ref, mix_ref, xs_ref, mixs_ref, eg_ref, eb_ref, wo_ref, g1_ref, b1_ref,
               wgu_ref, wd_ref, g2_ref, b2_ref, y_ref, ys_ref):
    part = x_ref.shape[0] // BACK_PARTS
    parts = [slice(i * part, (i + 1) * part) for i in range(BACK_PARTS)]
    dot = functools.partial(jnp.dot, preferred_element_type=F32)

    def tile(x_in, mix_in, y_out):
        h = [_layer_norm(x_in[p, :], eg_ref[...], eb_ref[...]) for p in parts]
        mp = [dot(mix_in[p, :], wo_ref[...]) for p in parts]
        h1 = [_layer_norm(ALPHA * a + b, g1_ref[...], b1_ref[...]) for a, b in zip(h, mp)]
        gu = [dot(a.astype(BF16), wgu_ref[...]) for a in h1]
        act = [(_silu(a[:, :D_FF]) * a[:, D_FF:]).astype(BF16) for a in gu]
        ff = [dot(a, wd_ref[...]) for a in act]
        for p, a, b in zip(parts, h1, ff):
            y_out[p, :] = _layer_norm(ALPHA * a + b, g2_ref[...], b2_ref[...])

    extra = pl.program_id(0) == n_main
    pl.when(jnp.logical_not(extra))(lambda: tile(x_ref, mix_ref, y_ref))
    pl.when(extra)(lambda: tile(xs_ref, mixs_ref, ys_ref))


def _back(x2d, mix, x_extra, mix_extra, eg, eb, w_out, g1, b1, w_gu, w_down, g2, b2, tm):
    rows = x2d.shape[0]
    n_main = rows // tm
    assert x_extra.shape == (tm, D_MODEL)
    const = lambda i: (0, 0)
    main = lambda i: (jnp.minimum(i, n_main - 1), 0)
    single = pl.Buffered(1)
    vec = pl.BlockSpec((1, D_MODEL), const)
    return pl.pallas_call(
        functools.partial(_back_body, n_main),
        grid=(n_main + 1,),
        in_specs=[
            pl.BlockSpec((tm, D_MODEL), main),
            pl.BlockSpec((tm, D_MODEL), main),
            pl.BlockSpec((tm, D_MODEL), const),
            pl.BlockSpec((tm, D_MODEL), const),
            vec, vec,
            pl.BlockSpec((D_MODEL, D_MODEL), const, pipeline_mode=single),
            vec, vec,
            pl.BlockSpec((D_MODEL, 2 * D_FF), const, pipeline_mode=single),
            pl.BlockSpec((D_FF, D_MODEL), const, pipeline_mode=single),
            vec, vec,
        ],
        out_specs=[pl.BlockSpec((tm, D_MODEL), main), pl.BlockSpec((tm, D_MODEL), const)],
        out_shape=[jax.ShapeDtypeStruct((rows, D_MODEL), F32), jax.ShapeDtypeStruct((tm, D_MODEL), F32)],
        compiler_params=_cparams(("arbitrary",)),
        name="back",
    )(x2d, mix, x_extra, mix_extra, eg, eb, w_out, g1, b1, w_gu, w_down, g2, b2)


def _iota2(shape, dim):
    return lax.broadcasted_iota(jnp.int32, shape, dim)


def _valid_rows(variant, shape):
    r = _iota2(shape, 0)
    if variant == "meta":
        return r < N_META
    if variant == "sample":
        return (r & (SAMPLE_GROUP - 1)) >= SAMPLE_GROUP - SAMPLE_TOKENS
    return None


def _mask_rows(valid, x):
    return x if valid is None else jnp.where(valid, x, 0.0)


def _tri_inverse(variant, nmats, row, col, between=None):
    between = between or (lambda: None)
    eye = (row == col).astype(F32)
    if variant == "sample":
        n2 = _mm_each(nmats, nmats)
        ps = [eye + n for n in nmats]
        return [p + t for p, t in zip(ps, _mm_each(ps, n2))]
    base_log2 = 3
    in_block = (row >> base_log2) == (col >> base_log2)
    ds = [jnp.where(in_block, n, 0.0) for n in nmats]
    ps = [eye + d for d in ds]
    for _ in range(base_log2 - 1):
        ds = _mm_each(ds, ds)
        between()
        ps = [p + t for p, t in zip(ps, _mm_each(ps, ds))]
        between()
    live_rows = N_META if variant == "meta" else TILE
    s = base_log2
    while (1 << s) < live_rows:
        size = 1 << s
        lower_left = (((row >> (s + 1)) == (col >> (s + 1))) & (((row >> s) & 1) == 1)
                      & (((col >> s) & 1) == 0))
        cs = [jnp.where(lower_left, n, 0.0) for n in nmats]
        lower = [slice(start + size, start + 2 * size) for start in range(0, TILE, 2 * size)]
        upper = [slice(start, start + size) for start in range(0, TILE, 2 * size)]
        p_low = [jnp.concatenate([p[rows, :] for rows in lower], 0) for p in ps]
        half = _mm_each(p_low, cs)
        between()
        fix = _mm_each(half, ps)
        between()
        merged = []
        for p, f in zip(ps, fix):
            pieces = []
            for i, (up, lo) in enumerate(zip(upper, lower)):
                pieces += [p[up, :], p[lo, :] + f[i * size:(i + 1) * size, :]]
            merged.append(jnp.concatenate(pieces, 0))
        ps = merged
        s += 1
    return ps


def _rotary(t, cosf, sins):
    return t * cosf + pltpu.roll(t, HEAD_DIM // 2, 1) * sins


def _ret_head_ops(valid, q, k, v, gate, cosf, sins, q_scale, k_scale):
    rq = _mask_rows(valid, _rotary(q, cosf, sins))
    rk = _mask_rows(valid, _rotary(k, cosf, sins) * (HEAD_DIM ** -0.5))
    return dict(rq=rq, rk=rk, rv=_mask_rows(valid, v), qd=rq * q_scale, kd=rk * k_scale, sgr=_silu(gate))


def _short_conv(x, halo, cw_ref, col0):
    rows, ncols = x.shape
    cols = slice(col0, col0 + ncols)
    n = rows // HALO
    full = jnp.concatenate([halo, x], 0).reshape(n + 1, HALO, ncols)
    sub = lax.broadcasted_iota(jnp.int32, (n, HALO, ncols), 1)
    conv = full[1:] * cw_ref[CONV_W - 1:CONV_W, cols]
    for shift in range(1, CONV_W):
        rolled = pltpu.roll(full, shift, 1)
        shifted = jnp.where(sub < shift, rolled[:-1], rolled[1:])
        conv = conv + shifted * cw_ref[CONV_W - 1 - shift:CONV_W - shift, cols]
    return _silu(conv).reshape(rows, ncols)


def _l2_normalize(t):
    return t * lax.rsqrt(jnp.sum(t * t, -1, keepdims=True) + RMS_EPS)


def _tokenwise(variant, pret, pgdn, cosf, sins, rsc, cw_ref, convpad):
    rows = pret.shape[0]
    heads = range(N_HEADS)
    valid = _valid_rows(variant, (rows, HEAD_DIM))
    ret = [_ret_head_ops(valid, *(_head_cols(pret, h * GROUP_W, g) for g in range(4)), cosf, sins,
                         _lane_bcast(rsc, h), _lane_bcast(rsc, N_HEADS + h)) for h in heads]
    x = pgdn[:, :CONV_COLS]
    if variant == "sample":
        x = jnp.where(_valid_rows(variant, x.shape), x, convpad)
    conv = _short_conv(x, jnp.zeros((HALO, CONV_COLS), F32), cw_ref, 0)
    gq = [_mask_rows(valid, _l2_normalize(_head_cols(conv, 0, h)) * (HEAD_DIM ** -0.5)) for h in heads]
    gk = [_mask_rows(valid, _l2_normalize(_head_cols(conv, GROUP_W, h))) for h in heads]
    gv = [_mask_rows(valid, _head_cols(conv, 2 * GROUP_W, h)) for h in heads]
    sgz = [_silu(_head_cols(pgdn, 3 * GROUP_W, h)) for h in heads]
    tok = {key: [r[key] for r in ret] for key in ret[0]}
    tok.update(gq=gq, gk=gk, gv=gv, sgz=sgz)
    return tok


def _chunk_prep(variant, toks, pabs, dmat_ref, alog, dtb, between=None):
    shape = (TILE, HEAD_DIM)
    tiles = range(len(toks))
    items = [(j, h) for j in tiles for h in range(N_HEADS)]
    pick = lambda key: [toks[j][key][h] for j, h in items]
    row = _iota2(shape, 0)
    col = _iota2(shape, 1)
    valid = _valid_rows(variant, shape)
    if variant == "sample":
        same = (row >> 3) == (col >> 3)
        incl = same & (row >= col)
        strict = same & (row > col)
    else:
        incl = row >= col
        strict = row > col

    incl01 = incl.astype(BF16)
    beta_all, gcum, gam, ktail, cdr, gcum_t = [], [], [], [], [], []
    for j in tiles:
        g_all = _mask_rows(valid, -jnp.exp(alog) * _softplus(pabs[j] + dtb))
        beta_all.append(_mask_rows(valid, _sigmoid(pabs[j])))
        if variant == "sample":
            both = _mm_split3(jnp.concatenate([incl01, same.astype(BF16)], 0), g_all)
            gc, gseg = both[:TILE], both[TILE:]
        else:
            gc = _mm_split3(incl01, g_all)
            gseg = jnp.broadcast_to(gc[TILE - 1:TILE, :], shape)
        gcum.append(gc)
        gam.append(jnp.exp(gc))
        ktail.append(jnp.exp(gseg - gc))
        cdr.append(jnp.exp(gseg))
        gcum_t.append(gc.T)

    gq, gk, gv = pick("gq"), pick("gk"), pick("gv")
    n_items = range(len(items))
    dec = [jnp.where(incl, jnp.exp(jnp.minimum(
        _lane_bcast(gcum[j], h) - jnp.broadcast_to(gcum_t[j][h:h + 1, :], shape), 0.0)), 0.0)
        for j, h in items]
    bcol = [_lane_bcast(beta_all[j], N_HEADS + h) for j, h in items]
    gamc = [_lane_bcast(gam[j], h) for j, h in items]
    kk = _mm_nt_each(gk, gk)
    qk = _mm_nt_each(gq, gk)
    nmats = [-(jnp.where(strict, kk[i] * dec[i], 0.0) * bcol[i]) for i in n_items]
    tinv = _tri_inverse(variant, nmats, row, col, between)
    rhs = [jnp.concatenate([gv[i] * bcol[i], gk[i] * (bcol[i] * gamc[i])], 1) for i in n_items]
    sol = _mm_each(tinv, rhs)
    while between is not None and between():
        pass
    ret = dict(
        items=items,
        scores=[qk_r * dmat_ref[h] for qk_r, (_, h) in zip(
            _mm_nt_each(pick("rq"), pick("rk")), items)],
        qd=pick("qd"), kdT=pick("kdT"), v=pick("rv"), gate=pick("sgr"),
    )
    gdn = dict(
        items=items,
        wv=[t[:, :HEAD_DIM] for t in sol],
        wk=[t[:, HEAD_DIM:] for t in sol],
        attn=[qk[i] * dec[i] for i in n_items],
        qg=[gq[i] * gamc[i] for i in n_items],
        ktT=[(gk[i] * _lane_bcast(ktail[j], h)).T for i, (j, h) in enumerate(items)],
        cdr=cdr,
        gate=pick("sgz"),
    )
    return ret, gdn


def _rms_gate(o, gate, weight=None):
    o = o * lax.rsqrt(jnp.mean(o * o, -1, keepdims=True) + RMS_EPS)
    if weight is not None:
        o = o * weight
    return (o * gate).astype(BF16)


_STATE_SHAPE = (N_HEADS, HEAD_DIM, HEAD_DIM)


def _seq_init(s0r_ref, s0g_ref, sr_ref, sg_ref):
    @pl.when(pl.program_id(1) == 0)
    def _():
        for j in range(sr_ref.shape[0]):
            sr_ref[j] = s0r_ref[...]
            sg_ref[j] = s0g_ref[...]


def _seq_step(ret_cd, ret, gdn, gnw, mix_ref, sr_out, sg_out, sr_ref, sg_ref):
    items = ret["items"]
    idx = range(len(items))
    s_r = [sr_ref[j, h] for j, h in items]
    s_g = [sg_ref[j, h] for j, h in items]
    bf = lambda t: t.astype(BF16)
    cat = lambda a, b, axis: jnp.concatenate([bf(a), bf(b)], axis)
    wk_s = _mm_each(gdn["wk"], s_g)
    o_r = _mm_each([cat(ret["scores"][i], ret["qd"][i], 1) for i in idx],
                   [cat(ret["v"][i], s_r[i], 0) for i in idx])
    u = [gdn["wv"][i] - wk_s[i] for i in idx]
    o_g = _mm_each([cat(gdn["qg"][i], gdn["attn"][i], 1) for i in idx],
                   [cat(s_g[i], u[i], 0) for i in idx])
    upd_g = _mm_each(gdn["ktT"], u)
    upd_r = _mm_each(ret["kdT"], ret["v"])
    for i, (j, h) in enumerate(items):
        cd = jnp.broadcast_to(gdn["cdr"][j][0:1, h:h + 1], (HEAD_DIM, HEAD_DIM))
        sg_ref[j, h] = cd * s_g[i] + upd_g[i]
        sr_ref[j, h] = ret_cd[h] * s_r[i] + upd_r[i]
    for i, (j, h) in enumerate(items):
        mix_ref[j, :, h * HEAD_DIM:(h + 1) * HEAD_DIM] = _rms_gate(o_r[i], ret["gate"][i])
        mix_ref[j, :, GROUP_W + h * HEAD_DIM:GROUP_W + (h + 1) * HEAD_DIM] = _rms_gate(
            o_g[i], gdn["gate"][i], gnw)

    @pl.when(pl.program_id(1) == pl.num_programs(1) - 1)
    def _():
        sr_out[...] = sr_ref[...]
        sg_out[...] = sg_ref[...]


def _seq_out(n_batch, n_chunks, nb):
    state = pl.BlockSpec((nb,) + _STATE_SHAPE, lambda b, n: (b, 0, 0, 0))
    out_specs = [pl.BlockSpec((nb, TILE, 2 * GROUP_W), lambda b, n: (b, n, 0)), state, state]
    out_shape = [
        jax.ShapeDtypeStruct((n_batch, n_chunks * TILE, 2 * GROUP_W), BF16),
        jax.ShapeDtypeStruct((n_batch,) + _STATE_SHAPE, F32),
        jax.ShapeDtypeStruct((n_batch,) + _STATE_SHAPE, F32),
    ]
    scratch = [pltpu.VMEM((nb,) + _STATE_SHAPE, F32), pltpu.VMEM((nb,) + _STATE_SHAPE, F32)]
    return out_specs, out_shape, scratch


def _meta_mixer_body(ret_cd, pr_ref, pg_ref, pab_ref, cos_ref, sin_ref, rsc_ref, dmat_ref,
                     cw_ref, alog_ref, dtb_ref, gnw_ref, s0r_ref, s0g_ref,
                     mix_ref, sr_out, sg_out, sr_ref, sg_ref):
    _seq_init(s0r_ref, s0g_ref, sr_ref, sg_ref)
    tok = _tokenwise("meta", pr_ref[...], pg_ref[...], cos_ref[...], sin_ref[...], rsc_ref[...],
                     cw_ref, None)
    tok["kdT"] = [t.T for t in tok["kd"]]
    ret, gdn = _chunk_prep("meta", [tok], [pab_ref[...]], dmat_ref, alog_ref[...], dtb_ref[...])
    _seq_step(ret_cd, ret, gdn, gnw_ref[...], mix_ref, sr_out, sg_out, sr_ref, sg_ref)


def _meta_mixer(ret_cd, pr, pg, pab, cosf, sins, rsc, dmat, cw, alog, dtb, gnw, s0r, s0g):
    c2 = lambda b, n: (0, 0)
    c3 = lambda b, n: (0, 0, 0)
    out_specs, out_shape, state_scratch = _seq_out(1, 1, 1)
    return pl.pallas_call(
        functools.partial(_meta_mixer_body, ret_cd),
        grid=(1, 1),
        in_specs=[
            pl.BlockSpec((TILE, 4 * GROUP_W), c2),
            pl.BlockSpec((TILE, 4 * GROUP_W), c2),
            pl.BlockSpec((TILE, HEAD_DIM), c2),
            pl.BlockSpec((TILE, HEAD_DIM), c2),
            pl.BlockSpec((TILE, HEAD_DIM), c2),
            pl.BlockSpec((TILE, HEAD_DIM), c2),
            pl.BlockSpec((N_HEADS, TILE, TILE), c3),
            pl.BlockSpec((CONV_W, CONV_COLS), c2),
            pl.BlockSpec((1, HEAD_DIM), c2),
            pl.BlockSpec((1, HEAD_DIM), c2),
            pl.BlockSpec((1, HEAD_DIM), c2),
            pl.BlockSpec(_STATE_SHAPE, c3),
            pl.BlockSpec(_STATE_SHAPE, c3),
        ],
        out_specs=out_specs,
        out_shape=out_shape,
        scratch_shapes=state_scratch,
        compiler_params=_cparams(("parallel", "arbitrary")),
        name="mixer_meta",
    )(pr, pg, pab, cosf, sins, rsc, dmat, cw, alog, dtb, gnw, s0r, s0g)


def _prompt_fused_body(ret_cd, x_ref, g_ref, b_ref, w_ref, cos_ref, sin_ref, rsc_ref, cw_ref, halo0_ref,
                       dmat_ref, alog_ref, dtb_ref, gnw_ref, s0r_ref, s0g_ref,
                       mix_ref, sr_out, sg_out, tail_ref, sr_ref, sg_ref, halo_ref):
    nb = x_ref.shape[0]
    seqs, heads = range(nb), range(N_HEADS)
    _seq_init(s0r_ref, s0g_ref, sr_ref, sg_ref)

    @pl.when(pl.program_id(1) == 0)
    def _():
        for j in seqs:
            halo_ref[j] = halo0_ref[...]

    hidden = _layer_norm(x_ref[...].reshape(nb * TILE, D_MODEL), g_ref[...], b_ref[...]).astype(BF16)
    cosf, sins, rsc = cos_ref[...], sin_ref[...], rsc_ref[...]
    keys = ("rq", "rk", "rv", "qd", "kdT", "sgr", "gq", "gk", "gv", "sgz")
    toks = [{key: [None] * N_HEADS for key in keys} for _ in seqs]
    pabs = [None] * nb
    rows = lambda j: slice(j * TILE, (j + 1) * TILE)

    def project(group, width=GROUP_W):
        return jnp.dot(hidden, w_ref[:, group * GROUP_W:group * GROUP_W + width], preferred_element_type=F32)

    def conv_group(group, key):
        val = project(N_HEADS + group)
        cols = slice(group * GROUP_W, (group + 1) * GROUP_W)
        for j in seqs:
            part = val[rows(j), :]
            conv = _short_conv(part, halo_ref[j, :, cols], cw_ref, group * GROUP_W)
            halo_ref[j, :, cols] = part[TILE - HALO:, :]
            for h in heads:
                t = _head_cols(conv, 0, h)
                if key == "gq":
                    t = _l2_normalize(t) * (HEAD_DIM ** -0.5)
                elif key == "gk":
                    t = _l2_normalize(t)
                toks[j][key][h] = t

    def ab_group():
        val = project(2 * N_HEADS, HEAD_DIM)
        for j in seqs:
            pabs[j] = val[rows(j), :]

    def ret_head(h):
        val = project(h)
        for j in seqs:
            part = val[rows(j), :]
            ops = _ret_head_ops(None, *(_head_cols(part, 0, g) for g in range(4)), cosf, sins,
                                _lane_bcast(rsc, h), _lane_bcast(rsc, N_HEADS + h))
            for key in ("rq", "rk", "rv", "qd", "sgr"):
                toks[j][key][h] = ops[key].astype(BF16)
            toks[j]["kdT"][h] = ops["kd"].T.astype(BF16)

    def gate_group():
        val = project(2 * N_HEADS - 1)
        for j in seqs:
            for h in heads:
                toks[j]["sgz"][h] = _silu(_head_cols(val[rows(j), :], 0, h)).astype(BF16)

    ab_group()
    conv_group(1, "gk")
    conv_group(0, "gq")
    conv_group(2, "gv")
    for j in seqs:
        tail_ref[j] = halo_ref[j]

    fillers = [functools.partial(ret_head, h) for h in heads] + [gate_group]

    def between():
        if not fillers:
            return False
        fillers.pop(0)()
        return True

    ret, gdn = _chunk_prep("prompt", toks, pabs, dmat_ref, alog_ref[...], dtb_ref[...], between)
    _seq_step(ret_cd, ret, gdn, gnw_ref[...], mix_ref, sr_out, sg_out, sr_ref, sg_ref)


def _prompt_fused(ret_cd, x, ln_g, ln_b, w_all, cosf, sins, rsc, cw, halo0, dmat, alog, dtb, gnw, s0r, s0g, nb):
    n_batch, seq, _ = x.shape
    n_chunks = seq // TILE
    tile = lambda b, n: (b, n, 0)
    pos = lambda b, n: (n, 0)
    c2 = lambda b, n: (0, 0)
    c3 = lambda b, n: (0, 0, 0)
    out_specs, out_shape, state_scratch = _seq_out(n_batch, n_chunks, nb)
    return pl.pallas_call(
        functools.partial(_prompt_fused_body, ret_cd),
        grid=(n_batch // nb, n_chunks),
        in_specs=[
            pl.BlockSpec((nb, TILE, D_MODEL), tile),
            pl.BlockSpec((1, D_MODEL), c2),
            pl.BlockSpec((1, D_MODEL), c2),
            pl.BlockSpec((D_MODEL, W_ALL_COLS), c2, pipeline_mode=pl.Buffered(1)),
            pl.BlockSpec((TILE, HEAD_DIM), pos),
            pl.BlockSpec((TILE, HEAD_DIM), pos),
            pl.BlockSpec((TILE, HEAD_DIM), c2),
            pl.BlockSpec((CONV_W, CONV_COLS), c2),
            pl.BlockSpec((HALO, CONV_COLS), c2),
            pl.BlockSpec((N_HEADS, TILE, TILE), c3),
            pl.BlockSpec((1, HEAD_DIM), c2),
            pl.BlockSpec((1, HEAD_DIM), c2),
            pl.BlockSpec((1, HEAD_DIM), c2),
            pl.BlockSpec(_STATE_SHAPE, c3),
            pl.BlockSpec(_STATE_SHAPE, c3),
        ],
        out_specs=out_specs + [pl.BlockSpec((nb, HALO, CONV_COLS), lambda b, n: (b, 0, 0))],
        out_shape=out_shape + [jax.ShapeDtypeStruct((n_batch, HALO, CONV_COLS), F32)],
        scratch_shapes=state_scratch + [pltpu.VMEM((nb, HALO, CONV_COLS), F32)],
        compiler_params=_cparams(("parallel", "arbitrary")),
        name="prompt_fused",
    )(x, ln_g, ln_b, w_all, cosf, sins, rsc, cw, halo0, dmat, alog, dtb, gnw, s0r, s0g)


SAMPLE_PER_TILE = TILE // SAMPLE_GROUP


def _stack_by_group(xt):
    shape3 = (SAMPLE_PER_TILE, HEAD_DIM, TILE)
    keep = lax.broadcasted_iota(jnp.int32, shape3, 0) == (lax.broadcasted_iota(jnp.int32, shape3, 2) >> 3)
    stacked = jnp.where(keep, jnp.broadcast_to(xt[None], shape3), 0.0)
    return stacked.reshape(SAMPLE_PER_TILE * HEAD_DIM, TILE)


def _sample_mixer_body(ret_cd, pr_ref, pg_ref, pab_ref, cos_ref, sin_ref, rsc_ref, dmat_ref, cw_ref,
                       alog_ref, dtb_ref, gnw_ref, convpad_ref, sr_in, sg_in,
                       mix_ref, sr_out, sg_out,
                       qd_s, wk_s, qg_s, wv_s, cd_s, inter_s, u_s, qs_s):
    heads = range(N_HEADS)
    tok = _tokenwise("sample", pr_ref[...], pg_ref[...], cos_ref[...], sin_ref[...], rsc_ref[...],
                     cw_ref, convpad_ref[...])
    tok["kdT"] = [t.T for t in tok["kd"]]
    ret, gdn = _chunk_prep("sample", [tok], [pab_ref[...]], dmat_ref, alog_ref[...], dtb_ref[...])
    for h in heads:
        qd_s[h] = ret["qd"][h]
        wk_s[h] = gdn["wk"][h]
        qg_s[h] = gdn["qg"][h]
        wv_s[h] = gdn["wv"][h]
    cd_s[...] = gdn["cdr"][0]

    def per_batch(b, carry):
        rows = pl.ds(pl.multiple_of(b * SAMPLE_GROUP, SAMPLE_GROUP), SAMPLE_GROUP)
        cd_rows = cd_s[rows, :]
        for h in heads:
            inter_s[h, rows, :] = _mm(qd_s[h, rows, :], sr_in[b, h])
            s = sg_in[b, h]
            both = _mm(jnp.concatenate([wk_s[h, rows, :], qg_s[h, rows, :]], 0), s)
            u_s[h, rows, :] = wv_s[h, rows, :] - both[:SAMPLE_GROUP]
            qs_s[h, rows, :] = both[SAMPLE_GROUP:]
            cd = jnp.broadcast_to(cd_rows[SAMPLE_GROUP - 1:SAMPLE_GROUP, h:h + 1], (HEAD_DIM, HEAD_DIM))
            sg_out[b, h] = cd * s
        return carry

    lax.fori_loop(0, SAMPLE_PER_TILE, per_batch, 0)

    out_rows = SAMPLE_PER_TILE * SAMPLE_TOKENS
    out_row = _iota2((out_rows, TILE), 0)
    token_row = ((out_row >> 2) << 3) + (SAMPLE_GROUP - SAMPLE_TOKENS) + (out_row & (SAMPLE_TOKENS - 1))
    select = (_iota2((out_rows, TILE), 1) == token_row).astype(BF16)
    compact = lambda t: jnp.dot(select, t, preferred_element_type=F32).astype(BF16)

    state_shape = (SAMPLE_PER_TILE, HEAD_DIM, HEAD_DIM)
    for h in heads:
        o = _mm(ret["scores"][h], ret["v"][h]) + inter_s[h]
        upd = _mm(_stack_by_group(ret["kdT"][h]), ret["v"][h]).reshape(state_shape)
        sr_out[:, h] = ret_cd[h] * sr_in[:, h] + upd
        mix_ref[:, h * HEAD_DIM:(h + 1) * HEAD_DIM] = compact(_rms_gate(o, ret["gate"][h]))

    for h in heads:
        u = u_s[h]
        o = qs_s[h] + _mm(gdn["attn"][h], u)
        upd = _mm(_stack_by_group(gdn["ktT"][h]), u).reshape(state_shape)
        sg_out[:, h] = sg_out[:, h] + upd
        mix_ref[:, GROUP_W + h * HEAD_DIM:GROUP_W + (h + 1) * HEAD_DIM] = compact(_rms_gate(
            o, gdn["gate"][h], gnw_ref[...]))


def _sample_mixer(ret_cd, pr, pg, pab, cosf, sins, rsc, dmat, cw, alog, dtb, gnw, convpad, sr, sg):
    n_tiles = pr.shape[0] // TILE
    tile = lambda i: (i, 0)
    c2 = lambda i: (0, 0)
    c3 = lambda i: (0, 0, 0)
    state = pl.BlockSpec((SAMPLE_PER_TILE,) + _STATE_SHAPE, lambda i: (i, 0, 0, 0))
    head_scratch = pltpu.VMEM((N_HEADS, TILE, HEAD_DIM), F32)
    return pl.pallas_call(
        functools.partial(_sample_mixer_body, ret_cd),
        grid=(n_tiles,),
        in_specs=[
            pl.BlockSpec((TILE, 4 * GROUP_W), tile),
            pl.BlockSpec((TILE, 4 * GROUP_W), tile),
            pl.BlockSpec((TILE, HEAD_DIM), tile),
            pl.BlockSpec((TILE, HEAD_DIM), c2),
            pl.BlockSpec((TILE, HEAD_DIM), c2),
            pl.BlockSpec((TILE, HEAD_DIM), c2),
            pl.BlockSpec((N_HEADS, TILE, TILE), c3),
            pl.BlockSpec((CONV_W, CONV_COLS), c2),
            pl.BlockSpec((1, HEAD_DIM), c2),
            pl.BlockSpec((1, HEAD_DIM), c2),
            pl.BlockSpec((1, HEAD_DIM), c2),
            pl.BlockSpec((TILE, CONV_COLS), tile),
            state, state,
        ],
        out_specs=[pl.BlockSpec((SAMPLE_PER_TILE * SAMPLE_TOKENS, 2 * GROUP_W), tile), state, state],
        out_shape=[
            jax.ShapeDtypeStruct((n_tiles * SAMPLE_PER_TILE * SAMPLE_TOKENS, 2 * GROUP_W), BF16),
            jax.ShapeDtypeStruct(sr.shape, F32),
            jax.ShapeDtypeStruct(sg.shape, F32),
        ],
        scratch_shapes=[head_scratch] * 4 + [pltpu.VMEM((TILE, HEAD_DIM), F32)] + [head_scratch] * 3,
        compiler_params=_cparams(("parallel",)),
        name="mixer_sample",
    )(pr, pg, pab, cosf, sins, rsc, dmat, cw, alog, dtb, gnw, convpad, sr, sg)


def _rotary_tables(pos):
    half = HEAD_DIM // 2
    inv = ROPE_BASE ** (-np.arange(half, dtype=np.float64) / half)
    ang = np.asarray(pos, np.float64)[:, None] * inv[None, :]
    cos, sin = np.cos(ang), np.sin(ang)
    return (jnp.asarray(np.concatenate([cos, cos], -1), F32),
            jnp.asarray(np.concatenate([-sin, sin], -1), F32))


def _retention_tables(seg, pos, valid, seg_len):
    gamma = 1.0 - 2.0 ** (-5.0 - np.arange(N_HEADS, dtype=np.float64))
    posf = np.asarray(pos, np.float64)
    rel = posf[:, None] - posf[None, :]
    causal = (seg[:, None] == seg[None, :]) & (rel >= 0)
    dmat = np.where(causal[None], gamma[:, None, None] ** np.where(causal, rel, 0.0)[None], 0.0)
    q_scale = gamma[None, :] ** (posf[:, None] + 1.0)
    k_scale = np.where(valid[:, None], gamma[None, :] ** (seg_len - 1.0 - posf[:, None]), 0.0)
    rsc = np.concatenate([q_scale, k_scale, np.zeros((TILE, HEAD_DIM - 2 * N_HEADS))], -1)
    chunk_decay = tuple(float(g ** seg_len) for g in gamma)
    return jnp.asarray(dmat, F32), jnp.asarray(rsc, F32), chunk_decay


def _pad_lanes(v):
    return jnp.pad(v.astype(F32), (0, HEAD_DIM - v.shape[0]))[None, :]


def kernel(x_prompt, x_sample, state_ret, state_gdn, state_conv, meta_tokens, emb_ln_g, emb_ln_b,
           w_in, conv_w, a_log, dt_bias, gdn_norm_w, w_out, ln1_g, ln1_b, w_gate_up, w_down,
           ln2_g, ln2_b):
    n_batch, seq, _ = x_prompt.shape
    dec_batch, dec_seq, _ = x_sample.shape
    assert seq % TILE == 0 and dec_seq == SAMPLE_TOKENS and N_META <= TILE
    assert n_batch % MIXER_SEQS_PER_STEP == 0
    layer = 0

    w_all = _weight_prep(jnp.swapaxes(w_in[layer], 0, 1))
    w_out_b = w_out[layer].astype(BF16)
    w_gu_b = w_gate_up[layer].astype(BF16)
    w_down_b = w_down[layer].astype(BF16)
    row = lambda v: v.astype(F32)[None, :]
    eg, eb = row(emb_ln_g), row(emb_ln_b)
    cw = conv_w[layer].astype(F32)
    alog, dtb, gnw = _pad_lanes(a_log[layer]), _pad_lanes(dt_bias[layer]), row(gdn_norm_w[layer])

    xp = x_prompt.reshape(n_batch * seq, D_MODEL)
    xm = jnp.pad(meta_tokens.astype(F32), ((0, TILE - N_META), (0, 0)))
    xs = jnp.pad(x_sample, ((0, 0), (SAMPLE_GROUP - dec_seq, 0), (0, 0))).reshape(
        dec_batch * SAMPLE_GROUP, D_MODEL)
    convpad = jnp.pad(state_conv[layer].astype(F32), ((0, 0), (1, SAMPLE_GROUP - CONV_W), (0, 0))).reshape(
        dec_batch * SAMPLE_GROUP, CONV_COLS)

    tile_idx = np.arange(TILE)
    cos_p, sin_p = _rotary_tables(N_META + np.arange(seq))
    cos_m, sin_m = _rotary_tables(tile_idx)
    tok = (tile_idx % SAMPLE_GROUP) - (SAMPLE_GROUP - dec_seq)
    cos_s, sin_s = _rotary_tables(PAST_LEN + np.maximum(tok, 0))
    zeros_i = np.zeros((TILE,), np.int32)
    all_valid = np.ones((TILE,), bool)
    dmat_p, rsc_p, cd_p = _retention_tables(zeros_i, tile_idx, all_valid, float(TILE))
    dmat_m, rsc_m, cd_m = _retention_tables(zeros_i, tile_idx, tile_idx < N_META, float(N_META))
    dmat_s, rsc_s, cd_s = _retention_tables(tile_idx // SAMPLE_GROUP, tok, tok >= 0, float(dec_seq))

    zero_state = jnp.zeros(_STATE_SHAPE, F32)
    pr_m, pg_m, pab_m = _front(xm, eg, eb, w_all, TILE)
    _, sr_m, sg_m = _meta_mixer(cd_m, pr_m, pg_m, pab_m, cos_m, sin_m, rsc_m, dmat_m, cw, alog, dtb, gnw,
                                zero_state, zero_state)
    halo_p = pg_m[N_META - HALO:N_META, :CONV_COLS]

    mix_p, sr_p, sg_p, tail_p = _prompt_fused(cd_p, x_prompt, eg, eb, w_all, cos_p, sin_p, rsc_p, cw, halo_p,
                                              dmat_p, alog, dtb, gnw, sr_m[0], sg_m[0], MIXER_SEQS_PER_STEP)
    mix_p = mix_p.reshape(n_batch * seq, 2 * GROUP_W)

    pr_s, pg_s, pab_s = _front(xs, eg, eb, w_all, 512)
    mix_s, sr_s, sg_s = _sample_mixer(cd_s, pr_s, pg_s, pab_s, cos_s, sin_s, rsc_s, dmat_s, cw, alog,
                                      dtb, gnw, convpad, state_ret[layer].astype(F32),
                                      state_gdn[layer].astype(F32))

    y_p, y_s = _back(xp, mix_p, x_sample.reshape(dec_batch * dec_seq, D_MODEL), mix_s, eg, eb, w_out_b,
                     row(ln1_g[layer]), row(ln1_b[layer]), w_gu_b, w_down_b, row(ln2_g[layer]),
                     row(ln2_b[layer]), dec_batch * dec_seq)
    y_p = y_p.reshape(n_batch, seq, D_MODEL)
    y_s = y_s.reshape(dec_batch, dec_seq, D_MODEL)

    conv_p = tail_p[:, HALO - (CONV_W - 1):, :]
    conv_s = pg_s.reshape(dec_batch, SAMPLE_GROUP, 4 * GROUP_W)[:, SAMPLE_GROUP - (CONV_W - 1):, :CONV_COLS]
    return (y_p, y_s, sr_p[None], sg_p[None], conv_p[None], sr_s[None], sg_s[None], conv_s[None])
```

```python
import functools

import numpy as np
import jax
import jax.numpy as jnp
from jax import lax
from jax.experimental import pallas as pl
from jax.experimental.pallas import tpu as pltpu

F32 = jnp.float32
BF16 = jnp.bfloat16

D_MODEL = 1024
N_META = 16
N_HEADS = 4
HEAD_DIM = 128
GROUP_W = N_HEADS * HEAD_DIM
CONV_W = 4
CONV_COLS = 3 * GROUP_W
D_FF = 2816
PAST_LEN = 16384
ROPE_BASE = 10000.0
LN_EPS = 1e-5
RMS_EPS = 1e-6
ALPHA = 2.0 ** 0.25
TILE = 128
SAMPLE_GROUP = 8
SAMPLE_TOKENS = 4
HALO = 8
MIXER_SEQS_PER_STEP = 4
BACK_PARTS = 2
W_ALL_COLS = 9 * GROUP_W
AB_ROWS = 2 * N_HEADS
VMEM_LIMIT = 56 * 1024 * 1024


def _cparams(sem):
    return pltpu.CompilerParams(dimension_semantics=sem, vmem_limit_bytes=VMEM_LIMIT)


def _layer_norm(x, g, b):
    mu = jnp.mean(x, -1, keepdims=True)
    xc = x - mu
    var = jnp.mean(xc * xc, -1, keepdims=True)
    return xc * lax.rsqrt(var + LN_EPS) * g + b


def _sigmoid(x):
    return 1.0 / (1.0 + jnp.exp(-x))


def _silu(x):
    return x * _sigmoid(x)


def _softplus(x):
    return jnp.maximum(x, 0.0) + jnp.log(1.0 + jnp.exp(-jnp.abs(x)))


def _mm(a, b):
    return jnp.dot(a.astype(BF16), b.astype(BF16), preferred_element_type=F32)


def _mm_nt(a, b):
    return lax.dot_general(a.astype(BF16), b.astype(BF16), (((1,), (1,)), ((), ())),
                           preferred_element_type=F32)


def _mm_each(xs, ys):
    return [_mm(x, y) for x, y in zip(xs, ys)]


def _mm_nt_each(xs, ys):
    return [_mm_nt(x, y) for x, y in zip(xs, ys)]


def _mm_split3(m01, x):
    x1 = x.astype(BF16)
    r1 = x - x1.astype(F32)
    x2 = r1.astype(BF16)
    x3 = (r1 - x2.astype(F32)).astype(BF16)
    dot = functools.partial(jnp.dot, preferred_element_type=F32)
    return (dot(m01, x3) + dot(m01, x2)) + dot(m01, x1)


def _lane_bcast(x, lane):
    return jnp.broadcast_to(x[:, lane:lane + 1], (x.shape[0], HEAD_DIM))


def _head_cols(x, base, h):
    return x[:, base + h * HEAD_DIM:base + (h + 1) * HEAD_DIM]


def _weight_prep_body(q_ref, k_ref, v_ref, g_ref, ab_ref, o_ref):
    j = pl.program_id(0)

    @pl.when(j < 2 * N_HEADS)
    def _():
        for piece, ref in enumerate((q_ref, k_ref, v_ref, g_ref)):
            o_ref[:, piece * HEAD_DIM:(piece + 1) * HEAD_DIM] = ref[...].T.astype(BF16)

    @pl.when(j == 2 * N_HEADS)
    def _():
        ab = jnp.concatenate([ab_ref[...], jnp.zeros((HEAD_DIM - AB_ROWS, D_MODEL), F32)], 0)
        o_ref[:, :HEAD_DIM] = ab.T.astype(BF16)
        o_ref[:, HEAD_DIM:] = jnp.zeros((D_MODEL, GROUP_W - HEAD_DIM), BF16)


def _weight_prep(w_in_t):
    assert w_in_t.shape == (8 * GROUP_W + AB_ROWS, D_MODEL)

    def piece_spec(piece):
        def index(j):
            ret_block = piece * N_HEADS + j
            gdn_block = N_HEADS * j + piece
            return (jnp.where(j < N_HEADS, ret_block, jnp.where(j < 2 * N_HEADS, gdn_block, 0)), 0)
        return pl.BlockSpec((HEAD_DIM, D_MODEL), index)

    ab_spec = pl.BlockSpec((AB_ROWS, D_MODEL), lambda j: (8 * GROUP_W // AB_ROWS, 0))
    return pl.pallas_call(
        _weight_prep_body,
        grid=(2 * N_HEADS + 1,),
        in_specs=[piece_spec(p) for p in range(4)] + [ab_spec],
        out_specs=pl.BlockSpec((D_MODEL, GROUP_W), lambda j: (0, j)),
        out_shape=jax.ShapeDtypeStruct((D_MODEL, W_ALL_COLS), BF16),
        compiler_params=_cparams(("arbitrary",)),
        name="weight_prep",
    )(w_in_t, w_in_t, w_in_t, w_in_t, w_in_t)


def _back_body(x_ref, mix_ref, eg_ref, eb_ref, wo_ref, g1_ref, b1_ref, wgu_ref, wd_ref,
               g2_ref, b2_ref, y_ref):
    part = x_ref.shape[0] // BACK_PARTS
    parts = [slice(i * part, (i + 1) * part) for i in range(BACK_PARTS)]
    dot = functools.partial(jnp.dot, preferred_element_type=F32)
    h = [_layer_norm(x_ref[p, :], eg_ref[...], eb_ref[...]) for p in parts]
    mp = [dot(mix_ref[p, :], wo_ref[...]) for p in parts]
    h1 = [_layer_norm(ALPHA * a + b, g1_ref[...], b1_ref[...]) for a, b in zip(h, mp)]
    gu = [dot(a.astype(BF16), wgu_ref[...]) for a in h1]
    act = [(_silu(a[:, :D_FF]) * a[:, D_FF:]).astype(BF16) for a in gu]
    ff = [dot(a, wd_ref[...]) for a in act]
    for p, a, b in zip(parts, h1, ff):
        y_ref[p, :] = _layer_norm(ALPHA * a + b, g2_ref[...], b2_ref[...])


def _back(x2d, mix, eg, eb, w_out, g1, b1, w_gu, w_down, g2, b2, tm):
    rows = x2d.shape[0]
    const = lambda i: (0, 0)
    tile = lambda i: (i, 0)
    single = pl.Buffered(1)
    vec = pl.BlockSpec((1, D_MODEL), const)
    return pl.pallas_call(
        _back_body,
        grid=(rows // tm,),
        in_specs=[
            pl.BlockSpec((tm, D_MODEL), tile),
            pl.BlockSpec((tm, D_MODEL), tile),
            vec, vec,
            pl.BlockSpec((D_MODEL, D_MODEL), const, pipeline_mode=single),
            vec, vec,
            pl.BlockSpec((D_MODEL, 2 * D_FF), const, pipeline_mode=single),
            pl.BlockSpec((D_FF, D_MODEL), const, pipeline_mode=single),
            vec, vec,
        ],
        out_specs=pl.BlockSpec((tm, D_MODEL), tile),
        out_shape=jax.ShapeDtypeStruct((rows, D_MODEL), F32),
        compiler_params=_cparams(("parallel",)),
        name="back",
    )(x2d, mix, eg, eb, w_out, g1, b1, w_gu, w_down, g2, b2)


def _iota2(shape, dim):
    return lax.broadcasted_iota(jnp.int32, shape, dim)


def _valid_rows(variant, shape):
    r = _iota2(shape, 0)
    if variant == "meta":
        return r < N_META
    if variant == "sample":
        return (r & (SAMPLE_GROUP - 1)) >= SAMPLE_GROUP - SAMPLE_TOKENS
    return None


def _mask_rows(valid, x):
    return x if valid is None else jnp.where(valid, x, 0.0)


def _tri_inverse(variant, nmats, row, col, between=None):
    between = between or (lambda: None)
    eye = (row == col).astype(F32)
    if variant == "sample":
        n2 = _mm_each(nmats, nmats)
        ps = [eye + n for n in nmats]
        return [p + t for p, t in zip(ps, _mm_each(ps, n2))]
    base_log2 = 3
    in_block = (row >> base_log2) == (col >> base_log2)
    ds = [jnp.where(in_block, n, 0.0) for n in nmats]
    ps = [eye + d for d in ds]
    for _ in range(base_log2 - 1):
        ds = _mm_each(ds, ds)
        between()
        ps = [p + t for p, t in zip(ps, _mm_each(ps, ds))]
        between()
    live_rows = N_META if variant == "meta" else TILE
    s = base_log2
    while (1 << s) < live_rows:
        size = 1 << s
        lower_left = (((row >> (s + 1)) == (col >> (s + 1))) & (((row >> s) & 1) == 1)
                      & (((col >> s) & 1) == 0))
        cs = [jnp.where(lower_left, n, 0.0) for n in nmats]
        lower = [slice(start + size, start + 2 * size) for start in range(0, TILE, 2 * size)]
        upper = [slice(start, start + size) for start in range(0, TILE, 2 * size)]
        p_low = [jnp.concatenate([p[rows, :] for rows in lower], 0) for p in ps]
        half = _mm_each(p_low, cs)
        between()
        fix = _mm_each(half, ps)
        between()
        merged = []
        for p, f in zip(ps, fix):
            pieces = []
            for i, (up, lo) in enumerate(zip(upper, lower)):
                pieces += [p[up, :], p[lo, :] + f[i * size:(i + 1) * size, :]]
            merged.append(jnp.concatenate(pieces, 0))
        ps = merged
        s += 1
    return ps


def _rotary(t, cosf, sins):
    return t * cosf + pltpu.roll(t, HEAD_DIM // 2, 1) * sins


def _ret_head_ops(valid, q, k, v, gate, cosf, sins, q_scale, k_scale):
    rq = _mask_rows(valid, _rotary(q, cosf, sins))
    rk = _mask_rows(valid, _rotary(k, cosf, sins) * (HEAD_DIM ** -0.5))
    return dict(rq=rq, rk=rk, rv=_mask_rows(valid, v), qd=rq * q_scale, kd=rk * k_scale, sgr=_silu(gate))


def _short_conv(x, halo, cw_ref, col0):
    rows, ncols = x.shape
    cols = slice(col0, col0 + ncols)
    n = rows // HALO
    full = jnp.concatenate([halo, x], 0).reshape(n + 1, HALO, ncols)
    sub = lax.broadcasted_iota(jnp.int32, (n, HALO, ncols), 1)
    conv = full[1:] * cw_ref[CONV_W - 1:CONV_W, cols]
    for shift in range(1, CONV_W):
        rolled = pltpu.roll(full, shift, 1)
        shifted = jnp.where(sub < shift, rolled[:-1], rolled[1:])
        conv = conv + shifted * cw_ref[CONV_W - 1 - shift:CONV_W - shift, cols]
    return _silu(conv).reshape(rows, ncols)


def _l2_normalize(t):
    return t * lax.rsqrt(jnp.sum(t * t, -1, keepdims=True) + RMS_EPS)


def _tokenwise(variant, pret, pgdn, cosf, sins, rsc, cw_ref, convpad):
    rows = pret.shape[0]
    heads = range(N_HEADS)
    valid = _valid_rows(variant, (rows, HEAD_DIM))
    ret = [_ret_head_ops(valid, *(_head_cols(pret, h * GROUP_W, g) for g in range(4)), cosf, sins,
                         _lane_bcast(rsc, h), _lane_bcast(rsc, N_HEADS + h)) for h in heads]
    x = pgdn[:, :CONV_COLS]
    if variant == "sample":
        x = jnp.where(_valid_rows(variant, x.shape), x, convpad)
    conv = _short_conv(x, jnp.zeros((HALO, CONV_COLS), F32), cw_ref, 0)
    gq = [_mask_rows(valid, _l2_normalize(_head_cols(conv, 0, h)) * (HEAD_DIM ** -0.5)) for h in heads]
    gk = [_mask_rows(valid, _l2_normalize(_head_cols(conv, GROUP_W, h))) for h in heads]
    gv = [_mask_rows(valid, _head_cols(conv, 2 * GROUP_W, h)) for h in heads]
    sgz = [_silu(_head_cols(pgdn, 3 * GROUP_W, h)) for h in heads]
    tok = {key: [r[key] for r in ret] for key in ret[0]}
    tok.update(gq=gq, gk=gk, gv=gv, sgz=sgz)
    return tok


def _chunk_prep(variant, toks, pabs, dmat_ref, alog, dtb, between=None):
    shape = (TILE, HEAD_DIM)
    tiles = range(len(toks))
    items = [(j, h) for j in tiles for h in range(N_HEADS)]
    pick = lambda key: [toks[j][key][h] for j, h in items]
    row = _iota2(shape, 0)
    col = _iota2(shape, 1)
    valid = _valid_rows(variant, shape)
    if variant == "sample":
        same = (row >> 3) == (col >> 3)
        incl = same & (row >= col)
        strict = same & (row > col)
    else:
        incl = row >= col
        strict = row > col

    incl01 = incl.astype(BF16)
    beta_all, gcum, gam, ktail, cdr, gcum_t = [], [], [], [], [], []
    for j in tiles:
        g_all = _mask_rows(valid, -jnp.exp(alog) * _softplus(pabs[j] + dtb))
        beta_all.append(_mask_rows(valid, _sigmoid(pabs[j])))
        if variant == "sample":
            both = _mm_split3(jnp.concatenate([incl01, same.astype(BF16)], 0), g_all)
            gc, gseg = both[:TILE], both[TILE:]
        else:
            gc = _mm_split3(incl01, g_all)
            gseg = jnp.broadcast_to(gc[TILE - 1:TILE, :], shape)
        gcum.append(gc)
        gam.append(jnp.exp(gc))
        ktail.append(jnp.exp(gseg - gc))
        cdr.append(jnp.exp(gseg))
        gcum_t.append(gc.T)

    gq, gk, gv = pick("gq"), pick("gk"), pick("gv")
    n_items = range(len(items))
    dec = [jnp.where(incl, jnp.exp(jnp.minimum(
        _lane_bcast(gcum[j], h) - jnp.broadcast_to(gcum_t[j][h:h + 1, :], shape), 0.0)), 0.0)
        for j, h in items]
    bcol = [_lane_bcast(beta_all[j], N_HEADS + h) for j, h in items]
    gamc = [_lane_bcast(gam[j], h) for j, h in items]
    kk = _mm_nt_each(gk, gk)
    qk = _mm_nt_each(gq, gk)
    nmats = [-(jnp.where(strict, kk[i] * dec[i], 0.0) * bcol[i]) for i in n_items]
    tinv = _tri_inverse(variant, nmats, row, col, between)
    rhs = [jnp.concatenate([gv[i] * bcol[i], gk[i] * (bcol[i] * gamc[i])], 1) for i in n_items]
    sol = _mm_each(tinv, rhs)
    while between is not None and between():
        pass
    ret = dict(
        items=items,
        scores=[qk_r * dmat_ref[h] for qk_r, (_, h) in zip(
            _mm_nt_each(pick("rq"), pick("rk")), items)],
        qd=pick("qd"), kdT=pick("kdT"), v=pick("rv"), gate=pick("sgr"),
    )
    gdn = dict(
        items=items,
        wv=[t[:, :HEAD_DIM] for t in sol],
        wk=[t[:, HEAD_DIM:] for t in sol],
        attn=[qk[i] * dec[i] for i in n_items],
        qg=[gq[i] * gamc[i] for i in n_items],
        ktT=[(gk[i] * _lane_bcast(ktail[j], h)).T for i, (j, h) in enumerate(items)],
        cdr=cdr,
        gate=pick("sgz"),
    )
    return ret, gdn


def _rms_gate(o, gate, weight=None):
    o = o * lax.rsqrt(jnp.mean(o * o, -1, keepdims=True) + RMS_EPS)
    if weight is not None:
        o = o * weight
    return (o * gate).astype(BF16)


_STATE_SHAPE = (N_HEADS, HEAD_DIM, HEAD_DIM)


def _seq_init(s0r_ref, s0g_ref, sr_ref, sg_ref):
    @pl.when(pl.program_id(1) == 0)
    def _():
        for j in range(sr_ref.shape[0]):
            sr_ref[j] = s0r_ref[...]
            sg_ref[j] = s0g_ref[...]


def _seq_step(ret_cd, ret, gdn, gnw, mix_ref, sr_out, sg_out, sr_ref, sg_ref):
    items = ret["items"]
    idx = range(len(items))
    s_r = [sr_ref[j, h] for j, h in items]
    s_g = [sg_ref[j, h] for j, h in items]
    bf = lambda t: t.astype(BF16)
    cat = lambda a, b, axis: jnp.concatenate([bf(a), bf(b)], axis)
    wk_s = _mm_each(gdn["wk"], s_g)
    o_r = _mm_each([cat(ret["scores"][i], ret["qd"][i], 1) for i in idx],
                   [cat(ret["v"][i], s_r[i], 0) for i in idx])
    u = [gdn["wv"][i] - wk_s[i] for i in idx]
    o_g = _mm_each([cat(gdn["qg"][i], gdn["attn"][i], 1) for i in idx],
                   [cat(s_g[i], u[i], 0) for i in idx])
    upd_g = _mm_each(gdn["ktT"], u)
    upd_r = _mm_each(ret["kdT"], ret["v"])
    for i, (j, h) in enumerate(items):
        cd = jnp.broadcast_to(gdn["cdr"][j][0:1, h:h + 1], (HEAD_DIM, HEAD_DIM))
        sg_ref[j, h] = cd * s_g[i] + upd_g[i]
        sr_ref[j, h] = ret_cd[h] * s_r[i] + upd_r[i]
    for i, (j, h) in enumerate(items):
        mix_ref[j, :, h * HEAD_DIM:(h + 1) * HEAD_DIM] = _rms_gate(o_r[i], ret["gate"][i])
        mix_ref[j, :, GROUP_W + h * HEAD_DIM:GROUP_W + (h + 1) * HEAD_DIM] = _rms_gate(
            o_g[i], gdn["gate"][i], gnw)

    @pl.when(pl.program_id(1) == pl.num_programs(1) - 1)
    def _():
        sr_out[...] = sr_ref[...]
        sg_out[...] = sg_ref[...]


def _seq_out(n_batch, n_chunks, nb):
    state = pl.BlockSpec((nb,) + _STATE_SHAPE, lambda b, n: (b, 0, 0, 0))
    out_specs = [pl.BlockSpec((nb, TILE, 2 * GROUP_W), lambda b, n: (b, n, 0)), state, state]
    out_shape = [
        jax.ShapeDtypeStruct((n_batch, n_chunks * TILE, 2 * GROUP_W), BF16),
        jax.ShapeDtypeStruct((n_batch,) + _STATE_SHAPE, F32),
        jax.ShapeDtypeStruct((n_batch,) + _STATE_SHAPE, F32),
    ]
    scratch = [pltpu.VMEM((nb,) + _STATE_SHAPE, F32), pltpu.VMEM((nb,) + _STATE_SHAPE, F32)]
    return out_specs, out_shape, scratch


def _project_tile(x_ref, g_ref, b_ref, w_ref):
    hidden = _layer_norm(x_ref[...], g_ref[...], b_ref[...]).astype(BF16)
    project = lambda lo, hi: jnp.dot(hidden, w_ref[:, lo:hi], preferred_element_type=F32)
    return (project(0, 4 * GROUP_W), project(4 * GROUP_W, 8 * GROUP_W),
            project(8 * GROUP_W, 8 * GROUP_W + HEAD_DIM))


def _meta_mixer_body(ret_cd, x_ref, eg_ref, eb_ref, w_ref, cos_ref, sin_ref, rsc_ref, dmat_ref,
                     cw_ref, alog_ref, dtb_ref, gnw_ref, s0r_ref, s0g_ref,
                     mix_ref, sr_out, sg_out, preconv_ref, sr_ref, sg_ref):
    _seq_init(s0r_ref, s0g_ref, sr_ref, sg_ref)
    pret, pgdn, pab = _project_tile(x_ref, eg_ref, eb_ref, w_ref)
    preconv_ref[...] = pgdn[:, :CONV_COLS]
    tok = _tokenwise("meta", pret, pgdn, cos_ref[...], sin_ref[...], rsc_ref[...], cw_ref, None)
    tok["kdT"] = [t.T for t in tok["kd"]]
    ret, gdn = _chunk_prep("meta", [tok], [pab], dmat_ref, alog_ref[...], dtb_ref[...])
    _seq_step(ret_cd, ret, gdn, gnw_ref[...], mix_ref, sr_out, sg_out, sr_ref, sg_ref)


def _meta_mixer(ret_cd, x, ln_g, ln_b, w_all, cosf, sins, rsc, dmat, cw, alog, dtb, gnw, s0r, s0g):
    c2 = lambda b, n: (0, 0)
    c3 = lambda b, n: (0, 0, 0)
    out_specs, out_shape, state_scratch = _seq_out(1, 1, 1)
    return pl.pallas_call(
        functools.partial(_meta_mixer_body, ret_cd),
        grid=(1, 1),
        in_specs=[
            pl.BlockSpec((TILE, D_MODEL), c2),
            pl.BlockSpec((1, D_MODEL), c2),
            pl.BlockSpec((1, D_MODEL), c2),
            pl.BlockSpec((D_MODEL, W_ALL_COLS), c2),
            pl.BlockSpec((TILE, HEAD_DIM), c2),
            pl.BlockSpec((TILE, HEAD_DIM), c2),
            pl.BlockSpec((TILE, HEAD_DIM), c2),
            pl.BlockSpec((N_HEADS, TILE, TILE), c3),
            pl.BlockSpec((CONV_W, CONV_COLS), c2),
            pl.BlockSpec((1, HEAD_DIM), c2),
            pl.BlockSpec((1, HEAD_DIM), c2),
            pl.BlockSpec((1, HEAD_DIM), c2),
            pl.BlockSpec(_STATE_SHAPE, c3),
            pl.BlockSpec(_STATE_SHAPE, c3),
        ],
        out_specs=out_specs + [pl.BlockSpec((TILE, CONV_COLS), c2)],
        out_shape=out_shape + [jax.ShapeDtypeStruct((TILE, CONV_COLS), F32)],
        scratch_shapes=state_scratch,
        compiler_params=_cparams(("parallel", "arbitrary")),
        name="mixer_meta",
    )(x, ln_g, ln_b, w_all, cosf, sins, rsc, dmat, cw, alog, dtb, gnw, s0r, s0g)


def _prompt_fused_body(ret_cd, x_ref, g_ref, b_ref, w_ref, cos_ref, sin_ref, rsc_ref, cw_ref, halo0_ref,
                       dmat_ref, alog_ref, dtb_ref, gnw_ref, s0r_ref, s0g_ref,
                       mix_ref, sr_out, sg_out, tail_ref, sr_ref, sg_ref, halo_ref):
    nb = x_ref.shape[0]
    seqs, heads = range(nb), range(N_HEADS)
    _seq_init(s0r_ref, s0g_ref, sr_ref, sg_ref)

    @pl.when(pl.program_id(1) == 0)
    def _():
        for j in seqs:
            halo_ref[j] = halo0_ref[...]

    hidden = _layer_norm(x_ref[...].reshape(nb * TILE, D_MODEL), g_ref[...], b_ref[...]).astype(BF16)
    cosf, sins, rsc = cos_ref[...], sin_ref[...], rsc_ref[...]
    keys = ("rq", "rk", "rv", "qd", "kdT", "sgr", "gq", "gk", "gv", "sgz")
    toks = [{key: [None] * N_HEADS for key in keys} for _ in seqs]
    pabs = [None] * nb
    rows = lambda j: slice(j * TILE, (j + 1) * TILE)

    def project(group, width=GROUP_W):
        return jnp.dot(hidden, w_ref[:, group * GROUP_W:group * GROUP_W + width], preferred_element_type=F32)

    def conv_group(group, key):
        val = project(N_HEADS + group)
        cols = slice(group * GROUP_W, (group + 1) * GROUP_W)
        for j in seqs:
            part = val[rows(j), :]
            conv = _short_conv(part, halo_ref[j, :, cols], cw_ref, group * GROUP_W)
            halo_ref[j, :, cols] = part[TILE - HALO:, :]
            for h in heads:
                t = _head_cols(conv, 0, h)
                if key == "gq":
                    t = _l2_normalize(t) * (HEAD_DIM ** -0.5)
                elif key == "gk":
                    t = _l2_normalize(t)
                toks[j][key][h] = t

    def ab_group():
        val = project(2 * N_HEADS, HEAD_DIM)
        for j in seqs:
            pabs[j] = val[rows(j), :]

    def ret_head(h):
        val = project(h)
        for j in seqs:
            part = val[rows(j), :]
            ops = _ret_head_ops(None, *(_head_cols(part, 0, g) for g in range(4)), cosf, sins,
                                _lane_bcast(rsc, h), _lane_bcast(rsc, N_HEADS + h))
            for key in ("rq", "rk", "rv", "qd", "sgr"):
                toks[j][key][h] = ops[key]
            toks[j]["kdT"][h] = ops["kd"].T

    def gate_group():
        val = project(2 * N_HEADS - 1)
        for j in seqs:
            for h in heads:
                toks[j]["sgz"][h] = _silu(_head_cols(val[rows(j), :], 0, h))

    ab_group()
    conv_group(1, "gk")
    conv_group(0, "gq")
    conv_group(2, "gv")
    for j in seqs:
        tail_ref[j] = halo_ref[j]

    fillers = [functools.partial(ret_head, h) for h in heads] + [gate_group]

    def between():
        if not fillers:
            return False
        fillers.pop(0)()
        return True

    ret, gdn = _chunk_prep("prompt", toks, pabs, dmat_ref, alog_ref[...], dtb_ref[...], between)
    _seq_step(ret_cd, ret, gdn, gnw_ref[...], mix_ref, sr_out, sg_out, sr_ref, sg_ref)


def _prompt_fused(ret_cd, x, ln_g, ln_b, w_all, cosf, sins, rsc, cw, halo0, dmat, alog, dtb, gnw, s0r, s0g, nb):
    n_batch, seq, _ = x.shape
    n_chunks = seq // TILE
    tile = lambda b, n: (b, n, 0)
    pos = lambda b, n: (n, 0)
    c2 = lambda b, n: (0, 0)
    c3 = lambda b, n: (0, 0, 0)
    out_specs, out_shape, state_scratch = _seq_out(n_batch, n_chunks, nb)
    return pl.pallas_call(
        functools.partial(_prompt_fused_body, ret_cd),
        grid=(n_batch // nb, n_chunks),
        in_specs=[
            pl.BlockSpec((nb, TILE, D_MODEL), tile),
            pl.BlockSpec((1, D_MODEL), c2),
            pl.BlockSpec((1, D_MODEL), c2),
            pl.BlockSpec((D_MODEL, W_ALL_COLS), c2, pipeline_mode=pl.Buffered(1)),
            pl.BlockSpec((TILE, HEAD_DIM), pos),
            pl.BlockSpec((TILE, HEAD_DIM), pos),
            pl.BlockSpec((TILE, HEAD_DIM), c2),
            pl.BlockSpec((CONV_W, CONV_COLS), c2),
            pl.BlockSpec((HALO, CONV_COLS), c2),
            pl.BlockSpec((N_HEADS, TILE, TILE), c3),
            pl.BlockSpec((1, HEAD_DIM), c2),
            pl.BlockSpec((1, HEAD_DIM), c2),
            pl.BlockSpec((1, HEAD_DIM), c2),
            pl.BlockSpec(_STATE_SHAPE, c3),
            pl.BlockSpec(_STATE_SHAPE, c3),
        ],
        out_specs=out_specs + [pl.BlockSpec((nb, HALO, CONV_COLS), lambda b, n: (b, 0, 0))],
        out_shape=out_shape + [jax.ShapeDtypeStruct((n_batch, HALO, CONV_COLS), F32)],
        scratch_shapes=state_scratch + [pltpu.VMEM((nb, HALO, CONV_COLS), F32)],
        compiler_params=_cparams(("parallel", "arbitrary")),
        name="prompt_fused",
    )(x, ln_g, ln_b, w_all, cosf, sins, rsc, cw, halo0, dmat, alog, dtb, gnw, s0r, s0g)


SAMPLE_PER_TILE = TILE // SAMPLE_GROUP


def _stack_by_group(xt):
    shape3 = (SAMPLE_PER_TILE, HEAD_DIM, TILE)
    keep = lax.broadcasted_iota(jnp.int32, shape3, 0) == (lax.broadcasted_iota(jnp.int32, shape3, 2) >> 3)
    stacked = jnp.where(keep, jnp.broadcast_to(xt[None], shape3), 0.0)
    return stacked.reshape(SAMPLE_PER_TILE * HEAD_DIM, TILE)


def _sample_mixer_body(ret_cd, x_ref, eg_ref, eb_ref, w_ref, cos_ref, sin_ref, rsc_ref, dmat_ref, cw_ref,
                       alog_ref, dtb_ref, gnw_ref, convpad_ref, sr_in, sg_in,
                       mix_ref, sr_out, sg_out, preconv_ref,
                       qd_s, wk_s, qg_s, wv_s, cd_s, inter_s, u_s, qs_s):
    heads = range(N_HEADS)
    pret, pgdn, pab = _project_tile(x_ref, eg_ref, eb_ref, w_ref)
    preconv_ref[...] = pgdn[:, :CONV_COLS]
    tok = _tokenwise("sample", pret, pgdn, cos_ref[...], sin_ref[...], rsc_ref[...],
                     cw_ref, convpad_ref[...])
    tok["kdT"] = [t.T for t in tok["kd"]]
    ret, gdn = _chunk_prep("sample", [tok], [pab], dmat_ref, alog_ref[...], dtb_ref[...])
    for h in heads:
        qd_s[h] = ret["qd"][h]
        wk_s[h] = gdn["wk"][h]
        qg_s[h] = gdn["qg"][h]
        wv_s[h] = gdn["wv"][h]
    cd_s[...] = gdn["cdr"][0]

    def per_batch(b, carry):
        rows = pl.ds(pl.multiple_of(b * SAMPLE_GROUP, SAMPLE_GROUP), SAMPLE_GROUP)
        cd_rows = cd_s[rows, :]
        for h in heads:
            inter_s[h, rows, :] = _mm(qd_s[h, rows, :], sr_in[b, h])
            s = sg_in[b, h]
            both = _mm(jnp.concatenate([wk_s[h, rows, :], qg_s[h, rows, :]], 0), s)
            u_s[h, rows, :] = wv_s[h, rows, :] - both[:SAMPLE_GROUP]
            qs_s[h, rows, :] = both[SAMPLE_GROUP:]
            cd = jnp.broadcast_to(cd_rows[SAMPLE_GROUP - 1:SAMPLE_GROUP, h:h + 1], (HEAD_DIM, HEAD_DIM))
            sg_out[b, h] = cd * s
        return carry

    lax.fori_loop(0, SAMPLE_PER_TILE, per_batch, 0)

    out_rows = SAMPLE_PER_TILE * SAMPLE_TOKENS
    out_row = _iota2((out_rows, TILE), 0)
    token_row = ((out_row >> 2) << 3) + (SAMPLE_GROUP - SAMPLE_TOKENS) + (out_row & (SAMPLE_TOKENS - 1))
    select = (_iota2((out_rows, TILE), 1) == token_row).astype(BF16)
    compact = lambda t: jnp.dot(select, t, preferred_element_type=F32).astype(BF16)

    state_shape = (SAMPLE_PER_TILE, HEAD_DIM, HEAD_DIM)
    for h in heads:
        o = _mm(ret["scores"][h], ret["v"][h]) + inter_s[h]
        upd = _mm(_stack_by_group(ret["kdT"][h]), ret["v"][h]).reshape(state_shape)
        sr_out[:, h] = ret_cd[h] * sr_in[:, h] + upd
        mix_ref[:, h * HEAD_DIM:(h + 1) * HEAD_DIM] = compact(_rms_gate(o, ret["gate"][h]))

    for h in heads:
        u = u_s[h]
        o = qs_s[h] + _mm(gdn["attn"][h], u)
        upd = _mm(_stack_by_group(gdn["ktT"][h]), u).reshape(state_shape)
        sg_out[:, h] = sg_out[:, h] + upd
        mix_ref[:, GROUP_W + h * HEAD_DIM:GROUP_W + (h + 1) * HEAD_DIM] = compact(_rms_gate(
            o, gdn["gate"][h], gnw_ref[...]))


def _sample_mixer(ret_cd, x, ln_g, ln_b, w_all, cosf, sins, rsc, dmat, cw, alog, dtb, gnw, convpad, sr, sg):
    n_tiles = x.shape[0] // TILE
    tile = lambda i: (i, 0)
    c2 = lambda i: (0, 0)
    c3 = lambda i: (0, 0, 0)
    state = pl.BlockSpec((SAMPLE_PER_TILE,) + _STATE_SHAPE, lambda i: (i, 0, 0, 0))
    head_scratch = pltpu.VMEM((N_HEADS, TILE, HEAD_DIM), F32)
    return pl.pallas_call(
        functools.partial(_sample_mixer_body, ret_cd),
        grid=(n_tiles,),
        in_specs=[
            pl.BlockSpec((TILE, D_MODEL), tile),
            pl.BlockSpec((1, D_MODEL), c2),
            pl.BlockSpec((1, D_MODEL), c2),
            pl.BlockSpec((D_MODEL, W_ALL_COLS), c2, pipeline_mode=pl.Buffered(1)),
            pl.BlockSpec((TILE, HEAD_DIM), c2),
            pl.BlockSpec((TILE, HEAD_DIM), c2),
            pl.BlockSpec((TILE, HEAD_DIM), c2),
            pl.BlockSpec((N_HEADS, TILE, TILE), c3),
            pl.BlockSpec((CONV_W, CONV_COLS), c2),
            pl.BlockSpec((1, HEAD_DIM), c2),
            pl.BlockSpec((1, HEAD_DIM), c2),
            pl.BlockSpec((1, HEAD_DIM), c2),
            pl.BlockSpec((TILE, CONV_COLS), tile),
            state, state,
        ],
        out_specs=[pl.BlockSpec((SAMPLE_PER_TILE * SAMPLE_TOKENS, 2 * GROUP_W), tile), state, state,
                   pl.BlockSpec((TILE, CONV_COLS), tile)],
        out_shape=[
            jax.ShapeDtypeStruct((n_tiles * SAMPLE_PER_TILE * SAMPLE_TOKENS, 2 * GROUP_W), BF16),
            jax.ShapeDtypeStruct(sr.shape, F32),
            jax.ShapeDtypeStruct(sg.shape, F32),
            jax.ShapeDtypeStruct((x.shape[0], CONV_COLS), F32),
        ],
        scratch_shapes=[head_scratch] * 4 + [pltpu.VMEM((TILE, HEAD_DIM), F32)] + [head_scratch] * 3,
        compiler_params=_cparams(("parallel",)),
        name="mixer_sample",
    )(x, ln_g, ln_b, w_all, cosf, sins, rsc, dmat, cw, alog, dtb, gnw, convpad, sr, sg)


def _rotary_tables(pos):
    half = HEAD_DIM // 2
    inv = ROPE_BASE ** (-np.arange(half, dtype=np.float64) / half)
    ang = np.asarray(pos, np.float64)[:, None] * inv[None, :]
    cos, sin = np.cos(ang), np.sin(ang)
    return (jnp.asarray(np.concatenate([cos, cos], -1), F32),
            jnp.asarray(np.concatenate([-sin, sin], -1), F32))


def _retention_tables(seg, pos, valid, seg_len):
    gamma = 1.0 - 2.0 ** (-5.0 - np.arange(N_HEADS, dtype=np.float64))
    posf = np.asarray(pos, np.float64)
    rel = posf[:, None] - posf[None, :]
    causal = (seg[:, None] == seg[None, :]) & (rel >= 0)
    dmat = np.where(causal[None], gamma[:, None, None] ** np.where(causal, rel, 0.0)[None], 0.0)
    q_scale = gamma[None, :] ** (posf[:, None] + 1.0)
    k_scale = np.where(valid[:, None], gamma[None, :] ** (seg_len - 1.0 - posf[:, None]), 0.0)
    rsc = np.concatenate([q_scale, k_scale, np.zeros((TILE, HEAD_DIM - 2 * N_HEADS))], -1)
    chunk_decay = tuple(float(g ** seg_len) for g in gamma)
    return jnp.asarray(dmat, F32), jnp.asarray(rsc, F32), chunk_decay


def _pad_lanes(v):
    return jnp.pad(v.astype(F32), (0, HEAD_DIM - v.shape[0]))[None, :]


def kernel(x_prompt, x_sample, state_ret, state_gdn, state_conv, meta_tokens, emb_ln_g, emb_ln_b,
           w_in, conv_w, a_log, dt_bias, gdn_norm_w, w_out, ln1_g, ln1_b, w_gate_up, w_down,
           ln2_g, ln2_b):
    n_batch, seq, _ = x_prompt.shape
    dec_batch, dec_seq, _ = x_sample.shape
    assert seq % TILE == 0 and dec_seq == SAMPLE_TOKENS and N_META <= TILE
    assert n_batch % MIXER_SEQS_PER_STEP == 0
    layer = 0

    w_all = _weight_prep(jnp.swapaxes(w_in[layer], 0, 1))
    w_out_b = w_out[layer].astype(BF16)
    w_gu_b = w_gate_up[layer].astype(BF16)
    w_down_b = w_down[layer].astype(BF16)
    row = lambda v: v.astype(F32)[None, :]
    eg, eb = row(emb_ln_g), row(emb_ln_b)
    cw = conv_w[layer].astype(F32)
    alog, dtb, gnw = _pad_lanes(a_log[layer]), _pad_lanes(dt_bias[layer]), row(gdn_norm_w[layer])

    xp = x_prompt.reshape(n_batch * seq, D_MODEL)
    xm = jnp.pad(meta_tokens.astype(F32), ((0, TILE - N_META), (0, 0)))
    xs = jnp.pad(x_sample, ((0, 0), (SAMPLE_GROUP - dec_seq, 0), (0, 0))).reshape(
        dec_batch * SAMPLE_GROUP, D_MODEL)
    convpad = jnp.pad(state_conv[layer].astype(F32), ((0, 0), (1, SAMPLE_GROUP - CONV_W), (0, 0))).reshape(
        dec_batch * SAMPLE_GROUP, CONV_COLS)

    tile_idx = np.arange(TILE)
    cos_p, sin_p = _rotary_tables(N_META + np.arange(seq))
    cos_m, sin_m = _rotary_tables(tile_idx)
    tok = (tile_idx % SAMPLE_GROUP) - (SAMPLE_GROUP - dec_seq)
    cos_s, sin_s = _rotary_tables(PAST_LEN + np.maximum(tok, 0))
    zeros_i = np.zeros((TILE,), np.int32)
    all_valid = np.ones((TILE,), bool)
    dmat_p, rsc_p, cd_p = _retention_tables(zeros_i, tile_idx, all_valid, float(TILE))
    dmat_m, rsc_m, cd_m = _retention_tables(zeros_i, tile_idx, tile_idx < N_META, float(N_META))
    dmat_s, rsc_s, cd_s = _retention_tables(tile_idx // SAMPLE_GROUP, tok, tok >= 0, float(dec_seq))

    zero_state = jnp.zeros(_STATE_SHAPE, F32)
    _, sr_m, sg_m, preconv_m = _meta_mixer(cd_m, xm, eg, eb, w_all, cos_m, sin_m, rsc_m, dmat_m, cw, alog, dtb,
                                           gnw, zero_state, zero_state)
    halo_p = preconv_m[N_META - HALO:N_META]

    mix_p, sr_p, sg_p, tail_p = _prompt_fused(cd_p, x_prompt, eg, eb, w_all, cos_p, sin_p, rsc_p, cw, halo_p,
                                              dmat_p, alog, dtb, gnw, sr_m[0], sg_m[0], MIXER_SEQS_PER_STEP)
    mix_p = mix_p.reshape(n_batch * seq, 2 * GROUP_W)

    mix_s, sr_s, sg_s, preconv_s = _sample_mixer(cd_s, xs, eg, eb, w_all, cos_s, sin_s, rsc_s, dmat_s, cw,
                                                 alog, dtb, gnw, convpad, state_ret[layer].astype(F32),
                                                 state_gdn[layer].astype(F32))

    back = functools.partial(_back, eg=eg, eb=eb, w_out=w_out_b, g1=row(ln1_g[layer]), b1=row(ln1_b[layer]),
                             w_gu=w_gu_b, w_down=w_down_b, g2=row(ln2_g[layer]), b2=row(ln2_b[layer]),
                             tm=512)
    y_p = back(xp, mix_p).reshape(n_batch, seq, D_MODEL)
    y_s = back(x_sample.reshape(dec_batch * dec_seq, D_MODEL), mix_s).reshape(dec_batch, dec_seq, D_MODEL)

    conv_p = tail_p[:, HALO - (CONV_W - 1):, :]
    conv_s = preconv_s.reshape(dec_batch, SAMPLE_GROUP, CONV_COLS)[:, SAMPLE_GROUP - (CONV_W - 1):, :]
    return (y_p, y_s, sr_p[None], sg_p[None], conv_p[None], sr_s[None], sg_s[None], conv_s[None])
```

```python
import functools

import numpy as np
import jax
import jax.numpy as jnp
from jax import lax
from jax.experimental import pallas as pl
from jax.experimental.pallas import tpu as pltpu

F32 = jnp.float32
BF16 = jnp.bfloat16

D_MODEL = 1024
N_META = 16
N_HEADS = 4
HEAD_DIM = 128
GROUP_W = N_HEADS * HEAD_DIM
CONV_W = 4
CONV_COLS = 3 * GROUP_W
D_FF = 2816
PAST_LEN = 16384
ROPE_BASE = 10000.0
LN_EPS = 1e-5
RMS_EPS = 1e-6
ALPHA = 2.0 ** 0.25
TILE = 128
SAMPLE_GROUP = 8
SAMPLE_TOKENS = 4
HALO = 8
MIXER_SEQS_PER_STEP = 4
BACK_PARTS = 2
W_ALL_COLS = 9 * GROUP_W
AB_ROWS = 2 * N_HEADS
VMEM_LIMIT = 56 * 1024 * 1024


def _cparams(sem):
    return pltpu.CompilerParams(dimension_semantics=sem, vmem_limit_bytes=VMEM_LIMIT)


def _layer_norm(x, g, b):
    mu = jnp.mean(x, -1, keepdims=True)
    xc = x - mu
    var = jnp.mean(xc * xc, -1, keepdims=True)
    return xc * lax.rsqrt(var + LN_EPS) * g + b


def _sigmoid(x):
    return 1.0 / (1.0 + jnp.exp(-x))


def _silu(x):
    return x * _sigmoid(x)


def _softplus(x):
    return jnp.maximum(x, 0.0) + jnp.log(1.0 + jnp.exp(-jnp.abs(x)))


def _mm(a, b):
    return jnp.dot(a.astype(BF16), b.astype(BF16), preferred_element_type=F32)


def _mm_nt(a, b):
    return lax.dot_general(a.astype(BF16), b.astype(BF16), (((1,), (1,)), ((), ())),
                           preferred_element_type=F32)


def _mm_each(xs, ys):
    return [_mm(x, y) for x, y in zip(xs, ys)]


def _mm_nt_each(xs, ys):
    return [_mm_nt(x, y) for x, y in zip(xs, ys)]


def _mm_split3(m01, x):
    x1 = x.astype(BF16)
    r1 = x - x1.astype(F32)
    x2 = r1.astype(BF16)
    x3 = (r1 - x2.astype(F32)).astype(BF16)
    dot = functools.partial(jnp.dot, preferred_element_type=F32)
    return (dot(m01, x3) + dot(m01, x2)) + dot(m01, x1)


def _lane_bcast(x, lane):
    return jnp.broadcast_to(x[:, lane:lane + 1], (x.shape[0], HEAD_DIM))


def _head_cols(x, base, h):
    return x[:, base + h * HEAD_DIM:base + (h + 1) * HEAD_DIM]


def _weight_prep_body(q_ref, k_ref, v_ref, g_ref, ab_ref, o_ref):
    j = pl.program_id(0)

    @pl.when(j < 2 * N_HEADS)
    def _():
        for piece, ref in enumerate((q_ref, k_ref, v_ref, g_ref)):
            o_ref[:, piece * HEAD_DIM:(piece + 1) * HEAD_DIM] = ref[...].T.astype(BF16)

    @pl.when(j == 2 * N_HEADS)
    def _():
        ab = jnp.concatenate([ab_ref[...], jnp.zeros((HEAD_DIM - AB_ROWS, D_MODEL), F32)], 0)
        o_ref[:, :HEAD_DIM] = ab.T.astype(BF16)
        o_ref[:, HEAD_DIM:] = jnp.zeros((D_MODEL, GROUP_W - HEAD_DIM), BF16)


def _weight_prep(w_in_t):
    assert w_in_t.shape == (8 * GROUP_W + AB_ROWS, D_MODEL)

    def piece_spec(piece):
        def index(j):
            ret_block = piece * N_HEADS + j
            gdn_block = N_HEADS * j + piece
            return (jnp.where(j < N_HEADS, ret_block, jnp.where(j < 2 * N_HEADS, gdn_block, 0)), 0)
        return pl.BlockSpec((HEAD_DIM, D_MODEL), index)

    ab_spec = pl.BlockSpec((AB_ROWS, D_MODEL), lambda j: (8 * GROUP_W // AB_ROWS, 0))
    return pl.pallas_call(
        _weight_prep_body,
        grid=(2 * N_HEADS + 1,),
        in_specs=[piece_spec(p) for p in range(4)] + [ab_spec],
        out_specs=pl.BlockSpec((D_MODEL, GROUP_W), lambda j: (0, j)),
        out_shape=jax.ShapeDtypeStruct((D_MODEL, W_ALL_COLS), BF16),
        compiler_params=_cparams(("arbitrary",)),
        name="weight_prep",
    )(w_in_t, w_in_t, w_in_t, w_in_t, w_in_t)


def _back_body(x_ref, mix_ref, eg_ref, eb_ref, wo_ref, g1_ref, b1_ref, wgu_ref, wd_ref,
               g2_ref, b2_ref, y_ref):
    part = x_ref.shape[0] // BACK_PARTS
    parts = [slice(i * part, (i + 1) * part) for i in range(BACK_PARTS)]
    dot = functools.partial(jnp.dot, preferred_element_type=F32)
    h = [_layer_norm(x_ref[p, :], eg_ref[...], eb_ref[...]) for p in parts]
    mp = [dot(mix_ref[p, :], wo_ref[...]) for p in parts]
    h1 = [_layer_norm(ALPHA * a + b, g1_ref[...], b1_ref[...]) for a, b in zip(h, mp)]
    gu = [dot(a.astype(BF16), wgu_ref[...]) for a in h1]
    act = [(_silu(a[:, :D_FF]) * a[:, D_FF:]).astype(BF16) for a in gu]
    ff = [dot(a, wd_ref[...]) for a in act]
    for p, a, b in zip(parts, h1, ff):
        y_ref[p, :] = _layer_norm(ALPHA * a + b, g2_ref[...], b2_ref[...])


def _back(x2d, mix, eg, eb, w_out, g1, b1, w_gu, w_down, g2, b2, tm):
    rows = x2d.shape[0]
    const = lambda i: (0, 0)
    tile = lambda i: (i, 0)
    single = pl.Buffered(1)
    vec = pl.BlockSpec((1, D_MODEL), const)
    return pl.pallas_call(
        _back_body,
        grid=(rows // tm,),
        in_specs=[
            pl.BlockSpec((tm, D_MODEL), tile),
            pl.BlockSpec((tm, D_MODEL), tile),
            vec, vec,
            pl.BlockSpec((D_MODEL, D_MODEL), const, pipeline_mode=single),
            vec, vec,
            pl.BlockSpec((D_MODEL, 2 * D_FF), const, pipeline_mode=single),
            pl.BlockSpec((D_FF, D_MODEL), const, pipeline_mode=single),
            vec, vec,
        ],
        out_specs=pl.BlockSpec((tm, D_MODEL), tile),
        out_shape=jax.ShapeDtypeStruct((rows, D_MODEL), F32),
        compiler_params=_cparams(("parallel",)),
        name="back",
    )(x2d, mix, eg, eb, w_out, g1, b1, w_gu, w_down, g2, b2)


def _iota2(shape, dim):
    return lax.broadcasted_iota(jnp.int32, shape, dim)


def _valid_rows(variant, shape):
    r = _iota2(shape, 0)
    if variant == "meta":
        return r < N_META
    if variant == "sample":
        return (r & (SAMPLE_GROUP - 1)) >= SAMPLE_GROUP - SAMPLE_TOKENS
    return None


def _mask_rows(valid, x):
    return x if valid is None else jnp.where(valid, x, 0.0)


def _tri_inverse(variant, nmats, row, col):
    eye = (row == col).astype(F32)
    if variant == "sample":
        n2 = _mm_each(nmats, nmats)
        ps = [eye + n for n in nmats]
        return [p + t for p, t in zip(ps, _mm_each(ps, n2))]
    base_log2 = 3
    in_block = (row >> base_log2) == (col >> base_log2)
    ds = [jnp.where(in_block, n, 0.0) for n in nmats]
    ps = [eye + d for d in ds]
    for _ in range(base_log2 - 1):
        ds = _mm_each(ds, ds)
        ps = [p + t for p, t in zip(ps, _mm_each(ps, ds))]
    live_rows = N_META if variant == "meta" else TILE
    s = base_log2
    while (1 << s) < live_rows:
        size = 1 << s
        lower_left = (((row >> (s + 1)) == (col >> (s + 1))) & (((row >> s) & 1) == 1)
                      & (((col >> s) & 1) == 0))
        cs = [jnp.where(lower_left, n, 0.0) for n in nmats]
        lower = [slice(start + size, start + 2 * size) for start in range(0, TILE, 2 * size)]
        upper = [slice(start, start + size) for start in range(0, TILE, 2 * size)]
        p_low = [jnp.concatenate([p[rows, :] for rows in lower], 0) for p in ps]
        fix = _mm_each(_mm_each(p_low, cs), ps)
        merged = []
        for p, f in zip(ps, fix):
            pieces = []
            for i, (up, lo) in enumerate(zip(upper, lower)):
                pieces += [p[up, :], p[lo, :] + f[i * size:(i + 1) * size, :]]
            merged.append(jnp.concatenate(pieces, 0))
        ps = merged
        s += 1
    return ps


def _rotary(t, cosf, sins):
    return t * cosf + pltpu.roll(t, HEAD_DIM // 2, 1) * sins


def _ret_head_ops(valid, q, k, v, gate, cosf, sins, q_scale, k_scale):
    rq = _mask_rows(valid, _rotary(q, cosf, sins))
    rk = _mask_rows(valid, _rotary(k, cosf, sins) * (HEAD_DIM ** -0.5))
    return dict(rq=rq, rk=rk, rv=_mask_rows(valid, v), qd=rq * q_scale, kd=rk * k_scale, sgr=_silu(gate))


def _short_conv(x, halo, cw_ref, col0):
    rows, ncols = x.shape
    cols = slice(col0, col0 + ncols)
    n = rows // HALO
    full = jnp.concatenate([halo, x], 0).reshape(n + 1, HALO, ncols)
    sub = lax.broadcasted_iota(jnp.int32, (n, HALO, ncols), 1)
    conv = full[1:] * cw_ref[CONV_W - 1:CONV_W, cols]
    for shift in range(1, CONV_W):
        rolled = pltpu.roll(full, shift, 1)
        shifted = jnp.where(sub < shift, rolled[:-1], rolled[1:])
        conv = conv + shifted * cw_ref[CONV_W - 1 - shift:CONV_W - shift, cols]
    return _silu(conv).reshape(rows, ncols)


def _l2_normalize(t):
    return t * lax.rsqrt(jnp.sum(t * t, -1, keepdims=True) + RMS_EPS)


def _tokenwise(variant, pret, pgdn, cosf, sins, rsc, cw_ref, convpad):
    rows = pret.shape[0]
    heads = range(N_HEADS)
    valid = _valid_rows(variant, (rows, HEAD_DIM))
    ret = [_ret_head_ops(valid, *(_head_cols(pret, h * GROUP_W, g) for g in range(4)), cosf, sins,
                         _lane_bcast(rsc, h), _lane_bcast(rsc, N_HEADS + h)) for h in heads]
    x = pgdn[:, :CONV_COLS]
    if variant == "sample":
        x = jnp.where(_valid_rows(variant, x.shape), x, convpad)
    conv = _short_conv(x, jnp.zeros((HALO, CONV_COLS), F32), cw_ref, 0)
    gq = [_mask_rows(valid, _l2_normalize(_head_cols(conv, 0, h)) * (HEAD_DIM ** -0.5)) for h in heads]
    gk = [_mask_rows(valid, _l2_normalize(_head_cols(conv, GROUP_W, h))) for h in heads]
    gv = [_mask_rows(valid, _head_cols(conv, 2 * GROUP_W, h)) for h in heads]
    sgz = [_silu(_head_cols(pgdn, 3 * GROUP_W, h)) for h in heads]
    tok = {key: [r[key] for r in ret] for key in ret[0]}
    tok.update(gq=gq, gk=gk, gv=gv, sgz=sgz)
    return tok


def _chunk_prep(variant, toks, pabs, dmat_ref, alog, dtb):
    shape = (TILE, HEAD_DIM)
    tiles = range(len(toks))
    items = [(j, h) for j in tiles for h in range(N_HEADS)]
    pick = lambda key: [toks[j][key][h] for j, h in items]
    row = _iota2(shape, 0)
    col = _iota2(shape, 1)
    valid = _valid_rows(variant, shape)
    if variant == "sample":
        same = (row >> 3) == (col >> 3)
        incl = same & (row >= col)
        strict = same & (row > col)
    else:
        incl = row >= col
        strict = row > col

    incl01 = incl.astype(BF16)
    beta_all, gcum, gam, ktail, cdr, gcum_t = [], [], [], [], [], []
    for j in tiles:
        g_all = _mask_rows(valid, -jnp.exp(alog) * _softplus(pabs[j] + dtb))
        beta_all.append(_mask_rows(valid, _sigmoid(pabs[j])))
        if variant == "sample":
            both = _mm_split3(jnp.concatenate([incl01, same.astype(BF16)], 0), g_all)
            gc, gseg = both[:TILE], both[TILE:]
        else:
            gc = _mm_split3(incl01, g_all)
            gseg = jnp.broadcast_to(gc[TILE - 1:TILE, :], shape)
        gcum.append(gc)
        gam.append(jnp.exp(gc))
        ktail.append(jnp.exp(gseg - gc))
        cdr.append(jnp.exp(gseg))
        gcum_t.append(gc.T)

    gq, gk, gv = pick("gq"), pick("gk"), pick("gv")
    n_items = range(len(items))
    dec = [jnp.where(incl, jnp.exp(jnp.minimum(
        _lane_bcast(gcum[j], h) - jnp.broadcast_to(gcum_t[j][h:h + 1, :], shape), 0.0)), 0.0)
        for j, h in items]
    bcol = [_lane_bcast(beta_all[j], N_HEADS + h) for j, h in items]
    gamc = [_lane_bcast(gam[j], h) for j, h in items]
    kk = _mm_nt_each(gk, gk)
    qk = _mm_nt_each(gq, gk)
    nmats = [-(jnp.where(strict, kk[i] * dec[i], 0.0) * bcol[i]) for i in n_items]
    tinv = _tri_inverse(variant, nmats, row, col)
    rhs = [jnp.concatenate([gv[i] * bcol[i], gk[i] * (bcol[i] * gamc[i])], 1) for i in n_items]
    sol = _mm_each(tinv, rhs)
    ret = dict(
        items=items,
        scores=[qk_r * dmat_ref[h] for qk_r, (_, h) in zip(
            _mm_nt_each(pick("rq"), pick("rk")), items)],
        qd=pick("qd"), kdT=pick("kdT"), v=pick("rv"), gate=pick("sgr"),
    )
    gdn = dict(
        items=items,
        wv=[t[:, :HEAD_DIM] for t in sol],
        wk=[t[:, HEAD_DIM:] for t in sol],
        attn=[qk[i] * dec[i] for i in n_items],
        qg=[gq[i] * gamc[i] for i in n_items],
        ktT=[(gk[i] * _lane_bcast(ktail[j], h)).T for i, (j, h) in enumerate(items)],
        cdr=cdr,
        gate=pick("sgz"),
    )
    return ret, gdn


def _rms_gate(o, gate, weight=None):
    o = o * lax.rsqrt(jnp.mean(o * o, -1, keepdims=True) + RMS_EPS)
    if weight is not None:
        o = o * weight
    return (o * gate).astype(BF16)


_STATE_SHAPE = (N_HEADS, HEAD_DIM, HEAD_DIM)


def _seq_init(s0r_ref, s0g_ref, sr_ref, sg_ref):
    @pl.when(pl.program_id(1) == 0)
    def _():
        for j in range(sr_ref.shape[0]):
            sr_ref[j] = s0r_ref[...]
            sg_ref[j] = s0g_ref[...]


def _seq_step(ret_cd, ret, gdn, gnw, mix_ref, sr_out, sg_out, sr_ref, sg_ref):
    items = ret["items"]
    idx = range(len(items))
    s_r = [sr_ref[j, h] for j, h in items]
    s_g = [sg_ref[j, h] for j, h in items]
    bf = lambda t: t.astype(BF16)
    cat = lambda a, b, axis: jnp.concatenate([bf(a), bf(b)], axis)
    wk_s = _mm_each(gdn["wk"], s_g)
    o_r = _mm_each([cat(ret["scores"][i], ret["qd"][i], 1) for i in idx],
                   [cat(ret["v"][i], s_r[i], 0) for i in idx])
    u = [gdn["wv"][i] - wk_s[i] for i in idx]
    o_g = _mm_each([cat(gdn["qg"][i], gdn["attn"][i], 1) for i in idx],
                   [cat(s_g[i], u[i], 0) for i in idx])
    upd_g = _mm_each(gdn["ktT"], u)
    upd_r = _mm_each(ret["kdT"], ret["v"])
    for i, (j, h) in enumerate(items):
        cd = jnp.broadcast_to(gdn["cdr"][j][0:1, h:h + 1], (HEAD_DIM, HEAD_DIM))
        sg_ref[j, h] = cd * s_g[i] + upd_g[i]
        sr_ref[j, h] = ret_cd[h] * s_r[i] + upd_r[i]
    for i, (j, h) in enumerate(items):
        mix_ref[j, :, h * HEAD_DIM:(h + 1) * HEAD_DIM] = _rms_gate(o_r[i], ret["gate"][i])
        mix_ref[j, :, GROUP_W + h * HEAD_DIM:GROUP_W + (h + 1) * HEAD_DIM] = _rms_gate(
            o_g[i], gdn["gate"][i], gnw)

    @pl.when(pl.program_id(1) == pl.num_programs(1) - 1)
    def _():
        sr_out[...] = sr_ref[...]
        sg_out[...] = sg_ref[...]


def _seq_out(n_batch, n_chunks, nb):
    state = pl.BlockSpec((nb,) + _STATE_SHAPE, lambda b, n: (b, 0, 0, 0))
    out_specs = [pl.BlockSpec((nb, TILE, 2 * GROUP_W), lambda b, n: (b, n, 0)), state, state]
    out_shape = [
        jax.ShapeDtypeStruct((n_batch, n_chunks * TILE, 2 * GROUP_W), BF16),
        jax.ShapeDtypeStruct((n_batch,) + _STATE_SHAPE, F32),
        jax.ShapeDtypeStruct((n_batch,) + _STATE_SHAPE, F32),
    ]
    scratch = [pltpu.VMEM((nb,) + _STATE_SHAPE, F32), pltpu.VMEM((nb,) + _STATE_SHAPE, F32)]
    return out_specs, out_shape, scratch


def _project_tile(x_ref, g_ref, b_ref, w_ref):
    hidden = _layer_norm(x_ref[...], g_ref[...], b_ref[...]).astype(BF16)
    project = lambda lo, hi: jnp.dot(hidden, w_ref[:, lo:hi], preferred_element_type=F32)
    return (project(0, 4 * GROUP_W), project(4 * GROUP_W, 8 * GROUP_W),
            project(8 * GROUP_W, 8 * GROUP_W + HEAD_DIM))


def _meta_mixer_body(ret_cd, x_ref, eg_ref, eb_ref, w_ref, cos_ref, sin_ref, rsc_ref, dmat_ref,
                     cw_ref, alog_ref, dtb_ref, gnw_ref, s0r_ref, s0g_ref,
                     mix_ref, sr_out, sg_out, preconv_ref, sr_ref, sg_ref):
    _seq_init(s0r_ref, s0g_ref, sr_ref, sg_ref)
    pret, pgdn, pab = _project_tile(x_ref, eg_ref, eb_ref, w_ref)
    preconv_ref[...] = pgdn[:, :CONV_COLS]
    tok = _tokenwise("meta", pret, pgdn, cos_ref[...], sin_ref[...], rsc_ref[...], cw_ref, None)
    tok["kdT"] = [t.T for t in tok["kd"]]
    ret, gdn = _chunk_prep("meta", [tok], [pab], dmat_ref, alog_ref[...], dtb_ref[...])
    _seq_step(ret_cd, ret, gdn, gnw_ref[...], mix_ref, sr_out, sg_out, sr_ref, sg_ref)


def _meta_mixer(ret_cd, x, ln_g, ln_b, w_all, cosf, sins, rsc, dmat, cw, alog, dtb, gnw, s0r, s0g):
    c2 = lambda b, n: (0, 0)
    c3 = lambda b, n: (0, 0, 0)
    out_specs, out_shape, state_scratch = _seq_out(1, 1, 1)
    return pl.pallas_call(
        functools.partial(_meta_mixer_body, ret_cd),
        grid=(1, 1),
        in_specs=[
            pl.BlockSpec((TILE, D_MODEL), c2),
            pl.BlockSpec((1, D_MODEL), c2),
            pl.BlockSpec((1, D_MODEL), c2),
            pl.BlockSpec((D_MODEL, W_ALL_COLS), c2),
            pl.BlockSpec((TILE, HEAD_DIM), c2),
            pl.BlockSpec((TILE, HEAD_DIM), c2),
            pl.BlockSpec((TILE, HEAD_DIM), c2),
            pl.BlockSpec((N_HEADS, TILE, TILE), c3),
            pl.BlockSpec((CONV_W, CONV_COLS), c2),
            pl.BlockSpec((1, HEAD_DIM), c2),
            pl.BlockSpec((1, HEAD_DIM), c2),
            pl.BlockSpec((1, HEAD_DIM), c2),
            pl.BlockSpec(_STATE_SHAPE, c3),
            pl.BlockSpec(_STATE_SHAPE, c3),
        ],
        out_specs=out_specs + [pl.BlockSpec((TILE, CONV_COLS), c2)],
        out_shape=out_shape + [jax.ShapeDtypeStruct((TILE, CONV_COLS), F32)],
        scratch_shapes=state_scratch,
        compiler_params=_cparams(("parallel", "arbitrary")),
        name="mixer_meta",
    )(x, ln_g, ln_b, w_all, cosf, sins, rsc, dmat, cw, alog, dtb, gnw, s0r, s0g)


def _prompt_fused_body(ret_cd, x_ref, g_ref, b_ref, w_ref, cos_ref, sin_ref, rsc_ref, cw_ref, halo0_ref,
                       dmat_ref, alog_ref, dtb_ref, gnw_ref, s0r_ref, s0g_ref,
                       mix_ref, sr_out, sg_out, tail_ref, sr_ref, sg_ref, halo_ref):
    nb = x_ref.shape[0]
    seqs, heads = range(nb), range(N_HEADS)
    _seq_init(s0r_ref, s0g_ref, sr_ref, sg_ref)

    @pl.when(pl.program_id(1) == 0)
    def _():
        for j in seqs:
            halo_ref[j] = halo0_ref[...]

    hidden = _layer_norm(x_ref[...].reshape(nb * TILE, D_MODEL), g_ref[...], b_ref[...]).astype(BF16)
    cosf, sins, rsc = cos_ref[...], sin_ref[...], rsc_ref[...]
    keys = ("rq", "rk", "rv", "qd", "kdT", "sgr", "gq", "gk", "gv", "sgz")
    toks = [{key: [None] * N_HEADS for key in keys} for _ in seqs]
    pabs = [None] * nb
    rows = lambda j: slice(j * TILE, (j + 1) * TILE)

    def project(group, width=GROUP_W):
        return jnp.dot(hidden, w_ref[:, group * GROUP_W:group * GROUP_W + width], preferred_element_type=F32)

    def conv_group(group, key):
        val = project(N_HEADS + group)
        cols = slice(group * GROUP_W, (group + 1) * GROUP_W)
        for j in seqs:
            part = val[rows(j), :]
            conv = _short_conv(part, halo_ref[j, :, cols], cw_ref, group * GROUP_W)
            halo_ref[j, :, cols] = part[TILE - HALO:, :]
            for h in heads:
                t = _head_cols(conv, 0, h)
                if key == "gq":
                    t = _l2_normalize(t) * (HEAD_DIM ** -0.5)
                elif key == "gk":
                    t = _l2_normalize(t)
                toks[j][key][h] = t

    def ab_group():
        val = project(2 * N_HEADS, HEAD_DIM)
        for j in seqs:
            pabs[j] = val[rows(j), :]

    def ret_head(h):
        val = project(h)
        for j in seqs:
            part = val[rows(j), :]
            ops = _ret_head_ops(None, *(_head_cols(part, 0, g) for g in range(4)), cosf, sins,
                                _lane_bcast(rsc, h), _lane_bcast(rsc, N_HEADS + h))
            for key in ("rq", "rk", "rv", "qd", "sgr"):
                toks[j][key][h] = ops[key]
            toks[j]["kdT"][h] = ops["kd"].T

    def gate_group():
        val = project(2 * N_HEADS - 1)
        for j in seqs:
            for h in heads:
                toks[j]["sgz"][h] = _silu(_head_cols(val[rows(j), :], 0, h))

    ab_group()
    conv_group(1, "gk")
    conv_group(0, "gq")
    conv_group(2, "gv")
    for j in seqs:
        tail_ref[j] = halo_ref[j]

    for h in heads:
        ret_head(h)
    gate_group()

    ret, gdn = _chunk_prep("prompt", toks, pabs, dmat_ref, alog_ref[...], dtb_ref[...])
    _seq_step(ret_cd, ret, gdn, gnw_ref[...], mix_ref, sr_out, sg_out, sr_ref, sg_ref)


def _prompt_fused(ret_cd, x, ln_g, ln_b, w_all, cosf, sins, rsc, cw, halo0, dmat, alog, dtb, gnw, s0r, s0g, nb):
    n_batch, seq, _ = x.shape
    n_chunks = seq // TILE
    tile = lambda b, n: (b, n, 0)
    pos = lambda b, n: (n, 0)
    c2 = lambda b, n: (0, 0)
    c3 = lambda b, n: (0, 0, 0)
    out_specs, out_shape, state_scratch = _seq_out(n_batch, n_chunks, nb)
    return pl.pallas_call(
        functools.partial(_prompt_fused_body, ret_cd),
        grid=(n_batch // nb, n_chunks),
        in_specs=[
            pl.BlockSpec((nb, TILE, D_MODEL), tile),
            pl.BlockSpec((1, D_MODEL), c2),
            pl.BlockSpec((1, D_MODEL), c2),
            pl.BlockSpec((D_MODEL, W_ALL_COLS), c2, pipeline_mode=pl.Buffered(1)),
            pl.BlockSpec((TILE, HEAD_DIM), pos),
            pl.BlockSpec((TILE, HEAD_DIM), pos),
            pl.BlockSpec((TILE, HEAD_DIM), c2),
            pl.BlockSpec((CONV_W, CONV_COLS), c2),
            pl.BlockSpec((HALO, CONV_COLS), c2),
            pl.BlockSpec((N_HEADS, TILE, TILE), c3),
            pl.BlockSpec((1, HEAD_DIM), c2),
            pl.BlockSpec((1, HEAD_DIM), c2),
            pl.BlockSpec((1, HEAD_DIM), c2),
            pl.BlockSpec(_STATE_SHAPE, c3),
            pl.BlockSpec(_STATE_SHAPE, c3),
        ],
        out_specs=out_specs + [pl.BlockSpec((nb, HALO, CONV_COLS), lambda b, n: (b, 0, 0))],
        out_shape=out_shape + [jax.ShapeDtypeStruct((n_batch, HALO, CONV_COLS), F32)],
        scratch_shapes=state_scratch + [pltpu.VMEM((nb, HALO, CONV_COLS), F32)],
        compiler_params=_cparams(("parallel", "arbitrary")),
        name="prompt_fused",
    )(x, ln_g, ln_b, w_all, cosf, sins, rsc, cw, halo0, dmat, alog, dtb, gnw, s0r, s0g)


SAMPLE_PER_TILE = TILE // SAMPLE_GROUP


def _stack_by_group(xt):
    shape3 = (SAMPLE_PER_TILE, HEAD_DIM, TILE)
    keep = lax.broadcasted_iota(jnp.int32, shape3, 0) == (lax.broadcasted_iota(jnp.int32, shape3, 2) >> 3)
    stacked = jnp.where(keep, jnp.broadcast_to(xt[None], shape3), 0.0)
    return stacked.reshape(SAMPLE_PER_TILE * HEAD_DIM, TILE)


def _sample_mixer_body(ret_cd, x_ref, eg_ref, eb_ref, w_ref, cos_ref, sin_ref, rsc_ref, dmat_ref, cw_ref,
                       alog_ref, dtb_ref, gnw_ref, convpad_ref, sr_in, sg_in,
                       mix_ref, sr_out, sg_out, preconv_ref,
                       qd_s, wk_s, qg_s, wv_s, cd_s, inter_s, u_s, qs_s):
    heads = range(N_HEADS)
    pret, pgdn, pab = _project_tile(x_ref, eg_ref, eb_ref, w_ref)
    preconv_ref[...] = pgdn[:, :CONV_COLS]
    tok = _tokenwise("sample", pret, pgdn, cos_ref[...], sin_ref[...], rsc_ref[...],
                     cw_ref, convpad_ref[...])
    tok["kdT"] = [t.T for t in tok["kd"]]
    ret, gdn = _chunk_prep("sample", [tok], [pab], dmat_ref, alog_ref[...], dtb_ref[...])
    for h in heads:
        qd_s[h] = ret["qd"][h]
        wk_s[h] = gdn["wk"][h]
        qg_s[h] = gdn["qg"][h]
        wv_s[h] = gdn["wv"][h]
    cd_s[...] = gdn["cdr"][0]

    def per_batch(b, carry):
        rows = pl.ds(pl.multiple_of(b * SAMPLE_GROUP, SAMPLE_GROUP), SAMPLE_GROUP)
        cd_rows = cd_s[rows, :]
        for h in heads:
            inter_s[h, rows, :] = _mm(qd_s[h, rows, :], sr_in[b, h])
            s = sg_in[b, h]
            both = _mm(jnp.concatenate([wk_s[h, rows, :], qg_s[h, rows, :]], 0), s)
            u_s[h, rows, :] = wv_s[h, rows, :] - both[:SAMPLE_GROUP]
            qs_s[h, rows, :] = both[SAMPLE_GROUP:]
            cd = jnp.broadcast_to(cd_rows[SAMPLE_GROUP - 1:SAMPLE_GROUP, h:h + 1], (HEAD_DIM, HEAD_DIM))
            sg_out[b, h] = cd * s
        return carry

    lax.fori_loop(0, SAMPLE_PER_TILE, per_batch, 0)

    out_rows = SAMPLE_PER_TILE * SAMPLE_TOKENS
    out_row = _iota2((out_rows, TILE), 0)
    token_row = ((out_row >> 2) << 3) + (SAMPLE_GROUP - SAMPLE_TOKENS) + (out_row & (SAMPLE_TOKENS - 1))
    select = (_iota2((out_rows, TILE), 1) == token_row).astype(BF16)
    compact = lambda t: jnp.dot(select, t, preferred_element_type=F32).astype(BF16)

    state_shape = (SAMPLE_PER_TILE, HEAD_DIM, HEAD_DIM)
    for h in heads:
        o = _mm(ret["scores"][h], ret["v"][h]) + inter_s[h]
        upd = _mm(_stack_by_group(ret["kdT"][h]), ret["v"][h]).reshape(state_shape)
        sr_out[:, h] = ret_cd[h] * sr_in[:, h] + upd
        mix_ref[:, h * HEAD_DIM:(h + 1) * HEAD_DIM] = compact(_rms_gate(o, ret["gate"][h]))

    for h in heads:
        u = u_s[h]
        o = qs_s[h] + _mm(gdn["attn"][h], u)
        upd = _mm(_stack_by_group(gdn["ktT"][h]), u).reshape(state_shape)
        sg_out[:, h] = sg_out[:, h] + upd
        mix_ref[:, GROUP_W + h * HEAD_DIM:GROUP_W + (h + 1) * HEAD_DIM] = compact(_rms_gate(
            o, gdn["gate"][h], gnw_ref[...]))


def _sample_mixer(ret_cd, x, ln_g, ln_b, w_all, cosf, sins, rsc, dmat, cw, alog, dtb, gnw, convpad, sr, sg):
    n_tiles = x.shape[0] // TILE
    tile = lambda i: (i, 0)
    c2 = lambda i: (0, 0)
    c3 = lambda i: (0, 0, 0)
    state = pl.BlockSpec((SAMPLE_PER_TILE,) + _STATE_SHAPE, lambda i: (i, 0, 0, 0))
    head_scratch = pltpu.VMEM((N_HEADS, TILE, HEAD_DIM), F32)
    return pl.pallas_call(
        functools.partial(_sample_mixer_body, ret_cd),
        grid=(n_tiles,),
        in_specs=[
            pl.BlockSpec((TILE, D_MODEL), tile),
            pl.BlockSpec((1, D_MODEL), c2),
            pl.BlockSpec((1, D_MODEL), c2),
            pl.BlockSpec((D_MODEL, W_ALL_COLS), c2, pipeline_mode=pl.Buffered(1)),
            pl.BlockSpec((TILE, HEAD_DIM), c2),
            pl.BlockSpec((TILE, HEAD_DIM), c2),
            pl.BlockSpec((TILE, HEAD_DIM), c2),
            pl.BlockSpec((N_HEADS, TILE, TILE), c3),
            pl.BlockSpec((CONV_W, CONV_COLS), c2),
            pl.BlockSpec((1, HEAD_DIM), c2),
            pl.BlockSpec((1, HEAD_DIM), c2),
            pl.BlockSpec((1, HEAD_DIM), c2),
            pl.BlockSpec((TILE, CONV_COLS), tile),
            state, state,
        ],
        out_specs=[pl.BlockSpec((SAMPLE_PER_TILE * SAMPLE_TOKENS, 2 * GROUP_W), tile), state, state,
                   pl.BlockSpec((TILE, CONV_COLS), tile)],
        out_shape=[
            jax.ShapeDtypeStruct((n_tiles * SAMPLE_PER_TILE * SAMPLE_TOKENS, 2 * GROUP_W), BF16),
            jax.ShapeDtypeStruct(sr.shape, F32),
            jax.ShapeDtypeStruct(sg.shape, F32),
            jax.ShapeDtypeStruct((x.shape[0], CONV_COLS), F32),
        ],
        scratch_shapes=[head_scratch] * 4 + [pltpu.VMEM((TILE, HEAD_DIM), F32)] + [head_scratch] * 3,
        compiler_params=_cparams(("parallel",)),
        name="mixer_sample",
    )(x, ln_g, ln_b, w_all, cosf, sins, rsc, dmat, cw, alog, dtb, gnw, convpad, sr, sg)


def _rotary_tables(pos):
    half = HEAD_DIM // 2
    inv = ROPE_BASE ** (-np.arange(half, dtype=np.float64) / half)
    ang = np.asarray(pos, np.float64)[:, None] * inv[None, :]
    cos, sin = np.cos(ang), np.sin(ang)
    return (jnp.asarray(np.concatenate([cos, cos], -1), F32),
            jnp.asarray(np.concatenate([-sin, sin], -1), F32))


def _retention_tables(seg, pos, valid, seg_len):
    gamma = 1.0 - 2.0 ** (-5.0 - np.arange(N_HEADS, dtype=np.float64))
    posf = np.asarray(pos, np.float64)
    rel = posf[:, None] - posf[None, :]
    causal = (seg[:, None] == seg[None, :]) & (rel >= 0)
    dmat = np.where(causal[None], gamma[:, None, None] ** np.where(causal, rel, 0.0)[None], 0.0)
    q_scale = gamma[None, :] ** (posf[:, None] + 1.0)
    k_scale = np.where(valid[:, None], gamma[None, :] ** (seg_len - 1.0 - posf[:, None]), 0.0)
    rsc = np.concatenate([q_scale, k_scale, np.zeros((TILE, HEAD_DIM - 2 * N_HEADS))], -1)
    chunk_decay = tuple(float(g ** seg_len) for g in gamma)
    return jnp.asarray(dmat, F32), jnp.asarray(rsc, F32), chunk_decay


def _pad_lanes(v):
    return jnp.pad(v.astype(F32), (0, HEAD_DIM - v.shape[0]))[None, :]


def kernel(x_prompt, x_sample, state_ret, state_gdn, state_conv, meta_tokens, emb_ln_g, emb_ln_b,
           w_in, conv_w, a_log, dt_bias, gdn_norm_w, w_out, ln1_g, ln1_b, w_gate_up, w_down,
           ln2_g, ln2_b):
    n_batch, seq, _ = x_prompt.shape
    dec_batch, dec_seq, _ = x_sample.shape
    assert seq % TILE == 0 and dec_seq == SAMPLE_TOKENS and N_META <= TILE
    assert n_batch % MIXER_SEQS_PER_STEP == 0
    layer = 0

    w_all = _weight_prep(jnp.swapaxes(w_in[layer], 0, 1))
    w_out_b = w_out[layer].astype(BF16)
    w_gu_b = w_gate_up[layer].astype(BF16)
    w_down_b = w_down[layer].astype(BF16)
    row = lambda v: v.astype(F32)[None, :]
    eg, eb = row(emb_ln_g), row(emb_ln_b)
    cw = conv_w[layer].astype(F32)
    alog, dtb, gnw = _pad_lanes(a_log[layer]), _pad_lanes(dt_bias[layer]), row(gdn_norm_w[layer])

    xp = x_prompt.reshape(n_batch * seq, D_MODEL)
    xm = jnp.pad(meta_tokens.astype(F32), ((0, TILE - N_META), (0, 0)))
    xs = jnp.pad(x_sample, ((0, 0), (SAMPLE_GROUP - dec_seq, 0), (0, 0))).reshape(
        dec_batch * SAMPLE_GROUP, D_MODEL)
    convpad = jnp.pad(state_conv[layer].astype(F32), ((0, 0), (1, SAMPLE_GROUP - CONV_W), (0, 0))).reshape(
        dec_batch * SAMPLE_GROUP, CONV_COLS)

    tile_idx = np.arange(TILE)
    cos_p, sin_p = _rotary_tables(N_META + np.arange(seq))
    cos_m, sin_m = _rotary_tables(tile_idx)
    tok = (tile_idx % SAMPLE_GROUP) - (SAMPLE_GROUP - dec_seq)
    cos_s, sin_s = _rotary_tables(PAST_LEN + np.maximum(tok, 0))
    zeros_i = np.zeros((TILE,), np.int32)
    all_valid = np.ones((TILE,), bool)
    dmat_p, rsc_p, cd_p = _retention_tables(zeros_i, tile_idx, all_valid, float(TILE))
    dmat_m, rsc_m, cd_m = _retention_tables(zeros_i, tile_idx, tile_idx < N_META, float(N_META))
    dmat_s, rsc_s, cd_s = _retention_tables(tile_idx // SAMPLE_GROUP, tok, tok >= 0, float(dec_seq))

    zero_state = jnp.zeros(_STATE_SHAPE, F32)
    _, sr_m, sg_m, preconv_m = _meta_mixer(cd_m, xm, eg, eb, w_all, cos_m, sin_m, rsc_m, dmat_m, cw, alog, dtb,
                                           gnw, zero_state, zero_state)
    halo_p = preconv_m[N_META - HALO:N_META]

    mix_p, sr_p, sg_p, tail_p = _prompt_fused(cd_p, x_prompt, eg, eb, w_all, cos_p, sin_p, rsc_p, cw, halo_p,
                                              dmat_p, alog, dtb, gnw, sr_m[0], sg_m[0], MIXER_SEQS_PER_STEP)
    mix_p = mix_p.reshape(n_batch * seq, 2 * GROUP_W)

    mix_s, sr_s, sg_s, preconv_s = _sample_mixer(cd_s, xs, eg, eb, w_all, cos_s, sin_s, rsc_s, dmat_s, cw,
                                                 alog, dtb, gnw, convpad, state_ret[layer].astype(F32),
                                                 state_gdn[layer].astype(F32))

    back = functools.partial(_back, eg=eg, eb=eb, w_out=w_out_b, g1=row(ln1_g[layer]), b1=row(ln1_b[layer]),
                             w_gu=w_gu_b, w_down=w_down_b, g2=row(ln2_g[layer]), b2=row(ln2_b[layer]),
                             tm=512)
    y_p = back(xp, mix_p).reshape(n_batch, seq, D_MODEL)
    y_s = back(x_sample.reshape(dec_batch * dec_seq, D_MODEL), mix_s).reshape(dec_batch, dec_seq, D_MODEL)

    conv_p = tail_p[:, HALO - (CONV_W - 1):, :]
    conv_s = preconv_s.reshape(dec_batch, SAMPLE_GROUP, CONV_COLS)[:, SAMPLE_GROUP - (CONV_W - 1):, :]
    return (y_p, y_s, sr_p[None], sg_p[None], conv_p[None], sr_s[None], sg_s[None], conv_s[None])
```

```python
import functools

import numpy as np
import jax
import jax.numpy as jnp
from jax import lax
from jax.experimental import pallas as pl
from jax.experimental.pallas import tpu as pltpu

F32 = jnp.float32
BF16 = jnp.bfloat16

D_MODEL = 1024
N_META = 16
N_HEADS = 4
HEAD_DIM = 128
GROUP_W = N_HEADS * HEAD_DIM
CONV_W = 4
CONV_COLS = 3 * GROUP_W
D_FF = 2816
PAST_LEN = 16384
ROPE_BASE = 10000.0
LN_EPS = 1e-5
RMS_EPS = 1e-6
ALPHA = 2.0 ** 0.25
TILE = 128
SAMPLE_GROUP = 8
SAMPLE_TOKENS = 4
HALO = 8
MIXER_SEQS_PER_STEP = 4
BACK_PARTS = 2
W_ALL_COLS = 9 * GROUP_W
AB_ROWS = 2 * N_HEADS
VMEM_LIMIT = 56 * 1024 * 1024


def _cparams(sem):
    return pltpu.CompilerParams(dimension_semantics=sem, vmem_limit_bytes=VMEM_LIMIT)


def _layer_norm(x, g, b):
    mu = jnp.mean(x, -1, keepdims=True)
    xc = x - mu
    var = jnp.mean(xc * xc, -1, keepdims=True)
    return xc * lax.rsqrt(var + LN_EPS) * g + b


def _sigmoid(x):
    return 1.0 / (1.0 + jnp.exp(-x))


def _silu(x):
    return x * _sigmoid(x)


def _softplus(x):
    return jnp.maximum(x, 0.0) + jnp.log(1.0 + jnp.exp(-jnp.abs(x)))


def _mm(a, b):
    return jnp.dot(a.astype(BF16), b.astype(BF16), preferred_element_type=F32)


def _mm_nt(a, b):
    return lax.dot_general(a.astype(BF16), b.astype(BF16), (((1,), (1,)), ((), ())),
                           preferred_element_type=F32)


def _mm_each(xs, ys):
    return [_mm(x, y) for x, y in zip(xs, ys)]


def _mm_nt_each(xs, ys):
    return [_mm_nt(x, y) for x, y in zip(xs, ys)]


def _mm_split3(m01, x):
    x1 = x.astype(BF16)
    r1 = x - x1.astype(F32)
    x2 = r1.astype(BF16)
    x3 = (r1 - x2.astype(F32)).astype(BF16)
    dot = functools.partial(jnp.dot, preferred_element_type=F32)
    return (dot(m01, x3) + dot(m01, x2)) + dot(m01, x1)


def _lane_bcast(x, lane):
    return jnp.broadcast_to(x[:, lane:lane + 1], (x.shape[0], HEAD_DIM))


def _head_cols(x, base, h):
    return x[:, base + h * HEAD_DIM:base + (h + 1) * HEAD_DIM]


def _weight_prep_body(q_ref, k_ref, v_ref, g_ref, ab_ref, o_ref):
    j = pl.program_id(0)

    @pl.when(j < 2 * N_HEADS)
    def _():
        for piece, ref in enumerate((q_ref, k_ref, v_ref, g_ref)):
            o_ref[:, piece * HEAD_DIM:(piece + 1) * HEAD_DIM] = ref[...].T.astype(BF16)

    @pl.when(j == 2 * N_HEADS)
    def _():
        ab = jnp.concatenate([ab_ref[...], jnp.zeros((HEAD_DIM - AB_ROWS, D_MODEL), F32)], 0)
        o_ref[:, :HEAD_DIM] = ab.T.astype(BF16)
        o_ref[:, HEAD_DIM:] = jnp.zeros((D_MODEL, GROUP_W - HEAD_DIM), BF16)


def _weight_prep(w_in_t):
    assert w_in_t.shape == (8 * GROUP_W + AB_ROWS, D_MODEL)

    def piece_spec(piece):
        def index(j):
            ret_block = piece * N_HEADS + j
            gdn_block = N_HEADS * j + piece
            return (jnp.where(j < N_HEADS, ret_block, jnp.where(j < 2 * N_HEADS, gdn_block, 0)), 0)
        return pl.BlockSpec((HEAD_DIM, D_MODEL), index)

    ab_spec = pl.BlockSpec((AB_ROWS, D_MODEL), lambda j: (8 * GROUP_W // AB_ROWS, 0))
    return pl.pallas_call(
        _weight_prep_body,
        grid=(2 * N_HEADS + 1,),
        in_specs=[piece_spec(p) for p in range(4)] + [ab_spec],
        out_specs=pl.BlockSpec((D_MODEL, GROUP_W), lambda j: (0, j)),
        out_shape=jax.ShapeDtypeStruct((D_MODEL, W_ALL_COLS), BF16),
        compiler_params=_cparams(("arbitrary",)),
        name="weight_prep",
    )(w_in_t, w_in_t, w_in_t, w_in_t, w_in_t)


def _back_body(x_ref, mix_ref, eg_ref, eb_ref, wo_ref, g1_ref, b1_ref, wgu_ref, wd_ref,
               g2_ref, b2_ref, y_ref):
    part = x_ref.shape[0] // BACK_PARTS
    parts = [slice(i * part, (i + 1) * part) for i in range(BACK_PARTS)]
    dot = functools.partial(jnp.dot, preferred_element_type=F32)
    h = [_layer_norm(x_ref[p, :], eg_ref[...], eb_ref[...]) for p in parts]
    mp = [dot(mix_ref[p, :], wo_ref[...]) for p in parts]
    h1 = [_layer_norm(ALPHA * a + b, g1_ref[...], b1_ref[...]) for a, b in zip(h, mp)]
    gu = [dot(a.astype(BF16), wgu_ref[...]) for a in h1]
    act = [(_silu(a[:, :D_FF]) * a[:, D_FF:]).astype(BF16) for a in gu]
    ff = [dot(a, wd_ref[...]) for a in act]
    for p, a, b in zip(parts, h1, ff):
        y_ref[p, :] = _layer_norm(ALPHA * a + b, g2_ref[...], b2_ref[...])


def _back(x2d, mix, eg, eb, w_out, g1, b1, w_gu, w_down, g2, b2, tm):
    rows = x2d.shape[0]
    const = lambda i: (0, 0)
    tile = lambda i: (i, 0)
    single = pl.Buffered(1)
    vec = pl.BlockSpec((1, D_MODEL), const)
    return pl.pallas_call(
        _back_body,
        grid=(rows // tm,),
        in_specs=[
            pl.BlockSpec((tm, D_MODEL), tile),
            pl.BlockSpec((tm, D_MODEL), tile),
            vec, vec,
            pl.BlockSpec((D_MODEL, D_MODEL), const, pipeline_mode=single),
            vec, vec,
            pl.BlockSpec((D_MODEL, 2 * D_FF), const, pipeline_mode=single),
            pl.BlockSpec((D_FF, D_MODEL), const, pipeline_mode=single),
            vec, vec,
        ],
        out_specs=pl.BlockSpec((tm, D_MODEL), tile),
        out_shape=jax.ShapeDtypeStruct((rows, D_MODEL), F32),
        compiler_params=_cparams(("parallel",)),
        name="back",
    )(x2d, mix, eg, eb, w_out, g1, b1, w_gu, w_down, g2, b2)


def _iota2(shape, dim):
    return lax.broadcasted_iota(jnp.int32, shape, dim)


def _valid_rows(variant, shape):
    r = _iota2(shape, 0)
    if variant == "meta":
        return r < N_META
    if variant == "sample":
        return (r & (SAMPLE_GROUP - 1)) >= SAMPLE_GROUP - SAMPLE_TOKENS
    return None


def _mask_rows(valid, x):
    return x if valid is None else jnp.where(valid, x, 0.0)


def _tri_inverse(variant, nmats, row, col):
    eye = (row == col).astype(F32)
    if variant == "sample":
        n2 = _mm_each(nmats, nmats)
        ps = [eye + n for n in nmats]
        return [p + t for p, t in zip(ps, _mm_each(ps, n2))]
    base_log2 = 3
    in_block = (row >> base_log2) == (col >> base_log2)
    ds = [jnp.where(in_block, n, 0.0) for n in nmats]
    ps = [eye + d for d in ds]
    for _ in range(base_log2 - 1):
        ds = _mm_each(ds, ds)
        ps = [p + t for p, t in zip(ps, _mm_each(ps, ds))]
    live_rows = N_META if variant == "meta" else TILE
    s = base_log2
    while (1 << s) < live_rows:
        size = 1 << s
        lower_left = (((row >> (s + 1)) == (col >> (s + 1))) & (((row >> s) & 1) == 1)
                      & (((col >> s) & 1) == 0))
        cs = [jnp.where(lower_left, n, 0.0) for n in nmats]
        lower = [slice(start + size, start + 2 * size) for start in range(0, TILE, 2 * size)]
        upper = [slice(start, start + size) for start in range(0, TILE, 2 * size)]
        p_low = [jnp.concatenate([p[rows, :] for rows in lower], 0) for p in ps]
        fix = _mm_each(_mm_each(p_low, cs), ps)
        merged = []
        for p, f in zip(ps, fix):
            pieces = []
            for i, (up, lo) in enumerate(zip(upper, lower)):
                pieces += [p[up, :], p[lo, :] + f[i * size:(i + 1) * size, :]]
            merged.append(jnp.concatenate(pieces, 0))
        ps = merged
        s += 1
    return ps


def _rotary(t, cosf, sins):
    return t * cosf + pltpu.roll(t, HEAD_DIM // 2, 1) * sins


def _ret_head_ops(valid, q, k, v, gate, cosf, sins, q_scale, k_scale):
    rq = _mask_rows(valid, _rotary(q, cosf, sins))
    rk = _mask_rows(valid, _rotary(k, cosf, sins) * (HEAD_DIM ** -0.5))
    return dict(rq=rq, rk=rk, rv=_mask_rows(valid, v), qd=rq * q_scale, kd=rk * k_scale, sgr=_silu(gate))


def _short_conv(x, halo, cw_ref, col0):
    rows, ncols = x.shape
    cols = slice(col0, col0 + ncols)
    n = rows // HALO
    full = jnp.concatenate([halo, x], 0).reshape(n + 1, HALO, ncols)
    sub = lax.broadcasted_iota(jnp.int32, (n, HALO, ncols), 1)
    conv = full[1:] * cw_ref[CONV_W - 1:CONV_W, cols]
    for shift in range(1, CONV_W):
        rolled = pltpu.roll(full, shift, 1)
        shifted = jnp.where(sub < shift, rolled[:-1], rolled[1:])
        conv = conv + shifted * cw_ref[CONV_W - 1 - shift:CONV_W - shift, cols]
    return _silu(conv).reshape(rows, ncols)


def _l2_normalize(t):
    return t * lax.rsqrt(jnp.sum(t * t, -1, keepdims=True) + RMS_EPS)


def _tokenwise(variant, pret, pgdn, cosf, sins, rsc, cw_ref, convpad):
    rows = pret.shape[0]
    heads = range(N_HEADS)
    valid = _valid_rows(variant, (rows, HEAD_DIM))
    ret = [_ret_head_ops(valid, *(_head_cols(pret, h * GROUP_W, g) for g in range(4)), cosf, sins,
                         _lane_bcast(rsc, h), _lane_bcast(rsc, N_HEADS + h)) for h in heads]
    x = pgdn[:, :CONV_COLS]
    if variant == "sample":
        x = jnp.where(_valid_rows(variant, x.shape), x, convpad)
    conv = _short_conv(x, jnp.zeros((HALO, CONV_COLS), F32), cw_ref, 0)
    gq = [_mask_rows(valid, _l2_normalize(_head_cols(conv, 0, h)) * (HEAD_DIM ** -0.5)) for h in heads]
    gk = [_mask_rows(valid, _l2_normalize(_head_cols(conv, GROUP_W, h))) for h in heads]
    gv = [_mask_rows(valid, _head_cols(conv, 2 * GROUP_W, h)) for h in heads]
    sgz = [_silu(_head_cols(pgdn, 3 * GROUP_W, h)) for h in heads]
    tok = {key: [r[key] for r in ret] for key in ret[0]}
    tok.update(gq=gq, gk=gk, gv=gv, sgz=sgz)
    return tok


def _chunk_prep(variant, toks, pabs, dmat_ref, alog, dtb):
    shape = (TILE, HEAD_DIM)
    tiles = range(len(toks))
    items = [(j, h) for j in tiles for h in range(N_HEADS)]
    pick = lambda key: [toks[j][key][h] for j, h in items]
    row = _iota2(shape, 0)
    col = _iota2(shape, 1)
    valid = _valid_rows(variant, shape)
    if variant == "sample":
        same = (row >> 3) == (col >> 3)
        incl = same & (row >= col)
        strict = same & (row > col)
    else:
        incl = row >= col
        strict = row > col

    incl01 = incl.astype(BF16)
    beta_all, gcum, gam, ktail, cdr, gcum_t = [], [], [], [], [], []
    for j in tiles:
        g_all = _mask_rows(valid, -jnp.exp(alog) * _softplus(pabs[j] + dtb))
        beta_all.append(_mask_rows(valid, _sigmoid(pabs[j])))
        if variant == "sample":
            both = _mm_split3(jnp.concatenate([incl01, same.astype(BF16)], 0), g_all)
            gc, gseg = both[:TILE], both[TILE:]
        else:
            gc = _mm_split3(incl01, g_all)
            gseg = jnp.broadcast_to(gc[TILE - 1:TILE, :], shape)
        gcum.append(gc)
        gam.append(jnp.exp(gc))
        ktail.append(jnp.exp(gseg - gc))
        cdr.append(jnp.exp(gseg))
        gcum_t.append(gc.T)

    gq, gk, gv = pick("gq"), pick("gk"), pick("gv")
    n_items = range(len(items))
    dec = [jnp.where(incl, jnp.exp(jnp.minimum(
        _lane_bcast(gcum[j], h) - jnp.broadcast_to(gcum_t[j][h:h + 1, :], shape), 0.0)), 0.0)
        for j, h in items]
    bcol = [_lane_bcast(beta_all[j], N_HEADS + h) for j, h in items]
    gamc = [_lane_bcast(gam[j], h) for j, h in items]
    kk = _mm_nt_each(gk, gk)
    qk = _mm_nt_each(gq, gk)
    nmats = [-(jnp.where(strict, kk[i] * dec[i], 0.0) * bcol[i]) for i in n_items]
    tinv = _tri_inverse(variant, nmats, row, col)
    rhs = [jnp.concatenate([gv[i] * bcol[i], gk[i] * (bcol[i] * gamc[i])], 1) for i in n_items]
    sol = _mm_each(tinv, rhs)
    ret = dict(
        items=items,
        scores=[qk_r * dmat_ref[h] for qk_r, (_, h) in zip(
            _mm_nt_each(pick("rq"), pick("rk")), items)],
        qd=pick("qd"), kdT=pick("kdT"), v=pick("rv"), gate=pick("sgr"),
    )
    gdn = dict(
        items=items,
        wv=[t[:, :HEAD_DIM] for t in sol],
        wk=[t[:, HEAD_DIM:] for t in sol],
        attn=[qk[i] * dec[i] for i in n_items],
        qg=[gq[i] * gamc[i] for i in n_items],
        ktT=[(gk[i] * _lane_bcast(ktail[j], h)).T for i, (j, h) in enumerate(items)],
        cdr=cdr,
        gate=pick("sgz"),
    )
    return ret, gdn


def _rms_gate(o, gate, weight=None):
    o = o * lax.rsqrt(jnp.mean(o * o, -1, keepdims=True) + RMS_EPS)
    if weight is not None:
        o = o * weight
    return (o * gate).astype(BF16)


_STATE_SHAPE = (N_HEADS, HEAD_DIM, HEAD_DIM)


def _seq_init(s0r_ref, s0g_ref, sr_ref, sg_ref):
    @pl.when(pl.program_id(1) == 0)
    def _():
        for j in range(sr_ref.shape[0]):
            sr_ref[j] = s0r_ref[...]
            sg_ref[j] = s0g_ref[...]


def _seq_step(ret_cd, ret, gdn, gnw, mix_ref, sr_out, sg_out, sr_ref, sg_ref):
    items = ret["items"]
    idx = range(len(items))
    s_r = [sr_ref[j, h] for j, h in items]
    s_g = [sg_ref[j, h] for j, h in items]
    bf = lambda t: t.astype(BF16)
    cat = lambda a, b, axis: jnp.concatenate([bf(a), bf(b)], axis)
    wk_s = _mm_each(gdn["wk"], s_g)
    o_r = _mm_each([cat(ret["scores"][i], ret["qd"][i], 1) for i in idx],
                   [cat(ret["v"][i], s_r[i], 0) for i in idx])
    u = [gdn["wv"][i] - wk_s[i] for i in idx]
    o_g = _mm_each([cat(gdn["qg"][i], gdn["attn"][i], 1) for i in idx],
                   [cat(s_g[i], u[i], 0) for i in idx])
    upd_g = _mm_each(gdn["ktT"], u)
    upd_r = _mm_each(ret["kdT"], ret["v"])
    for i, (j, h) in enumerate(items):
        cd = jnp.broadcast_to(gdn["cdr"][j][0:1, h:h + 1], (HEAD_DIM, HEAD_DIM))
        sg_ref[j, h] = cd * s_g[i] + upd_g[i]
        sr_ref[j, h] = ret_cd[h] * s_r[i] + upd_r[i]
    for i, (j, h) in enumerate(items):
        mix_ref[j, :, h * HEAD_DIM:(h + 1) * HEAD_DIM] = _rms_gate(o_r[i], ret["gate"][i])
        mix_ref[j, :, GROUP_W + h * HEAD_DIM:GROUP_W + (h + 1) * HEAD_DIM] = _rms_gate(
            o_g[i], gdn["gate"][i], gnw)

    @pl.when(pl.program_id(1) == pl.num_programs(1) - 1)
    def _():
        sr_out[...] = sr_ref[...]
        sg_out[...] = sg_ref[...]


def _seq_out(n_batch, n_chunks, nb):
    state = pl.BlockSpec((nb,) + _STATE_SHAPE, lambda b, n: (b, 0, 0, 0))
    out_specs = [pl.BlockSpec((nb, TILE, 2 * GROUP_W), lambda b, n: (b, n, 0)), state, state]
    out_shape = [
        jax.ShapeDtypeStruct((n_batch, n_chunks * TILE, 2 * GROUP_W), BF16),
        jax.ShapeDtypeStruct((n_batch,) + _STATE_SHAPE, F32),
        jax.ShapeDtypeStruct((n_batch,) + _STATE_SHAPE, F32),
    ]
    scratch = [pltpu.VMEM((nb,) + _STATE_SHAPE, F32), pltpu.VMEM((nb,) + _STATE_SHAPE, F32)]
    return out_specs, out_shape, scratch


def _project_tile(x_ref, g_ref, b_ref, w_ref):
    hidden = _layer_norm(x_ref[...], g_ref[...], b_ref[...]).astype(BF16)
    project = lambda lo, hi: jnp.dot(hidden, w_ref[:, lo:hi], preferred_element_type=F32)
    return (project(0, 4 * GROUP_W), project(4 * GROUP_W, 8 * GROUP_W),
            project(8 * GROUP_W, 8 * GROUP_W + HEAD_DIM))


def _meta_mixer_body(ret_cd, x_ref, eg_ref, eb_ref, w_ref, cos_ref, sin_ref, rsc_ref, dmat_ref,
                     cw_ref, alog_ref, dtb_ref, gnw_ref, s0r_ref, s0g_ref,
                     mix_ref, sr_out, sg_out, preconv_ref, sr_ref, sg_ref):
    _seq_init(s0r_ref, s0g_ref, sr_ref, sg_ref)
    pret, pgdn, pab = _project_tile(x_ref, eg_ref, eb_ref, w_ref)
    preconv_ref[...] = pgdn[:, :CONV_COLS]
    tok = _tokenwise("meta", pret, pgdn, cos_ref[...], sin_ref[...], rsc_ref[...], cw_ref, None)
    tok["kdT"] = [t.T for t in tok["kd"]]
    ret, gdn = _chunk_prep("meta", [tok], [pab], dmat_ref, alog_ref[...], dtb_ref[...])
    _seq_step(ret_cd, ret, gdn, gnw_ref[...], mix_ref, sr_out, sg_out, sr_ref, sg_ref)


def _meta_mixer(ret_cd, x, ln_g, ln_b, w_all, cosf, sins, rsc, dmat, cw, alog, dtb, gnw, s0r, s0g):
    c2 = lambda b, n: (0, 0)
    c3 = lambda b, n: (0, 0, 0)
    out_specs, out_shape, state_scratch = _seq_out(1, 1, 1)
    return pl.pallas_call(
        functools.partial(_meta_mixer_body, ret_cd),
        grid=(1, 1),
        in_specs=[
            pl.BlockSpec((TILE, D_MODEL), c2),
            pl.BlockSpec((1, D_MODEL), c2),
            pl.BlockSpec((1, D_MODEL), c2),
            pl.BlockSpec((D_MODEL, W_ALL_COLS), c2),
            pl.BlockSpec((TILE, HEAD_DIM), c2),
            pl.BlockSpec((TILE, HEAD_DIM), c2),
            pl.BlockSpec((TILE, HEAD_DIM), c2),
            pl.BlockSpec((N_HEADS, TILE, TILE), c3),
            pl.BlockSpec((CONV_W, CONV_COLS), c2),
            pl.BlockSpec((1, HEAD_DIM), c2),
            pl.BlockSpec((1, HEAD_DIM), c2),
            pl.BlockSpec((1, HEAD_DIM), c2),
            pl.BlockSpec(_STATE_SHAPE, c3),
            pl.BlockSpec(_STATE_SHAPE, c3),
        ],
        out_specs=out_specs + [pl.BlockSpec((TILE, CONV_COLS), c2)],
        out_shape=out_shape + [jax.ShapeDtypeStruct((TILE, CONV_COLS), F32)],
        scratch_shapes=state_scratch,
        compiler_params=_cparams(("parallel", "arbitrary")),
        name="mixer_meta",
    )(x, ln_g, ln_b, w_all, cosf, sins, rsc, dmat, cw, alog, dtb, gnw, s0r, s0g)


def _prompt_fused_body(ret_cd, x_ref, g_ref, b_ref, w_ref, cos_ref, sin_ref, rsc_ref, cw_ref, halo0_ref,
                       dmat_ref, alog_ref, dtb_ref, gnw_ref, s0r_ref, s0g_ref,
                       mix_ref, sr_out, sg_out, tail_ref, sr_ref, sg_ref, halo_ref):
    nb = x_ref.shape[0]
    seqs, heads = range(nb), range(N_HEADS)
    _seq_init(s0r_ref, s0g_ref, sr_ref, sg_ref)

    @pl.when(pl.program_id(1) == 0)
    def _():
        for j in seqs:
            halo_ref[j] = halo0_ref[...]

    hidden = _layer_norm(x_ref[...].reshape(nb * TILE, D_MODEL), g_ref[...], b_ref[...]).astype(BF16)
    cosf, sins, rsc = cos_ref[...], sin_ref[...], rsc_ref[...]
    keys = ("rq", "rk", "rv", "qd", "kdT", "sgr", "gq", "gk", "gv", "sgz")
    toks = [{key: [None] * N_HEADS for key in keys} for _ in seqs]
    pabs = [None] * nb
    rows = lambda j: slice(j * TILE, (j + 1) * TILE)

    def project(group, width=GROUP_W):
        return jnp.dot(hidden, w_ref[:, group * GROUP_W:group * GROUP_W + width], preferred_element_type=F32)

    def conv_group(group, key):
        val = project(N_HEADS + group)
        cols = slice(group * GROUP_W, (group + 1) * GROUP_W)
        for j in seqs:
            part = val[rows(j), :]
            conv = _short_conv(part, halo_ref[j, :, cols], cw_ref, group * GROUP_W)
            halo_ref[j, :, cols] = part[TILE - HALO:, :]
            for h in heads:
                t = _head_cols(conv, 0, h)
                if key == "gq":
                    t = _l2_normalize(t) * (HEAD_DIM ** -0.5)
                elif key == "gk":
                    t = _l2_normalize(t)
                toks[j][key][h] = t

    def ab_group():
        val = project(2 * N_HEADS, HEAD_DIM)
        for j in seqs:
            pabs[j] = val[rows(j), :]

    def ret_head(h):
        val = project(h)
        for j in seqs:
            part = val[rows(j), :]
            ops = _ret_head_ops(None, *(_head_cols(part, 0, g) for g in range(4)), cosf, sins,
                                _lane_bcast(rsc, h), _lane_bcast(rsc, N_HEADS + h))
            for key in ("rq", "rk", "rv", "qd", "sgr"):
                toks[j][key][h] = ops[key]
            toks[j]["kdT"][h] = ops["kd"].T

    def gate_group():
        val = project(2 * N_HEADS - 1)
        for j in seqs:
            for h in heads:
                toks[j]["sgz"][h] = _silu(_head_cols(val[rows(j), :], 0, h))

    ab_group()
    conv_group(1, "gk")
    conv_group(0, "gq")
    conv_group(2, "gv")
    for j in seqs:
        tail_ref[j] = halo_ref[j]

    gate_group()
    for h in heads:
        ret_head(h)

    ret, gdn = _chunk_prep("prompt", toks, pabs, dmat_ref, alog_ref[...], dtb_ref[...])
    _seq_step(ret_cd, ret, gdn, gnw_ref[...], mix_ref, sr_out, sg_out, sr_ref, sg_ref)


def _prompt_fused(ret_cd, x, ln_g, ln_b, w_all, cosf, sins, rsc, cw, halo0, dmat, alog, dtb, gnw, s0r, s0g, nb):
    n_batch, seq, _ = x.shape
    n_chunks = seq // TILE
    tile = lambda b, n: (b, n, 0)
    pos = lambda b, n: (n, 0)
    c2 = lambda b, n: (0, 0)
    c3 = lambda b, n: (0, 0, 0)
    out_specs, out_shape, state_scratch = _seq_out(n_batch, n_chunks, nb)
    return pl.pallas_call(
        functools.partial(_prompt_fused_body, ret_cd),
        grid=(n_batch // nb, n_chunks),
        in_specs=[
            pl.BlockSpec((nb, TILE, D_MODEL), tile),
            pl.BlockSpec((1, D_MODEL), c2),
            pl.BlockSpec((1, D_MODEL), c2),
            pl.BlockSpec((D_MODEL, W_ALL_COLS), c2, pipeline_mode=pl.Buffered(1)),
            pl.BlockSpec((TILE, HEAD_DIM), pos),
            pl.BlockSpec((TILE, HEAD_DIM), pos),
            pl.BlockSpec((TILE, HEAD_DIM), c2),
            pl.BlockSpec((CONV_W, CONV_COLS), c2),
            pl.BlockSpec((HALO, CONV_COLS), c2),
            pl.BlockSpec((N_HEADS, TILE, TILE), c3),
            pl.BlockSpec((1, HEAD_DIM), c2),
            pl.BlockSpec((1, HEAD_DIM), c2),
            pl.BlockSpec((1, HEAD_DIM), c2),
            pl.BlockSpec(_STATE_SHAPE, c3),
            pl.BlockSpec(_STATE_SHAPE, c3),
        ],
        out_specs=out_specs + [pl.BlockSpec((nb, HALO, CONV_COLS), lambda b, n: (b, 0, 0))],
        out_shape=out_shape + [jax.ShapeDtypeStruct((n_batch, HALO, CONV_COLS), F32)],
        scratch_shapes=state_scratch + [pltpu.VMEM((nb, HALO, CONV_COLS), F32)],
        compiler_params=_cparams(("parallel", "arbitrary")),
        name="prompt_fused",
    )(x, ln_g, ln_b, w_all, cosf, sins, rsc, cw, halo0, dmat, alog, dtb, gnw, s0r, s0g)


SAMPLE_PER_TILE = TILE // SAMPLE_GROUP


def _stack_by_group(xt):
    shape3 = (SAMPLE_PER_TILE, HEAD_DIM, TILE)
    keep = lax.broadcasted_iota(jnp.int32, shape3, 0) == (lax.broadcasted_iota(jnp.int32, shape3, 2) >> 3)
    stacked = jnp.where(keep, jnp.broadcast_to(xt[None], shape3), 0.0)
    return stacked.reshape(SAMPLE_PER_TILE * HEAD_DIM, TILE)


def _sample_mixer_body(ret_cd, x_ref, eg_ref, eb_ref, w_ref, cos_ref, sin_ref, rsc_ref, dmat_ref, cw_ref,
                       alog_ref, dtb_ref, gnw_ref, convpad_ref, sr_in, sg_in,
                       mix_ref, sr_out, sg_out, preconv_ref,
                       qd_s, wk_s, qg_s, wv_s, cd_s, inter_s, u_s, qs_s):
    heads = range(N_HEADS)
    pret, pgdn, pab = _project_tile(x_ref, eg_ref, eb_ref, w_ref)
    preconv_ref[...] = pgdn[:, :CONV_COLS]
    tok = _tokenwise("sample", pret, pgdn, cos_ref[...], sin_ref[...], rsc_ref[...],
                     cw_ref, convpad_ref[...])
    tok["kdT"] = [t.T for t in tok["kd"]]
    ret, gdn = _chunk_prep("sample", [tok], [pab], dmat_ref, alog_ref[...], dtb_ref[...])
    for h in heads:
        qd_s[h] = ret["qd"][h]
        wk_s[h] = gdn["wk"][h]
        qg_s[h] = gdn["qg"][h]
        wv_s[h] = gdn["wv"][h]
    cd_s[...] = gdn["cdr"][0]

    def per_batch(b, carry):
        rows = pl.ds(pl.multiple_of(b * SAMPLE_GROUP, SAMPLE_GROUP), SAMPLE_GROUP)
        cd_rows = cd_s[rows, :]
        for h in heads:
            inter_s[h, rows, :] = _mm(qd_s[h, rows, :], sr_in[b, h])
            s = sg_in[b, h]
            both = _mm(jnp.concatenate([wk_s[h, rows, :], qg_s[h, rows, :]], 0), s)
            u_s[h, rows, :] = wv_s[h, rows, :] - both[:SAMPLE_GROUP]
            qs_s[h, rows, :] = both[SAMPLE_GROUP:]
            cd = jnp.broadcast_to(cd_rows[SAMPLE_GROUP - 1:SAMPLE_GROUP, h:h + 1], (HEAD_DIM, HEAD_DIM))
            sg_out[b, h] = cd * s
        return carry

    lax.fori_loop(0, SAMPLE_PER_TILE, per_batch, 0)

    out_rows = SAMPLE_PER_TILE * SAMPLE_TOKENS
    out_row = _iota2((out_rows, TILE), 0)
    token_row = ((out_row >> 2) << 3) + (SAMPLE_GROUP - SAMPLE_TOKENS) + (out_row & (SAMPLE_TOKENS - 1))
    select = (_iota2((out_rows, TILE), 1) == token_row).astype(BF16)
    compact = lambda t: jnp.dot(select, t, preferred_element_type=F32).astype(BF16)

    state_shape = (SAMPLE_PER_TILE, HEAD_DIM, HEAD_DIM)
    for h in heads:
        o = _mm(ret["scores"][h], ret["v"][h]) + inter_s[h]
        upd = _mm(_stack_by_group(ret["kdT"][h]), ret["v"][h]).reshape(state_shape)
        sr_out[:, h] = ret_cd[h] * sr_in[:, h] + upd
        mix_ref[:, h * HEAD_DIM:(h + 1) * HEAD_DIM] = compact(_rms_gate(o, ret["gate"][h]))

    for h in heads:
        u = u_s[h]
        o = qs_s[h] + _mm(gdn["attn"][h], u)
        upd = _mm(_stack_by_group(gdn["ktT"][h]), u).reshape(state_shape)
        sg_out[:, h] = sg_out[:, h] + upd
        mix_ref[:, GROUP_W + h * HEAD_DIM:GROUP_W + (h + 1) * HEAD_DIM] = compact(_rms_gate(
            o, gdn["gate"][h], gnw_ref[...]))


def _sample_mixer(ret_cd, x, ln_g, ln_b, w_all, cosf, sins, rsc, dmat, cw, alog, dtb, gnw, convpad, sr, sg):
    n_tiles = x.shape[0] // TILE
    tile = lambda i: (i, 0)
    c2 = lambda i: (0, 0)
    c3 = lambda i: (0, 0, 0)
    state = pl.BlockSpec((SAMPLE_PER_TILE,) + _STATE_SHAPE, lambda i: (i, 0, 0, 0))
    head_scratch = pltpu.VMEM((N_HEADS, TILE, HEAD_DIM), F32)
    return pl.pallas_call(
        functools.partial(_sample_mixer_body, ret_cd),
        grid=(n_tiles,),
        in_specs=[
            pl.BlockSpec((TILE, D_MODEL), tile),
            pl.BlockSpec((1, D_MODEL), c2),
            pl.BlockSpec((1, D_MODEL), c2),
            pl.BlockSpec((D_MODEL, W_ALL_COLS), c2, pipeline_mode=pl.Buffered(1)),
            pl.BlockSpec((TILE, HEAD_DIM), c2),
            pl.BlockSpec((TILE, HEAD_DIM), c2),
            pl.BlockSpec((TILE, HEAD_DIM), c2),
            pl.BlockSpec((N_HEADS, TILE, TILE), c3),
            pl.BlockSpec((CONV_W, CONV_COLS), c2),
            pl.BlockSpec((1, HEAD_DIM), c2),
            pl.BlockSpec((1, HEAD_DIM), c2),
            pl.BlockSpec((1, HEAD_DIM), c2),
            pl.BlockSpec((TILE, CONV_COLS), tile),
            state, state,
        ],
        out_specs=[pl.BlockSpec((SAMPLE_PER_TILE * SAMPLE_TOKENS, 2 * GROUP_W), tile), state, state,
                   pl.BlockSpec((TILE, CONV_COLS), tile)],
        out_shape=[
            jax.ShapeDtypeStruct((n_tiles * SAMPLE_PER_TILE * SAMPLE_TOKENS, 2 * GROUP_W), BF16),
            jax.ShapeDtypeStruct(sr.shape, F32),
            jax.ShapeDtypeStruct(sg.shape, F32),
            jax.ShapeDtypeStruct((x.shape[0], CONV_COLS), F32),
        ],
        scratch_shapes=[head_scratch] * 4 + [pltpu.VMEM((TILE, HEAD_DIM), F32)] + [head_scratch] * 3,
        compiler_params=_cparams(("parallel",)),
        name="mixer_sample",
    )(x, ln_g, ln_b, w_all, cosf, sins, rsc, dmat, cw, alog, dtb, gnw, convpad, sr, sg)


def _rotary_tables(pos):
    half = HEAD_DIM // 2
    inv = ROPE_BASE ** (-np.arange(half, dtype=np.float64) / half)
    ang = np.asarray(pos, np.float64)[:, None] * inv[None, :]
    cos, sin = np.cos(ang), np.sin(ang)
    return (jnp.asarray(np.concatenate([cos, cos], -1), F32),
            jnp.asarray(np.concatenate([-sin, sin], -1), F32))


def _retention_tables(seg, pos, valid, seg_len):
    gamma = 1.0 - 2.0 ** (-5.0 - np.arange(N_HEADS, dtype=np.float64))
    posf = np.asarray(pos, np.float64)
    rel = posf[:, None] - posf[None, :]
    causal = (seg[:, None] == seg[None, :]) & (rel >= 0)
    dmat = np.where(causal[None], gamma[:, None, None] ** np.where(causal, rel, 0.0)[None], 0.0)
    q_scale = gamma[None, :] ** (posf[:, None] + 1.0)
    k_scale = np.where(valid[:, None], gamma[None, :] ** (seg_len - 1.0 - posf[:, None]), 0.0)
    rsc = np.concatenate([q_scale, k_scale, np.zeros((TILE, HEAD_DIM - 2 * N_HEADS))], -1)
    chunk_decay = tuple(float(g ** seg_len) for g in gamma)
    return jnp.asarray(dmat, F32), jnp.asarray(rsc, F32), chunk_decay


def _pad_lanes(v):
    return jnp.pad(v.astype(F32), (0, HEAD_DIM - v.shape[0]))[None, :]


def kernel(x_prompt, x_sample, state_ret, state_gdn, state_conv, meta_tokens, emb_ln_g, emb_ln_b,
           w_in, conv_w, a_log, dt_bias, gdn_norm_w, w_out, ln1_g, ln1_b, w_gate_up, w_down,
           ln2_g, ln2_b):
    n_batch, seq, _ = x_prompt.shape
    dec_batch, dec_seq, _ = x_sample.shape
    assert seq % TILE == 0 and dec_seq == SAMPLE_TOKENS and N_META <= TILE
    assert n_batch % MIXER_SEQS_PER_STEP == 0
    layer = 0

    w_all = _weight_prep(jnp.swapaxes(w_in[layer], 0, 1))
    w_out_b = w_out[layer].astype(BF16)
    w_gu_b = w_gate_up[layer].astype(BF16)
    w_down_b = w_down[layer].astype(BF16)
    row = lambda v: v.astype(F32)[None, :]
    eg, eb = row(emb_ln_g), row(emb_ln_b)
    cw = conv_w[layer].astype(F32)
    alog, dtb, gnw = _pad_lanes(a_log[layer]), _pad_lanes(dt_bias[layer]), row(gdn_norm_w[layer])

    xp = x_prompt.reshape(n_batch * seq, D_MODEL)
    xm = jnp.pad(meta_tokens.astype(F32), ((0, TILE - N_META), (0, 0)))
    xs = jnp.pad(x_sample, ((0, 0), (SAMPLE_GROUP - dec_seq, 0), (0, 0))).reshape(
        dec_batch * SAMPLE_GROUP, D_MODEL)
    convpad = jnp.pad(state_conv[layer].astype(F32), ((0, 0), (1, SAMPLE_GROUP - CONV_W), (0, 0))).reshape(
        dec_batch * SAMPLE_GROUP, CONV_COLS)

    tile_idx = np.arange(TILE)
    cos_p, sin_p = _rotary_tables(N_META + np.arange(seq))
    cos_m, sin_m = _rotary_tables(tile_idx)
    tok = (tile_idx % SAMPLE_GROUP) - (SAMPLE_GROUP - dec_seq)
    cos_s, sin_s = _rotary_tables(PAST_LEN + np.maximum(tok, 0))
    zeros_i = np.zeros((TILE,), np.int32)
    all_valid = np.ones((TILE,), bool)
    dmat_p, rsc_p, cd_p = _retention_tables(zeros_i, tile_idx, all_valid, float(TILE))
    dmat_m, rsc_m, cd_m = _retention_tables(zeros_i, tile_idx, tile_idx < N_META, float(N_META))
    dmat_s, rsc_s, cd_s = _retention_tables(tile_idx // SAMPLE_GROUP, tok, tok >= 0, float(dec_seq))

    zero_state = jnp.zeros(_STATE_SHAPE, F32)
    _, sr_m, sg_m, preconv_m = _meta_mixer(cd_m, xm, eg, eb, w_all, cos_m, sin_m, rsc_m, dmat_m, cw, alog, dtb,
                                           gnw, zero_state, zero_state)
    halo_p = preconv_m[N_META - HALO:N_META]

    mix_p, sr_p, sg_p, tail_p = _prompt_fused(cd_p, x_prompt, eg, eb, w_all, cos_p, sin_p, rsc_p, cw, halo_p,
                                              dmat_p, alog, dtb, gnw, sr_m[0], sg_m[0], MIXER_SEQS_PER_STEP)
    mix_p = mix_p.reshape(n_batch * seq, 2 * GROUP_W)

    mix_s, sr_s, sg_s, preconv_s = _sample_mixer(cd_s, xs, eg, eb, w_all, cos_s, sin_s, rsc_s, dmat_s, cw,
                                                 alog, dtb, gnw, convpad, state_ret[layer].astype(F32),
                                                 state_gdn[layer].astype(F32))

    back = functools.partial(_back, eg=eg, eb=eb, w_out=w_out_b, g1=row(ln1_g[layer]), b1=row(ln1_b[layer]),
                             w_gu=w_gu_b, w_down=w_down_b, g2=row(ln2_g[layer]), b2=row(ln2_b[layer]),
                             tm=512)
    y_p = back(xp, mix_p).reshape(n_batch, seq, D_MODEL)
    y_s = back(x_sample.reshape(dec_batch * dec_seq, D_MODEL), mix_s).reshape(dec_batch, dec_seq, D_MODEL)

    conv_p = tail_p[:, HALO - (CONV_W - 1):, :]
    conv_s = preconv_s.reshape(dec_batch, SAMPLE_GROUP, CONV_COLS)[:, SAMPLE_GROUP - (CONV_W - 1):, :]
    return (y_p, y_s, sr_p[None], sg_p[None], conv_p[None], sr_s[None], sg_s[None], conv_s[None])
```

```python
import functools

import numpy as np
import jax
import jax.numpy as jnp
from jax import lax
from jax.experimental import pallas as pl
from jax.experimental.pallas import tpu as pltpu

F32 = jnp.float32
BF16 = jnp.bfloat16

D_MODEL = 1024
N_META = 16
N_HEADS = 4
HEAD_DIM = 128
GROUP_W = N_HEADS * HEAD_DIM
CONV_W = 4
CONV_COLS = 3 * GROUP_W
D_FF = 2816
PAST_LEN = 16384
ROPE_BASE = 10000.0
LN_EPS = 1e-5
RMS_EPS = 1e-6
ALPHA = 2.0 ** 0.25
TILE = 128
SAMPLE_GROUP = 8
SAMPLE_TOKENS = 4
HALO = 8
MIXER_SEQS_PER_STEP = 4
BACK_PARTS = 2
W_ALL_COLS = 9 * GROUP_W
AB_ROWS = 2 * N_HEADS
VMEM_LIMIT = 56 * 1024 * 1024


def _cparams(sem):
    return pltpu.CompilerParams(dimension_semantics=sem, vmem_limit_bytes=VMEM_LIMIT)


def _layer_norm(x, g, b):
    mu = jnp.mean(x, -1, keepdims=True)
    xc = x - mu
    var = jnp.mean(xc * xc, -1, keepdims=True)
    return xc * lax.rsqrt(var + LN_EPS) * g + b


def _sigmoid(x):
    return 1.0 / (1.0 + jnp.exp(-x))


def _silu(x):
    return x * _sigmoid(x)


def _softplus(x):
    return jnp.maximum(x, 0.0) + jnp.log(1.0 + jnp.exp(-jnp.abs(x)))


def _mm(a, b):
    return jnp.dot(a.astype(BF16), b.astype(BF16), preferred_element_type=F32)


def _mm_nt(a, b):
    return lax.dot_general(a.astype(BF16), b.astype(BF16), (((1,), (1,)), ((), ())),
                           preferred_element_type=F32)


def _mm_each(xs, ys):
    return [_mm(x, y) for x, y in zip(xs, ys)]


def _mm_nt_each(xs, ys):
    return [_mm_nt(x, y) for x, y in zip(xs, ys)]


def _mm_split3(m01, x):
    x1 = x.astype(BF16)
    r1 = x - x1.astype(F32)
    x2 = r1.astype(BF16)
    x3 = (r1 - x2.astype(F32)).astype(BF16)
    dot = functools.partial(jnp.dot, preferred_element_type=F32)
    return (dot(m01, x3) + dot(m01, x2)) + dot(m01, x1)


def _lane_bcast(x, lane):
    return jnp.broadcast_to(x[:, lane:lane + 1], (x.shape[0], HEAD_DIM))


def _head_cols(x, base, h):
    return x[:, base + h * HEAD_DIM:base + (h + 1) * HEAD_DIM]


def _weight_prep_body(q_ref, k_ref, v_ref, g_ref, ab_ref, o_ref):
    j = pl.program_id(0)

    @pl.when(j < 2 * N_HEADS)
    def _():
        for piece, ref in enumerate((q_ref, k_ref, v_ref, g_ref)):
            o_ref[:, piece * HEAD_DIM:(piece + 1) * HEAD_DIM] = ref[...].T.astype(BF16)

    @pl.when(j == 2 * N_HEADS)
    def _():
        ab = jnp.concatenate([ab_ref[...], jnp.zeros((HEAD_DIM - AB_ROWS, D_MODEL), F32)], 0)
        o_ref[:, :HEAD_DIM] = ab.T.astype(BF16)
        o_ref[:, HEAD_DIM:] = jnp.zeros((D_MODEL, GROUP_W - HEAD_DIM), BF16)


def _weight_prep(w_in_t):
    assert w_in_t.shape == (8 * GROUP_W + AB_ROWS, D_MODEL)

    def piece_spec(piece):
        def index(j):
            ret_block = piece * N_HEADS + j
            gdn_block = N_HEADS * j + piece
            return (jnp.where(j < N_HEADS, ret_block, jnp.where(j < 2 * N_HEADS, gdn_block, 0)), 0)
        return pl.BlockSpec((HEAD_DIM, D_MODEL), index)

    ab_spec = pl.BlockSpec((AB_ROWS, D_MODEL), lambda j: (8 * GROUP_W // AB_ROWS, 0))
    return pl.pallas_call(
        _weight_prep_body,
        grid=(2 * N_HEADS + 1,),
        in_specs=[piece_spec(p) for p in range(4)] + [ab_spec],
        out_specs=pl.BlockSpec((D_MODEL, GROUP_W), lambda j: (0, j)),
        out_shape=jax.ShapeDtypeStruct((D_MODEL, W_ALL_COLS), BF16),
        compiler_params=_cparams(("arbitrary",)),
        name="weight_prep",
    )(w_in_t, w_in_t, w_in_t, w_in_t, w_in_t)


def _back_body(x_ref, mix_ref, eg_ref, eb_ref, wo_ref, g1_ref, b1_ref, wgu_ref, wd_ref,
               g2_ref, b2_ref, y_ref):
    part = x_ref.shape[0] // BACK_PARTS
    parts = [slice(i * part, (i + 1) * part) for i in range(BACK_PARTS)]
    dot = functools.partial(jnp.dot, preferred_element_type=F32)
    h = [_layer_norm(x_ref[p, :], eg_ref[...], eb_ref[...]) for p in parts]
    mp = [dot(mix_ref[p, :], wo_ref[...]) for p in parts]
    h1 = [_layer_norm(ALPHA * a + b, g1_ref[...], b1_ref[...]) for a, b in zip(h, mp)]
    gu = [dot(a.astype(BF16), wgu_ref[...]) for a in h1]
    act = [(_silu(a[:, :D_FF]) * a[:, D_FF:]).astype(BF16) for a in gu]
    ff = [dot(a, wd_ref[...]) for a in act]
    for p, a, b in zip(parts, h1, ff):
        y_ref[p, :] = _layer_norm(ALPHA * a + b, g2_ref[...], b2_ref[...])


def _back(x2d, mix, eg, eb, w_out, g1, b1, w_gu, w_down, g2, b2, tm):
    rows = x2d.shape[0]
    const = lambda i: (0, 0)
    tile = lambda i: (i, 0)
    single = pl.Buffered(1)
    vec = pl.BlockSpec((1, D_MODEL), const)
    return pl.pallas_call(
        _back_body,
        grid=(rows // tm,),
        in_specs=[
            pl.BlockSpec((tm, D_MODEL), tile),
            pl.BlockSpec((tm, D_MODEL), tile),
            vec, vec,
            pl.BlockSpec((D_MODEL, D_MODEL), const, pipeline_mode=single),
            vec, vec,
            pl.BlockSpec((D_MODEL, 2 * D_FF), const, pipeline_mode=single),
            pl.BlockSpec((D_FF, D_MODEL), const, pipeline_mode=single),
            vec, vec,
        ],
        out_specs=pl.BlockSpec((tm, D_MODEL), tile),
        out_shape=jax.ShapeDtypeStruct((rows, D_MODEL), F32),
        compiler_params=_cparams(("parallel",)),
        name="back",
    )(x2d, mix, eg, eb, w_out, g1, b1, w_gu, w_down, g2, b2)


def _iota2(shape, dim):
    return lax.broadcasted_iota(jnp.int32, shape, dim)


def _valid_rows(variant, shape):
    r = _iota2(shape, 0)
    if variant == "meta":
        return r < N_META
    if variant == "sample":
        return (r & (SAMPLE_GROUP - 1)) >= SAMPLE_GROUP - SAMPLE_TOKENS
    return None


def _mask_rows(valid, x):
    return x if valid is None else jnp.where(valid, x, 0.0)


def _tri_inverse(variant, nmats, row, col):
    eye = (row == col).astype(F32)
    if variant == "sample":
        n2 = _mm_each(nmats, nmats)
        ps = [eye + n for n in nmats]
        return [p + t for p, t in zip(ps, _mm_each(ps, n2))]
    base_log2 = 3
    in_block = (row >> base_log2) == (col >> base_log2)
    ds = [jnp.where(in_block, n, 0.0) for n in nmats]
    ps = [eye + d for d in ds]
    for _ in range(base_log2 - 1):
        ds = _mm_each(ds, ds)
        ps = [p + t for p, t in zip(ps, _mm_each(ps, ds))]
    live_rows = N_META if variant == "meta" else TILE
    s = base_log2
    while (1 << s) < live_rows:
        size = 1 << s
        lower_left = (((row >> (s + 1)) == (col >> (s + 1))) & (((row >> s) & 1) == 1)
                      & (((col >> s) & 1) == 0))
        cs = [jnp.where(lower_left, n, 0.0) for n in nmats]
        lower = [slice(start + size, start + 2 * size) for start in range(0, TILE, 2 * size)]
        upper = [slice(start, start + size) for start in range(0, TILE, 2 * size)]
        p_low = [jnp.concatenate([p[rows, :] for rows in lower], 0) for p in ps]
        fix = _mm_each(_mm_each(p_low, cs), ps)
        merged = []
        for p, f in zip(ps, fix):
            pieces = []
            for i, (up, lo) in enumerate(zip(upper, lower)):
                pieces += [p[up, :], p[lo, :] + f[i * size:(i + 1) * size, :]]
            merged.append(jnp.concatenate(pieces, 0))
        ps = merged
        s += 1
    return ps


def _rotary(t, cosf, sins):
    return t * cosf + pltpu.roll(t, HEAD_DIM // 2, 1) * sins


def _ret_head_ops(valid, q, k, v, gate, cosf, sins, q_scale, k_scale):
    rq = _mask_rows(valid, _rotary(q, cosf, sins))
    rk = _mask_rows(valid, _rotary(k, cosf, sins) * (HEAD_DIM ** -0.5))
    return dict(rq=rq, rk=rk, rv=_mask_rows(valid, v), qd=rq * q_scale, kd=rk * k_scale, sgr=_silu(gate))


def _short_conv(x, halo, cw_ref, col0):
    rows, ncols = x.shape
    cols = slice(col0, col0 + ncols)
    n = rows // HALO
    full = jnp.concatenate([halo, x], 0).reshape(n + 1, HALO, ncols)
    sub = lax.broadcasted_iota(jnp.int32, (n, HALO, ncols), 1)
    conv = full[1:] * cw_ref[CONV_W - 1:CONV_W, cols]
    for shift in range(1, CONV_W):
        rolled = pltpu.roll(full, shift, 1)
        shifted = jnp.where(sub < shift, rolled[:-1], rolled[1:])
        conv = conv + shifted * cw_ref[CONV_W - 1 - shift:CONV_W - shift, cols]
    return _silu(conv).reshape(rows, ncols)


def _l2_normalize(t):
    return t * lax.rsqrt(jnp.sum(t * t, -1, keepdims=True) + RMS_EPS)


def _tokenwise(variant, pret, pgdn, cosf, sins, rsc, cw_ref, convpad):
    rows = pret.shape[0]
    heads = range(N_HEADS)
    valid = _valid_rows(variant, (rows, HEAD_DIM))
    ret = [_ret_head_ops(valid, *(_head_cols(pret, h * GROUP_W, g) for g in range(4)), cosf, sins,
                         _lane_bcast(rsc, h), _lane_bcast(rsc, N_HEADS + h)) for h in heads]
    x = pgdn[:, :CONV_COLS]
    if variant == "sample":
        x = jnp.where(_valid_rows(variant, x.shape), x, convpad)
    conv = _short_conv(x, jnp.zeros((HALO, CONV_COLS), F32), cw_ref, 0)
    gq = [_mask_rows(valid, _l2_normalize(_head_cols(conv, 0, h)) * (HEAD_DIM ** -0.5)) for h in heads]
    gk = [_mask_rows(valid, _l2_normalize(_head_cols(conv, GROUP_W, h))) for h in heads]
    gv = [_mask_rows(valid, _head_cols(conv, 2 * GROUP_W, h)) for h in heads]
    sgz = [_silu(_head_cols(pgdn, 3 * GROUP_W, h)) for h in heads]
    tok = {key: [r[key] for r in ret] for key in ret[0]}
    tok.update(gq=gq, gk=gk, gv=gv, sgz=sgz)
    return tok


def _chunk_prep(variant, toks, pabs, dmat_ref, alog, dtb):
    shape = (TILE, HEAD_DIM)
    tiles = range(len(toks))
    items = [(j, h) for j in tiles for h in range(N_HEADS)]
    pick = lambda key: [toks[j][key][h] for j, h in items]
    row = _iota2(shape, 0)
    col = _iota2(shape, 1)
    valid = _valid_rows(variant, shape)
    if variant == "sample":
        same = (row >> 3) == (col >> 3)
        incl = same & (row >= col)
        strict = same & (row > col)
    else:
        incl = row >= col
        strict = row > col

    incl01 = incl.astype(BF16)
    beta_all, gcum, gam, ktail, cdr, gcum_t = [], [], [], [], [], []
    for j in tiles:
        g_all = _mask_rows(valid, -jnp.exp(alog) * _softplus(pabs[j] + dtb))
        beta_all.append(_mask_rows(valid, _sigmoid(pabs[j])))
        if variant == "sample":
            both = _mm_split3(jnp.concatenate([incl01, same.astype(BF16)], 0), g_all)
            gc, gseg = both[:TILE], both[TILE:]
        else:
            gc = _mm_split3(incl01, g_all)
            gseg = jnp.broadcast_to(gc[TILE - 1:TILE, :], shape)
        gcum.append(gc)
        gam.append(jnp.exp(gc))
        ktail.append(jnp.exp(gseg - gc))
        cdr.append(jnp.exp(gseg))
        gcum_t.append(gc.T)

    gq, gk, gv = pick("gq"), pick("gk"), pick("gv")
    n_items = range(len(items))
    dec = [jnp.where(incl, jnp.exp(
        _lane_bcast(gcum[j], h) - jnp.broadcast_to(gcum_t[j][h:h + 1, :], shape)), 0.0)
        for j, h in items]
    bcol = [_lane_bcast(beta_all[j], N_HEADS + h) for j, h in items]
    gamc = [_lane_bcast(gam[j], h) for j, h in items]
    kk = _mm_nt_each(gk, gk)
    qk = _mm_nt_each(gq, gk)
    nmats = [-(jnp.where(strict, kk[i] * dec[i], 0.0) * bcol[i]) for i in n_items]
    tinv = _tri_inverse(variant, nmats, row, col)
    rhs = [jnp.concatenate([gv[i] * bcol[i], gk[i] * (bcol[i] * gamc[i])], 1) for i in n_items]
    sol = _mm_each(tinv, rhs)
    ret = dict(
        items=items,
        scores=[qk_r * dmat_ref[h] for qk_r, (_, h) in zip(
            _mm_nt_each(pick("rq"), pick("rk")), items)],
        qd=pick("qd"), kdT=pick("kdT"), v=pick("rv"), gate=pick("sgr"),
    )
    gdn = dict(
        items=items,
        wv=[t[:, :HEAD_DIM] for t in sol],
        wk=[t[:, HEAD_DIM:] for t in sol],
        attn=[qk[i] * dec[i] for i in n_items],
        qg=[gq[i] * gamc[i] for i in n_items],
        ktT=[(gk[i] * _lane_bcast(ktail[j], h)).T for i, (j, h) in enumerate(items)],
        cdr=cdr,
        gate=pick("sgz"),
    )
    return ret, gdn


def _rms_gate(o, gate, weight=None):
    o = o * lax.rsqrt(jnp.mean(o * o, -1, keepdims=True) + RMS_EPS)
    if weight is not None:
        o = o * weight
    return (o * gate).astype(BF16)


_STATE_SHAPE = (N_HEADS, HEAD_DIM, HEAD_DIM)


def _seq_init(s0r_ref, s0g_ref, sr_ref, sg_ref):
    @pl.when(pl.program_id(1) == 0)
    def _():
        for j in range(sr_ref.shape[0]):
            sr_ref[j] = s0r_ref[...]
            sg_ref[j] = s0g_ref[...]


def _seq_step(ret_cd, ret, gdn, gnw, mix_ref, sr_out, sg_out, sr_ref, sg_ref):
    items = ret["items"]
    idx = range(len(items))
    s_r = [sr_ref[j, h] for j, h in items]
    s_g = [sg_ref[j, h] for j, h in items]
    bf = lambda t: t.astype(BF16)
    cat = lambda a, b, axis: jnp.concatenate([bf(a), bf(b)], axis)
    wk_s = _mm_each(gdn["wk"], s_g)
    o_r = _mm_each([cat(ret["scores"][i], ret["qd"][i], 1) for i in idx],
                   [cat(ret["v"][i], s_r[i], 0) for i in idx])
    u = [gdn["wv"][i] - wk_s[i] for i in idx]
    o_g = _mm_each([cat(gdn["qg"][i], gdn["attn"][i], 1) for i in idx],
                   [cat(s_g[i], u[i], 0) for i in idx])
    upd_g = _mm_each(gdn["ktT"], u)
    upd_r = _mm_each(ret["kdT"], ret["v"])
    for i, (j, h) in enumerate(items):
        cd = jnp.broadcast_to(gdn["cdr"][j][0:1, h:h + 1], (HEAD_DIM, HEAD_DIM))
        sg_ref[j, h] = cd * s_g[i] + upd_g[i]
        sr_ref[j, h] = ret_cd[h] * s_r[i] + upd_r[i]
    for i, (j, h) in enumerate(items):
        mix_ref[j, :, h * HEAD_DIM:(h + 1) * HEAD_DIM] = _rms_gate(o_r[i], ret["gate"][i])
        mix_ref[j, :, GROUP_W + h * HEAD_DIM:GROUP_W + (h + 1) * HEAD_DIM] = _rms_gate(
            o_g[i], gdn["gate"][i], gnw)

    @pl.when(pl.program_id(1) == pl.num_programs(1) - 1)
    def _():
        sr_out[...] = sr_ref[...]
        sg_out[...] = sg_ref[...]


def _seq_out(n_batch, n_chunks, nb):
    state = pl.BlockSpec((nb,) + _STATE_SHAPE, lambda b, n: (b, 0, 0, 0))
    out_specs = [pl.BlockSpec((nb, TILE, 2 * GROUP_W), lambda b, n: (b, n, 0)), state, state]
    out_shape = [
        jax.ShapeDtypeStruct((n_batch, n_chunks * TILE, 2 * GROUP_W), BF16),
        jax.ShapeDtypeStruct((n_batch,) + _STATE_SHAPE, F32),
        jax.ShapeDtypeStruct((n_batch,) + _STATE_SHAPE, F32),
    ]
    scratch = [pltpu.VMEM((nb,) + _STATE_SHAPE, F32), pltpu.VMEM((nb,) + _STATE_SHAPE, F32)]
    return out_specs, out_shape, scratch


def _project_tile(x_ref, g_ref, b_ref, w_ref):
    hidden = _layer_norm(x_ref[...], g_ref[...], b_ref[...]).astype(BF16)
    project = lambda lo, hi: jnp.dot(hidden, w_ref[:, lo:hi], preferred_element_type=F32)
    return (project(0, 4 * GROUP_W), project(4 * GROUP_W, 8 * GROUP_W),
            project(8 * GROUP_W, 8 * GROUP_W + HEAD_DIM))


def _meta_mixer_body(ret_cd, x_ref, eg_ref, eb_ref, w_ref, cos_ref, sin_ref, rsc_ref, dmat_ref,
                     cw_ref, alog_ref, dtb_ref, gnw_ref, s0r_ref, s0g_ref,
                     mix_ref, sr_out, sg_out, preconv_ref, sr_ref, sg_ref):
    _seq_init(s0r_ref, s0g_ref, sr_ref, sg_ref)
    pret, pgdn, pab = _project_tile(x_ref, eg_ref, eb_ref, w_ref)
    preconv_ref[...] = pgdn[:, :CONV_COLS]
    tok = _tokenwise("meta", pret, pgdn, cos_ref[...], sin_ref[...], rsc_ref[...], cw_ref, None)
    tok["kdT"] = [t.T for t in tok["kd"]]
    ret, gdn = _chunk_prep("meta", [tok], [pab], dmat_ref, alog_ref[...], dtb_ref[...])
    _seq_step(ret_cd, ret, gdn, gnw_ref[...], mix_ref, sr_out, sg_out, sr_ref, sg_ref)


def _meta_mixer(ret_cd, x, ln_g, ln_b, w_all, cosf, sins, rsc, dmat, cw, alog, dtb, gnw, s0r, s0g):
    c2 = lambda b, n: (0, 0)
    c3 = lambda b, n: (0, 0, 0)
    out_specs, out_shape, state_scratch = _seq_out(1, 1, 1)
    return pl.pallas_call(
        functools.partial(_meta_mixer_body, ret_cd),
        grid=(1, 1),
        in_specs=[
            pl.BlockSpec((TILE, D_MODEL), c2),
            pl.BlockSpec((1, D_MODEL), c2),
            pl.BlockSpec((1, D_MODEL), c2),
            pl.BlockSpec((D_MODEL, W_ALL_COLS), c2),
            pl.BlockSpec((TILE, HEAD_DIM), c2),
            pl.BlockSpec((TILE, HEAD_DIM), c2),
            pl.BlockSpec((TILE, HEAD_DIM), c2),
            pl.BlockSpec((N_HEADS, TILE, TILE), c3),
            pl.BlockSpec((CONV_W, CONV_COLS), c2),
            pl.BlockSpec((1, HEAD_DIM), c2),
            pl.BlockSpec((1, HEAD_DIM), c2),
            pl.BlockSpec((1, HEAD_DIM), c2),
            pl.BlockSpec(_STATE_SHAPE, c3),
            pl.BlockSpec(_STATE_SHAPE, c3),
        ],
        out_specs=out_specs + [pl.BlockSpec((TILE, CONV_COLS), c2)],
        out_shape=out_shape + [jax.ShapeDtypeStruct((TILE, CONV_COLS), F32)],
        scratch_shapes=state_scratch,
        compiler_params=_cparams(("parallel", "arbitrary")),
        name="mixer_meta",
    )(x, ln_g, ln_b, w_all, cosf, sins, rsc, dmat, cw, alog, dtb, gnw, s0r, s0g)


def _prompt_fused_body(ret_cd, x_ref, g_ref, b_ref, w_ref, cos_ref, sin_ref, rsc_ref, cw_ref, halo0_ref,
                       dmat_ref, alog_ref, dtb_ref, gnw_ref, s0r_ref, s0g_ref,
                       mix_ref, sr_out, sg_out, tail_ref, sr_ref, sg_ref, halo_ref):
    nb = x_ref.shape[0]
    seqs, heads = range(nb), range(N_HEADS)
    _seq_init(s0r_ref, s0g_ref, sr_ref, sg_ref)

    @pl.when(pl.program_id(1) == 0)
    def _():
        for j in seqs:
            halo_ref[j] = halo0_ref[...]

    hidden = _layer_norm(x_ref[...].reshape(nb * TILE, D_MODEL), g_ref[...], b_ref[...]).astype(BF16)
    cosf, sins, rsc = cos_ref[...], sin_ref[...], rsc_ref[...]
    keys = ("rq", "rk", "rv", "qd", "kdT", "sgr", "gq", "gk", "gv", "sgz")
    toks = [{key: [None] * N_HEADS for key in keys} for _ in seqs]
    pabs = [None] * nb
    rows = lambda j: slice(j * TILE, (j + 1) * TILE)

    def project(group, width=GROUP_W):
        return jnp.dot(hidden, w_ref[:, group * GROUP_W:group * GROUP_W + width], preferred_element_type=F32)

    def conv_group(group, key):
        val = project(N_HEADS + group)
        cols = slice(group * GROUP_W, (group + 1) * GROUP_W)
        for j in seqs:
            part = val[rows(j), :]
            conv = _short_conv(part, halo_ref[j, :, cols], cw_ref, group * GROUP_W)
            halo_ref[j, :, cols] = part[TILE - HALO:, :]
            for h in heads:
                t = _head_cols(conv, 0, h)
                if key == "gq":
                    t = _l2_normalize(t) * (HEAD_DIM ** -0.5)
                elif key == "gk":
                    t = _l2_normalize(t)
                toks[j][key][h] = t

    def ab_group():
        val = project(2 * N_HEADS, HEAD_DIM)
        for j in seqs:
            pabs[j] = val[rows(j), :]

    def ret_head(h):
        val = project(h)
        for j in seqs:
            part = val[rows(j), :]
            ops = _ret_head_ops(None, *(_head_cols(part, 0, g) for g in range(4)), cosf, sins,
                                _lane_bcast(rsc, h), _lane_bcast(rsc, N_HEADS + h))
            for key in ("rq", "rk", "rv", "qd", "sgr"):
                toks[j][key][h] = ops[key]
            toks[j]["kdT"][h] = ops["kd"].T

    def gate_group():
        val = project(2 * N_HEADS - 1)
        for j in seqs:
            for h in heads:
                toks[j]["sgz"][h] = _silu(_head_cols(val[rows(j), :], 0, h))

    ab_group()
    conv_group(1, "gk")
    conv_group(0, "gq")
    conv_group(2, "gv")
    for j in seqs:
        tail_ref[j] = halo_ref[j]

    gate_group()
    for h in heads:
        ret_head(h)

    ret, gdn = _chunk_prep("prompt", toks, pabs, dmat_ref, alog_ref[...], dtb_ref[...])
    _seq_step(ret_cd, ret, gdn, gnw_ref[...], mix_ref, sr_out, sg_out, sr_ref, sg_ref)


def _prompt_fused(ret_cd, x, ln_g, ln_b, w_all, cosf, sins, rsc, cw, halo0, dmat, alog, dtb, gnw, s0r, s0g, nb):
    n_batch, seq, _ = x.shape
    n_chunks = seq // TILE
    tile = lambda b, n: (b, n, 0)
    pos = lambda b, n: (n, 0)
    c2 = lambda b, n: (0, 0)
    c3 = lambda b, n: (0, 0, 0)
    out_specs, out_shape, state_scratch = _seq_out(n_batch, n_chunks, nb)
    return pl.pallas_call(
        functools.partial(_prompt_fused_body, ret_cd),
        grid=(n_batch // nb, n_chunks),
        in_specs=[
            pl.BlockSpec((nb, TILE, D_MODEL), tile),
            pl.BlockSpec((1, D_MODEL), c2),
            pl.BlockSpec((1, D_MODEL), c2),
            pl.BlockSpec((D_MODEL, W_ALL_COLS), c2, pipeline_mode=pl.Buffered(1)),
            pl.BlockSpec((TILE, HEAD_DIM), pos),
            pl.BlockSpec((TILE, HEAD_DIM), pos),
            pl.BlockSpec((TILE, HEAD_DIM), c2),
            pl.BlockSpec((CONV_W, CONV_COLS), c2),
            pl.BlockSpec((HALO, CONV_COLS), c2),
            pl.BlockSpec((N_HEADS, TILE, TILE), c3),
            pl.BlockSpec((1, HEAD_DIM), c2),
            pl.BlockSpec((1, HEAD_DIM), c2),
            pl.BlockSpec((1, HEAD_DIM), c2),
            pl.BlockSpec(_STATE_SHAPE, c3),
            pl.BlockSpec(_STATE_SHAPE, c3),
        ],
        out_specs=out_specs + [pl.BlockSpec((nb, HALO, CONV_COLS), lambda b, n: (b, 0, 0))],
        out_shape=out_shape + [jax.ShapeDtypeStruct((n_batch, HALO, CONV_COLS), F32)],
        scratch_shapes=state_scratch + [pltpu.VMEM((nb, HALO, CONV_COLS), F32)],
        compiler_params=_cparams(("parallel", "arbitrary")),
        name="prompt_fused",
    )(x, ln_g, ln_b, w_all, cosf, sins, rsc, cw, halo0, dmat, alog, dtb, gnw, s0r, s0g)


SAMPLE_PER_TILE = TILE // SAMPLE_GROUP


def _stack_by_group(xt):
    shape3 = (SAMPLE_PER_TILE, HEAD_DIM, TILE)
    keep = lax.broadcasted_iota(jnp.int32, shape3, 0) == (lax.broadcasted_iota(jnp.int32, shape3, 2) >> 3)
    stacked = jnp.where(keep, jnp.broadcast_to(xt[None], shape3), 0.0)
    return stacked.reshape(SAMPLE_PER_TILE * HEAD_DIM, TILE)


def _sample_mixer_body(ret_cd, x_ref, eg_ref, eb_ref, w_ref, cos_ref, sin_ref, rsc_ref, dmat_ref, cw_ref,
                       alog_ref, dtb_ref, gnw_ref, convpad_ref, sr_in, sg_in,
                       mix_ref, sr_out, sg_out, preconv_ref,
                       qd_s, wk_s, qg_s, wv_s, cd_s, inter_s, u_s, qs_s):
    heads = range(N_HEADS)
    pret, pgdn, pab = _project_tile(x_ref, eg_ref, eb_ref, w_ref)
    preconv_ref[...] = pgdn[:, :CONV_COLS]
    tok = _tokenwise("sample", pret, pgdn, cos_ref[...], sin_ref[...], rsc_ref[...],
                     cw_ref, convpad_ref[...])
    tok["kdT"] = [t.T for t in tok["kd"]]
    ret, gdn = _chunk_prep("sample", [tok], [pab], dmat_ref, alog_ref[...], dtb_ref[...])
    for h in heads:
        qd_s[h] = ret["qd"][h]
        wk_s[h] = gdn["wk"][h]
        qg_s[h] = gdn["qg"][h]
        wv_s[h] = gdn["wv"][h]
    cd_s[...] = gdn["cdr"][0]

    def per_batch(b, carry):
        rows = pl.ds(pl.multiple_of(b * SAMPLE_GROUP, SAMPLE_GROUP), SAMPLE_GROUP)
        cd_rows = cd_s[rows, :]
        for h in heads:
            inter_s[h, rows, :] = _mm(qd_s[h, rows, :], sr_in[b, h])
            s = sg_in[b, h]
            both = _mm(jnp.concatenate([wk_s[h, rows, :], qg_s[h, rows, :]], 0), s)
            u_s[h, rows, :] = wv_s[h, rows, :] - both[:SAMPLE_GROUP]
            qs_s[h, rows, :] = both[SAMPLE_GROUP:]
            cd = jnp.broadcast_to(cd_rows[SAMPLE_GROUP - 1:SAMPLE_GROUP, h:h + 1], (HEAD_DIM, HEAD_DIM))
            sg_out[b, h] = cd * s
        return carry

    lax.fori_loop(0, SAMPLE_PER_TILE, per_batch, 0)

    out_rows = SAMPLE_PER_TILE * SAMPLE_TOKENS
    out_row = _iota2((out_rows, TILE), 0)
    token_row = ((out_row >> 2) << 3) + (SAMPLE_GROUP - SAMPLE_TOKENS) + (out_row & (SAMPLE_TOKENS - 1))
    select = (_iota2((out_rows, TILE), 1) == token_row).astype(BF16)
    compact = lambda t: jnp.dot(select, t, preferred_element_type=F32).astype(BF16)

    state_shape = (SAMPLE_PER_TILE, HEAD_DIM, HEAD_DIM)
    for h in heads:
        o = _mm(ret["scores"][h], ret["v"][h]) + inter_s[h]
        upd = _mm(_stack_by_group(ret["kdT"][h]), ret["v"][h]).reshape(state_shape)
        sr_out[:, h] = ret_cd[h] * sr_in[:, h] + upd
        mix_ref[:, h * HEAD_DIM:(h + 1) * HEAD_DIM] = compact(_rms_gate(o, ret["gate"][h]))

    for h in heads:
        u = u_s[h]
        o = qs_s[h] + _mm(gdn["attn"][h], u)
        upd = _mm(_stack_by_group(gdn["ktT"][h]), u).reshape(state_shape)
        sg_out[:, h] = sg_out[:, h] + upd
        mix_ref[:, GROUP_W + h * HEAD_DIM:GROUP_W + (h + 1) * HEAD_DIM] = compact(_rms_gate(
            o, gdn["gate"][h], gnw_ref[...]))


def _sample_mixer(ret_cd, x, ln_g, ln_b, w_all, cosf, sins, rsc, dmat, cw, alog, dtb, gnw, convpad, sr, sg):
    n_tiles = x.shape[0] // TILE
    tile = lambda i: (i, 0)
    c2 = lambda i: (0, 0)
    c3 = lambda i: (0, 0, 0)
    state = pl.BlockSpec((SAMPLE_PER_TILE,) + _STATE_SHAPE, lambda i: (i, 0, 0, 0))
    head_scratch = pltpu.VMEM((N_HEADS, TILE, HEAD_DIM), F32)
    return pl.pallas_call(
        functools.partial(_sample_mixer_body, ret_cd),
        grid=(n_tiles,),
        in_specs=[
            pl.BlockSpec((TILE, D_MODEL), tile),
            pl.BlockSpec((1, D_MODEL), c2),
            pl.BlockSpec((1, D_MODEL), c2),
            pl.BlockSpec((D_MODEL, W_ALL_COLS), c2, pipeline_mode=pl.Buffered(1)),
            pl.BlockSpec((TILE, HEAD_DIM), c2),
            pl.BlockSpec((TILE, HEAD_DIM), c2),
            pl.BlockSpec((TILE, HEAD_DIM), c2),
            pl.BlockSpec((N_HEADS, TILE, TILE), c3),
            pl.BlockSpec((CONV_W, CONV_COLS), c2),
            pl.BlockSpec((1, HEAD_DIM), c2),
            pl.BlockSpec((1, HEAD_DIM), c2),
            pl.BlockSpec((1, HEAD_DIM), c2),
            pl.BlockSpec((TILE, CONV_COLS), tile),
            state, state,
        ],
        out_specs=[pl.BlockSpec((SAMPLE_PER_TILE * SAMPLE_TOKENS, 2 * GROUP_W), tile), state, state,
                   pl.BlockSpec((TILE, CONV_COLS), tile)],
        out_shape=[
            jax.ShapeDtypeStruct((n_tiles * SAMPLE_PER_TILE * SAMPLE_TOKENS, 2 * GROUP_W), BF16),
            jax.ShapeDtypeStruct(sr.shape, F32),
            jax.ShapeDtypeStruct(sg.shape, F32),
            jax.ShapeDtypeStruct((x.shape[0], CONV_COLS), F32),
        ],
        scratch_shapes=[head_scratch] * 4 + [pltpu.VMEM((TILE, HEAD_DIM), F32)] + [head_scratch] * 3,
        compiler_params=_cparams(("parallel",)),
        name="mixer_sample",
    )(x, ln_g, ln_b, w_all, cosf, sins, rsc, dmat, cw, alog, dtb, gnw, convpad, sr, sg)


def _rotary_tables(pos):
    half = HEAD_DIM // 2
    inv = ROPE_BASE ** (-np.arange(half, dtype=np.float64) / half)
    ang = np.asarray(pos, np.float64)[:, None] * inv[None, :]
    cos, sin = np.cos(ang), np.sin(ang)
    return (jnp.asarray(np.concatenate([cos, cos], -1), F32),
            jnp.asarray(np.concatenate([-sin, sin], -1), F32))


def _retention_tables(seg, pos, valid, seg_len):
    gamma = 1.0 - 2.0 ** (-5.0 - np.arange(N_HEADS, dtype=np.float64))
    posf = np.asarray(pos, np.float64)
    rel = posf[:, None] - posf[None, :]
    causal = (seg[:, None] == seg[None, :]) & (rel >= 0)
    dmat = np.where(causal[None], gamma[:, None, None] ** np.where(causal, rel, 0.0)[None], 0.0)
    q_scale = gamma[None, :] ** (posf[:, None] + 1.0)
    k_scale = np.where(valid[:, None], gamma[None, :] ** (seg_len - 1.0 - posf[:, None]), 0.0)
    rsc = np.concatenate([q_scale, k_scale, np.zeros((TILE, HEAD_DIM - 2 * N_HEADS))], -1)
    chunk_decay = tuple(float(g ** seg_len) for g in gamma)
    return jnp.asarray(dmat, F32), jnp.asarray(rsc, F32), chunk_decay


def _pad_lanes(v):
    return jnp.pad(v.astype(F32), (0, HEAD_DIM - v.shape[0]))[None, :]


def kernel(x_prompt, x_sample, state_ret, state_gdn, state_conv, meta_tokens, emb_ln_g, emb_ln_b,
           w_in, conv_w, a_log, dt_bias, gdn_norm_w, w_out, ln1_g, ln1_b, w_gate_up, w_down,
           ln2_g, ln2_b):
    n_batch, seq, _ = x_prompt.shape
    dec_batch, dec_seq, _ = x_sample.shape
    assert seq % TILE == 0 and dec_seq == SAMPLE_TOKENS and N_META <= TILE
    assert n_batch % MIXER_SEQS_PER_STEP == 0
    layer = 0

    w_all = _weight_prep(jnp.swapaxes(w_in[layer], 0, 1))
    w_out_b = w_out[layer].astype(BF16)
    w_gu_b = w_gate_up[layer].astype(BF16)
    w_down_b = w_down[layer].astype(BF16)
    row = lambda v: v.astype(F32)[None, :]
    eg, eb = row(emb_ln_g), row(emb_ln_b)
    cw = conv_w[layer].astype(F32)
    alog, dtb, gnw = _pad_lanes(a_log[layer]), _pad_lanes(dt_bias[layer]), row(gdn_norm_w[layer])

    xp = x_prompt.reshape(n_batch * seq, D_MODEL)
    xm = jnp.pad(meta_tokens.astype(F32), ((0, TILE - N_META), (0, 0)))
    xs = jnp.pad(x_sample, ((0, 0), (SAMPLE_GROUP - dec_seq, 0), (0, 0))).reshape(
        dec_batch * SAMPLE_GROUP, D_MODEL)
    convpad = jnp.pad(state_conv[layer].astype(F32), ((0, 0), (1, SAMPLE_GROUP - CONV_W), (0, 0))).reshape(
        dec_batch * SAMPLE_GROUP, CONV_COLS)

    tile_idx = np.arange(TILE)
    cos_p, sin_p = _rotary_tables(N_META + np.arange(seq))
    cos_m, sin_m = _rotary_tables(tile_idx)
    tok = (tile_idx % SAMPLE_GROUP) - (SAMPLE_GROUP - dec_seq)
    cos_s, sin_s = _rotary_tables(PAST_LEN + np.maximum(tok, 0))
    zeros_i = np.zeros((TILE,), np.int32)
    all_valid = np.ones((TILE,), bool)
    dmat_p, rsc_p, cd_p = _retention_tables(zeros_i, tile_idx, all_valid, float(TILE))
    dmat_m, rsc_m, cd_m = _retention_tables(zeros_i, tile_idx, tile_idx < N_META, float(N_META))
    dmat_s, rsc_s, cd_s = _retention_tables(tile_idx // SAMPLE_GROUP, tok, tok >= 0, float(dec_seq))

    zero_state = jnp.zeros(_STATE_SHAPE, F32)
    _, sr_m, sg_m, preconv_m = _meta_mixer(cd_m, xm, eg, eb, w_all, cos_m, sin_m, rsc_m, dmat_m, cw, alog, dtb,
                                           gnw, zero_state, zero_state)
    halo_p = preconv_m[N_META - HALO:N_META]

    mix_p, sr_p, sg_p, tail_p = _prompt_fused(cd_p, x_prompt, eg, eb, w_all, cos_p, sin_p, rsc_p, cw, halo_p,
                                              dmat_p, alog, dtb, gnw, sr_m[0], sg_m[0], MIXER_SEQS_PER_STEP)
    mix_p = mix_p.reshape(n_batch * seq, 2 * GROUP_W)

    mix_s, sr_s, sg_s, preconv_s = _sample_mixer(cd_s, xs, eg, eb, w_all, cos_s, sin_s, rsc_s, dmat_s, cw,
                                                 alog, dtb, gnw, convpad, state_ret[layer].astype(F32),
                                                 state_gdn[layer].astype(F32))

    back = functools.partial(_back, eg=eg, eb=eb, w_out=w_out_b, g1=row(ln1_g[layer]), b1=row(ln1_b[layer]),
                             w_gu=w_gu_b, w_down=w_down_b, g2=row(ln2_g[layer]), b2=row(ln2_b[layer]),
                             tm=512)
    y_p = back(xp, mix_p).reshape(n_batch, seq, D_MODEL)
    y_s = back(x_sample.reshape(dec_batch * dec_seq, D_MODEL), mix_s).reshape(dec_batch, dec_seq, D_MODEL)

    conv_p = tail_p[:, HALO - (CONV_W - 1):, :]
    conv_s = preconv_s.reshape(dec_batch, SAMPLE_GROUP, CONV_COLS)[:, SAMPLE_GROUP - (CONV_W - 1):, :]
    return (y_p, y_s, sr_p[None], sg_p[None], conv_p[None], sr_s[None], sg_s[None], conv_s[None])
```

```python
import functools

import numpy as np
import jax
import jax.numpy as jnp
from jax import lax
from jax.experimental import pallas as pl
from jax.experimental.pallas import tpu as pltpu

F32 = jnp.float32
BF16 = jnp.bfloat16

D_MODEL = 1024
N_META = 16
N_HEADS = 4
HEAD_DIM = 128
GROUP_W = N_HEADS * HEAD_DIM
CONV_W = 4
CONV_COLS = 3 * GROUP_W
D_FF = 2816
PAST_LEN = 16384
ROPE_BASE = 10000.0
LN_EPS = 1e-5
RMS_EPS = 1e-6
ALPHA = 2.0 ** 0.25
TILE = 128
SAMPLE_GROUP = 8
SAMPLE_TOKENS = 4
HALO = 8
MIXER_SEQS_PER_STEP = 4
BACK_PARTS = 2
W_ALL_COLS = 9 * GROUP_W
AB_ROWS = 2 * N_HEADS
VMEM_LIMIT = 56 * 1024 * 1024


def _cparams(sem):
    return pltpu.CompilerParams(dimension_semantics=sem, vmem_limit_bytes=VMEM_LIMIT)


def _layer_norm(x, g, b):
    mu = jnp.mean(x, -1, keepdims=True)
    xc = x - mu
    var = jnp.mean(xc * xc, -1, keepdims=True)
    return xc * lax.rsqrt(var + LN_EPS) * g + b


def _sigmoid(x):
    return 1.0 / (1.0 + jnp.exp(-x))


def _silu(x):
    return x * _sigmoid(x)


def _softplus(x):
    return jnp.maximum(x, 0.0) + jnp.log(1.0 + jnp.exp(-jnp.abs(x)))


def _mm(a, b):
    return jnp.dot(a.astype(BF16), b.astype(BF16), preferred_element_type=F32)


def _mm_nt(a, b):
    return lax.dot_general(a.astype(BF16), b.astype(BF16), (((1,), (1,)), ((), ())),
                           preferred_element_type=F32)


def _mm_each(xs, ys):
    return [_mm(x, y) for x, y in zip(xs, ys)]


def _mm_nt_each(xs, ys):
    return [_mm_nt(x, y) for x, y in zip(xs, ys)]


def _mm_split3(m01, x):
    x1 = x.astype(BF16)
    r1 = x - x1.astype(F32)
    x2 = r1.astype(BF16)
    x3 = (r1 - x2.astype(F32)).astype(BF16)
    dot = functools.partial(jnp.dot, preferred_element_type=F32)
    return (dot(m01, x3) + dot(m01, x2)) + dot(m01, x1)


def _lane_bcast(x, lane):
    return jnp.broadcast_to(x[:, lane:lane + 1], (x.shape[0], HEAD_DIM))


def _head_cols(x, base, h):
    return x[:, base + h * HEAD_DIM:base + (h + 1) * HEAD_DIM]


PREP_GROUPS_PER_STEP = 3


def _weight_prep_body(*refs):
    piece_refs, ab_ref, o_ref = refs[:-2], refs[-2], refs[-1]
    j = pl.program_id(0)

    def copy_group(t):
        for piece in range(4):
            lo = t * GROUP_W + piece * HEAD_DIM
            o_ref[:, lo:lo + HEAD_DIM] = piece_refs[4 * t + piece][...].T.astype(BF16)

    for t in range(PREP_GROUPS_PER_STEP - 1):
        copy_group(t)
    last = PREP_GROUPS_PER_STEP - 1
    pl.when(j < pl.num_programs(0) - 1)(functools.partial(copy_group, last))

    @pl.when(j == pl.num_programs(0) - 1)
    def _():
        ab = jnp.concatenate([ab_ref[...], jnp.zeros((HEAD_DIM - AB_ROWS, D_MODEL), F32)], 0)
        o_ref[:, last * GROUP_W:last * GROUP_W + HEAD_DIM] = ab.T.astype(BF16)
        o_ref[:, last * GROUP_W + HEAD_DIM:] = jnp.zeros((D_MODEL, GROUP_W - HEAD_DIM), BF16)


def _weight_prep(w_in_t):
    assert w_in_t.shape == (8 * GROUP_W + AB_ROWS, D_MODEL)

    n_groups = 2 * N_HEADS + 1
    assert n_groups % PREP_GROUPS_PER_STEP == 0

    def piece_spec(t, piece):
        def index(j):
            g = PREP_GROUPS_PER_STEP * j + t
            ret_block = piece * N_HEADS + g
            gdn_block = N_HEADS * g + piece
            return (jnp.where(g < N_HEADS, ret_block, jnp.where(g < 2 * N_HEADS, gdn_block, 0)), 0)
        return pl.BlockSpec((HEAD_DIM, D_MODEL), index)

    pieces = [piece_spec(t, p) for t in range(PREP_GROUPS_PER_STEP) for p in range(4)]
    ab_spec = pl.BlockSpec((AB_ROWS, D_MODEL), lambda j: (8 * GROUP_W // AB_ROWS, 0))
    return pl.pallas_call(
        _weight_prep_body,
        grid=(n_groups // PREP_GROUPS_PER_STEP,),
        in_specs=pieces + [ab_spec],
        out_specs=pl.BlockSpec((D_MODEL, PREP_GROUPS_PER_STEP * GROUP_W), lambda j: (0, j)),
        out_shape=jax.ShapeDtypeStruct((D_MODEL, W_ALL_COLS), BF16),
        compiler_params=_cparams(("arbitrary",)),
        name="weight_prep",
    )(*([w_in_t] * (len(pieces) + 1)))


def _back_body(x_ref, mix_ref, eg_ref, eb_ref, wo_ref, g1_ref, b1_ref, wgu_ref, wd_ref,
               g2_ref, b2_ref, y_ref):
    part = x_ref.shape[0] // BACK_PARTS
    parts = [slice(i * part, (i + 1) * part) for i in range(BACK_PARTS)]
    dot = functools.partial(jnp.dot, preferred_element_type=F32)
    h = [_layer_norm(x_ref[p, :], eg_ref[...], eb_ref[...]) for p in parts]
    mp = [dot(mix_ref[p, :], wo_ref[...]) for p in parts]
    h1 = [_layer_norm(ALPHA * a + b, g1_ref[...], b1_ref[...]) for a, b in zip(h, mp)]
    gu = [dot(a.astype(BF16), wgu_ref[...]) for a in h1]
    act = [(_silu(a[:, :D_FF]) * a[:, D_FF:]).astype(BF16) for a in gu]
    ff = [dot(a, wd_ref[...]) for a in act]
    for p, a, b in zip(parts, h1, ff):
        y_ref[p, :] = _layer_norm(ALPHA * a + b, g2_ref[...], b2_ref[...])


def _back(x2d, mix, eg, eb, w_out, g1, b1, w_gu, w_down, g2, b2, tm):
    rows = x2d.shape[0]
    const = lambda i: (0, 0)
    tile = lambda i: (i, 0)
    single = pl.Buffered(1)
    vec = pl.BlockSpec((1, D_MODEL), const)
    return pl.pallas_call(
        _back_body,
        grid=(rows // tm,),
        in_specs=[
            pl.BlockSpec((tm, D_MODEL), tile),
            pl.BlockSpec((tm, D_MODEL), tile),
            vec, vec,
            pl.BlockSpec((D_MODEL, D_MODEL), const, pipeline_mode=single),
            vec, vec,
            pl.BlockSpec((D_MODEL, 2 * D_FF), const, pipeline_mode=single),
            pl.BlockSpec((D_FF, D_MODEL), const, pipeline_mode=single),
            vec, vec,
        ],
        out_specs=pl.BlockSpec((tm, D_MODEL), tile),
        out_shape=jax.ShapeDtypeStruct((rows, D_MODEL), F32),
        compiler_params=_cparams(("parallel",)),
        name="back",
    )(x2d, mix, eg, eb, w_out, g1, b1, w_gu, w_down, g2, b2)


def _iota2(shape, dim):
    return lax.broadcasted_iota(jnp.int32, shape, dim)


def _valid_rows(variant, shape):
    r = _iota2(shape, 0)
    if variant == "meta":
        return r < N_META
    if variant == "sample":
        return (r & (SAMPLE_GROUP - 1)) >= SAMPLE_GROUP - SAMPLE_TOKENS
    return None


def _mask_rows(valid, x):
    return x if valid is None else jnp.where(valid, x, 0.0)


def _tri_inverse(variant, nmats, row, col):
    eye = (row == col).astype(F32)
    if variant == "sample":
        n2 = _mm_each(nmats, nmats)
        ps = [eye + n for n in nmats]
        return [p + t for p, t in zip(ps, _mm_each(ps, n2))]
    base_log2 = 3
    in_block = (row >> base_log2) == (col >> base_log2)
    ds = [jnp.where(in_block, n, 0.0) for n in nmats]
    ps = [eye + d for d in ds]
    for _ in range(base_log2 - 1):
        ds = _mm_each(ds, ds)
        ps = [p + t for p, t in zip(ps, _mm_each(ps, ds))]
    live_rows = N_META if variant == "meta" else TILE
    s = base_log2
    while (1 << s) < live_rows:
        size = 1 << s
        lower_left = (((row >> (s + 1)) == (col >> (s + 1))) & (((row >> s) & 1) == 1)
                      & (((col >> s) & 1) == 0))
        cs = [jnp.where(lower_left, n, 0.0) for n in nmats]
        lower = [slice(start + size, start + 2 * size) for start in range(0, TILE, 2 * size)]
        upper = [slice(start, start + size) for start in range(0, TILE, 2 * size)]
        p_low = [jnp.concatenate([p[rows, :] for rows in lower], 0) for p in ps]
        fix = _mm_each(_mm_each(p_low, cs), ps)
        merged = []
        for p, f in zip(ps, fix):
            pieces = []
            for i, (up, lo) in enumerate(zip(upper, lower)):
                pieces += [p[up, :], p[lo, :] + f[i * size:(i + 1) * size, :]]
            merged.append(jnp.concatenate(pieces, 0))
        ps = merged
        s += 1
    return ps


def _rotary(t, cosf, sins):
    return t * cosf + pltpu.roll(t, HEAD_DIM // 2, 1) * sins


def _ret_head_ops(valid, q, k, v, gate, cosf, sins, q_scale, k_scale):
    rq = _mask_rows(valid, _rotary(q, cosf, sins))
    rk = _mask_rows(valid, _rotary(k, cosf, sins) * (HEAD_DIM ** -0.5))
    return dict(rq=rq, rk=rk, rv=_mask_rows(valid, v), qd=rq * q_scale, kd=rk * k_scale, sgr=_silu(gate))


def _short_conv(x, halo, cw_ref, col0):
    rows, ncols = x.shape
    cols = slice(col0, col0 + ncols)
    n = rows // HALO
    full = jnp.concatenate([halo, x], 0).reshape(n + 1, HALO, ncols)
    sub = lax.broadcasted_iota(jnp.int32, (n, HALO, ncols), 1)
    conv = full[1:] * cw_ref[CONV_W - 1:CONV_W, cols]
    for shift in range(1, CONV_W):
        rolled = pltpu.roll(full, shift, 1)
        shifted = jnp.where(sub < shift, rolled[:-1], rolled[1:])
        conv = conv + shifted * cw_ref[CONV_W - 1 - shift:CONV_W - shift, cols]
    return _silu(conv).reshape(rows, ncols)


def _l2_normalize(t):
    return t * lax.rsqrt(jnp.sum(t * t, -1, keepdims=True) + RMS_EPS)


def _tokenwise(variant, pret, pgdn, cosf, sins, rsc, cw_ref, convpad):
    rows = pret.shape[0]
    heads = range(N_HEADS)
    valid = _valid_rows(variant, (rows, HEAD_DIM))
    ret = [_ret_head_ops(valid, *(_head_cols(pret, h * GROUP_W, g) for g in range(4)), cosf, sins,
                         _lane_bcast(rsc, h), _lane_bcast(rsc, N_HEADS + h)) for h in heads]
    x = pgdn[:, :CONV_COLS]
    if variant == "sample":
        x = jnp.where(_valid_rows(variant, x.shape), x, convpad)
    conv = _short_conv(x, jnp.zeros((HALO, CONV_COLS), F32), cw_ref, 0)
    gq = [_mask_rows(valid, _l2_normalize(_head_cols(conv, 0, h)) * (HEAD_DIM ** -0.5)) for h in heads]
    gk = [_mask_rows(valid, _l2_normalize(_head_cols(conv, GROUP_W, h))) for h in heads]
    gv = [_mask_rows(valid, _head_cols(conv, 2 * GROUP_W, h)) for h in heads]
    sgz = [_silu(_head_cols(pgdn, 3 * GROUP_W, h)) for h in heads]
    tok = {key: [r[key] for r in ret] for key in ret[0]}
    tok.update(gq=gq, gk=gk, gv=gv, sgz=sgz)
    return tok


def _chunk_prep(variant, toks, pabs, dmat_ref, alog, dtb):
    shape = (TILE, HEAD_DIM)
    tiles = range(len(toks))
    items = [(j, h) for j in tiles for h in range(N_HEADS)]
    pick = lambda key: [toks[j][key][h] for j, h in items]
    row = _iota2(shape, 0)
    col = _iota2(shape, 1)
    valid = _valid_rows(variant, shape)
    if variant == "sample":
        same = (row >> 3) == (col >> 3)
        incl = same & (row >= col)
        strict = same & (row > col)
    else:
        incl = row >= col
        strict = row > col

    incl01 = incl.astype(BF16)
    beta_all, gcum, gam, ktail, cdr, gcum_t = [], [], [], [], [], []
    for j in tiles:
        g_all = _mask_rows(valid, -jnp.exp(alog) * _softplus(pabs[j] + dtb))
        beta_all.append(_mask_rows(valid, _sigmoid(pabs[j])))
        if variant == "sample":
            both = _mm_split3(jnp.concatenate([incl01, same.astype(BF16)], 0), g_all)
            gc, gseg = both[:TILE], both[TILE:]
        else:
            gc = _mm_split3(incl01, g_all)
            gseg = jnp.broadcast_to(gc[TILE - 1:TILE, :], shape)
        gcum.append(gc)
        gam.append(jnp.exp(gc))
        ktail.append(jnp.exp(gseg - gc))
        cdr.append(jnp.exp(gseg))
        gcum_t.append(gc.T)

    gq, gk, gv = pick("gq"), pick("gk"), pick("gv")
    n_items = range(len(items))
    dec = [jnp.where(incl, jnp.exp(jnp.minimum(
        _lane_bcast(gcum[j], h) - jnp.broadcast_to(gcum_t[j][h:h + 1, :], shape), 0.0)), 0.0)
        for j, h in items]
    bcol = [_lane_bcast(beta_all[j], N_HEADS + h) for j, h in items]
    gamc = [_lane_bcast(gam[j], h) for j, h in items]
    kk = _mm_nt_each(gk, gk)
    qk = _mm_nt_each(gq, gk)
    nmats = [-(jnp.where(strict, kk[i] * dec[i], 0.0) * bcol[i]) for i in n_items]
    tinv = _tri_inverse(variant, nmats, row, col)
    rhs = [jnp.concatenate([gv[i] * bcol[i], gk[i] * (bcol[i] * gamc[i])], 1) for i in n_items]
    sol = _mm_each(tinv, rhs)
    ret = dict(
        items=items,
        scores=[qk_r * dmat_ref[h] for qk_r, (_, h) in zip(
            _mm_nt_each(pick("rq"), pick("rk")), items)],
        qd=pick("qd"), kdT=pick("kdT"), v=pick("rv"), gate=pick("sgr"),
    )
    gdn = dict(
        items=items,
        wv=[t[:, :HEAD_DIM] for t in sol],
        wk=[t[:, HEAD_DIM:] for t in sol],
        attn=[qk[i] * dec[i] for i in n_items],
        qg=[gq[i] * gamc[i] for i in n_items],
        ktT=[(gk[i] * _lane_bcast(ktail[j], h)).T for i, (j, h) in enumerate(items)],
        cdr=cdr,
        gate=pick("sgz"),
    )
    return ret, gdn


def _rms_gate(o, gate, weight=None):
    o = o * lax.rsqrt(jnp.mean(o * o, -1, keepdims=True) + RMS_EPS)
    if weight is not None:
        o = o * weight
    return (o * gate).astype(BF16)


_STATE_SHAPE = (N_HEADS, HEAD_DIM, HEAD_DIM)


def _seq_init(s0r_ref, s0g_ref, sr_ref, sg_ref):
    @pl.when(pl.program_id(1) == 0)
    def _():
        for j in range(sr_ref.shape[0]):
            sr_ref[j] = s0r_ref[...]
            sg_ref[j] = s0g_ref[...]


def _seq_step(ret_cd, ret, gdn, gnw, mix_ref, sr_out, sg_out, sr_ref, sg_ref):
    items = ret["items"]
    idx = range(len(items))
    s_r = [sr_ref[j, h] for j, h in items]
    s_g = [sg_ref[j, h] for j, h in items]
    bf = lambda t: t.astype(BF16)
    cat = lambda a, b, axis: jnp.concatenate([bf(a), bf(b)], axis)
    wk_s = _mm_each(gdn["wk"], s_g)
    o_r = _mm_each([cat(ret["scores"][i], ret["qd"][i], 1) for i in idx],
                   [cat(ret["v"][i], s_r[i], 0) for i in idx])
    u = [gdn["wv"][i] - wk_s[i] for i in idx]
    o_g = _mm_each([cat(gdn["qg"][i], gdn["attn"][i], 1) for i in idx],
                   [cat(s_g[i], u[i], 0) for i in idx])
    upd_g = _mm_each(gdn["ktT"], u)
    upd_r = _mm_each(ret["kdT"], ret["v"])
    for i, (j, h) in enumerate(items):
        cd = jnp.broadcast_to(gdn["cdr"][j][0:1, h:h + 1], (HEAD_DIM, HEAD_DIM))
        sg_ref[j, h] = cd * s_g[i] + upd_g[i]
        sr_ref[j, h] = ret_cd[h] * s_r[i] + upd_r[i]
    for i, (j, h) in enumerate(items):
        mix_ref[j, :, h * HEAD_DIM:(h + 1) * HEAD_DIM] = _rms_gate(o_r[i], ret["gate"][i])
        mix_ref[j, :, GROUP_W + h * HEAD_DIM:GROUP_W + (h + 1) * HEAD_DIM] = _rms_gate(
            o_g[i], gdn["gate"][i], gnw)

    @pl.when(pl.program_id(1) == pl.num_programs(1) - 1)
    def _():
        sr_out[...] = sr_ref[...]
        sg_out[...] = sg_ref[...]


def _seq_out(n_batch, n_chunks, nb):
    state = pl.BlockSpec((nb,) + _STATE_SHAPE, lambda b, n: (b, 0, 0, 0))
    out_specs = [pl.BlockSpec((nb, TILE, 2 * GROUP_W), lambda b, n: (b, n, 0)), state, state]
    out_shape = [
        jax.ShapeDtypeStruct((n_batch, n_chunks * TILE, 2 * GROUP_W), BF16),
        jax.ShapeDtypeStruct((n_batch,) + _STATE_SHAPE, F32),
        jax.ShapeDtypeStruct((n_batch,) + _STATE_SHAPE, F32),
    ]
    scratch = [pltpu.VMEM((nb,) + _STATE_SHAPE, F32), pltpu.VMEM((nb,) + _STATE_SHAPE, F32)]
    return out_specs, out_shape, scratch


def _project_tile(x_ref, g_ref, b_ref, w_ref):
    hidden = _layer_norm(x_ref[...], g_ref[...], b_ref[...]).astype(BF16)
    project = lambda lo, hi: jnp.dot(hidden, w_ref[:, lo:hi], preferred_element_type=F32)
    return (project(0, 4 * GROUP_W), project(4 * GROUP_W, 8 * GROUP_W),
            project(8 * GROUP_W, 8 * GROUP_W + HEAD_DIM))


def _meta_mixer_body(ret_cd, x_ref, eg_ref, eb_ref, w_ref, cos_ref, sin_ref, rsc_ref, dmat_ref,
                     cw_ref, alog_ref, dtb_ref, gnw_ref, s0r_ref, s0g_ref,
                     mix_ref, sr_out, sg_out, preconv_ref, sr_ref, sg_ref):
    _seq_init(s0r_ref, s0g_ref, sr_ref, sg_ref)
    pret, pgdn, pab = _project_tile(x_ref, eg_ref, eb_ref, w_ref)
    preconv_ref[...] = pgdn[:, :CONV_COLS]
    tok = _tokenwise("meta", pret, pgdn, cos_ref[...], sin_ref[...], rsc_ref[...], cw_ref, None)
    tok["kdT"] = [t.T for t in tok["kd"]]
    ret, gdn = _chunk_prep("meta", [tok], [pab], dmat_ref, alog_ref[...], dtb_ref[...])
    _seq_step(ret_cd, ret, gdn, gnw_ref[...], mix_ref, sr_out, sg_out, sr_ref, sg_ref)


def _meta_mixer(ret_cd, x, ln_g, ln_b, w_all, cosf, sins, rsc, dmat, cw, alog, dtb, gnw, s0r, s0g):
    c2 = lambda b, n: (0, 0)
    c3 = lambda b, n: (0, 0, 0)
    out_specs, out_shape, state_scratch = _seq_out(1, 1, 1)
    return pl.pallas_call(
        functools.partial(_meta_mixer_body, ret_cd),
        grid=(1, 1),
        in_specs=[
            pl.BlockSpec((TILE, D_MODEL), c2),
            pl.BlockSpec((1, D_MODEL), c2),
            pl.BlockSpec((1, D_MODEL), c2),
            pl.BlockSpec((D_MODEL, W_ALL_COLS), c2),
            pl.BlockSpec((TILE, HEAD_DIM), c2),
            pl.BlockSpec((TILE, HEAD_DIM), c2),
            pl.BlockSpec((TILE, HEAD_DIM), c2),
            pl.BlockSpec((N_HEADS, TILE, TILE), c3),
            pl.BlockSpec((CONV_W, CONV_COLS), c2),
            pl.BlockSpec((1, HEAD_DIM), c2),
            pl.BlockSpec((1, HEAD_DIM), c2),
            pl.BlockSpec((1, HEAD_DIM), c2),
            pl.BlockSpec(_STATE_SHAPE, c3),
            pl.BlockSpec(_STATE_SHAPE, c3),
        ],
        out_specs=out_specs + [pl.BlockSpec((TILE, CONV_COLS), c2)],
        out_shape=out_shape + [jax.ShapeDtypeStruct((TILE, CONV_COLS), F32)],
        scratch_shapes=state_scratch,
        compiler_params=_cparams(("parallel", "arbitrary")),
        name="mixer_meta",
    )(x, ln_g, ln_b, w_all, cosf, sins, rsc, dmat, cw, alog, dtb, gnw, s0r, s0g)


def _prompt_fused_body(ret_cd, x_ref, g_ref, b_ref, w_ref, cos_ref, sin_ref, rsc_ref, cw_ref, halo0_ref,
                       dmat_ref, alog_ref, dtb_ref, gnw_ref, s0r_ref, s0g_ref,
                       mix_ref, sr_out, sg_out, tail_ref, sr_ref, sg_ref, halo_ref):
    nb = x_ref.shape[0]
    seqs, heads = range(nb), range(N_HEADS)
    _seq_init(s0r_ref, s0g_ref, sr_ref, sg_ref)

    @pl.when(pl.program_id(1) == 0)
    def _():
        for j in seqs:
            halo_ref[j] = halo0_ref[...]

    hidden = _layer_norm(x_ref[...].reshape(nb * TILE, D_MODEL), g_ref[...], b_ref[...]).astype(BF16)
    cosf, sins, rsc = cos_ref[...], sin_ref[...], rsc_ref[...]
    keys = ("rq", "rk", "rv", "qd", "kdT", "sgr", "gq", "gk", "gv", "sgz")
    toks = [{key: [None] * N_HEADS for key in keys} for _ in seqs]
    pabs = [None] * nb
    rows = lambda j: slice(j * TILE, (j + 1) * TILE)

    def project(group, width=GROUP_W):
        return jnp.dot(hidden, w_ref[:, group * GROUP_W:group * GROUP_W + width], preferred_element_type=F32)

    def conv_group(group, key):
        val = project(N_HEADS + group)
        cols = slice(group * GROUP_W, (group + 1) * GROUP_W)
        for j in seqs:
            part = val[rows(j), :]
            conv = _short_conv(part, halo_ref[j, :, cols], cw_ref, group * GROUP_W)
            halo_ref[j, :, cols] = part[TILE - HALO:, :]
            for h in heads:
                t = _head_cols(conv, 0, h)
                if key == "gq":
                    t = _l2_normalize(t) * (HEAD_DIM ** -0.5)
                elif key == "gk":
                    t = _l2_normalize(t)
                toks[j][key][h] = t

    def ab_group():
        val = project(2 * N_HEADS, HEAD_DIM)
        for j in seqs:
            pabs[j] = val[rows(j), :]

    def ret_head(h):
        val = project(h)
        for j in seqs:
            part = val[rows(j), :]
            ops = _ret_head_ops(None, *(_head_cols(part, 0, g) for g in range(4)), cosf, sins,
                                _lane_bcast(rsc, h), _lane_bcast(rsc, N_HEADS + h))
            for key in ("rq", "rk", "rv", "qd", "sgr"):
                toks[j][key][h] = ops[key]
            toks[j]["kdT"][h] = ops["kd"].T

    def gate_group():
        val = project(2 * N_HEADS - 1)
        for j in seqs:
            for h in heads:
                toks[j]["sgz"][h] = _silu(_head_cols(val[rows(j), :], 0, h))

    ab_group()
    conv_group(1, "gk")
    conv_group(0, "gq")
    conv_group(2, "gv")
    for j in seqs:
        tail_ref[j] = halo_ref[j]

    gate_group()
    for h in heads:
        ret_head(h)

    ret, gdn = _chunk_prep("prompt", toks, pabs, dmat_ref, alog_ref[...], dtb_ref[...])
    _seq_step(ret_cd, ret, gdn, gnw_ref[...], mix_ref, sr_out, sg_out, sr_ref, sg_ref)


def _prompt_fused(ret_cd, x, ln_g, ln_b, w_all, cosf, sins, rsc, cw, halo0, dmat, alog, dtb, gnw, s0r, s0g, nb):
    n_batch, seq, _ = x.shape
    n_chunks = seq // TILE
    tile = lambda b, n: (b, n, 0)
    pos = lambda b, n: (n, 0)
    c2 = lambda b, n: (0, 0)
    c3 = lambda b, n: (0, 0, 0)
    out_specs, out_shape, state_scratch = _seq_out(n_batch, n_chunks, nb)
    return pl.pallas_call(
        functools.partial(_prompt_fused_body, ret_cd),
        grid=(n_batch // nb, n_chunks),
        in_specs=[
            pl.BlockSpec((nb, TILE, D_MODEL), tile),
            pl.BlockSpec((1, D_MODEL), c2),
            pl.BlockSpec((1, D_MODEL), c2),
            pl.BlockSpec((D_MODEL, W_ALL_COLS), c2, pipeline_mode=pl.Buffered(1)),
            pl.BlockSpec((TILE, HEAD_DIM), pos),
            pl.BlockSpec((TILE, HEAD_DIM), pos),
            pl.BlockSpec((TILE, HEAD_DIM), c2),
            pl.BlockSpec((CONV_W, CONV_COLS), c2),
            pl.BlockSpec((HALO, CONV_COLS), c2),
            pl.BlockSpec((N_HEADS, TILE, TILE), c3),
            pl.BlockSpec((1, HEAD_DIM), c2),
            pl.BlockSpec((1, HEAD_DIM), c2),
            pl.BlockSpec((1, HEAD_DIM), c2),
            pl.BlockSpec(_STATE_SHAPE, c3),
            pl.BlockSpec(_STATE_SHAPE, c3),
        ],
        out_specs=out_specs + [pl.BlockSpec((nb, HALO, CONV_COLS), lambda b, n: (b, 0, 0))],
        out_shape=out_shape + [jax.ShapeDtypeStruct((n_batch, HALO, CONV_COLS), F32)],
        scratch_shapes=state_scratch + [pltpu.VMEM((nb, HALO, CONV_COLS), F32)],
        compiler_params=_cparams(("parallel", "arbitrary")),
        name="prompt_fused",
    )(x, ln_g, ln_b, w_all, cosf, sins, rsc, cw, halo0, dmat, alog, dtb, gnw, s0r, s0g)


SAMPLE_PER_TILE = TILE // SAMPLE_GROUP


def _stack_by_group(xt):
    shape3 = (SAMPLE_PER_TILE, HEAD_DIM, TILE)
    keep = lax.broadcasted_iota(jnp.int32, shape3, 0) == (lax.broadcasted_iota(jnp.int32, shape3, 2) >> 3)
    stacked = jnp.where(keep, jnp.broadcast_to(xt[None], shape3), 0.0)
    return stacked.reshape(SAMPLE_PER_TILE * HEAD_DIM, TILE)


def _sample_mixer_body(ret_cd, x_ref, eg_ref, eb_ref, w_ref, cos_ref, sin_ref, rsc_ref, dmat_ref, cw_ref,
                       alog_ref, dtb_ref, gnw_ref, convpad_ref, sr_in, sg_in,
                       mix_ref, sr_out, sg_out, preconv_ref,
                       qd_s, wk_s, qg_s, wv_s, cd_s, inter_s, u_s, qs_s):
    heads = range(N_HEADS)
    pret, pgdn, pab = _project_tile(x_ref, eg_ref, eb_ref, w_ref)
    preconv_ref[...] = pgdn[:, :CONV_COLS]
    tok = _tokenwise("sample", pret, pgdn, cos_ref[...], sin_ref[...], rsc_ref[...],
                     cw_ref, convpad_ref[...])
    tok["kdT"] = [t.T for t in tok["kd"]]
    ret, gdn = _chunk_prep("sample", [tok], [pab], dmat_ref, alog_ref[...], dtb_ref[...])
    for h in heads:
        qd_s[h] = ret["qd"][h]
        wk_s[h] = gdn["wk"][h]
        qg_s[h] = gdn["qg"][h]
        wv_s[h] = gdn["wv"][h]
    cd_s[...] = gdn["cdr"][0]

    def per_batch(b, carry):
        rows = pl.ds(pl.multiple_of(b * SAMPLE_GROUP, SAMPLE_GROUP), SAMPLE_GROUP)
        cd_rows = cd_s[rows, :]
        for h in heads:
            inter_s[h, rows, :] = _mm(qd_s[h, rows, :], sr_in[b, h])
            s = sg_in[b, h]
            both = _mm(jnp.concatenate([wk_s[h, rows, :], qg_s[h, rows, :]], 0), s)
            u_s[h, rows, :] = wv_s[h, rows, :] - both[:SAMPLE_GROUP]
            qs_s[h, rows, :] = both[SAMPLE_GROUP:]
            cd = jnp.broadcast_to(cd_rows[SAMPLE_GROUP - 1:SAMPLE_GROUP, h:h + 1], (HEAD_DIM, HEAD_DIM))
            sg_out[b, h] = cd * s
        return carry

    lax.fori_loop(0, SAMPLE_PER_TILE, per_batch, 0)

    out_rows = SAMPLE_PER_TILE * SAMPLE_TOKENS
    out_row = _iota2((out_rows, TILE), 0)
    token_row = ((out_row >> 2) << 3) + (SAMPLE_GROUP - SAMPLE_TOKENS) + (out_row & (SAMPLE_TOKENS - 1))
    select = (_iota2((out_rows, TILE), 1) == token_row).astype(BF16)
    compact = lambda t: jnp.dot(select, t, preferred_element_type=F32).astype(BF16)

    state_shape = (SAMPLE_PER_TILE, HEAD_DIM, HEAD_DIM)
    for h in heads:
        o = _mm(ret["scores"][h], ret["v"][h]) + inter_s[h]
        upd = _mm(_stack_by_group(ret["kdT"][h]), ret["v"][h]).reshape(state_shape)
        sr_out[:, h] = ret_cd[h] * sr_in[:, h] + upd
        mix_ref[:, h * HEAD_DIM:(h + 1) * HEAD_DIM] = compact(_rms_gate(o, ret["gate"][h]))

    for h in heads:
        u = u_s[h]
        o = qs_s[h] + _mm(gdn["attn"][h], u)
        upd = _mm(_stack_by_group(gdn["ktT"][h]), u).reshape(state_shape)
        sg_out[:, h] = sg_out[:, h] + upd
        mix_ref[:, GROUP_W + h * HEAD_DIM:GROUP_W + (h + 1) * HEAD_DIM] = compact(_rms_gate(
            o, gdn["gate"][h], gnw_ref[...]))


def _sample_mixer(ret_cd, x, ln_g, ln_b, w_all, cosf, sins, rsc, dmat, cw, alog, dtb, gnw, convpad, sr, sg):
    n_tiles = x.shape[0] // TILE
    tile = lambda i: (i, 0)
    c2 = lambda i: (0, 0)
    c3 = lambda i: (0, 0, 0)
    state = pl.BlockSpec((SAMPLE_PER_TILE,) + _STATE_SHAPE, lambda i: (i, 0, 0, 0))
    head_scratch = pltpu.VMEM((N_HEADS, TILE, HEAD_DIM), F32)
    return pl.pallas_call(
        functools.partial(_sample_mixer_body, ret_cd),
        grid=(n_tiles,),
        in_specs=[
            pl.BlockSpec((TILE, D_MODEL), tile),
            pl.BlockSpec((1, D_MODEL), c2),
            pl.BlockSpec((1, D_MODEL), c2),
            pl.BlockSpec((D_MODEL, W_ALL_COLS), c2, pipeline_mode=pl.Buffered(1)),
            pl.BlockSpec((TILE, HEAD_DIM), c2),
            pl.BlockSpec((TILE, HEAD_DIM), c2),
            pl.BlockSpec((TILE, HEAD_DIM), c2),
            pl.BlockSpec((N_HEADS, TILE, TILE), c3),
            pl.BlockSpec((CONV_W, CONV_COLS), c2),
            pl.BlockSpec((1, HEAD_DIM), c2),
            pl.BlockSpec((1, HEAD_DIM), c2),
            pl.BlockSpec((1, HEAD_DIM), c2),
            pl.BlockSpec((TILE, CONV_COLS), tile),
            state, state,
        ],
        out_specs=[pl.BlockSpec((SAMPLE_PER_TILE * SAMPLE_TOKENS, 2 * GROUP_W), tile), state, state,
                   pl.BlockSpec((TILE, CONV_COLS), tile)],
        out_shape=[
            jax.ShapeDtypeStruct((n_tiles * SAMPLE_PER_TILE * SAMPLE_TOKENS, 2 * GROUP_W), BF16),
            jax.ShapeDtypeStruct(sr.shape, F32),
            jax.ShapeDtypeStruct(sg.shape, F32),
            jax.ShapeDtypeStruct((x.shape[0], CONV_COLS), F32),
        ],
        scratch_shapes=[head_scratch] * 4 + [pltpu.VMEM((TILE, HEAD_DIM), F32)] + [head_scratch] * 3,
        compiler_params=_cparams(("parallel",)),
        name="mixer_sample",
    )(x, ln_g, ln_b, w_all, cosf, sins, rsc, dmat, cw, alog, dtb, gnw, convpad, sr, sg)


def _rotary_tables(pos):
    half = HEAD_DIM // 2
    inv = ROPE_BASE ** (-np.arange(half, dtype=np.float64) / half)
    ang = np.asarray(pos, np.float64)[:, None] * inv[None, :]
    cos, sin = np.cos(ang), np.sin(ang)
    return (jnp.asarray(np.concatenate([cos, cos], -1), F32),
            jnp.asarray(np.concatenate([-sin, sin], -1), F32))


def _retention_tables(seg, pos, valid, seg_len):
    gamma = 1.0 - 2.0 ** (-5.0 - np.arange(N_HEADS, dtype=np.float64))
    posf = np.asarray(pos, np.float64)
    rel = posf[:, None] - posf[None, :]
    causal = (seg[:, None] == seg[None, :]) & (rel >= 0)
    dmat = np.where(causal[None], gamma[:, None, None] ** np.where(causal, rel, 0.0)[None], 0.0)
    q_scale = gamma[None, :] ** (posf[:, None] + 1.0)
    k_scale = np.where(valid[:, None], gamma[None, :] ** (seg_len - 1.0 - posf[:, None]), 0.0)
    rsc = np.concatenate([q_scale, k_scale, np.zeros((TILE, HEAD_DIM - 2 * N_HEADS))], -1)
    chunk_decay = tuple(float(g ** seg_len) for g in gamma)
    return jnp.asarray(dmat, F32), jnp.asarray(rsc, F32), chunk_decay


def _pad_lanes(v):
    return jnp.pad(v.astype(F32), (0, HEAD_DIM - v.shape[0]))[None, :]


def kernel(x_prompt, x_sample, state_ret, state_gdn, state_conv, meta_tokens, emb_ln_g, emb_ln_b,
           w_in, conv_w, a_log, dt_bias, gdn_norm_w, w_out, ln1_g, ln1_b, w_gate_up, w_down,
           ln2_g, ln2_b):
    n_batch, seq, _ = x_prompt.shape
    dec_batch, dec_seq, _ = x_sample.shape
    assert seq % TILE == 0 and dec_seq == SAMPLE_TOKENS and N_META <= TILE
    assert n_batch % MIXER_SEQS_PER_STEP == 0
    layer = 0

    w_all = _weight_prep(jnp.swapaxes(w_in[layer], 0, 1))
    w_out_b = w_out[layer].astype(BF16)
    w_gu_b = w_gate_up[layer].astype(BF16)
    w_down_b = w_down[layer].astype(BF16)
    row = lambda v: v.astype(F32)[None, :]
    eg, eb = row(emb_ln_g), row(emb_ln_b)
    cw = conv_w[layer].astype(F32)
    alog, dtb, gnw = _pad_lanes(a_log[layer]), _pad_lanes(dt_bias[layer]), row(gdn_norm_w[layer])

    xp = x_prompt.reshape(n_batch * seq, D_MODEL)
    xm = jnp.pad(meta_tokens.astype(F32), ((0, TILE - N_META), (0, 0)))
    xs = jnp.pad(x_sample, ((0, 0), (SAMPLE_GROUP - dec_seq, 0), (0, 0))).reshape(
        dec_batch * SAMPLE_GROUP, D_MODEL)
    convpad = jnp.pad(state_conv[layer].astype(F32), ((0, 0), (1, SAMPLE_GROUP - CONV_W), (0, 0))).reshape(
        dec_batch * SAMPLE_GROUP, CONV_COLS)

    tile_idx = np.arange(TILE)
    cos_p, sin_p = _rotary_tables(N_META + np.arange(seq))
    cos_m, sin_m = _rotary_tables(tile_idx)
    tok = (tile_idx % SAMPLE_GROUP) - (SAMPLE_GROUP - dec_seq)
    cos_s, sin_s = _rotary_tables(PAST_LEN + np.maximum(tok, 0))
    zeros_i = np.zeros((TILE,), np.int32)
    all_valid = np.ones((TILE,), bool)
    dmat_p, rsc_p, cd_p = _retention_tables(zeros_i, tile_idx, all_valid, float(TILE))
    dmat_m, rsc_m, cd_m = _retention_tables(zeros_i, tile_idx, tile_idx < N_META, float(N_META))
    dmat_s, rsc_s, cd_s = _retention_tables(tile_idx // SAMPLE_GROUP, tok, tok >= 0, float(dec_seq))

    zero_state = jnp.zeros(_STATE_SHAPE, F32)
    _, sr_m, sg_m, preconv_m = _meta_mixer(cd_m, xm, eg, eb, w_all, cos_m, sin_m, rsc_m, dmat_m, cw, alog, dtb,
                                           gnw, zero_state, zero_state)
    halo_p = preconv_m[N_META - HALO:N_META]

    mix_p, sr_p, sg_p, tail_p = _prompt_fused(cd_p, x_prompt, eg, eb, w_all, cos_p, sin_p, rsc_p, cw, halo_p,
                                              dmat_p, alog, dtb, gnw, sr_m[0], sg_m[0], MIXER_SEQS_PER_STEP)
    mix_p = mix_p.reshape(n_batch * seq, 2 * GROUP_W)

    mix_s, sr_s, sg_s, preconv_s = _sample_mixer(cd_s, xs, eg, eb, w_all, cos_s, sin_s, rsc_s, dmat_s, cw,
                                                 alog, dtb, gnw, convpad, state_ret[layer].astype(F32),
                                                 state_gdn[layer].astype(F32))

    back = functools.partial(_back, eg=eg, eb=eb, w_out=w_out_b, g1=row(ln1_g[layer]), b1=row(ln1_b[layer]),
                             w_gu=w_gu_b, w_down=w_down_b, g2=row(ln2_g[layer]), b2=row(ln2_b[layer]),
                             tm=512)
    y_p = back(xp, mix_p).reshape(n_batch, seq, D_MODEL)
    y_s = back(x_sample.reshape(dec_batch * dec_seq, D_MODEL), mix_s).reshape(dec_batch, dec_seq, D_MODEL)

    conv_p = tail_p[:, HALO - (CONV_W - 1):, :]
    conv_s = preconv_s.reshape(dec_batch, SAMPLE_GROUP, CONV_COLS)[:, SAMPLE_GROUP - (CONV_W - 1):, :]
    return (y_p, y_s, sr_p[None], sg_p[None], conv_p[None], sr_s[None], sg_s[None], conv_s[None])
```
